```python
import jax, jax.numpy as jnp
from jax import lax
import numpy as np


D_MODEL = 2048
BATCH = 8
SEQ = 4096
DEPTH = 2

GRID_W = 64
CTX_LEN = 256
N_HEADS = 8
N_KV_HEADS = 2
HEAD_DIM = 128
ATTN_W = N_HEADS * HEAD_DIM
KV_W = N_KV_HEADS * HEAD_DIM
Q_BLOCK = 128
ROPE_THETA = 10000.0
ROPE_PAIRS = HEAD_DIM // 4
D_RNN = 1024
RNN_BLOCKS = 8
RNN_BLOCK_W = D_RNN // RNN_BLOCKS
CONV_W = 4
RG_C = 8.0
AR_IN = ATTN_W + 2 * KV_W + 2 * D_RNN
AR_OUT = ATTN_W + D_RNN
D_GM = 2048
GM_GROUPS = 16
GM_GROUP_W = D_GM // GM_GROUPS
CHUNK = 128
D_FF = 4 * D_MODEL
EPS = 1e-6
N_EVEN = (DEPTH + 1) // 2
N_ODD = DEPTH // 2

kernel_name = 'hybrid_attn_rglru_chunkgmlp_diffusion'


def rms_norm(x, g):
    xf = x.astype(jnp.float32)
    y = xf * lax.rsqrt(jnp.mean(xf * xf, axis=-1, keepdims=True) + EPS)
    return (y * g.astype(jnp.float32)).astype(x.dtype)


def layer_norm(x, g, b):
    xf = x.astype(jnp.float32)
    mu = jnp.mean(xf, axis=-1, keepdims=True)
    xc = xf - mu
    y = xc * lax.rsqrt(jnp.mean(xc * xc, axis=-1, keepdims=True) + EPS)
    return (y * g.astype(jnp.float32) + b.astype(jnp.float32)).astype(x.dtype)


def modulate(h, shift, scale):
    return h * (1 + scale) + shift


def axial_angles(n):
    rows = n // GRID_W
    r_idx, c_idx = jnp.meshgrid(jnp.arange(rows), jnp.arange(GRID_W), indexing='ij')
    r_idx = r_idx.reshape(-1).astype(jnp.float32)
    c_idx = c_idx.reshape(-1).astype(jnp.float32)
    freqs = ROPE_THETA ** (-jnp.arange(ROPE_PAIRS, dtype=jnp.float32) / ROPE_PAIRS)
    return r_idx[:, None] * freqs, c_idx[:, None] * freqs


def rope_1d(x, ang):
    x1, x2 = jnp.split(x.astype(jnp.float32), 2, axis=-1)
    cos = jnp.cos(ang)[None, :, None, :]
    sin = jnp.sin(ang)[None, :, None, :]
    return jnp.concatenate([x1 * cos - x2 * sin, x2 * cos + x1 * sin], axis=-1)


def rope_2d(x, ang_row, ang_col):
    half = HEAD_DIM // 2
    out = jnp.concatenate([rope_1d(x[..., :half], ang_row), rope_1d(x[..., half:], ang_col)], axis=-1)
    return out.astype(x.dtype)


def gqa_attend(q, k, v):
    bsz, n = q.shape[0], q.shape[1]
    nb = n // Q_BLOCK
    groups = N_HEADS // N_KV_HEADS
    scale = HEAD_DIM ** -0.5
    qb = q.reshape(bsz, nb, Q_BLOCK, N_KV_HEADS, groups, HEAD_DIM).transpose(1, 0, 2, 3, 4, 5)

    def one_block(qi):
        s = jnp.einsum('bqkgd,btkd->bkgqt', qi, k).astype(jnp.float32) * scale
        p = jax.nn.softmax(s, axis=-1).astype(v.dtype)
        return jnp.einsum('bkgqt,btkd->bqkgd', p, v)

    o = lax.map(one_block, qb)
    return o.transpose(1, 0, 2, 3, 4, 5).reshape(bsz, n, N_HEADS * HEAD_DIM)


def centred_dwconv(x, w, b):
    n = x.shape[1]
    left = CONV_W // 2
    xp = jnp.pad(x, ((0, 0), (left, CONV_W - 1 - left), (0, 0)))
    y = b
    for j in range(CONV_W):
        y = y + xp[:, j:j + n] * w[j]
    return y


def block_diag(x, w, b):
    xb = x.reshape(x.shape[0], x.shape[1], RNN_BLOCKS, RNN_BLOCK_W)
    return jnp.einsum('bsnc,ncd->bsnd', xb, w).reshape(x.shape) + b


def rglru_direction(x, wa, ba, wx, bx, lam, h0, reverse):
    r = jax.nn.sigmoid(block_diag(x, wa, ba).astype(jnp.float32))
    i = jax.nn.sigmoid(block_diag(x, wx, bx).astype(jnp.float32))
    log_a = -RG_C * r * jax.nn.softplus(-lam.astype(jnp.float32))
    a = jnp.exp(log_a)
    b = jnp.sqrt(-jnp.expm1(2.0 * log_a)) * (i * x.astype(jnp.float32))

    def combine(lhs, rhs):
        return (lhs[0] * rhs[0], rhs[0] * lhs[1] + rhs[1])

    a_cum, h = lax.associative_scan(combine, (a, b), reverse=reverse, axis=1)
    h = h + a_cum * h0[:, None, :]
    final = h[:, 0] if reverse else h[:, -1]
    return h, final


def mix_attn_rglru(hl, hc, w_in, q_g, k_g, conv_w, conv_b, wa, ba, wx, bx, lam, w_out, need_ctx):
    s1 = ATTN_W
    s2 = s1 + KV_W
    s3 = s2 + KV_W
    s4 = s3 + D_RNN

    def project(h):
        bsz, n = h.shape[0], h.shape[1]
        q, k, v, xr, gr = jnp.split(h @ w_in, [s1, s2, s3, s4], axis=-1)
        q = rms_norm(q.reshape(bsz, n, N_HEADS, HEAD_DIM), q_g)
        k = rms_norm(k.reshape(bsz, n, N_KV_HEADS, HEAD_DIM), k_g)
        v = v.reshape(bsz, n, N_KV_HEADS, HEAD_DIM)
        xr = centred_dwconv(xr, conv_w, conv_b)
        return q, k, v, xr, gr

    ql, kl, vl, xl, gl = project(hl)
    qc, kc, vc, xc, gc = project(hc)

    ang_r, ang_c = axial_angles(hl.shape[1])
    ql = rope_2d(ql, ang_r, ang_c)
    kl = rope_2d(kl, ang_r, ang_c)
    k_all = jnp.concatenate([kc, kl], axis=1)
    v_all = jnp.concatenate([vc, vl], axis=1)
    attn_l = gqa_attend(ql, k_all, v_all)

    zeros = jnp.zeros((hc.shape[0], D_RNN), jnp.float32)
    hcf, fin_f = rglru_direction(xc, wa[0], ba[0], wx[0], bx[0], lam[0], zeros, False)
    hcb, fin_b = rglru_direction(xc, wa[1], ba[1], wx[1], bx[1], lam[1], zeros, True)
    hlf, _ = rglru_direction(xl, wa[0], ba[0], wx[0], bx[0], lam[0], fin_f, False)
    hlb, _ = rglru_direction(xl, wa[1], ba[1], wx[1], bx[1], lam[1], fin_b, True)
    rnn_l = ((hlf + hlb) * jax.nn.gelu(gl.astype(jnp.float32))).astype(hl.dtype)
    out_l = jnp.concatenate([attn_l, rnn_l], axis=-1) @ w_out

    out_c = None
    if need_ctx:
        attn_c = gqa_attend(qc, kc, vc)
        rnn_c = ((hcf + hcb) * jax.nn.gelu(gc.astype(jnp.float32))).astype(hc.dtype)
        out_c = jnp.concatenate([attn_c, rnn_c], axis=-1) @ w_out
    return out_l, out_c


def chunk_gmlp(h, w_in, b_in, v_g, v_b, w_sp, b_sp, w_out):
    bsz, n = h.shape[0], h.shape[1]
    z = jax.nn.gelu(h @ w_in + b_in)
    u, v = jnp.split(z, 2, axis=-1)
    v = layer_norm(v, v_g, v_b)
    v = v.reshape(bsz, n // CHUNK, CHUNK, GM_GROUPS, GM_GROUP_W)
    sv = jnp.einsum('gpq,bcqgd->bcpgd', w_sp, v) + b_sp.T[None, None, :, :, None]
    return (u * sv.reshape(bsz, n, D_GM)) @ w_out


def sq_relu_mlp(h, w1, w2):
    return jnp.square(jax.nn.relu(h @ w1)) @ w2


def _fwd_setup_inputs(seed: int = 0) -> dict:
    key = jax.random.key(seed)
    ks = jax.random.split(key, 32)
    f32 = jnp.float32
    D = D_MODEL

    def nrm(k, shape, scale):
        return jax.random.normal(k, shape, f32) * scale

    lam_u = jax.random.uniform(ks[20], (N_EVEN, 2, D_RNN), f32, 0.9, 0.999)
    a0 = lam_u ** (1.0 / RG_C)
    return {
        'x': nrm(ks[0], (BATCH, SEQ, D), 1.0),
        'c': nrm(ks[1], (BATCH, D), 1.0),
        'ctx': nrm(ks[2], (BATCH, CTX_LEN, D), 1.0),
        'c_ctx': nrm(ks[3], (D,), 1.0),
        'w_mod': nrm(ks[4], (DEPTH, D, 6 * D), 0.5 * D ** -0.5),
        'b_mod': nrm(ks[5], (DEPTH, 6 * D), 0.02),
        'norm_g': 1.0 + nrm(ks[6], (DEPTH, 4, D), 0.02),
        'w_ff_in': nrm(ks[7], (DEPTH, D, D_FF), D ** -0.5),
        'w_ff_out': nrm(ks[8], (DEPTH, D_FF, D), D_FF ** -0.5),
        'ar_w_in': nrm(ks[9], (N_EVEN, D, AR_IN), D ** -0.5),
        'ar_q_g': 1.0 + nrm(ks[10], (N_EVEN, HEAD_DIM), 0.02),
        'ar_k_g': 1.0 + nrm(ks[11], (N_EVEN, HEAD_DIM), 0.02),
        'ar_conv_w': nrm(ks[12], (N_EVEN, CONV_W, D_RNN), CONV_W ** -0.5),
        'ar_conv_b': nrm(ks[13], (N_EVEN, D_RNN), 0.02),
        'ar_wa': nrm(ks[14], (N_EVEN, 2, RNN_BLOCKS, RNN_BLOCK_W, RNN_BLOCK_W), RNN_BLOCK_W ** -0.5),
        'ar_ba': nrm(ks[15], (N_EVEN, 2, D_RNN), 0.02),
        'ar_wx': nrm(ks[16], (N_EVEN, 2, RNN_BLOCKS, RNN_BLOCK_W, RNN_BLOCK_W), RNN_BLOCK_W ** -0.5),
        'ar_bx': nrm(ks[17], (N_EVEN, 2, D_RNN), 0.02),
        'ar_lambda': jnp.log(a0) - jnp.log1p(-a0),
        'ar_w_out': nrm(ks[18], (N_EVEN, AR_OUT, D), AR_OUT ** -0.5),
        'gm_w_in': nrm(ks[19], (N_ODD, D, 2 * D_GM), D ** -0.5),
        'gm_b_in': nrm(ks[21], (N_ODD, 2 * D_GM), 0.02),
        'gm_v_g': 1.0 + nrm(ks[22], (N_ODD, D_GM), 0.02),
        'gm_v_b': nrm(ks[23], (N_ODD, D_GM), 0.02),
        'gm_w_sp': nrm(ks[24], (N_ODD, GM_GROUPS, CHUNK, CHUNK), CHUNK ** -0.5),
        'gm_b_sp': 1.0 + nrm(ks[25], (N_ODD, GM_GROUPS, CHUNK), 0.02),
        'gm_w_out': nrm(ks[26], (N_ODD, D_GM, D), D_GM ** -0.5),
    }


def _fwd_reference(x, c, ctx, c_ctx, w_mod, b_mod, norm_g, w_ff_in, w_ff_out, ar_w_in, ar_q_g, ar_k_g, ar_conv_w, ar_conv_b, ar_wa, ar_ba, ar_wx, ar_bx, ar_lambda, ar_w_out, gm_w_in, gm_b_in, gm_v_g, gm_v_b, gm_w_sp, gm_b_sp, gm_w_out):
    xl, xc = x, ctx
    s_c = jax.nn.silu(c)
    s_ctx = jax.nn.silu(c_ctx)
    for i in range(DEPTH):
        j = i // 2
        need_ctx = any(l % 2 == 0 for l in range(i + 1, DEPTH))
        g = norm_g[i]
        ml = jnp.split((s_c @ w_mod[i] + b_mod[i])[:, None, :], 6, axis=-1)
        mc = jnp.split((s_ctx @ w_mod[i] + b_mod[i])[None, None, :], 6, axis=-1)
        hl = modulate(rms_norm(xl, g[0]), ml[0], ml[1])
        oc = None
        if i % 2 == 0:
            hc = modulate(rms_norm(xc, g[0]), mc[0], mc[1])
            ol, oc = mix_attn_rglru(hl, hc, ar_w_in[j], ar_q_g[j], ar_k_g[j], ar_conv_w[j], ar_conv_b[j],
                                    ar_wa[j], ar_ba[j], ar_wx[j], ar_bx[j], ar_lambda[j], ar_w_out[j], need_ctx)
        else:
            ol = chunk_gmlp(hl, gm_w_in[j], gm_b_in[j], gm_v_g[j], gm_v_b[j], gm_w_sp[j], gm_b_sp[j], gm_w_out[j])
            if need_ctx:
                hc = modulate(rms_norm(xc, g[0]), mc[0], mc[1])
                oc = chunk_gmlp(hc, gm_w_in[j], gm_b_in[j], gm_v_g[j], gm_v_b[j], gm_w_sp[j], gm_b_sp[j], gm_w_out[j])
        xl = xl + ml[2] * rms_norm(ol, g[1])
        hl = modulate(rms_norm(xl, g[2]), ml[3], ml[4])
        xl = xl + ml[5] * rms_norm(sq_relu_mlp(hl, w_ff_in[i], w_ff_out[i]), g[3])
        if need_ctx:
            xc = xc + mc[2] * rms_norm(oc, g[1])
            hc = modulate(rms_norm(xc, g[2]), mc[3], mc[4])
            xc = xc + mc[5] * rms_norm(sq_relu_mlp(hc, w_ff_in[i], w_ff_out[i]), g[3])
    return xl


import jax as _jax
import jax.numpy as _jnp

TWIN_FORMAT = 'train_step'
FWD_PARAMS = ['x', 'c', 'ctx', 'c_ctx', 'w_mod', 'b_mod', 'norm_g', 'w_ff_in', 'w_ff_out', 'ar_w_in', 'ar_q_g', 'ar_k_g', 'ar_conv_w', 'ar_conv_b', 'ar_wa', 'ar_ba', 'ar_wx', 'ar_bx', 'ar_lambda', 'ar_w_out', 'gm_w_in', 'gm_b_in', 'gm_v_g', 'gm_v_b', 'gm_w_sp', 'gm_b_sp', 'gm_w_out']
TWIN_WEIGHTS = ['c_ctx', 'w_mod', 'b_mod', 'norm_g', 'w_ff_in', 'w_ff_out', 'ar_w_in', 'ar_q_g', 'ar_k_g', 'ar_conv_w', 'ar_conv_b', 'ar_wa', 'ar_ba', 'ar_wx', 'ar_bx', 'ar_lambda', 'ar_w_out', 'gm_w_in', 'gm_b_in', 'gm_v_g', 'gm_v_b', 'gm_w_sp', 'gm_b_sp', 'gm_w_out']
TWIN_DIFF_INPUT = 'x'
TWIN_INPUTS = ['x', 'c', 'ctx', 'c_ctx', 'w_mod', 'b_mod', 'norm_g', 'w_ff_in', 'w_ff_out', 'ar_w_in', 'ar_q_g', 'ar_k_g', 'ar_conv_w', 'ar_conv_b', 'ar_wa', 'ar_ba', 'ar_wx', 'ar_bx', 'ar_lambda', 'ar_w_out', 'gm_w_in', 'gm_b_in', 'gm_v_g', 'gm_v_b', 'gm_w_sp', 'gm_b_sp', 'gm_w_out', 'loss_target', 'm_c_ctx', 'm_w_mod', 'm_b_mod', 'm_norm_g', 'm_w_ff_in', 'm_w_ff_out', 'm_ar_w_in', 'm_ar_q_g', 'm_ar_k_g', 'm_ar_conv_w', 'm_ar_conv_b', 'm_ar_wa', 'm_ar_ba', 'm_ar_wx', 'm_ar_bx', 'm_ar_lambda', 'm_ar_w_out', 'm_gm_w_in', 'm_gm_b_in', 'm_gm_v_g', 'm_gm_v_b', 'm_gm_w_sp', 'm_gm_b_sp', 'm_gm_w_out', 'v_c_ctx', 'v_w_mod', 'v_b_mod', 'v_norm_g', 'v_w_ff_in', 'v_w_ff_out', 'v_ar_w_in', 'v_ar_q_g', 'v_ar_k_g', 'v_ar_conv_w', 'v_ar_conv_b', 'v_ar_wa', 'v_ar_ba', 'v_ar_wx', 'v_ar_bx', 'v_ar_lambda', 'v_ar_w_out', 'v_gm_w_in', 'v_gm_b_in', 'v_gm_v_g', 'v_gm_v_b', 'v_gm_w_sp', 'v_gm_b_sp', 'v_gm_w_out']
TWIN_OUTPUTS = ['loss', 'grad_x', 'grad_c_ctx', 'grad_w_mod', 'grad_b_mod', 'grad_norm_g', 'grad_w_ff_in', 'grad_w_ff_out', 'grad_ar_w_in', 'grad_ar_q_g', 'grad_ar_k_g', 'grad_ar_conv_w', 'grad_ar_conv_b', 'grad_ar_wa', 'grad_ar_ba', 'grad_ar_wx', 'grad_ar_bx', 'grad_ar_lambda', 'grad_ar_w_out', 'grad_gm_w_in', 'grad_gm_b_in', 'grad_gm_v_g', 'grad_gm_v_b', 'grad_gm_w_sp', 'grad_gm_b_sp', 'grad_gm_w_out', 'delta_c_ctx', 'delta_w_mod', 'delta_b_mod', 'delta_norm_g', 'delta_w_ff_in', 'delta_w_ff_out', 'delta_ar_w_in', 'delta_ar_q_g', 'delta_ar_k_g', 'delta_ar_conv_w', 'delta_ar_conv_b', 'delta_ar_wa', 'delta_ar_ba', 'delta_ar_wx', 'delta_ar_bx', 'delta_ar_lambda', 'delta_ar_w_out', 'delta_gm_w_in', 'delta_gm_b_in', 'delta_gm_v_g', 'delta_gm_v_b', 'delta_gm_w_sp', 'delta_gm_b_sp', 'delta_gm_w_out', 'new_m_c_ctx', 'new_m_w_mod', 'new_m_b_mod', 'new_m_norm_g', 'new_m_w_ff_in', 'new_m_w_ff_out', 'new_m_ar_w_in', 'new_m_ar_q_g', 'new_m_ar_k_g', 'new_m_ar_conv_w', 'new_m_ar_conv_b', 'new_m_ar_wa', 'new_m_ar_ba', 'new_m_ar_wx', 'new_m_ar_bx', 'new_m_ar_lambda', 'new_m_ar_w_out', 'new_m_gm_w_in', 'new_m_gm_b_in', 'new_m_gm_v_g', 'new_m_gm_v_b', 'new_m_gm_w_sp', 'new_m_gm_b_sp', 'new_m_gm_w_out', 'new_v_c_ctx', 'new_v_w_mod', 'new_v_b_mod', 'new_v_norm_g', 'new_v_w_ff_in', 'new_v_w_ff_out', 'new_v_ar_w_in', 'new_v_ar_q_g', 'new_v_ar_k_g', 'new_v_ar_conv_w', 'new_v_ar_conv_b', 'new_v_ar_wa', 'new_v_ar_ba', 'new_v_ar_wx', 'new_v_ar_bx', 'new_v_ar_lambda', 'new_v_ar_w_out', 'new_v_gm_w_in', 'new_v_gm_b_in', 'new_v_gm_v_g', 'new_v_gm_v_b', 'new_v_gm_w_sp', 'new_v_gm_b_sp', 'new_v_gm_w_out']
TWIN_LEAF_KINDS = {'loss': 'loss', 'grad_x': 'grad_x', 'grad_c_ctx': 'grad_w', 'grad_w_mod': 'grad_w', 'grad_b_mod': 'grad_w', 'grad_norm_g': 'grad_w', 'grad_w_ff_in': 'grad_w', 'grad_w_ff_out': 'grad_w', 'grad_ar_w_in': 'grad_w', 'grad_ar_q_g': 'grad_w', 'grad_ar_k_g': 'grad_w', 'grad_ar_conv_w': 'grad_w', 'grad_ar_conv_b': 'grad_w', 'grad_ar_wa': 'grad_w', 'grad_ar_ba': 'grad_w', 'grad_ar_wx': 'grad_w', 'grad_ar_bx': 'grad_w', 'grad_ar_lambda': 'grad_w', 'grad_ar_w_out': 'grad_w', 'grad_gm_w_in': 'grad_w', 'grad_gm_b_in': 'grad_w', 'grad_gm_v_g': 'grad_w', 'grad_gm_v_b': 'grad_w', 'grad_gm_w_sp': 'grad_w', 'grad_gm_b_sp': 'grad_w', 'grad_gm_w_out': 'grad_w', 'delta_c_ctx': 'delta_w', 'delta_w_mod': 'delta_w', 'delta_b_mod': 'delta_w', 'delta_norm_g': 'delta_w', 'delta_w_ff_in': 'delta_w', 'delta_w_ff_out': 'delta_w', 'delta_ar_w_in': 'delta_w', 'delta_ar_q_g': 'delta_w', 'delta_ar_k_g': 'delta_w', 'delta_ar_conv_w': 'delta_w', 'delta_ar_conv_b': 'delta_w', 'delta_ar_wa': 'delta_w', 'delta_ar_ba': 'delta_w', 'delta_ar_wx': 'delta_w', 'delta_ar_bx': 'delta_w', 'delta_ar_lambda': 'delta_w', 'delta_ar_w_out': 'delta_w', 'delta_gm_w_in': 'delta_w', 'delta_gm_b_in': 'delta_w', 'delta_gm_v_g': 'delta_w', 'delta_gm_v_b': 'delta_w', 'delta_gm_w_sp': 'delta_w', 'delta_gm_b_sp': 'delta_w', 'delta_gm_w_out': 'delta_w', 'new_m_c_ctx': 'new_m', 'new_m_w_mod': 'new_m', 'new_m_b_mod': 'new_m', 'new_m_norm_g': 'new_m', 'new_m_w_ff_in': 'new_m', 'new_m_w_ff_out': 'new_m', 'new_m_ar_w_in': 'new_m', 'new_m_ar_q_g': 'new_m', 'new_m_ar_k_g': 'new_m', 'new_m_ar_conv_w': 'new_m', 'new_m_ar_conv_b': 'new_m', 'new_m_ar_wa': 'new_m', 'new_m_ar_ba': 'new_m', 'new_m_ar_wx': 'new_m', 'new_m_ar_bx': 'new_m', 'new_m_ar_lambda': 'new_m', 'new_m_ar_w_out': 'new_m', 'new_m_gm_w_in': 'new_m', 'new_m_gm_b_in': 'new_m', 'new_m_gm_v_g': 'new_m', 'new_m_gm_v_b': 'new_m', 'new_m_gm_w_sp': 'new_m', 'new_m_gm_b_sp': 'new_m', 'new_m_gm_w_out': 'new_m', 'new_v_c_ctx': 'new_v', 'new_v_w_mod': 'new_v', 'new_v_b_mod': 'new_v', 'new_v_norm_g': 'new_v', 'new_v_w_ff_in': 'new_v', 'new_v_w_ff_out': 'new_v', 'new_v_ar_w_in': 'new_v', 'new_v_ar_q_g': 'new_v', 'new_v_ar_k_g': 'new_v', 'new_v_ar_conv_w': 'new_v', 'new_v_ar_conv_b': 'new_v', 'new_v_ar_wa': 'new_v', 'new_v_ar_ba': 'new_v', 'new_v_ar_wx': 'new_v', 'new_v_ar_bx': 'new_v', 'new_v_ar_lambda': 'new_v', 'new_v_ar_w_out': 'new_v', 'new_v_gm_w_in': 'new_v', 'new_v_gm_b_in': 'new_v', 'new_v_gm_v_g': 'new_v', 'new_v_gm_v_b': 'new_v', 'new_v_gm_w_sp': 'new_v', 'new_v_gm_b_sp': 'new_v', 'new_v_gm_w_out': 'new_v'}


def _forward(args):
    return _fwd_reference(*[args[k] for k in FWD_PARAMS])


def _output_shape():
    def fwd():
        inp = _fwd_setup_inputs(0)
        return _fwd_reference(*[inp[k] for k in FWD_PARAMS])
    out = _jax.eval_shape(fwd)
    return out.shape, out.dtype

N_MICROBATCH = 1
ADAM_LR = 0.001
ADAM_B1 = 0.9
ADAM_B2 = 0.999
ADAM_EPS = 1e-08
ADAM_WD = 0.01
ADAM_STEP = 10
PER_EXAMPLE_BATCH_AXIS = {'x': 0, 'c': 0, 'ctx': 0, 'loss_target': 0}
SHARED_INPUTS = []
_WEIGHT_DTYPES = {'c_ctx': _jnp.float32, 'w_mod': _jnp.float32, 'b_mod': _jnp.float32, 'norm_g': _jnp.float32, 'w_ff_in': _jnp.float32, 'w_ff_out': _jnp.float32, 'ar_w_in': _jnp.float32, 'ar_q_g': _jnp.float32, 'ar_k_g': _jnp.float32, 'ar_conv_w': _jnp.float32, 'ar_conv_b': _jnp.float32, 'ar_wa': _jnp.float32, 'ar_ba': _jnp.float32, 'ar_wx': _jnp.float32, 'ar_bx': _jnp.float32, 'ar_lambda': _jnp.float32, 'ar_w_out': _jnp.float32, 'gm_w_in': _jnp.float32, 'gm_b_in': _jnp.float32, 'gm_v_g': _jnp.float32, 'gm_v_b': _jnp.float32, 'gm_w_sp': _jnp.float32, 'gm_b_sp': _jnp.float32, 'gm_w_out': _jnp.float32}
MOMENT_SCALE = {'c_ctx': 3.353226e-02, 'w_mod': 6.790012e-01, 'b_mod': 1.425183e+00, 'norm_g': 1.180560e+00, 'w_ff_in': 5.573914e-02, 'w_ff_out': 2.334814e-01, 'ar_w_in': 1.914866e-01, 'ar_q_g': 9.444080e-03, 'ar_k_g': 9.488010e-03, 'ar_conv_w': 3.272540e-01, 'ar_conv_b': 1.095297e+00, 'ar_wa': 1.548667e-02, 'ar_ba': 2.287198e-02, 'ar_wx': 3.373269e-02, 'ar_bx': 6.849857e-02, 'ar_lambda': 6.233637e-02, 'ar_w_out': 2.359479e-01, 'gm_w_in': 6.466325e-02, 'gm_b_in': 1.746067e-01, 'gm_v_g': 3.217801e-02, 'gm_v_b': 3.253074e-02, 'gm_w_sp': 3.099448e-02, 'gm_b_sp': 3.119945e-02, 'gm_w_out': 1.714783e-01}


def _to_microbatches(a, axis):
    t = _jnp.moveaxis(a, axis, 0)
    t = t.reshape((N_MICROBATCH, t.shape[0] // N_MICROBATCH) + t.shape[1:])
    return _jnp.moveaxis(t, 1, axis + 1)


def setup_inputs(seed: int = 0) -> dict:
    inp = _fwd_setup_inputs(seed)
    key = _jax.random.fold_in(_jax.random.key(seed), 7919)
    shape, _ = _output_shape()
    out = dict(inp)
    out["loss_target"] = _jax.random.normal(_jax.random.fold_in(key, 0), shape, _jnp.float32)
    for i, name in enumerate(TWIN_WEIGHTS):
        w = inp[name].astype(_jnp.float32)
        if MOMENT_SCALE is None:
            s = _jnp.sqrt(_jnp.mean(_jnp.square(w)) + 1e-30)
        else:
            s = MOMENT_SCALE[name]
        km, kv = _jax.random.split(_jax.random.fold_in(key, i + 1))
        out[name] = w
        out["m_" + name] = s * _jax.random.normal(km, w.shape, _jnp.float32)
        out["v_" + name] = (s * s) * _jax.random.uniform(kv, w.shape, _jnp.float32, 0.5, 1.5)
    if N_MICROBATCH > 1:
        for name, axis in PER_EXAMPLE_BATCH_AXIS.items():
            out[name] = _to_microbatches(out[name], axis)
    return {'x': out['x'], 'c': out['c'], 'ctx': out['ctx'], 'c_ctx': out['c_ctx'], 'w_mod': out['w_mod'], 'b_mod': out['b_mod'], 'norm_g': out['norm_g'], 'w_ff_in': out['w_ff_in'], 'w_ff_out': out['w_ff_out'], 'ar_w_in': out['ar_w_in'], 'ar_q_g': out['ar_q_g'], 'ar_k_g': out['ar_k_g'], 'ar_conv_w': out['ar_conv_w'], 'ar_conv_b': out['ar_conv_b'], 'ar_wa': out['ar_wa'], 'ar_ba': out['ar_ba'], 'ar_wx': out['ar_wx'], 'ar_bx': out['ar_bx'], 'ar_lambda': out['ar_lambda'], 'ar_w_out': out['ar_w_out'], 'gm_w_in': out['gm_w_in'], 'gm_b_in': out['gm_b_in'], 'gm_v_g': out['gm_v_g'], 'gm_v_b': out['gm_v_b'], 'gm_w_sp': out['gm_w_sp'], 'gm_b_sp': out['gm_b_sp'], 'gm_w_out': out['gm_w_out'], 'loss_target': out['loss_target'], 'm_c_ctx': out['m_c_ctx'], 'm_w_mod': out['m_w_mod'], 'm_b_mod': out['m_b_mod'], 'm_norm_g': out['m_norm_g'], 'm_w_ff_in': out['m_w_ff_in'], 'm_w_ff_out': out['m_w_ff_out'], 'm_ar_w_in': out['m_ar_w_in'], 'm_ar_q_g': out['m_ar_q_g'], 'm_ar_k_g': out['m_ar_k_g'], 'm_ar_conv_w': out['m_ar_conv_w'], 'm_ar_conv_b': out['m_ar_conv_b'], 'm_ar_wa': out['m_ar_wa'], 'm_ar_ba': out['m_ar_ba'], 'm_ar_wx': out['m_ar_wx'], 'm_ar_bx': out['m_ar_bx'], 'm_ar_lambda': out['m_ar_lambda'], 'm_ar_w_out': out['m_ar_w_out'], 'm_gm_w_in': out['m_gm_w_in'], 'm_gm_b_in': out['m_gm_b_in'], 'm_gm_v_g': out['m_gm_v_g'], 'm_gm_v_b': out['m_gm_v_b'], 'm_gm_w_sp': out['m_gm_w_sp'], 'm_gm_b_sp': out['m_gm_b_sp'], 'm_gm_w_out': out['m_gm_w_out'], 'v_c_ctx': out['v_c_ctx'], 'v_w_mod': out['v_w_mod'], 'v_b_mod': out['v_b_mod'], 'v_norm_g': out['v_norm_g'], 'v_w_ff_in': out['v_w_ff_in'], 'v_w_ff_out': out['v_w_ff_out'], 'v_ar_w_in': out['v_ar_w_in'], 'v_ar_q_g': out['v_ar_q_g'], 'v_ar_k_g': out['v_ar_k_g'], 'v_ar_conv_w': out['v_ar_conv_w'], 'v_ar_conv_b': out['v_ar_conv_b'], 'v_ar_wa': out['v_ar_wa'], 'v_ar_ba': out['v_ar_ba'], 'v_ar_wx': out['v_ar_wx'], 'v_ar_bx': out['v_ar_bx'], 'v_ar_lambda': out['v_ar_lambda'], 'v_ar_w_out': out['v_ar_w_out'], 'v_gm_w_in': out['v_gm_w_in'], 'v_gm_b_in': out['v_gm_b_in'], 'v_gm_v_g': out['v_gm_v_g'], 'v_gm_v_b': out['v_gm_v_b'], 'v_gm_w_sp': out['v_gm_w_sp'], 'v_gm_b_sp': out['v_gm_b_sp'], 'v_gm_w_out': out['v_gm_w_out']}


def _loss(weights, diff, rest, loss_target):
    with _jax.named_scope("forward"):
        args = {**rest, TWIN_DIFF_INPUT: diff, **{k: w.astype(_WEIGHT_DTYPES[k]) for k, w in weights.items()}}
        y = _forward(args)
    with _jax.named_scope("loss_head"):
        err = _jnp.square(y.astype(_jnp.float32) - loss_target)
        return 0.5 * _jnp.sum(_jnp.mean(err, axis=-1)) if err.ndim else 0.5 * err


def _adamw(w, g, m, v):
    m = ADAM_B1 * m + (1.0 - ADAM_B1) * g
    v = ADAM_B2 * v + (1.0 - ADAM_B2) * _jnp.square(g)
    m_hat = m / (1.0 - ADAM_B1 ** ADAM_STEP)
    v_hat = v / (1.0 - ADAM_B2 ** ADAM_STEP)
    delta = -ADAM_LR * (m_hat / (_jnp.sqrt(v_hat) + ADAM_EPS) + ADAM_WD * w)
    return delta, m, v


def reference(x, c, ctx, c_ctx, w_mod, b_mod, norm_g, w_ff_in, w_ff_out, ar_w_in, ar_q_g, ar_k_g, ar_conv_w, ar_conv_b, ar_wa, ar_ba, ar_wx, ar_bx, ar_lambda, ar_w_out, gm_w_in, gm_b_in, gm_v_g, gm_v_b, gm_w_sp, gm_b_sp, gm_w_out, loss_target, m_c_ctx, m_w_mod, m_b_mod, m_norm_g, m_w_ff_in, m_w_ff_out, m_ar_w_in, m_ar_q_g, m_ar_k_g, m_ar_conv_w, m_ar_conv_b, m_ar_wa, m_ar_ba, m_ar_wx, m_ar_bx, m_ar_lambda, m_ar_w_out, m_gm_w_in, m_gm_b_in, m_gm_v_g, m_gm_v_b, m_gm_w_sp, m_gm_b_sp, m_gm_w_out, v_c_ctx, v_w_mod, v_b_mod, v_norm_g, v_w_ff_in, v_w_ff_out, v_ar_w_in, v_ar_q_g, v_ar_k_g, v_ar_conv_w, v_ar_conv_b, v_ar_wa, v_ar_ba, v_ar_wx, v_ar_bx, v_ar_lambda, v_ar_w_out, v_gm_w_in, v_gm_b_in, v_gm_v_g, v_gm_v_b, v_gm_w_sp, v_gm_b_sp, v_gm_w_out):
    given = dict(x=x, c=c, ctx=ctx, c_ctx=c_ctx, w_mod=w_mod, b_mod=b_mod, norm_g=norm_g, w_ff_in=w_ff_in, w_ff_out=w_ff_out, ar_w_in=ar_w_in, ar_q_g=ar_q_g, ar_k_g=ar_k_g, ar_conv_w=ar_conv_w, ar_conv_b=ar_conv_b, ar_wa=ar_wa, ar_ba=ar_ba, ar_wx=ar_wx, ar_bx=ar_bx, ar_lambda=ar_lambda, ar_w_out=ar_w_out, gm_w_in=gm_w_in, gm_b_in=gm_b_in, gm_v_g=gm_v_g, gm_v_b=gm_v_b, gm_w_sp=gm_w_sp, gm_b_sp=gm_b_sp, gm_w_out=gm_w_out, loss_target=loss_target, m_c_ctx=m_c_ctx, m_w_mod=m_w_mod, m_b_mod=m_b_mod, m_norm_g=m_norm_g, m_w_ff_in=m_w_ff_in, m_w_ff_out=m_w_ff_out, m_ar_w_in=m_ar_w_in, m_ar_q_g=m_ar_q_g, m_ar_k_g=m_ar_k_g, m_ar_conv_w=m_ar_conv_w, m_ar_conv_b=m_ar_conv_b, m_ar_wa=m_ar_wa, m_ar_ba=m_ar_ba, m_ar_wx=m_ar_wx, m_ar_bx=m_ar_bx, m_ar_lambda=m_ar_lambda, m_ar_w_out=m_ar_w_out, m_gm_w_in=m_gm_w_in, m_gm_b_in=m_gm_b_in, m_gm_v_g=m_gm_v_g, m_gm_v_b=m_gm_v_b, m_gm_w_sp=m_gm_w_sp, m_gm_b_sp=m_gm_b_sp, m_gm_w_out=m_gm_w_out, v_c_ctx=v_c_ctx, v_w_mod=v_w_mod, v_b_mod=v_b_mod, v_norm_g=v_norm_g, v_w_ff_in=v_w_ff_in, v_w_ff_out=v_w_ff_out, v_ar_w_in=v_ar_w_in, v_ar_q_g=v_ar_q_g, v_ar_k_g=v_ar_k_g, v_ar_conv_w=v_ar_conv_w, v_ar_conv_b=v_ar_conv_b, v_ar_wa=v_ar_wa, v_ar_ba=v_ar_ba, v_ar_wx=v_ar_wx, v_ar_bx=v_ar_bx, v_ar_lambda=v_ar_lambda, v_ar_w_out=v_ar_w_out, v_gm_w_in=v_gm_w_in, v_gm_b_in=v_gm_b_in, v_gm_v_g=v_gm_v_g, v_gm_v_b=v_gm_v_b, v_gm_w_sp=v_gm_w_sp, v_gm_b_sp=v_gm_b_sp, v_gm_w_out=v_gm_w_out)
    weights = {n: given[n] for n in TWIN_WEIGHTS}
    shared = {n: given[n] for n in SHARED_INPUTS}
    per_example = {n: given[n] for n in ['x', 'c', 'ctx']}
    grad_fn = _jax.value_and_grad(_loss, argnums=(0, 1))

    def one_microbatch(ex, loss_target):
        ex = dict(ex)
        diff = ex.pop(TWIN_DIFF_INPUT)
        return grad_fn(weights, diff, {**shared, **ex}, loss_target)

    if N_MICROBATCH == 1:
        loss, (grad_w, grad_x) = one_microbatch(per_example, given["loss_target"])
    else:
        def body(carry, xs):
            loss_sum, grad_sum = carry
            l_k, (gw_k, gx_k) = one_microbatch(xs[0], xs[1])
            with _jax.named_scope("update"):
                return (loss_sum + l_k, _jax.tree.map(_jnp.add, grad_sum, gw_k)), gx_k

        init = (_jnp.zeros((), _jnp.float32), _jax.tree.map(_jnp.zeros_like, weights))
        (loss, grad_w), grad_x = _jax.lax.scan(body, init, (per_example, given["loss_target"]))
    with _jax.named_scope("update"):
        delta_w, new_m, new_v = {}, {}, {}
        for n in TWIN_WEIGHTS:
            delta_w[n], new_m[n], new_v[n] = _adamw(weights[n], grad_w[n], given["m_" + n], given["v_" + n])
    return (loss, grad_x, *[grad_w[n] for n in TWIN_WEIGHTS], *[delta_w[n] for n in TWIN_WEIGHTS],
            *[new_m[n] for n in TWIN_WEIGHTS], *[new_v[n] for n in TWIN_WEIGHTS])
```

```python
import functools

import numpy as np
import jax
import jax.numpy as jnp
from jax import lax
from jax.experimental import pallas as pl
from jax.experimental.pallas import tpu as pltpu

F32 = jnp.float32
BF16 = jnp.bfloat16
HIGHEST = lax.Precision.HIGHEST

GRID_W = 64
N_HEADS = 8
N_KV_HEADS = 2
HEAD_DIM = 128
ROPE_THETA = 10000.0
RNN_BLOCK_W = 128
CONV_W = 4
RG_C = 8.0
GM_GROUP_W = 128
CHUNK = 128
EPS = 1e-6
ADAM_LR = 0.001
ADAM_B1 = 0.9
ADAM_B2 = 0.999
ADAM_EPS = 1e-08
ADAM_WD = 0.01
ADAM_STEP = 10

N_DEV = 8
MOD_ROWS = 16
LANES = 128
ROW_TILE = 256
ROW_TILE_BWD = 128
SCAN_BLOCK = 256
VMEM_LIMIT = 56 * 1024 * 1024
MM_TILE_M = 512
MM_TILE_N = 1024
MM_TILE_K = 512

WEIGHTS = ['c_ctx', 'w_mod', 'b_mod', 'norm_g', 'w_ff_in', 'w_ff_out', 'ar_w_in', 'ar_q_g', 'ar_k_g', 'ar_conv_w',
           'ar_conv_b', 'ar_wa', 'ar_ba', 'ar_wx', 'ar_bx', 'ar_lambda', 'ar_w_out', 'gm_w_in', 'gm_b_in', 'gm_v_g',
           'gm_v_b', 'gm_w_sp', 'gm_b_sp', 'gm_w_out']


def _sds(shape, dtype):
    return jax.ShapeDtypeStruct(tuple(shape), dtype)


def _tile(dim, pref, align):
    t = (min(pref, dim) // align) * align
    while t >= align:
        if dim % t == 0:
            return t
        t -= align
    return dim


def _params(sem):
    return pltpu.CompilerParams(dimension_semantics=sem, vmem_limit_bytes=VMEM_LIMIT)


def _rms(x, g):
    return x * lax.rsqrt(jnp.mean(x * x, axis=-1, keepdims=True) + EPS) * g


def _gelu(x):
    return 0.5 * x * (1.0 + jnp.tanh(0.7978845608028654 * (x + 0.044715 * (x * x * x))))


def _sigmoid(x):
    return 0.5 * (jnp.tanh(0.5 * x) + 1.0)


def _log1p_pos(u):
    small = u < 1e-3
    us = jnp.where(small, u, 0.0)
    return jnp.where(small, us * (1.0 - us * (0.5 - us * (1.0 / 3.0))), jnp.log(1.0 + u))


def _softplus(x):
    return jnp.maximum(x, 0.0) + _log1p_pos(jnp.exp(-jnp.abs(x)))


def _expm1(x):
    small = jnp.abs(x) < 0.3
    xs = jnp.where(small, x, 0.0)
    poly = xs * (1.0 + xs * (1.0 / 2 + xs * (1.0 / 6 + xs * (1.0 / 24 + xs * (1.0 / 120 + xs * (1.0 / 720 + xs * (1.0 / 5040)))))))
    return jnp.where(small, poly, jnp.exp(x) - 1.0)


def _f_pre_ctx(pid, x, g, shc, scc, shl, scl, *, n_ctx_tiles):
    is_ctx = pid < n_ctx_tiles
    sh = jnp.where(is_ctx, shc, shl)
    sc = jnp.where(is_ctx, scc, scl)
    return (_rms(x, g) * (1.0 + sc) + sh,)


def _f_pre(pid, x, g, sh, sc):
    return (_rms(x, g) * (1.0 + sc) + sh,)


def _f_mid(pid, x, o, g1, gate, g2, sh, sc):
    x1 = x + gate * _rms(o, g1)
    return (x1, _rms(x1, g2) * (1.0 + sc) + sh)


def _f_post(pid, x, o, g, gate):
    return (x + gate * _rms(o, g),)


def _f_qkv(pid, pq, pkv, cos, sin, q_g, k_g, perm, *, nh, nkv):
    hd = HEAD_DIM

    def norm_rope(xh, g):
        y = _rms(xh, g)
        return y * cos + jnp.dot(y, perm, precision=HIGHEST, preferred_element_type=F32) * sin

    qs = [norm_rope(pq[:, h * hd:(h + 1) * hd], q_g) for h in range(nh)]
    ks = [norm_rope(pkv[:, h * hd:(h + 1) * hd], k_g) for h in range(nkv)]
    return (jnp.concatenate(qs, axis=1), jnp.concatenate(ks + [pkv[:, nkv * hd:]], axis=1))


def _f_gates(pid, x, wa, ba, wx, bx, lam, *, nb):
    w = RNN_BLOCK_W
    outs = []
    for d in range(2):
        ra, ri = [], []
        for n in range(nb):
            xn = x[:, n * w:(n + 1) * w].astype(BF16)
            ra.append(jnp.dot(xn, wa[d * nb + n].astype(BF16), preferred_element_type=F32))
            ri.append(jnp.dot(xn, wx[d * nb + n].astype(BF16), preferred_element_type=F32))
        r = _sigmoid(jnp.concatenate(ra, axis=1) + ba[d:d + 1])
        i = _sigmoid(jnp.concatenate(ri, axis=1) + bx[d:d + 1])
        log_a = -RG_C * r * _softplus(-lam[d:d + 1])
        outs.append(jnp.exp(log_a))
        outs.append(jnp.sqrt(-_expm1(2.0 * log_a)) * (i * x))
    return tuple(outs)


def _f_rnnout(pid, hf, hb, gr):
    return ((hf + hb) * _gelu(gr),)


def _f_gm(pid, zu, zv, bu, bv, v_g, v_b, w_sp, bsp_t, expand, *, n_chunks, groups):
    u = _gelu(zu + bu)
    v = _gelu(zv + bv)
    mu = jnp.mean(v, axis=-1, keepdims=True)
    vc = v - mu
    v = vc * lax.rsqrt(jnp.mean(vc * vc, axis=-1, keepdims=True) + EPS) * v_g + v_b
    bias = jnp.dot(bsp_t, expand, precision=HIGHEST, preferred_element_type=F32)
    outs = []
    for c in range(n_chunks):
        vch = v[c * CHUNK:(c + 1) * CHUNK]
        cols = [jnp.dot(w_sp[g].astype(BF16), vch[:, g * GM_GROUP_W:(g + 1) * GM_GROUP_W].astype(BF16),
                        preferred_element_type=F32) for g in range(groups)]
        outs.append(u[c * CHUNK:(c + 1) * CHUNK] * (jnp.concatenate(cols, axis=1) + bias))
    return (jnp.concatenate(outs, axis=0),)


def _f_silu_mul(pid, c, d):
    return (d * jax.grad(lambda z: jnp.sum(z * _sigmoid(z)))(c),)


def _bwd_of(fn, n_tiled, n_ct, want):
    def bwd(pid, *args):
        tiles = [t.astype(F32) for t in args[:n_tiled]]
        cts = args[n_tiled:n_tiled + n_ct]
        fulls = list(args[n_tiled + n_ct:])
        outs, vjp = jax.vjp(lambda *a: fn(pid, *a), *tiles, *fulls)
        grads = vjp(tuple(ct.astype(o.dtype) for ct, o in zip(cts, outs)))
        return tuple(grads[i] for i in want)
    return bwd


def _rowwise(name, fn, rows, tm, tiled, full, outs, accs=()):
    n_t, n_f, n_o, n_a = len(tiled), len(full), len(outs), len(accs)

    def body(*refs):
        pid = pl.program_id(0)
        vals = [r[...] for r in refs[:n_t + n_f]]
        res = fn(pid, *vals)
        o_refs = refs[n_t + n_f:n_t + n_f + n_o]
        a_refs = refs[n_t + n_f + n_o:]
        for r, v in zip(o_refs, res[:n_o]):
            r[...] = v.astype(r.dtype)
        if n_a:
            @pl.when(pid == 0)
            def _():
                for r in a_refs:
                    r[...] = jnp.zeros_like(r)
            for r, v in zip(a_refs, res[n_o:]):
                r[...] += v.astype(F32)

    assert all(ro % tm == 0 for (_, ro, _, _) in tiled)
    in_specs = [pl.BlockSpec((tm, w), lambda i, ro=ro // tm, cb=cb: (i + ro, cb)) for (_, ro, cb, w) in tiled]
    in_specs += [pl.BlockSpec(a.shape, lambda i, nd=a.ndim: (0,) * nd) for a in full]
    out_shape = [_sds((rows, w), dt) for (w, dt) in outs] + [_sds(s, F32) for s in accs]
    out_specs = [pl.BlockSpec((tm, w), lambda i: (i, 0)) for (w, _) in outs]
    out_specs += [pl.BlockSpec(tuple(s), lambda i, nd=len(s): (0,) * nd) for s in accs]
    return pl.pallas_call(body, grid=(rows // tm,), in_specs=in_specs, out_specs=out_specs, out_shape=out_shape, name=name,
                          compiler_params=_params(("arbitrary",)))(*[t[0] for t in tiled], *full)


def _t(a, row_off=0, col_blk=0, width=None):
    return (a, row_off, col_blk, a.shape[1] if width is None else width)


def _matmul(name, a, b, *, ta=False, tb=False, outs=((F32,)), epilogue=None, extras=(), tm=None, tn=None, tk=None):
    m, k = (a.shape[1], a.shape[0]) if ta else a.shape
    n = b.shape[0] if tb else b.shape[1]
    tm = _tile(m, tm or MM_TILE_M, 128 if ta else 16)
    tn = _tile(n, tn or MM_TILE_N, 128)
    tk = _tile(k, tk or MM_TILE_K, 128 if not ta else 16)
    nk = k // tk
    n_e, n_o = len(extras), len(outs)
    dims = (((0 if ta else 1,), (1 if tb else 0,)), ((), ()))

    def body(*refs):
        a_ref, b_ref = refs[:2]
        e_refs = refs[2:2 + n_e]
        o_refs = refs[2 + n_e:2 + n_e + n_o]
        acc = refs[-1]
        kk = pl.program_id(2)

        @pl.when(kk == 0)
        def _():
            acc[...] = jnp.zeros_like(acc)

        acc[...] += lax.dot_general(a_ref[...].astype(BF16), b_ref[...].astype(BF16), dims, preferred_element_type=F32)

        @pl.when(kk == nk - 1)
        def _():
            res = (acc[...],) if epilogue is None else epilogue(acc[...], *[e[...] for e in e_refs])
            for r, v in zip(o_refs, res):
                r[...] = v.astype(r.dtype)

    a_spec = pl.BlockSpec((tk, tm), lambda i, j, kk: (kk, i)) if ta else pl.BlockSpec((tm, tk), lambda i, j, kk: (i, kk))
    b_spec = pl.BlockSpec((tn, tk), lambda i, j, kk: (j, kk)) if tb else pl.BlockSpec((tk, tn), lambda i, j, kk: (kk, j))
    mn_spec = pl.BlockSpec((tm, tn), lambda i, j, kk: (i, j))
    res = pl.pallas_call(body, grid=(m // tm, n // tn, nk), in_specs=[a_spec, b_spec] + [mn_spec] * n_e,
                         out_specs=[mn_spec] * n_o, out_shape=[_sds((m, n), dt) for dt in outs],
                         scratch_shapes=[pltpu.VMEM((tm, tn), F32)], name=name,
                         compiler_params=_params(("parallel", "parallel", "arbitrary")))(a, b, *extras)
    return res[0] if n_o == 1 else res


def _attn_fwd(q, kv, n_ctx_tiles, tq):
    t_all = q.shape[0]
    s_len = t_all - n_ctx_tiles * tq
    hd, groups = HEAD_DIM, N_HEADS // N_KV_HEADS
    scale = HEAD_DIM ** -0.5

    def body(q_ref, k_ref, v_ref, o_ref):
        s = lax.dot_general(q_ref[...], k_ref[...], (((1,), (1,)), ((), ())), preferred_element_type=F32) * scale
        p = jnp.exp(s - jnp.max(s, axis=-1, keepdims=True))
        l = jnp.sum(p, axis=-1, keepdims=True)
        o = jnp.dot(p.astype(BF16), v_ref[...], preferred_element_type=F32) / l
        o_ref[...] = o.astype(o_ref.dtype)

    return pl.pallas_call(
        body, grid=(N_KV_HEADS, groups, s_len // tq),
        in_specs=[pl.BlockSpec((tq, hd), lambda kh, g, i: (i + n_ctx_tiles, kh * groups + g)),
                  pl.BlockSpec((t_all, hd), lambda kh, g, i: (0, kh)),
                  pl.BlockSpec((t_all, hd), lambda kh, g, i: (0, N_KV_HEADS + kh))],
        out_specs=pl.BlockSpec((tq, hd), lambda kh, g, i: (i, kh * groups + g)),
        out_shape=_sds((s_len, N_HEADS * hd), BF16), name="attn_fwd",
        compiler_params=_params(("parallel", "parallel", "parallel")))(q, kv, kv)


def _attn_bwd(q, kv, d_ar, n_ctx_tiles, tq):
    t_all = q.shape[0]
    s_len = t_all - n_ctx_tiles * tq
    hd, groups = HEAD_DIM, N_HEADS // N_KV_HEADS
    scale = HEAD_DIM ** -0.5

    def body(q_ref, k_ref, v_ref, do_ref, dq_ref, dkt_ref, dvt_ref):
        first = jnp.logical_and(pl.program_id(1) == 0, pl.program_id(2) == 0)

        @pl.when(first)
        def _():
            dkt_ref[...] = jnp.zeros_like(dkt_ref)
            dvt_ref[...] = jnp.zeros_like(dvt_ref)

        qv, kk, vv = q_ref[...], k_ref[...], v_ref[...]
        do = do_ref[...].astype(BF16)
        s = lax.dot_general(qv, kk, (((1,), (1,)), ((), ())), preferred_element_type=F32) * scale
        p = jnp.exp(s - jnp.max(s, axis=-1, keepdims=True))
        p = p / jnp.sum(p, axis=-1, keepdims=True)
        dp = lax.dot_general(do, vv, (((1,), (1,)), ((), ())), preferred_element_type=F32)
        ds = (p * (dp - jnp.sum(p * dp, axis=-1, keepdims=True)) * scale).astype(BF16)
        dq_ref[...] = jnp.dot(ds, kk, preferred_element_type=F32)
        dkt_ref[...] += jnp.dot(qv.T, ds, preferred_element_type=F32)
        dvt_ref[...] += jnp.dot(do.T, p.astype(BF16), preferred_element_type=F32)

    return pl.pallas_call(
        body, grid=(N_KV_HEADS, groups, s_len // tq),
        in_specs=[pl.BlockSpec((tq, hd), lambda kh, g, i: (i + n_ctx_tiles, kh * groups + g)),
                  pl.BlockSpec((t_all, hd), lambda kh, g, i: (0, kh)),
                  pl.BlockSpec((t_all, hd), lambda kh, g, i: (0, N_KV_HEADS + kh)),
                  pl.BlockSpec((tq, hd), lambda kh, g, i: (i, kh * groups + g))],
        out_specs=[pl.BlockSpec((tq, hd), lambda kh, g, i: (i, kh * groups + g)),
                   pl.BlockSpec((hd, t_all), lambda kh, g, i: (kh, 0)),
                   pl.BlockSpec((hd, t_all), lambda kh, g, i: (kh, 0))],
        out_shape=[_sds((s_len, N_HEADS * hd), F32), _sds((N_KV_HEADS * hd, t_all), F32),
                   _sds((N_KV_HEADS * hd, t_all), F32)],
        name="attn_bwd", compiler_params=_params(("arbitrary", "arbitrary", "arbitrary")))(q, kv, kv, d_ar)


def _scan_order(nb, nc, reverse):
    if not reverse:
        return lambda i: i
    return lambda i: jnp.where(i < nc, nc - 1 - i, nb - 1 - (i - nc))


def _scan_fwd(name, a, b, nc, reverse):
    t_all, r, l = a.shape
    tb = SCAN_BLOCK
    nb = t_all // tb
    order = _scan_order(nb, nc, reverse)

    def body(a_ref, b_ref, h_ref, hp_ref, carry):
        @pl.when(pl.program_id(0) == 0)
        def _():
            carry[...] = jnp.zeros_like(carry)

        def step(s, h):
            t = tb - 1 - s if reverse else s
            hp_ref[t] = h
            h = a_ref[t] * h + b_ref[t]
            h_ref[t] = h
            return h

        carry[...] = lax.fori_loop(0, tb, step, carry[...], unroll=8)

    spec = pl.BlockSpec((tb, r, l), lambda i: (order(i), 0, 0))
    return pl.pallas_call(body, grid=(nb,), in_specs=[spec, spec], out_specs=[spec, spec],
                          out_shape=[_sds(a.shape, F32)] * 2, scratch_shapes=[pltpu.VMEM((r, l), F32)], name=name,
                          compiler_params=_params(("arbitrary",)))(a, b)


def _scan_bwd(name, a, dh, hp, nc, reverse):
    t_all, r, l = a.shape
    tb = SCAN_BLOCK
    nb = t_all // tb
    primal = _scan_order(nb, nc, reverse)

    def order(i):
        return primal(nb - 1 - i)

    def body(a_ref, dh_ref, hp_ref, da_ref, db_ref, carry):
        @pl.when(pl.program_id(0) == 0)
        def _():
            carry[...] = jnp.zeros_like(carry)

        def step(s, cr):
            t = s if reverse else tb - 1 - s
            lam = dh_ref[t] + cr
            db_ref[t] = lam
            da_ref[t] = lam * hp_ref[t]
            return a_ref[t] * lam

        carry[...] = lax.fori_loop(0, tb, step, carry[...], unroll=8)

    spec = pl.BlockSpec((tb, r, l), lambda i: (order(i), 0, 0))
    return pl.pallas_call(body, grid=(nb,), in_specs=[spec] * 3, out_specs=[spec, spec],
                          out_shape=[_sds(a.shape, F32)] * 2, scratch_shapes=[pltpu.VMEM((r, l), F32)], name=name,
                          compiler_params=_params(("arbitrary",)))(a, dh, hp)


def _f_conv(pid, x0, x1, x2, x3, w, b):
    return (b + x0 * w[0:1] + x1 * w[1:2] + x2 * w[2:3] + x3 * w[3:4],)


def _f_conv_bwd(pid, x0, x1, x2, x3, d0, d1, d2, d3, w):
    dx = d0 * w[0:1] + d1 * w[1:2] + d2 * w[2:3] + d3 * w[3:4]
    dw = jnp.concatenate([jnp.sum(d2 * xj, axis=0, keepdims=True) for xj in (x0, x1, x2, x3)], axis=0)
    return (dx, dw, jnp.sum(d2, axis=0, keepdims=True))


def _shift_rows(a, k, n_ctx):
    def sh(z):
        if k == 0:
            return z
        pad = jnp.zeros((abs(k), z.shape[1]), z.dtype)
        return jnp.concatenate([z[k:], pad], axis=0) if k > 0 else jnp.concatenate([pad, z[:k]], axis=0)
    return jnp.concatenate([sh(a[:n_ctx]), sh(a[n_ctx:])], axis=0)


def _mesh_pos():
    return lax.axis_index("x"), lax.axis_index("y"), lax.axis_index("c")


def _exchange(name, arrays, all_to_all):
    n = len(arrays)

    def body(*refs):
        ins, outs = refs[:n], refs[n:2 * n]
        send_sems, recv_sems, local_sems = refs[2 * n:]
        x, y, c = _mesh_pos()
        me = 4 * x + 2 * y + c
        copies = []
        for ai in range(n):
            src = ins[ai].at[me] if all_to_all else ins[ai]
            local = pltpu.make_async_copy(src, outs[ai].at[me], local_sems.at[ai])
            local.start()
            copies.append(local)
            for k in range(1, N_DEV):
                peer = (me + k) % N_DEV
                src = ins[ai].at[peer] if all_to_all else ins[ai]
                cp = pltpu.make_async_remote_copy(
                    src_ref=src, dst_ref=outs[ai].at[me], send_sem=send_sems.at[ai * (N_DEV - 1) + k - 1],
                    recv_sem=recv_sems.at[ai * (N_DEV - 1) + k - 1], device_id=(peer // 4, (peer // 2) % 2, peer % 2),
                    device_id_type=pl.DeviceIdType.MESH)
                cp.start()
        for ai in range(n):
            for k in range(1, N_DEV):
                sender = (me + N_DEV - k) % N_DEV
                src = ins[ai].at[sender] if all_to_all else ins[ai]
                cp = pltpu.make_async_remote_copy(
                    src_ref=src, dst_ref=outs[ai].at[sender], send_sem=send_sems.at[ai * (N_DEV - 1) + k - 1],
                    recv_sem=recv_sems.at[ai * (N_DEV - 1) + k - 1], device_id=(sender // 4, (sender // 2) % 2, sender % 2),
                    device_id_type=pl.DeviceIdType.MESH)
                cp.wait_recv()
                cp.wait_send()
        for local in copies:
            local.wait()

    any_spec = pl.BlockSpec(memory_space=pl.ANY)
    out_shape = [_sds(a.shape if all_to_all else (N_DEV,) + a.shape, a.dtype) for a in arrays]
    return pl.pallas_call(
        body, in_specs=[any_spec] * n, out_specs=[any_spec] * n, out_shape=out_shape, name=name,
        scratch_shapes=[pltpu.SemaphoreType.DMA((n * (N_DEV - 1),)), pltpu.SemaphoreType.DMA((n * (N_DEV - 1),)),
                        pltpu.SemaphoreType.DMA((n,))],
        compiler_params=pltpu.CompilerParams(has_side_effects=True))(*arrays)


def _sum_lead(name, a):
    n, r, c = a.shape
    tr = _tile(r, 512, 8)

    def body(a_ref, o_ref):
        acc = a_ref[0]
        for j in range(1, n):
            acc = acc + a_ref[j]
        o_ref[...] = acc

    return pl.pallas_call(body, grid=(r // tr,), in_specs=[pl.BlockSpec((n, tr, c), lambda i: (0, i, 0))],
                          out_specs=pl.BlockSpec((tr, c), lambda i: (i, 0)), out_shape=_sds((r, c), F32), name=name,
                          compiler_params=_params(("parallel",)))(a)


def _adam_math(w, g, m, v):
    m = ADAM_B1 * m + (1.0 - ADAM_B1) * g
    v = ADAM_B2 * v + (1.0 - ADAM_B2) * (g * g)
    m_hat = m / (1.0 - ADAM_B1 ** ADAM_STEP)
    v_hat = v / (1.0 - ADAM_B2 ** ADAM_STEP)
    delta = -ADAM_LR * (m_hat / (jnp.sqrt(v_hat) + ADAM_EPS) + ADAM_WD * w)
    return delta, m, v


def _adam_recv(name, recv, w, m, v):
    n, r, c = recv.shape
    tr = _tile(r, 256, 16)

    def body(g_ref, w_ref, m_ref, v_ref, go_ref, d_ref, mo_ref, vo_ref):
        g = g_ref[0].astype(F32)
        for j in range(1, n):
            g = g + g_ref[j].astype(F32)
        delta, m2, v2 = _adam_math(w_ref[...], g, m_ref[...], v_ref[...])
        go_ref[...] = g
        d_ref[...] = delta
        mo_ref[...] = m2
        vo_ref[...] = v2

    spec = pl.BlockSpec((tr, c), lambda i: (i, 0))
    return pl.pallas_call(body, grid=(r // tr,), in_specs=[pl.BlockSpec((n, tr, c), lambda i: (0, i, 0))] + [spec] * 3,
                          out_specs=[spec] * 4, out_shape=[_sds((r, c), F32)] * 4, name=name,
                          compiler_params=_params(("parallel",)))(recv, w, m, v)


def _adam_f32(name, g, w, m, v):
    r, c = g.shape
    tr = _tile(r, 512, 8)

    def body(g_ref, w_ref, m_ref, v_ref, d_ref, mo_ref, vo_ref):
        delta, m2, v2 = _adam_math(w_ref[...], g_ref[...], m_ref[...], v_ref[...])
        d_ref[...] = delta
        mo_ref[...] = m2
        vo_ref[...] = v2

    spec = pl.BlockSpec((tr, c), lambda i: (i, 0))
    return pl.pallas_call(body, grid=(r // tr,), in_specs=[spec] * 4, out_specs=[spec] * 3,
                          out_shape=[_sds((r, c), F32)] * 3, name=name, compiler_params=_params(("parallel",)))(g, w, m, v)


def _mod_fwd(c16, w_mod, b_loc):
    nl, d, n6 = w_mod.shape
    tn = _tile(n6, 512, 128)

    def body(c_ref, w_ref, b_ref, o_ref):
        cv = c_ref[...]
        s = cv * _sigmoid(cv)
        o_ref[0] = jnp.dot(s, w_ref[0], precision=HIGHEST, preferred_element_type=F32) + b_ref[0]

    return pl.pallas_call(
        body, grid=(nl, n6 // tn),
        in_specs=[pl.BlockSpec((MOD_ROWS, d), lambda i, j: (0, 0)), pl.BlockSpec((1, d, tn), lambda i, j: (i, 0, j)),
                  pl.BlockSpec((1, 1, tn), lambda i, j: (i, 0, j))],
        out_specs=pl.BlockSpec((1, MOD_ROWS, tn), lambda i, j: (i, 0, j)), out_shape=_sds((nl, MOD_ROWS, n6), F32),
        name="mod_fwd", compiler_params=_params(("parallel", "parallel")))(c16, w_mod, b_loc)


def _mod_bwd(c16, w_mod, dmod_loc):
    nl, d, n6 = w_mod.shape
    tn = _tile(n6, 512, 128)

    def body(c_ref, w_ref, dm_ref, dw_ref, ds_ref):
        @pl.when(jnp.logical_and(pl.program_id(0) == 0, pl.program_id(1) == 0))
        def _():
            ds_ref[...] = jnp.zeros_like(ds_ref)

        cv = c_ref[...]
        s = cv * _sigmoid(cv)
        dm = dm_ref[0]
        dw_ref[0] = lax.dot_general(s, dm, (((0,), (0,)), ((), ())), precision=HIGHEST, preferred_element_type=F32)
        ds_ref[...] += lax.dot_general(dm, w_ref[0], (((1,), (1,)), ((), ())), precision=HIGHEST, preferred_element_type=F32)

    return pl.pallas_call(
        body, grid=(nl, n6 // tn),
        in_specs=[pl.BlockSpec((MOD_ROWS, d), lambda i, j: (0, 0)), pl.BlockSpec((1, d, tn), lambda i, j: (i, 0, j)),
                  pl.BlockSpec((1, MOD_ROWS, tn), lambda i, j: (i, 0, j))],
        out_specs=[pl.BlockSpec((1, d, tn), lambda i, j: (i, 0, j)), pl.BlockSpec((MOD_ROWS, d), lambda i, j: (0, 0))],
        out_shape=[_sds((nl, d, n6), F32), _sds((MOD_ROWS, d), F32)], name="mod_bwd",
        compiler_params=_params(("arbitrary", "arbitrary")))(c16, w_mod, dmod_loc)


def _pack(parts):
    flat = [p.reshape(-1).astype(F32) for p in parts]
    offs = np.cumsum([0] + [f.shape[0] for f in flat])
    total = int(offs[-1])
    padded = -(-total // (8 * LANES)) * (8 * LANES)
    slab = jnp.concatenate(flat + [jnp.zeros((padded - total,), F32)])
    return slab.reshape(padded // LANES, LANES), [int(o) for o in offs]


def _unshard_cols(seg, lead):
    n = seg.shape[1] // int(np.prod(lead)) if lead else seg.shape[1]
    a = seg.reshape((N_DEV,) + tuple(lead) + (n,))
    a = jnp.moveaxis(a, 0, len(lead))
    return a.reshape(tuple(lead) + (N_DEV * n,))


def _my_cols(a, me, n):
    start = (0,) * (a.ndim - 1) + (me * n,)
    return lax.dynamic_slice(a, start, a.shape[:-1] + (n,))


def _rope_tables(seq, n_ctx):
    rows = seq // GRID_W
    r_idx, c_idx = jnp.meshgrid(jnp.arange(rows), jnp.arange(GRID_W), indexing='ij')
    r_idx = r_idx.reshape(-1).astype(F32)
    c_idx = c_idx.reshape(-1).astype(F32)
    pairs = HEAD_DIM // 4
    freqs = ROPE_THETA ** (-jnp.arange(pairs, dtype=F32) / pairs)
    ang_r, ang_c = r_idx[:, None] * freqs, c_idx[:, None] * freqs
    cos = jnp.concatenate([jnp.cos(ang_r)] * 2 + [jnp.cos(ang_c)] * 2, axis=1)
    sin = jnp.concatenate([-jnp.sin(ang_r), jnp.sin(ang_r), -jnp.sin(ang_c), jnp.sin(ang_c)], axis=1)
    cos = jnp.concatenate([jnp.ones((n_ctx, HEAD_DIM), F32), cos], axis=0)
    sin = jnp.concatenate([jnp.zeros((n_ctx, HEAD_DIM), F32), sin], axis=0)
    lane = np.arange(HEAD_DIM)
    partner = np.where(lane % (2 * pairs) < pairs, lane + pairs, lane - pairs)
    perm = np.zeros((HEAD_DIM, HEAD_DIM), np.float32)
    perm[partner, lane] = 1.0
    return cos, sin, jnp.asarray(perm)


def kernel(x, c, ctx, c_ctx, w_mod, b_mod, norm_g, w_ff_in, w_ff_out, ar_w_in, ar_q_g, ar_k_g, ar_conv_w, ar_conv_b, ar_wa, ar_ba, ar_wx, ar_bx, ar_lambda, ar_w_out, gm_w_in, gm_b_in, gm_v_g, gm_v_b, gm_w_sp, gm_b_sp, gm_w_out, loss_target, m_c_ctx, m_w_mod, m_b_mod, m_norm_g, m_w_ff_in, m_w_ff_out, m_ar_w_in, m_ar_q_g, m_ar_k_g, m_ar_conv_w, m_ar_conv_b, m_ar_wa, m_ar_ba, m_ar_wx, m_ar_bx, m_ar_lambda, m_ar_w_out, m_gm_w_in, m_gm_b_in, m_gm_v_g, m_gm_v_b, m_gm_w_sp, m_gm_b_sp, m_gm_w_out, v_c_ctx, v_w_mod, v_b_mod, v_norm_g, v_w_ff_in, v_w_ff_out, v_ar_w_in, v_ar_q_g, v_ar_k_g, v_ar_conv_w, v_ar_conv_b, v_ar_wa, v_ar_ba, v_ar_wx, v_ar_bx, v_ar_lambda, v_ar_w_out, v_gm_w_in, v_gm_b_in, v_gm_v_g, v_gm_v_b, v_gm_w_sp, v_gm_b_sp, v_gm_w_out):
    given = dict(locals())
    wts = {n: given[n] for n in WEIGHTS}
    mom1 = {n: given["m_" + n] for n in WEIGHTS}
    mom2 = {n: given["v_" + n] for n in WEIGHTS}

    xi, yi, ci = _mesh_pos()
    me = 4 * xi + 2 * yi + ci

    seq, d = x.shape[1], x.shape[2]
    n_ctx = ctx.shape[1]
    t_all = n_ctx + seq
    n_layers = w_mod.shape[0]
    assert n_layers == 2 and ar_w_in.shape[0] == 1 and gm_w_in.shape[0] == 1
    d_ff = w_ff_in.shape[2] * N_DEV
    attn_w, kv_w = N_HEADS * HEAD_DIM, N_KV_HEADS * HEAD_DIM
    rnn_blocks = ar_wa.shape[2]
    d_rnn = rnn_blocks * RNN_BLOCK_W
    gm_groups = gm_w_sp.shape[1]
    d_gm = gm_groups * GM_GROUP_W
    ar_in = ar_w_in.shape[2] * N_DEV
    n6 = w_mod.shape[2]
    tm, tmb = ROW_TILE, ROW_TILE_BWD
    assert attn_w == d_rnn and ar_in == 3 * attn_w + 2 * kv_w and (3 * attn_w) % (2 * kv_w) == 0
    assert n_ctx % tm == 0 and seq % tm == 0 and n_ctx % SCAN_BLOCK == 0 and seq % SCAN_BLOCK == 0 and tm % CHUNK == 0
    nct, nctb = n_ctx // tm, n_ctx // tmb
    kv_blk = (3 * attn_w) // (2 * kv_w)
    lr = d_rnn // LANES

    x2, ctx2, tgt = x[0], ctx[0], loss_target[0]

    g_ff_in, g_ff_out, g_ar_in, g_ar_out, g_gm_in, g_gm_out = _exchange(
        "gather_weights",
        [w_ff_in.astype(BF16), w_ff_out.astype(BF16), ar_w_in[0].astype(BF16), ar_w_out[0].astype(BF16),
         gm_w_in[0].astype(BF16), gm_w_out[0].astype(BF16)], False)

    def cols_full(g):
        return jnp.moveaxis(g, 0, 1).reshape(g.shape[1], N_DEV * g.shape[2])

    w1 = [cols_full(g_ff_in[:, i]) for i in range(n_layers)]
    w2 = [g_ff_out[:, i].reshape(d_ff, d) for i in range(n_layers)]
    w_in = cols_full(g_ar_in)
    split = [attn_w, attn_w + 2 * kv_w, attn_w + 2 * kv_w + d_rnn]
    w_in = jnp.concatenate([w_in[:, :split[0]], w_in[:, split[1]:], w_in[:, split[0]:split[1]]], axis=1)
    w_out = g_ar_out.reshape(attn_w + d_rnn, d)
    w_gi = cols_full(g_gm_in)
    w_go = g_gm_out.reshape(d_gm, d)

    small0, off0 = _pack([c[0], norm_g, ar_conv_w[0], ar_ba[0], ar_bx[0], ar_lambda[0], gm_b_in[0], gm_v_g[0], gm_v_b[0]])
    (gs0,) = _exchange("gather_small", [small0], False)
    gs0 = gs0.reshape(N_DEV, -1)

    def seg0(k):
        return gs0[:, off0[k]:off0[k + 1]]

    c_all = seg0(0)
    norm_full = _unshard_cols(seg0(1), (n_layers, 4))
    conv_w = _unshard_cols(seg0(2), (CONV_W,))
    ba, bx, lam = (_unshard_cols(seg0(k), (2,)) for k in (3, 4, 5))
    gm_b_in_f = seg0(6).reshape(1, 2 * d_gm)
    gm_vg, gm_vb = seg0(7).reshape(1, d_gm), seg0(8).reshape(1, d_gm)

    c16 = jnp.concatenate([c_all, c_ctx[None], jnp.zeros((MOD_ROWS - N_DEV - 1, d), F32)], axis=0)
    b_loc = _my_cols(b_mod, me, n6)[:, None, :]
    mod_loc = _mod_fwd(c16, w_mod, b_loc)
    (g_mod,) = _exchange("gather_mod", [mod_loc], False)
    mod_all = jnp.moveaxis(g_mod, 0, 2).reshape(n_layers, MOD_ROWS, N_DEV * n6)
    ml = lax.dynamic_index_in_dim(mod_all, me, axis=1, keepdims=False).reshape(n_layers, 6, d)
    mc = mod_all[:, N_DEV].reshape(n_layers, 6, d)

    def row(a, *idx):
        return a[idx][None]

    cos, sin, perm = _rope_tables(seq, n_ctx)
    wa3 = ar_wa[0].reshape(2 * rnn_blocks, RNN_BLOCK_W, RNN_BLOCK_W)
    wx3 = ar_wx[0].reshape(2 * rnn_blocks, RNN_BLOCK_W, RNN_BLOCK_W)
    conv_b = ar_conv_b
    q_g, k_g = ar_q_g, ar_k_g
    w_sp = gm_w_sp[0]
    bsp_t = jnp.pad(gm_b_sp[0].T, ((0, 0), (0, LANES - gm_groups)))
    expand = np.zeros((LANES, d_gm), np.float32)
    for g in range(gm_groups):
        expand[g, g * GM_GROUP_W:(g + 1) * GM_GROUP_W] = 1.0
    expand = jnp.asarray(expand)

    def mlp_fwd(i, h2):
        z, act = _matmul(f"ff_in_{i}", h2, w1[i], outs=(F32, BF16),
                         epilogue=lambda acc: (acc, jnp.maximum(acc, 0.0) * jnp.maximum(acc, 0.0)))
        return z, act, _matmul(f"ff_out_{i}", act, w2[i])

    xtok = jnp.concatenate([ctx2, x2], axis=0)
    pre0_args = [row(norm_full, 0, 0), row(mc, 0, 0), row(mc, 0, 1), row(ml, 0, 0), row(ml, 0, 1)]
    f_pre0 = functools.partial(_f_pre_ctx, n_ctx_tiles=nct)
    (h0,) = _rowwise("pre0", f_pre0, t_all, tm, [_t(xtok)], pre0_args, [(d, BF16)])
    proj = _matmul("ar_in", h0, w_in, tm=256)
    f_qkv = functools.partial(_f_qkv, nh=N_HEADS, nkv=N_KV_HEADS)
    qkv_tiled = [_t(proj, 0, 0, attn_w), _t(proj, 0, kv_blk, 2 * kv_w), _t(cos), _t(sin)]
    q_r, kv_r = _rowwise("qkv", f_qkv, t_all, tm, qkv_tiled, [q_g, k_g, perm], [(attn_w, BF16), (2 * kv_w, BF16)])
    attn_o = _attn_fwd(q_r, kv_r, nct, tm)

    xr = proj[:, attn_w:attn_w + d_rnn]
    xs = [_shift_rows(xr, j - CONV_W // 2, n_ctx) for j in range(CONV_W)]
    (xc,) = _rowwise("conv", _f_conv, t_all, tm, [_t(a) for a in xs], [conv_w, conv_b], [(d_rnn, F32)])
    f_gates = functools.partial(_f_gates, nb=rnn_blocks)
    gate_full = [wa3, ba, wx3, bx, lam]
    a_f, b_f, a_b, b_b = _rowwise("gates", f_gates, t_all, tm, [_t(xc)], gate_full, [(d_rnn, F32)] * 4)

    def to3(a):
        return a.reshape(a.shape[0], lr, LANES)

    nc_scan = n_ctx // SCAN_BLOCK
    h_f, hp_f = _scan_fwd("scan_f", to3(a_f), to3(b_f), nc_scan, False)
    h_b, hp_b = _scan_fwd("scan_b", to3(a_b), to3(b_b), nc_scan, True)
    h_f2, h_b2 = h_f.reshape(t_all, d_rnn), h_b.reshape(t_all, d_rnn)
    rnn_tiled = [_t(h_f2, n_ctx), _t(h_b2, n_ctx), _t(proj, n_ctx, 2, d_rnn)]
    (rnn_o,) = _rowwise("rnn_out", _f_rnnout, seq, tm, rnn_tiled, [], [(d_rnn, BF16)])
    ar = jnp.concatenate([attn_o, rnn_o], axis=1)
    o0 = _matmul("ar_out", ar, w_out)
    mid0_args = [row(norm_full, 0, 1), row(ml, 0, 2), row(norm_full, 0, 2), row(ml, 0, 3), row(ml, 0, 4)]
    x1, h2_0 = _rowwise("mid0", _f_mid, seq, tm, [_t(x2), _t(o0)], mid0_args, [(d, F32), (d, BF16)])
    z0, act0, m0 = mlp_fwd(0, h2_0)
    post0_args = [row(norm_full, 0, 3), row(ml, 0, 5)]
    (x2l,) = _rowwise("post0", _f_post, seq, tm, [_t(x1), _t(m0)], post0_args, [(d, F32)])

    pre1_args = [row(norm_full, 1, 0), row(ml, 1, 0), row(ml, 1, 1)]
    (h1,) = _rowwise("pre1", _f_pre, seq, tm, [_t(x2l)], pre1_args, [(d, BF16)])
    zg = _matmul("gm_in", h1, w_gi)
    f_gm = functools.partial(_f_gm, n_chunks=tmb // CHUNK, groups=gm_groups)
    gm_full = [gm_b_in_f[:, :d_gm], gm_b_in_f[:, d_gm:], gm_vg, gm_vb, w_sp, bsp_t, expand]
    gm_tiled = [_t(zg, 0, 0, d_gm), _t(zg, 0, 1, d_gm)]
    (gmix,) = _rowwise("gm_mix", f_gm, seq, tmb, gm_tiled, gm_full, [(d_gm, BF16)])
    o1 = _matmul("gm_out", gmix, w_go)
    mid1_args = [row(norm_full, 1, 1), row(ml, 1, 2), row(norm_full, 1, 2), row(ml, 1, 3), row(ml, 1, 4)]
    x3, h2_1 = _rowwise("mid1", _f_mid, seq, tm, [_t(x2l), _t(o1)], mid1_args, [(d, F32), (d, BF16)])
    z1, act1, m1 = mlp_fwd(1, h2_1)
    post1_args = [row(norm_full, 1, 3), row(ml, 1, 5)]

    def f_loss(pid, xv, ov, tv, g, gate):
        err = _f_post(pid, xv, ov, g, gate)[0] - tv
        part = 0.5 * jnp.sum(err * err) / d
        return (err / d, jnp.full((8, LANES), part, F32))

    dy, loss_acc = _rowwise("loss", f_loss, seq, tm, [_t(x3), _t(m1), _t(tgt)], post1_args, [(d, F32)], [(8, LANES)])
    loss = lax.psum(loss_acc[0, 0], ("x", "y", "c"))

    d_norm = [[None] * 4 for _ in range(n_layers)]
    d_ml = [[None] * 6 for _ in range(n_layers)]
    part = {}

    def mlp_bwd(i, dm, z, act, h2):
        dz = _matmul(f"ff_out_dx_{i}", dm, w2[i], tb=True, outs=(BF16,), extras=(z,),
                     epilogue=lambda acc, zt: (acc * (2.0 * jnp.maximum(zt, 0.0)),))
        part[f"ff_out_{i}"] = _matmul(f"ff_out_dw_{i}", act, dm, ta=True, outs=(BF16,))
        part[f"ff_in_{i}"] = _matmul(f"ff_in_dw_{i}", h2, dz, ta=True, outs=(BF16,))
        return _matmul(f"ff_in_dx_{i}", dz, w1[i], tb=True)

    def post_bwd(i, xin, m, args, dout):
        res = _rowwise(f"post_bwd{i}", _bwd_of(_f_post, 2, 1, (0, 1, 2, 3)), seq, tmb, [_t(xin), _t(m), _t(dout)], args,
                       [(d, F32), (d, BF16)], [(1, d), (1, d)])
        d_norm[i][3], d_ml[i][5] = res[2], res[3]
        return res[0], res[1]

    def mid_bwd(i, xin, o, args, dx1, dh2):
        res = _rowwise(f"mid_bwd{i}", _bwd_of(_f_mid, 2, 2, (0, 1, 2, 3, 4, 5, 6)), seq, tmb,
                       [_t(xin), _t(o), _t(dx1), _t(dh2)], args, [(d, F32), (d, BF16)], [(1, d)] * 5)
        d_norm[i][1], d_ml[i][2], d_norm[i][2], d_ml[i][3], d_ml[i][4] = res[2:]
        return res[0], res[1]

    dx3, dm1 = post_bwd(1, x3, m1, post1_args, dy)
    dh2_1 = mlp_bwd(1, dm1, z1, act1, h2_1)
    dx2a, do1 = mid_bwd(1, x2l, o1, mid1_args, dx3, dh2_1)
    dgmix = _matmul("gm_out_dx", do1, w_go, tb=True)
    part["gm_out"] = _matmul("gm_out_dw", gmix, do1, ta=True, outs=(BF16,))
    gm_res = _rowwise("gm_mix_bwd", _bwd_of(f_gm, 2, 1, (0, 1, 2, 3, 4, 5, 6, 7)), seq, tmb,
                      gm_tiled + [_t(dgmix)], gm_full, [(d_gm, BF16), (d_gm, BF16)],
                      [(1, d_gm)] * 4 + [w_sp.shape, bsp_t.shape])
    dzg = jnp.concatenate([gm_res[0], gm_res[1]], axis=1)
    g_gm_b_in = jnp.concatenate([gm_res[2], gm_res[3]], axis=1)
    g_gm_vg, g_gm_vb, g_w_sp = gm_res[4], gm_res[5], gm_res[6]
    g_b_sp = gm_res[7][:, :gm_groups].T
    dh1 = _matmul("gm_in_dx", dzg, w_gi, tb=True)
    part["gm_in"] = _matmul("gm_in_dw", h1, dzg, ta=True, outs=(BF16,))

    def f_pre_bwd(pid, xv, dh, dxa, g, sh, sc):
        dxv, dg, dsh, dsc = _bwd_of(_f_pre, 1, 1, (0, 1, 2, 3))(pid, xv, dh, g, sh, sc)
        return (dxv + dxa, dg, dsh, dsc)

    res = _rowwise("pre_bwd1", f_pre_bwd, seq, tmb, [_t(x2l), _t(dh1), _t(dx2a)], pre1_args, [(d, F32)], [(1, d)] * 3)
    dx2l = res[0]
    d_norm[1][0], d_ml[1][0], d_ml[1][1] = res[1:]

    dx1, dm0 = post_bwd(0, x1, m0, post0_args, dx2l)
    dh2_0 = mlp_bwd(0, dm0, z0, act0, h2_0)
    dxa, do0 = mid_bwd(0, x2, o0, mid0_args, dx1, dh2_0)
    d_ar = _matmul("ar_out_dx", do0, w_out, tb=True)
    part["ar_out"] = _matmul("ar_out_dw", ar, do0, ta=True, outs=(BF16,))

    dq, dkt, dvt = _attn_bwd(q_r, kv_r, d_ar, nct, tm)
    dq_all = jnp.concatenate([jnp.zeros((n_ctx, attn_w), F32), dq], axis=0)
    dkv_all = jnp.concatenate([dkt, dvt], axis=0).T
    qkv_res = _rowwise("qkv_bwd", _bwd_of(f_qkv, 4, 2, (0, 1, 4, 5)), t_all, tmb, qkv_tiled + [_t(dq_all), _t(dkv_all)],
                       [q_g, k_g, perm], [(attn_w, BF16), (2 * kv_w, BF16)], [q_g.shape, k_g.shape])
    dproj_q, dproj_kv, g_q_g, g_k_g = qkv_res

    rnn_res = _rowwise("rnn_out_bwd", _bwd_of(_f_rnnout, 3, 1, (0, 2)), seq, tmb, rnn_tiled + [_t(d_ar, 0, 1, d_rnn)], [],
                       [(d_rnn, F32), (d_rnn, BF16)])
    zc = jnp.zeros((n_ctx, d_rnn), F32)
    dh_all = to3(jnp.concatenate([zc, rnn_res[0]], axis=0))
    dproj_g = jnp.concatenate([zc.astype(BF16), rnn_res[1]], axis=0)
    da_f, db_f = _scan_bwd("scan_f_bwd", to3(a_f), dh_all, hp_f, nc_scan, False)
    da_b, db_b = _scan_bwd("scan_b_bwd", to3(a_b), dh_all, hp_b, nc_scan, True)
    gate_cts = [_t(a.reshape(t_all, d_rnn)) for a in (da_f, db_f, da_b, db_b)]
    gates_res = _rowwise("gates_bwd", _bwd_of(f_gates, 1, 4, (0, 1, 2, 3, 4, 5)), t_all, tmb, [_t(xc)] + gate_cts,
                         gate_full, [(d_rnn, F32)], [wa3.shape, ba.shape, wx3.shape, bx.shape, lam.shape])
    dxc, g_wa, g_ba, g_wx, g_bx, g_lam = gates_res
    dys = [_shift_rows(dxc, CONV_W // 2 - j, n_ctx) for j in range(CONV_W)]
    dproj_x, g_conv_w, g_conv_b = _rowwise("conv_bwd", _f_conv_bwd, t_all, tmb, [_t(a) for a in xs + dys], [conv_w],
                                           [(d_rnn, BF16)], [conv_w.shape, (1, d_rnn)])
    dproj = jnp.concatenate([dproj_q, dproj_x, dproj_g, dproj_kv], axis=1)
    dh0 = _matmul("ar_in_dx", dproj, w_in, tb=True, tm=256)
    g_w_in = _matmul("ar_in_dw", h0, dproj, ta=True, outs=(BF16,), tk=256)
    part["ar_in"] = jnp.concatenate([g_w_in[:, :attn_w], g_w_in[:, 3 * attn_w:], g_w_in[:, attn_w:3 * attn_w]], axis=1)

    dxa_all = jnp.concatenate([jnp.zeros((n_ctx, d), F32), dxa], axis=0)
    f_pre0b = functools.partial(_f_pre_ctx, n_ctx_tiles=nctb)

    def f_pre0_bwd(pid, xv, dh, dxp, g, shc, scc, shl, scl):
        grads = _bwd_of(f_pre0b, 1, 1, (0, 1, 2, 3, 4, 5))(pid, xv, dh, g, shc, scc, shl, scl)
        return (grads[0] + dxp,) + tuple(grads[1:])

    res = _rowwise("pre_bwd0", f_pre0_bwd, t_all, tmb, [_t(xtok), _t(dh0), _t(dxa_all)], pre0_args, [(d, F32)], [(1, d)] * 5)
    grad_x = res[0][n_ctx:][None]
    d_norm[0][0], d_mc_shift, d_mc_scale, d_ml[0][0], d_ml[0][1] = res[1:]

    z1d = jnp.zeros((1, d), F32)
    dml = jnp.concatenate([jnp.concatenate(r, axis=0)[None] for r in d_ml], axis=0)
    dmc = jnp.concatenate([jnp.concatenate([d_mc_shift, d_mc_scale] + [z1d] * 4, axis=0)[None],
                           jnp.zeros((n_layers - 1, 6, d), F32)], axis=0)
    g_norm = jnp.concatenate([jnp.concatenate(r, axis=0)[None] for r in d_norm], axis=0)
    small_parts = [dmc, g_norm, g_q_g, g_k_g, g_conv_w, g_conv_b, g_wa, g_wx, g_ba, g_bx, g_lam, g_gm_b_in, g_gm_vg,
                   g_gm_vb, g_w_sp, g_b_sp]
    small2, off2 = _pack([dml] + small_parts)
    (gs2,) = _exchange("gather_small_grads", [small2], False)
    dml_all = gs2.reshape(N_DEV, -1)[:, :off2[1]].reshape(N_DEV, n_layers, 6 * d)
    summed = _sum_lead("sum_small_grads", gs2).reshape(-1)

    def seg2(k, shape):
        return summed[off2[k + 1]:off2[k + 2]].reshape(shape)

    dmc_sum = seg2(0, (n_layers, 6 * d))
    dmod_rows = jnp.concatenate([jnp.moveaxis(dml_all, 0, 1), dmc_sum[:, None, :],
                                 jnp.zeros((n_layers, MOD_ROWS - N_DEV - 1, 6 * d), F32)], axis=1)
    g_b_mod = _sum_lead("sum_b_mod", jnp.moveaxis(dmod_rows, 1, 0).reshape(MOD_ROWS, n_layers * 6 * d // LANES, LANES))
    g_b_mod = g_b_mod.reshape(n_layers, 6 * d)
    g_w_mod, ds16 = _mod_bwd(c16, w_mod, _my_cols(dmod_rows, me, n6))
    (g_ds,) = _exchange("gather_dctx", [ds16[N_DEV].reshape(d // LANES, LANES)], False)
    ds_ctx = _sum_lead("sum_dctx", g_ds)
    (g_c_ctx,) = _rowwise("silu_bwd", _f_silu_mul, d // LANES, d // LANES, [_t(c_ctx.reshape(d // LANES, LANES)), _t(ds_ctx)],
                          [], [(LANES, F32)])
    g_c_ctx = g_c_ctx.reshape(d)

    grads = {
        'c_ctx': g_c_ctx, 'b_mod': g_b_mod,
        'norm_g': _my_cols(seg2(1, (n_layers, 4, d)), me, d // N_DEV),
        'ar_q_g': seg2(2, ar_q_g.shape), 'ar_k_g': seg2(3, ar_k_g.shape),
        'ar_conv_w': _my_cols(seg2(4, (1, CONV_W, d_rnn)), me, d_rnn // N_DEV),
        'ar_conv_b': seg2(5, ar_conv_b.shape), 'ar_wa': seg2(6, ar_wa.shape), 'ar_wx': seg2(7, ar_wx.shape),
        'ar_ba': _my_cols(seg2(8, (1, 2, d_rnn)), me, d_rnn // N_DEV),
        'ar_bx': _my_cols(seg2(9, (1, 2, d_rnn)), me, d_rnn // N_DEV),
        'ar_lambda': _my_cols(seg2(10, (1, 2, d_rnn)), me, d_rnn // N_DEV),
        'gm_b_in': _my_cols(seg2(11, (1, 2 * d_gm)), me, 2 * d_gm // N_DEV),
        'gm_v_g': _my_cols(seg2(12, (1, d_gm)), me, d_gm // N_DEV),
        'gm_v_b': _my_cols(seg2(13, (1, d_gm)), me, d_gm // N_DEV),
        'gm_w_sp': seg2(14, gm_w_sp.shape), 'gm_b_sp': seg2(15, gm_b_sp.shape),
    }
    deltas, new_m, new_v = {}, {}, {}

    small_names = list(grads)
    wp, offw = _pack([wts[n] for n in small_names])
    mp, _ = _pack([mom1[n] for n in small_names])
    vp, _ = _pack([mom2[n] for n in small_names])
    gp, _ = _pack([grads[n] for n in small_names])
    dp, mp2, vp2 = _adam_f32("adam_small", gp, wp, mp, vp)
    for k, n in enumerate(small_names):
        for dst, slab in ((deltas, dp), (new_m, mp2), (new_v, vp2)):
            dst[n] = slab.reshape(-1)[offw[k]:offw[k + 1]].reshape(wts[n].shape)

    grads['w_mod'] = g_w_mod
    dw, mw, vw = _adam_f32("adam_w_mod", g_w_mod.reshape(n_layers * d, n6), w_mod.reshape(n_layers * d, n6),
                           m_w_mod.reshape(n_layers * d, n6), v_w_mod.reshape(n_layers * d, n6))
    deltas['w_mod'], new_m['w_mod'], new_v['w_mod'] = (a.reshape(w_mod.shape) for a in (dw, mw, vw))

    def cols_blocks(g):
        return jnp.moveaxis(g.reshape(g.shape[0], N_DEV, g.shape[1] // N_DEV), 1, 0)

    def rows_blocks(g):
        return g.reshape(N_DEV, g.shape[0] // N_DEV, g.shape[1])

    send = [
        jnp.concatenate([cols_blocks(part[f"ff_in_{i}"])[:, None] for i in range(n_layers)], axis=1),
        jnp.concatenate([rows_blocks(part[f"ff_out_{i}"])[:, None] for i in range(n_layers)], axis=1),
        cols_blocks(part["ar_in"])[:, None], rows_blocks(part["ar_out"])[:, None],
        cols_blocks(part["gm_in"])[:, None], rows_blocks(part["gm_out"])[:, None]]
    recv = _exchange("scatter_grads", send, True)
    for n, r in zip(['w_ff_in', 'w_ff_out', 'ar_w_in', 'ar_w_out', 'gm_w_in', 'gm_w_out'], recv):
        shp = wts[n].shape
        r2 = r.reshape(N_DEV, shp[0] * shp[1], shp[2])
        flat = (shp[0] * shp[1], shp[2])
        res = _adam_recv("adam_" + n, r2, wts[n].reshape(flat), mom1[n].reshape(flat), mom2[n].reshape(flat))
        grads[n], deltas[n], new_m[n], new_v[n] = (a.reshape(shp) for a in res)

    return (loss, grad_x, *[grads[n] for n in WEIGHTS], *[deltas[n] for n in WEIGHTS],
            *[new_m[n] for n in WEIGHTS], *[new_v[n] for n in WEIGHTS])
```

```python
import functools

import numpy as np
import jax
import jax.numpy as jnp
from jax import lax
from jax.experimental import pallas as pl
from jax.experimental.pallas import tpu as pltpu

F32 = jnp.float32
BF16 = jnp.bfloat16
HIGHEST = lax.Precision.HIGHEST

GRID_W = 64
N_HEADS = 8
N_KV_HEADS = 2
HEAD_DIM = 128
ROPE_THETA = 10000.0
RNN_BLOCK_W = 128
CONV_W = 4
RG_C = 8.0
GM_GROUP_W = 128
CHUNK = 128
EPS = 1e-6
ADAM_LR = 0.001
ADAM_B1 = 0.9
ADAM_B2 = 0.999
ADAM_EPS = 1e-08
ADAM_WD = 0.01
ADAM_STEP = 10

N_DEV = 8
MOD_ROWS = 16
LANES = 128
ROW_TILE = 256
ROW_TILE_BWD = 128
SCAN_BLOCK = 256
VMEM_LIMIT = 56 * 1024 * 1024
MM_TILE_M = 512
MM_TILE_N = 1024
MM_TILE_K = 2048

WEIGHTS = ['c_ctx', 'w_mod', 'b_mod', 'norm_g', 'w_ff_in', 'w_ff_out', 'ar_w_in', 'ar_q_g', 'ar_k_g', 'ar_conv_w',
           'ar_conv_b', 'ar_wa', 'ar_ba', 'ar_wx', 'ar_bx', 'ar_lambda', 'ar_w_out', 'gm_w_in', 'gm_b_in', 'gm_v_g',
           'gm_v_b', 'gm_w_sp', 'gm_b_sp', 'gm_w_out']


def _sds(shape, dtype):
    return jax.ShapeDtypeStruct(tuple(shape), dtype)


def _tile(dim, pref, align):
    t = (min(pref, dim) // align) * align
    while t >= align:
        if dim % t == 0:
            return t
        t -= align
    return dim


def _params(sem):
    return pltpu.CompilerParams(dimension_semantics=sem, vmem_limit_bytes=VMEM_LIMIT)


def _rms(x, g):
    return x * lax.rsqrt(jnp.mean(x * x, axis=-1, keepdims=True) + EPS) * g


def _gelu(x):
    return 0.5 * x * (1.0 + jnp.tanh(0.7978845608028654 * (x + 0.044715 * (x * x * x))))


def _sigmoid(x):
    return 0.5 * (jnp.tanh(0.5 * x) + 1.0)


def _log1p_pos(u):
    small = u < 1e-3
    us = jnp.where(small, u, 0.0)
    return jnp.where(small, us * (1.0 - us * (0.5 - us * (1.0 / 3.0))), jnp.log(1.0 + u))


def _softplus(x):
    return jnp.maximum(x, 0.0) + _log1p_pos(jnp.exp(-jnp.abs(x)))


def _expm1(x):
    small = jnp.abs(x) < 0.3
    xs = jnp.where(small, x, 0.0)
    poly = xs * (1.0 + xs * (1.0 / 2 + xs * (1.0 / 6 + xs * (1.0 / 24 + xs * (1.0 / 120 + xs * (1.0 / 720 + xs * (1.0 / 5040)))))))
    return jnp.where(small, poly, jnp.exp(x) - 1.0)


def _f_pre_ctx(pid, x, g, shc, scc, shl, scl, *, n_ctx_tiles):
    is_ctx = pid < n_ctx_tiles
    sh = jnp.where(is_ctx, shc, shl)
    sc = jnp.where(is_ctx, scc, scl)
    return (_rms(x, g) * (1.0 + sc) + sh,)


def _f_pre(pid, x, g, sh, sc):
    return (_rms(x, g) * (1.0 + sc) + sh,)


def _f_mid(pid, x, o, g1, gate, g2, sh, sc):
    x1 = x + gate * _rms(o, g1)
    return (x1, _rms(x1, g2) * (1.0 + sc) + sh)


def _f_post(pid, x, o, g, gate):
    return (x + gate * _rms(o, g),)


def _f_qkv(pid, pq, pkv, cos, sin, q_g, k_g, perm, *, nh, nkv):
    hd = HEAD_DIM

    def norm_rope(xh, g):
        y = _rms(xh, g)
        return y * cos + jnp.dot(y, perm, precision=HIGHEST, preferred_element_type=F32) * sin

    qs = [norm_rope(pq[:, h * hd:(h + 1) * hd], q_g) for h in range(nh)]
    ks = [norm_rope(pkv[:, h * hd:(h + 1) * hd], k_g) for h in range(nkv)]
    return (jnp.concatenate(qs, axis=1), jnp.concatenate(ks + [pkv[:, nkv * hd:]], axis=1))


def _f_gates(pid, x, wa, ba, wx, bx, lam, *, nb):
    w = RNN_BLOCK_W
    outs = []
    for d in range(2):
        ra, ri = [], []
        for n in range(nb):
            xn = x[:, n * w:(n + 1) * w].astype(BF16)
            ra.append(jnp.dot(xn, wa[d * nb + n].astype(BF16), preferred_element_type=F32))
            ri.append(jnp.dot(xn, wx[d * nb + n].astype(BF16), preferred_element_type=F32))
        r = _sigmoid(jnp.concatenate(ra, axis=1) + ba[d:d + 1])
        i = _sigmoid(jnp.concatenate(ri, axis=1) + bx[d:d + 1])
        log_a = -RG_C * r * _softplus(-lam[d:d + 1])
        outs.append(jnp.exp(log_a))
        outs.append(jnp.sqrt(-_expm1(2.0 * log_a)) * (i * x))
    return tuple(outs)


def _f_rnnout(pid, hf, hb, gr):
    return ((hf + hb) * _gelu(gr),)


def _f_gm(pid, zu, zv, bu, bv, v_g, v_b, w_sp, bsp_t, expand, *, n_chunks, groups):
    u = _gelu(zu + bu)
    v = _gelu(zv + bv)
    mu = jnp.mean(v, axis=-1, keepdims=True)
    vc = v - mu
    v = vc * lax.rsqrt(jnp.mean(vc * vc, axis=-1, keepdims=True) + EPS) * v_g + v_b
    bias = jnp.dot(bsp_t, expand, precision=HIGHEST, preferred_element_type=F32)
    outs = []
    for c in range(n_chunks):
        vch = v[c * CHUNK:(c + 1) * CHUNK]
        cols = [jnp.dot(w_sp[g].astype(BF16), vch[:, g * GM_GROUP_W:(g + 1) * GM_GROUP_W].astype(BF16),
                        preferred_element_type=F32) for g in range(groups)]
        outs.append(u[c * CHUNK:(c + 1) * CHUNK] * (jnp.concatenate(cols, axis=1) + bias))
    return (jnp.concatenate(outs, axis=0),)


def _f_silu_mul(pid, c, d):
    return (d * jax.grad(lambda z: jnp.sum(z * _sigmoid(z)))(c),)


def _bwd_of(fn, n_tiled, n_ct, want):
    def bwd(pid, *args):
        tiles = [t.astype(F32) for t in args[:n_tiled]]
        cts = args[n_tiled:n_tiled + n_ct]
        fulls = list(args[n_tiled + n_ct:])
        outs, vjp = jax.vjp(lambda *a: fn(pid, *a), *tiles, *fulls)
        grads = vjp(tuple(ct.astype(o.dtype) for ct, o in zip(cts, outs)))
        return tuple(grads[i] for i in want)
    return bwd


def _rowwise(name, fn, rows, tm, tiled, full, outs, accs=()):
    n_t, n_f, n_o, n_a = len(tiled), len(full), len(outs), len(accs)

    def body(*refs):
        pid = pl.program_id(0)
        vals = [r[...] for r in refs[:n_t + n_f]]
        res = fn(pid, *vals)
        o_refs = refs[n_t + n_f:n_t + n_f + n_o]
        a_refs = refs[n_t + n_f + n_o:]
        for r, v in zip(o_refs, res[:n_o]):
            r[...] = v.astype(r.dtype)
        if n_a:
            @pl.when(pid == 0)
            def _():
                for r in a_refs:
                    r[...] = jnp.zeros_like(r)
            for r, v in zip(a_refs, res[n_o:]):
                r[...] += v.astype(F32)

    assert all(ro % tm == 0 for (_, ro, _, _) in tiled)
    in_specs = [pl.BlockSpec((tm, w), lambda i, ro=ro // tm, cb=cb: (i + ro, cb)) for (_, ro, cb, w) in tiled]
    in_specs += [pl.BlockSpec(a.shape, lambda i, nd=a.ndim: (0,) * nd) for a in full]
    out_shape = [_sds((rows, w), dt) for (w, dt) in outs] + [_sds(s, F32) for s in accs]
    out_specs = [pl.BlockSpec((tm, w), lambda i: (i, 0)) for (w, _) in outs]
    out_specs += [pl.BlockSpec(tuple(s), lambda i, nd=len(s): (0,) * nd) for s in accs]
    return pl.pallas_call(body, grid=(rows // tm,), in_specs=in_specs, out_specs=out_specs, out_shape=out_shape, name=name,
                          compiler_params=_params(("arbitrary",)))(*[t[0] for t in tiled], *full)


def _t(a, row_off=0, col_blk=0, width=None):
    return (a, row_off, col_blk, a.shape[1] if width is None else width)


class _Comm:
    def __init__(self, ins, out_shapes, n_sems, start, finish):
        self.ins, self.out_shapes, self.n_sems, self.start, self.finish = list(ins), list(out_shapes), n_sems, start, finish

    def scratch(self):
        return [pltpu.SemaphoreType.DMA((self.n_sems,)), pltpu.SemaphoreType.DMA((self.n_sems,)),
                pltpu.SemaphoreType.DMA((len(self.ins),))]


_ANY = pl.BlockSpec(memory_space=pl.ANY)


def _split_refs(refs, n_in, n_out, n_scratch, comm):
    ci, co, cs = (len(comm.ins), len(comm.out_shapes), 3) if comm is not None else (0, 0, 0)
    cuts = np.cumsum([0, n_in, ci, n_out, co, n_scratch, cs])
    return [refs[cuts[i]:cuts[i + 1]] for i in range(6)]


def _matmul(name, a, b, *, ta=False, tb=False, outs=((F32,)), epilogue=None, extras=(), tm=None, tn=None, tk=None, comm=None):
    m, k = (a.shape[1], a.shape[0]) if ta else a.shape
    n = b.shape[0] if tb else b.shape[1]
    tm = _tile(m, tm or MM_TILE_M, 128 if ta else 16)
    tn = _tile(n, tn or MM_TILE_N, 128)
    tk = _tile(k, tk or MM_TILE_K, 128 if not ta else 16)
    ni, nj, nk = m // tm, n // tn, k // tk
    n_e, n_o = len(extras), len(outs)
    dims = (((0 if ta else 1,), (1 if tb else 0,)), ((), ()))

    def body(*refs):
        ins, c_ins, o_refs, c_outs, scratch, c_sems = _split_refs(refs, 2 + n_e, n_o, 1 if nk > 1 else 0, comm)
        a_ref, b_ref, e_refs = ins[0], ins[1], ins[2:]
        i, j, kk = pl.program_id(0), pl.program_id(1), pl.program_id(2)
        if comm is not None:
            @pl.when(jnp.logical_and(jnp.logical_and(i == 0, j == 0), kk == 0))
            def _():
                comm.start(c_ins, c_outs, *c_sems)

        def finish(acc):
            res = (acc,) if epilogue is None else epilogue(acc, *[e[...] for e in e_refs])
            for r, v in zip(o_refs, res):
                r[...] = v.astype(r.dtype)

        prod = lax.dot_general(a_ref[...].astype(BF16), b_ref[...].astype(BF16), dims, preferred_element_type=F32)
        if nk == 1:
            finish(prod)
        else:
            acc = scratch[0]

            @pl.when(kk == 0)
            def _():
                acc[...] = prod

            @pl.when(kk > 0)
            def _():
                acc[...] += prod

            @pl.when(kk == nk - 1)
            def _():
                finish(acc[...])
        if comm is not None:
            @pl.when(jnp.logical_and(jnp.logical_and(i == ni - 1, j == nj - 1), kk == nk - 1))
            def _():
                comm.finish(c_ins, c_outs, *c_sems)

    a_spec = pl.BlockSpec((tk, tm), lambda i, j, kk: (kk, i)) if ta else pl.BlockSpec((tm, tk), lambda i, j, kk: (i, kk))
    b_spec = pl.BlockSpec((tn, tk), lambda i, j, kk: (j, kk)) if tb else pl.BlockSpec((tk, tn), lambda i, j, kk: (kk, j))
    mn_spec = pl.BlockSpec((tm, tn), lambda i, j, kk: (i, j))
    c_in, c_out, c_scr = (comm.ins, comm.out_shapes, comm.scratch()) if comm is not None else ([], [], [])
    res = pl.pallas_call(body, grid=(ni, nj, nk), in_specs=[a_spec, b_spec] + [mn_spec] * n_e + [_ANY] * len(c_in),
                         out_specs=[mn_spec] * n_o + [_ANY] * len(c_out),
                         out_shape=[_sds((m, n), dt) for dt in outs] + list(c_out),
                         scratch_shapes=([pltpu.VMEM((tm, tn), F32)] if nk > 1 else []) + c_scr, name=name,
                         compiler_params=_params(("arbitrary", "arbitrary", "arbitrary")))(a, b, *extras, *c_in)
    main = res[0] if n_o == 1 else res[:n_o]
    return main if comm is None else (main, res[n_o:])


def _attn_fwd(q, kv, n_ctx_tiles, tq, comm=None):
    t_all = q.shape[0]
    s_len = t_all - n_ctx_tiles * tq
    hd, groups = HEAD_DIM, N_HEADS // N_KV_HEADS
    scale = HEAD_DIM ** -0.5
    nq = s_len // tq

    def body(*refs):
        (q_ref, k_ref, v_ref), c_ins, (o_ref,), c_outs, _, c_sems = _split_refs(refs, 3, 1, 0, comm)
        kh, g, i = pl.program_id(0), pl.program_id(1), pl.program_id(2)
        if comm is not None:
            @pl.when(jnp.logical_and(jnp.logical_and(kh == 0, g == 0), i == 0))
            def _():
                comm.start(c_ins, c_outs, *c_sems)

        s = lax.dot_general(q_ref[...], k_ref[...], (((1,), (1,)), ((), ())), preferred_element_type=F32) * scale
        p = jnp.exp(s - jnp.max(s, axis=-1, keepdims=True))
        l = jnp.sum(p, axis=-1, keepdims=True)
        o = jnp.dot(p.astype(BF16), v_ref[...], preferred_element_type=F32) / l
        o_ref[...] = o.astype(o_ref.dtype)
        if comm is not None:
            @pl.when(jnp.logical_and(jnp.logical_and(kh == N_KV_HEADS - 1, g == groups - 1), i == nq - 1))
            def _():
                comm.finish(c_ins, c_outs, *c_sems)

    c_in, c_out, c_scr = (comm.ins, comm.out_shapes, comm.scratch()) if comm is not None else ([], [], [])
    res = pl.pallas_call(
        body, grid=(N_KV_HEADS, groups, nq),
        in_specs=[pl.BlockSpec((tq, hd), lambda kh, g, i: (i + n_ctx_tiles, kh * groups + g)),
                  pl.BlockSpec((t_all, hd), lambda kh, g, i: (0, kh)),
                  pl.BlockSpec((t_all, hd), lambda kh, g, i: (0, N_KV_HEADS + kh))] + [_ANY] * len(c_in),
        out_specs=[pl.BlockSpec((tq, hd), lambda kh, g, i: (i, kh * groups + g))] + [_ANY] * len(c_out),
        out_shape=[_sds((s_len, N_HEADS * hd), BF16)] + list(c_out), scratch_shapes=c_scr, name="attn_fwd",
        compiler_params=_params(("arbitrary", "arbitrary", "arbitrary")))(q, kv, kv, *c_in)
    return res[0] if comm is None else (res[0], res[1:])


def _attn_bwd(q, kv, d_ar, n_ctx_tiles, tq):
    t_all = q.shape[0]
    s_len = t_all - n_ctx_tiles * tq
    hd, groups = HEAD_DIM, N_HEADS // N_KV_HEADS
    scale = HEAD_DIM ** -0.5

    def body(q_ref, k_ref, v_ref, do_ref, dq_ref, dkt_ref, dvt_ref):
        first = jnp.logical_and(pl.program_id(1) == 0, pl.program_id(2) == 0)

        @pl.when(first)
        def _():
            dkt_ref[...] = jnp.zeros_like(dkt_ref)
            dvt_ref[...] = jnp.zeros_like(dvt_ref)

        qv, kk, vv = q_ref[...], k_ref[...], v_ref[...]
        do = do_ref[...].astype(BF16)
        s = lax.dot_general(qv, kk, (((1,), (1,)), ((), ())), preferred_element_type=F32) * scale
        p = jnp.exp(s - jnp.max(s, axis=-1, keepdims=True))
        p = p / jnp.sum(p, axis=-1, keepdims=True)
        dp = lax.dot_general(do, vv, (((1,), (1,)), ((), ())), preferred_element_type=F32)
        ds = (p * (dp - jnp.sum(p * dp, axis=-1, keepdims=True)) * scale).astype(BF16)
        dq_ref[...] = jnp.dot(ds, kk, preferred_element_type=F32)
        dkt_ref[...] += jnp.dot(qv.T, ds, preferred_element_type=F32)
        dvt_ref[...] += jnp.dot(do.T, p.astype(BF16), preferred_element_type=F32)

    return pl.pallas_call(
        body, grid=(N_KV_HEADS, groups, s_len // tq),
        in_specs=[pl.BlockSpec((tq, hd), lambda kh, g, i: (i + n_ctx_tiles, kh * groups + g)),
                  pl.BlockSpec((t_all, hd), lambda kh, g, i: (0, kh)),
                  pl.BlockSpec((t_all, hd), lambda kh, g, i: (0, N_KV_HEADS + kh)),
                  pl.BlockSpec((tq, hd), lambda kh, g, i: (i, kh * groups + g))],
        out_specs=[pl.BlockSpec((tq, hd), lambda kh, g, i: (i, kh * groups + g)),
                   pl.BlockSpec((hd, t_all), lambda kh, g, i: (kh, 0)),
                   pl.BlockSpec((hd, t_all), lambda kh, g, i: (kh, 0))],
        out_shape=[_sds((s_len, N_HEADS * hd), F32), _sds((N_KV_HEADS * hd, t_all), F32),
                   _sds((N_KV_HEADS * hd, t_all), F32)],
        name="attn_bwd", compiler_params=_params(("arbitrary", "arbitrary", "arbitrary")))(q, kv, kv, d_ar)


def _scan_order(nb, nc, reverse):
    if not reverse:
        return lambda i: i
    return lambda i: jnp.where(i < nc, nc - 1 - i, nb - 1 - (i - nc))


def _scan_fwd(name, a, b, nc, reverse):
    t_all, r, l = a.shape
    tb = SCAN_BLOCK
    nb = t_all // tb
    order = _scan_order(nb, nc, reverse)

    def body(a_ref, b_ref, h_ref, hp_ref, carry):
        @pl.when(pl.program_id(0) == 0)
        def _():
            carry[...] = jnp.zeros_like(carry)

        def step(s, h):
            t = tb - 1 - s if reverse else s
            hp_ref[t] = h
            h = a_ref[t] * h + b_ref[t]
            h_ref[t] = h
            return h

        carry[...] = lax.fori_loop(0, tb, step, carry[...], unroll=8)

    spec = pl.BlockSpec((tb, r, l), lambda i: (order(i), 0, 0))
    return pl.pallas_call(body, grid=(nb,), in_specs=[spec, spec], out_specs=[spec, spec],
                          out_shape=[_sds(a.shape, F32)] * 2, scratch_shapes=[pltpu.VMEM((r, l), F32)], name=name,
                          compiler_params=_params(("arbitrary",)))(a, b)


def _scan_bwd(name, a, dh, hp, nc, reverse):
    t_all, r, l = a.shape
    tb = SCAN_BLOCK
    nb = t_all // tb
    primal = _scan_order(nb, nc, reverse)

    def order(i):
        return primal(nb - 1 - i)

    def body(a_ref, dh_ref, hp_ref, da_ref, db_ref, carry):
        @pl.when(pl.program_id(0) == 0)
        def _():
            carry[...] = jnp.zeros_like(carry)

        def step(s, cr):
            t = s if reverse else tb - 1 - s
            lam = dh_ref[t] + cr
            db_ref[t] = lam
            da_ref[t] = lam * hp_ref[t]
            return a_ref[t] * lam

        carry[...] = lax.fori_loop(0, tb, step, carry[...], unroll=8)

    spec = pl.BlockSpec((tb, r, l), lambda i: (order(i), 0, 0))
    return pl.pallas_call(body, grid=(nb,), in_specs=[spec] * 3, out_specs=[spec, spec],
                          out_shape=[_sds(a.shape, F32)] * 2, scratch_shapes=[pltpu.VMEM((r, l), F32)], name=name,
                          compiler_params=_params(("arbitrary",)))(a, dh, hp)


def _f_conv(pid, x0, x1, x2, x3, w, b):
    return (b + x0 * w[0:1] + x1 * w[1:2] + x2 * w[2:3] + x3 * w[3:4],)


def _f_conv_bwd(pid, x0, x1, x2, x3, d0, d1, d2, d3, w):
    dx = d0 * w[0:1] + d1 * w[1:2] + d2 * w[2:3] + d3 * w[3:4]
    dw = jnp.concatenate([jnp.sum(d2 * xj, axis=0, keepdims=True) for xj in (x0, x1, x2, x3)], axis=0)
    return (dx, dw, jnp.sum(d2, axis=0, keepdims=True))


def _shift_rows(a, k, n_ctx):
    def sh(z):
        if k == 0:
            return z
        pad = jnp.zeros((abs(k), z.shape[1]), z.dtype)
        return jnp.concatenate([z[k:], pad], axis=0) if k > 0 else jnp.concatenate([pad, z[:k]], axis=0)
    return jnp.concatenate([sh(a[:n_ctx]), sh(a[n_ctx:])], axis=0)


def _mesh_pos():
    return lax.axis_index("x"), lax.axis_index("y"), lax.axis_index("c")


def _remote(src, dst, send_sems, recv_sems, k, to):
    return pltpu.make_async_remote_copy(src_ref=src, dst_ref=dst, send_sem=send_sems.at[k], recv_sem=recv_sems.at[k],
                                        device_id=to, device_id_type=pl.DeviceIdType.MESH)


def _neighbours():
    x, y, c = _mesh_pos()
    return (x, y, c), (x, y, 1 - c), [(1 - x, y), (x, 1 - y), (1 - x, 1 - y)]


def _gather_comm(arrays):
    n = len(arrays)
    per = 7

    def slot(out, blk):
        return out.at[4 * blk[0] + 2 * blk[1] + blk[2]]

    def start(ins, outs, send, recv, local):
        me, sib, chips = _neighbours()
        for ai in range(n):
            pltpu.make_async_copy(ins[ai], slot(outs[ai], me), local.at[ai]).start()
            _remote(ins[ai], slot(outs[ai], me), send, recv, ai * per, sib).start()
            for j, chip in enumerate(chips):
                _remote(ins[ai], slot(outs[ai], me), send, recv, ai * per + 1 + j, (*chip, me[2])).start()

    def finish(ins, outs, send, recv, local):
        me, sib, chips = _neighbours()
        for ai in range(n):
            for j, chip in enumerate(chips):
                blk = slot(outs[ai], (*chip, me[2]))
                _remote(blk, blk, send, recv, ai * per + 1 + j, me).wait_recv()
                _remote(blk, blk, send, recv, ai * per + 4 + j, sib).start()
        for ai in range(n):
            blk = slot(outs[ai], sib)
            _remote(blk, blk, send, recv, ai * per, me).wait_recv()
            for j, chip in enumerate(chips):
                blk = slot(outs[ai], (*chip, 1 - me[2]))
                _remote(blk, blk, send, recv, ai * per + 4 + j, me).wait_recv()
            for k in range(per):
                _remote(ins[ai], slot(outs[ai], me), send, recv, ai * per + k, sib).wait_send()
            pltpu.make_async_copy(ins[ai], slot(outs[ai], me), local.at[ai]).wait()

    return _Comm(arrays, [_sds((N_DEV,) + a.shape, a.dtype) for a in arrays], n * per, start, finish)


def _swap_comm(arrays):
    n = len(arrays)

    def start(ins, outs, send, recv, local):
        me, sib, _ = _neighbours()
        for ai in range(n):
            for q in range(4):
                _remote(ins[ai].at[2 * q + 1 - me[2]], outs[ai].at[q], send, recv, ai * 4 + q, sib).start()

    def finish(ins, outs, send, recv, local):
        me, sib, _ = _neighbours()
        for ai in range(n):
            for q in range(4):
                cp = _remote(ins[ai].at[q], outs[ai].at[q], send, recv, ai * 4 + q, sib)
                cp.wait_recv()
                cp.wait_send()

    return _Comm(arrays, [_sds((4,) + a.shape[1:], a.dtype) for a in arrays], n * 4, start, finish)


def _chips_comm(arrays):
    n = len(arrays)

    def start(ins, outs, send, recv, local):
        me, _, chips = _neighbours()
        mine = 2 * me[0] + me[1]
        for ai in range(n):
            pltpu.make_async_copy(ins[ai].at[mine], outs[ai].at[mine], local.at[ai]).start()
            for j, chip in enumerate(chips):
                _remote(ins[ai].at[2 * chip[0] + chip[1]], outs[ai].at[mine], send, recv, ai * 3 + j, (*chip, me[2])).start()

    def finish(ins, outs, send, recv, local):
        me, _, chips = _neighbours()
        mine = 2 * me[0] + me[1]
        for ai in range(n):
            for j, chip in enumerate(chips):
                theirs = 2 * chip[0] + chip[1]
                cp = _remote(ins[ai].at[theirs], outs[ai].at[theirs], send, recv, ai * 3 + j, (*chip, me[2]))
                cp.wait_recv()
                cp.wait_send()
            pltpu.make_async_copy(ins[ai].at[mine], outs[ai].at[mine], local.at[ai]).wait()

    return _Comm(arrays, [_sds(a.shape, a.dtype) for a in arrays], n * 3, start, finish)


def _run_comm(name, comm):
    n_in, n_out = len(comm.ins), len(comm.out_shapes)

    def body(*refs):
        ins, outs, sems = refs[:n_in], refs[n_in:n_in + n_out], refs[n_in + n_out:]
        comm.start(ins, outs, *sems)
        comm.finish(ins, outs, *sems)

    return pl.pallas_call(body, in_specs=[_ANY] * n_in, out_specs=[_ANY] * n_out, out_shape=comm.out_shapes, name=name,
                          scratch_shapes=comm.scratch(), compiler_params=pltpu.CompilerParams(has_side_effects=True))(*comm.ins)


def _pair_add(name, a, b):
    r, c = a.shape
    tr = _tile(r, 1024, 16)

    def body(a_ref, b_ref, o_ref):
        o_ref[...] = (a_ref[...].astype(F32) + b_ref[...].astype(F32)).astype(o_ref.dtype)

    spec = pl.BlockSpec((tr, c), lambda i: (i, 0))
    return pl.pallas_call(body, grid=(r // tr,), in_specs=[spec, spec], out_specs=spec, out_shape=_sds((r, c), a.dtype),
                          name=name, compiler_params=_params(("parallel",)))(a, b)


def _sum_lead(name, a):
    n, r, c = a.shape
    tr = _tile(r, 512, 8)

    def body(a_ref, o_ref):
        acc = a_ref[0]
        for j in range(1, n):
            acc = acc + a_ref[j]
        o_ref[...] = acc

    return pl.pallas_call(body, grid=(r // tr,), in_specs=[pl.BlockSpec((n, tr, c), lambda i: (0, i, 0))],
                          out_specs=pl.BlockSpec((tr, c), lambda i: (i, 0)), out_shape=_sds((r, c), F32), name=name,
                          compiler_params=_params(("parallel",)))(a)


def _adam_math(w, g, m, v):
    m = ADAM_B1 * m + (1.0 - ADAM_B1) * g
    v = ADAM_B2 * v + (1.0 - ADAM_B2) * (g * g)
    m_hat = m / (1.0 - ADAM_B1 ** ADAM_STEP)
    v_hat = v / (1.0 - ADAM_B2 ** ADAM_STEP)
    delta = -ADAM_LR * (m_hat / (jnp.sqrt(v_hat) + ADAM_EPS) + ADAM_WD * w)
    return delta, m, v


def _adam_recv(name, recv, w, m, v):
    n, r, c = recv.shape
    tr = _tile(r, 256, 16)

    def body(g_ref, w_ref, m_ref, v_ref, go_ref, d_ref, mo_ref, vo_ref):
        g = g_ref[0].astype(F32)
        for j in range(1, n):
            g = g + g_ref[j].astype(F32)
        delta, m2, v2 = _adam_math(w_ref[...], g, m_ref[...], v_ref[...])
        go_ref[...] = g
        d_ref[...] = delta
        mo_ref[...] = m2
        vo_ref[...] = v2

    spec = pl.BlockSpec((tr, c), lambda i: (i, 0))
    return pl.pallas_call(body, grid=(r // tr,), in_specs=[pl.BlockSpec((n, tr, c), lambda i: (0, i, 0))] + [spec] * 3,
                          out_specs=[spec] * 4, out_shape=[_sds((r, c), F32)] * 4, name=name,
                          compiler_params=_params(("parallel",)))(recv, w, m, v)


def _adam_f32(name, g, w, m, v):
    r, c = g.shape
    tr = _tile(r, 512, 8)

    def body(g_ref, w_ref, m_ref, v_ref, d_ref, mo_ref, vo_ref):
        delta, m2, v2 = _adam_math(w_ref[...], g_ref[...], m_ref[...], v_ref[...])
        d_ref[...] = delta
        mo_ref[...] = m2
        vo_ref[...] = v2

    spec = pl.BlockSpec((tr, c), lambda i: (i, 0))
    return pl.pallas_call(body, grid=(r // tr,), in_specs=[spec] * 4, out_specs=[spec] * 3,
                          out_shape=[_sds((r, c), F32)] * 3, name=name, compiler_params=_params(("parallel",)))(g, w, m, v)


def _mod_fwd(c16, w_mod, b_loc):
    nl, d, n6 = w_mod.shape
    tn = _tile(n6, 512, 128)

    def body(c_ref, w_ref, b_ref, o_ref):
        cv = c_ref[...]
        s = cv * _sigmoid(cv)
        o_ref[0] = jnp.dot(s, w_ref[0], precision=HIGHEST, preferred_element_type=F32) + b_ref[0]

    return pl.pallas_call(
        body, grid=(nl, n6 // tn),
        in_specs=[pl.BlockSpec((MOD_ROWS, d), lambda i, j: (0, 0)), pl.BlockSpec((1, d, tn), lambda i, j: (i, 0, j)),
                  pl.BlockSpec((1, 1, tn), lambda i, j: (i, 0, j))],
        out_specs=pl.BlockSpec((1, MOD_ROWS, tn), lambda i, j: (i, 0, j)), out_shape=_sds((nl, MOD_ROWS, n6), F32),
        name="mod_fwd", compiler_params=_params(("parallel", "parallel")))(c16, w_mod, b_loc)


def _mod_bwd(c16, w_mod, dmod_loc):
    nl, d, n6 = w_mod.shape
    tn = _tile(n6, 512, 128)

    def body(c_ref, w_ref, dm_ref, dw_ref, ds_ref):
        @pl.when(jnp.logical_and(pl.program_id(0) == 0, pl.program_id(1) == 0))
        def _():
            ds_ref[...] = jnp.zeros_like(ds_ref)

        cv = c_ref[...]
        s = cv * _sigmoid(cv)
        dm = dm_ref[0]
        dw_ref[0] = lax.dot_general(s, dm, (((0,), (0,)), ((), ())), precision=HIGHEST, preferred_element_type=F32)
        ds_ref[...] += lax.dot_general(dm, w_ref[0], (((1,), (1,)), ((), ())), precision=HIGHEST, preferred_element_type=F32)

    return pl.pallas_call(
        body, grid=(nl, n6 // tn),
        in_specs=[pl.BlockSpec((MOD_ROWS, d), lambda i, j: (0, 0)), pl.BlockSpec((1, d, tn), lambda i, j: (i, 0, j)),
                  pl.BlockSpec((1, MOD_ROWS, tn), lambda i, j: (i, 0, j))],
        out_specs=[pl.BlockSpec((1, d, tn), lambda i, j: (i, 0, j)), pl.BlockSpec((MOD_ROWS, d), lambda i, j: (0, 0))],
        out_shape=[_sds((nl, d, n6), F32), _sds((MOD_ROWS, d), F32)], name="mod_bwd",
        compiler_params=_params(("arbitrary", "arbitrary")))(c16, w_mod, dmod_loc)


def _pack(parts):
    flat = [p.reshape(-1).astype(F32) for p in parts]
    offs = np.cumsum([0] + [f.shape[0] for f in flat])
    total = int(offs[-1])
    padded = -(-total // (8 * LANES)) * (8 * LANES)
    slab = jnp.concatenate(flat + [jnp.zeros((padded - total,), F32)])
    return slab.reshape(padded // LANES, LANES), [int(o) for o in offs]


def _unshard_cols(seg, lead):
    n = seg.shape[1] // int(np.prod(lead)) if lead else seg.shape[1]
    a = seg.reshape((N_DEV,) + tuple(lead) + (n,))
    a = jnp.moveaxis(a, 0, len(lead))
    return a.reshape(tuple(lead) + (N_DEV * n,))


def _my_cols(a, me, n):
    start = (0,) * (a.ndim - 1) + (me * n,)
    return lax.dynamic_slice(a, start, a.shape[:-1] + (n,))


def _rope_tables(seq, n_ctx):
    rows = seq // GRID_W
    r_idx, c_idx = jnp.meshgrid(jnp.arange(rows), jnp.arange(GRID_W), indexing='ij')
    r_idx = r_idx.reshape(-1).astype(F32)
    c_idx = c_idx.reshape(-1).astype(F32)
    pairs = HEAD_DIM // 4
    freqs = ROPE_THETA ** (-jnp.arange(pairs, dtype=F32) / pairs)
    ang_r, ang_c = r_idx[:, None] * freqs, c_idx[:, None] * freqs
    cos = jnp.concatenate([jnp.cos(ang_r)] * 2 + [jnp.cos(ang_c)] * 2, axis=1)
    sin = jnp.concatenate([-jnp.sin(ang_r), jnp.sin(ang_r), -jnp.sin(ang_c), jnp.sin(ang_c)], axis=1)
    cos = jnp.concatenate([jnp.ones((n_ctx, HEAD_DIM), F32), cos], axis=0)
    sin = jnp.concatenate([jnp.zeros((n_ctx, HEAD_DIM), F32), sin], axis=0)
    lane = np.arange(HEAD_DIM)
    partner = np.where(lane % (2 * pairs) < pairs, lane + pairs, lane - pairs)
    perm = np.zeros((HEAD_DIM, HEAD_DIM), np.float32)
    perm[partner, lane] = 1.0
    return cos, sin, jnp.asarray(perm)


def kernel(x, c, ctx, c_ctx, w_mod, b_mod, norm_g, w_ff_in, w_ff_out, ar_w_in, ar_q_g, ar_k_g, ar_conv_w, ar_conv_b, ar_wa, ar_ba, ar_wx, ar_bx, ar_lambda, ar_w_out, gm_w_in, gm_b_in, gm_v_g, gm_v_b, gm_w_sp, gm_b_sp, gm_w_out, loss_target, m_c_ctx, m_w_mod, m_b_mod, m_norm_g, m_w_ff_in, m_w_ff_out, m_ar_w_in, m_ar_q_g, m_ar_k_g, m_ar_conv_w, m_ar_conv_b, m_ar_wa, m_ar_ba, m_ar_wx, m_ar_bx, m_ar_lambda, m_ar_w_out, m_gm_w_in, m_gm_b_in, m_gm_v_g, m_gm_v_b, m_gm_w_sp, m_gm_b_sp, m_gm_w_out, v_c_ctx, v_w_mod, v_b_mod, v_norm_g, v_w_ff_in, v_w_ff_out, v_ar_w_in, v_ar_q_g, v_ar_k_g, v_ar_conv_w, v_ar_conv_b, v_ar_wa, v_ar_ba, v_ar_wx, v_ar_bx, v_ar_lambda, v_ar_w_out, v_gm_w_in, v_gm_b_in, v_gm_v_g, v_gm_v_b, v_gm_w_sp, v_gm_b_sp, v_gm_w_out):
    given = dict(locals())
    wts = {n: given[n] for n in WEIGHTS}
    mom1 = {n: given["m_" + n] for n in WEIGHTS}
    mom2 = {n: given["v_" + n] for n in WEIGHTS}

    xi, yi, ci = _mesh_pos()
    me = 4 * xi + 2 * yi + ci

    seq, d = x.shape[1], x.shape[2]
    n_ctx = ctx.shape[1]
    t_all = n_ctx + seq
    n_layers = w_mod.shape[0]
    assert n_layers == 2 and ar_w_in.shape[0] == 1 and gm_w_in.shape[0] == 1
    d_ff = w_ff_in.shape[2] * N_DEV
    attn_w, kv_w = N_HEADS * HEAD_DIM, N_KV_HEADS * HEAD_DIM
    rnn_blocks = ar_wa.shape[2]
    d_rnn = rnn_blocks * RNN_BLOCK_W
    gm_groups = gm_w_sp.shape[1]
    d_gm = gm_groups * GM_GROUP_W
    ar_in = ar_w_in.shape[2] * N_DEV
    n6 = w_mod.shape[2]
    tm, tmb = ROW_TILE, ROW_TILE_BWD
    assert attn_w == d_rnn and ar_in == 3 * attn_w + 2 * kv_w and (3 * attn_w) % (2 * kv_w) == 0
    assert n_ctx % tm == 0 and seq % tm == 0 and n_ctx % SCAN_BLOCK == 0 and seq % SCAN_BLOCK == 0 and tm % CHUNK == 0
    nct, nctb = n_ctx // tm, n_ctx // tmb
    kv_blk = (3 * attn_w) // (2 * kv_w)
    lr = d_rnn // LANES

    x2, ctx2, tgt = x[0], ctx[0], loss_target[0]

    def cols_full(g):
        return jnp.moveaxis(g, 0, 1).reshape(g.shape[1], N_DEV * g.shape[2])

    small0, off0 = _pack([c[0], norm_g, ar_conv_w[0], ar_ba[0], ar_bx[0], ar_lambda[0], gm_b_in[0], gm_v_g[0], gm_v_b[0]])
    g_ar_in, gs0 = _run_comm("gather_first", _gather_comm([ar_w_in[0].astype(BF16), small0]))
    gs0 = gs0.reshape(N_DEV, -1)
    w_in = cols_full(g_ar_in)
    split = [attn_w, attn_w + 2 * kv_w, attn_w + 2 * kv_w + d_rnn]
    w_in = jnp.concatenate([w_in[:, :split[0]], w_in[:, split[1]:], w_in[:, split[0]:split[1]]], axis=1)
    w1, w2 = [None] * n_layers, [None] * n_layers

    def seg0(k):
        return gs0[:, off0[k]:off0[k + 1]]

    c_all = seg0(0)
    norm_full = _unshard_cols(seg0(1), (n_layers, 4))
    conv_w = _unshard_cols(seg0(2), (CONV_W,))
    ba, bx, lam = (_unshard_cols(seg0(k), (2,)) for k in (3, 4, 5))
    gm_b_in_f = seg0(6).reshape(1, 2 * d_gm)
    gm_vg, gm_vb = seg0(7).reshape(1, d_gm), seg0(8).reshape(1, d_gm)

    c16 = jnp.concatenate([c_all, c_ctx[None], jnp.zeros((MOD_ROWS - N_DEV - 1, d), F32)], axis=0)
    b_loc = _my_cols(b_mod, me, n6)[:, None, :]
    mod_loc = _mod_fwd(c16, w_mod, b_loc)
    (g_mod,) = _run_comm("gather_mod", _gather_comm([mod_loc]))
    mod_all = jnp.moveaxis(g_mod, 0, 2).reshape(n_layers, MOD_ROWS, N_DEV * n6)
    ml = lax.dynamic_index_in_dim(mod_all, me, axis=1, keepdims=False).reshape(n_layers, 6, d)
    mc = mod_all[:, N_DEV].reshape(n_layers, 6, d)

    def row(a, *idx):
        return a[idx][None]

    cos, sin, perm = _rope_tables(seq, n_ctx)
    wa3 = ar_wa[0].reshape(2 * rnn_blocks, RNN_BLOCK_W, RNN_BLOCK_W)
    wx3 = ar_wx[0].reshape(2 * rnn_blocks, RNN_BLOCK_W, RNN_BLOCK_W)
    conv_b = ar_conv_b
    q_g, k_g = ar_q_g, ar_k_g
    w_sp = gm_w_sp[0]
    bsp_t = jnp.pad(gm_b_sp[0].T, ((0, 0), (0, LANES - gm_groups)))
    expand = np.zeros((LANES, d_gm), np.float32)
    for g in range(gm_groups):
        expand[g, g * GM_GROUP_W:(g + 1) * GM_GROUP_W] = 1.0
    expand = jnp.asarray(expand)

    def relu2(acc):
        r = jnp.maximum(acc, 0.0)
        return (acc, r * r)

    def ff_in_shard(i):
        return w_ff_in[i].astype(BF16)

    def ff_out_shard(i):
        return w_ff_out[i].astype(BF16)

    xtok = jnp.concatenate([ctx2, x2], axis=0)
    pre0_args = [row(norm_full, 0, 0), row(mc, 0, 0), row(mc, 0, 1), row(ml, 0, 0), row(ml, 0, 1)]
    f_pre0 = functools.partial(_f_pre_ctx, n_ctx_tiles=nct)
    (h0,) = _rowwise("pre0", f_pre0, t_all, tm, [_t(xtok)], pre0_args, [(d, BF16)])
    tm_tok = _tile(t_all, 640, 16)
    proj, (g_ar_out,) = _matmul("ar_in", h0, w_in, tm=tm_tok, comm=_gather_comm([ar_w_out[0].astype(BF16)]))
    w_out = g_ar_out.reshape(attn_w + d_rnn, d)
    f_qkv = functools.partial(_f_qkv, nh=N_HEADS, nkv=N_KV_HEADS)
    qkv_tiled = [_t(proj, 0, 0, attn_w), _t(proj, 0, kv_blk, 2 * kv_w), _t(cos), _t(sin)]
    q_r, kv_r = _rowwise("qkv", f_qkv, t_all, tm, qkv_tiled, [q_g, k_g, perm], [(attn_w, BF16), (2 * kv_w, BF16)])
    attn_o, (g_ff_in0, g_ff_out0) = _attn_fwd(q_r, kv_r, nct, tm, comm=_gather_comm([ff_in_shard(0), ff_out_shard(0)]))
    w1[0], w2[0] = cols_full(g_ff_in0), g_ff_out0.reshape(d_ff, d)

    xr = proj[:, attn_w:attn_w + d_rnn]
    xs = [_shift_rows(xr, j - CONV_W // 2, n_ctx) for j in range(CONV_W)]
    (xc,) = _rowwise("conv", _f_conv, t_all, tm, [_t(a) for a in xs], [conv_w, conv_b], [(d_rnn, F32)])
    f_gates = functools.partial(_f_gates, nb=rnn_blocks)
    gate_full = [wa3, ba, wx3, bx, lam]
    a_f, b_f, a_b, b_b = _rowwise("gates", f_gates, t_all, tm, [_t(xc)], gate_full, [(d_rnn, F32)] * 4)

    def to3(a):
        return a.reshape(a.shape[0], lr, LANES)

    nc_scan = n_ctx // SCAN_BLOCK
    h_f, hp_f = _scan_fwd("scan_f", to3(a_f), to3(b_f), nc_scan, False)
    h_b, hp_b = _scan_fwd("scan_b", to3(a_b), to3(b_b), nc_scan, True)
    h_f2, h_b2 = h_f.reshape(t_all, d_rnn), h_b.reshape(t_all, d_rnn)
    rnn_tiled = [_t(h_f2, n_ctx), _t(h_b2, n_ctx), _t(proj, n_ctx, 2, d_rnn)]
    (rnn_o,) = _rowwise("rnn_out", _f_rnnout, seq, tm, rnn_tiled, [], [(d_rnn, BF16)])
    ar = jnp.concatenate([attn_o, rnn_o], axis=1)
    o0 = _matmul("ar_out", ar, w_out)
    mid0_args = [row(norm_full, 0, 1), row(ml, 0, 2), row(norm_full, 0, 2), row(ml, 0, 3), row(ml, 0, 4)]
    x1, h2_0 = _rowwise("mid0", _f_mid, seq, tm, [_t(x2), _t(o0)], mid0_args, [(d, F32), (d, BF16)])
    (z0, act0), (g_gm_in, g_gm_out) = _matmul("ff_in_0", h2_0, w1[0], outs=(F32, BF16), epilogue=relu2,
                                              comm=_gather_comm([gm_w_in[0].astype(BF16), gm_w_out[0].astype(BF16)]))
    w_gi, w_go = cols_full(g_gm_in), g_gm_out.reshape(d_gm, d)
    m0, (g_ff_in1,) = _matmul("ff_out_0", act0, w2[0], comm=_gather_comm([ff_in_shard(1)]))
    w1[1] = cols_full(g_ff_in1)
    post0_args = [row(norm_full, 0, 3), row(ml, 0, 5)]
    (x2l,) = _rowwise("post0", _f_post, seq, tm, [_t(x1), _t(m0)], post0_args, [(d, F32)])

    pre1_args = [row(norm_full, 1, 0), row(ml, 1, 0), row(ml, 1, 1)]
    (h1,) = _rowwise("pre1", _f_pre, seq, tm, [_t(x2l)], pre1_args, [(d, BF16)])
    zg = _matmul("gm_in", h1, w_gi)
    f_gm = functools.partial(_f_gm, n_chunks=tmb // CHUNK, groups=gm_groups)
    gm_full = [gm_b_in_f[:, :d_gm], gm_b_in_f[:, d_gm:], gm_vg, gm_vb, w_sp, bsp_t, expand]
    gm_tiled = [_t(zg, 0, 0, d_gm), _t(zg, 0, 1, d_gm)]
    (gmix,) = _rowwise("gm_mix", f_gm, seq, tmb, gm_tiled, gm_full, [(d_gm, BF16)])
    o1 = _matmul("gm_out", gmix, w_go)
    mid1_args = [row(norm_full, 1, 1), row(ml, 1, 2), row(norm_full, 1, 2), row(ml, 1, 3), row(ml, 1, 4)]
    x3, h2_1 = _rowwise("mid1", _f_mid, seq, tm, [_t(x2l), _t(o1)], mid1_args, [(d, F32), (d, BF16)])
    (z1, act1), (g_ff_out1,) = _matmul("ff_in_1", h2_1, w1[1], outs=(F32, BF16), epilogue=relu2,
                                       comm=_gather_comm([ff_out_shard(1)]))
    w2[1] = g_ff_out1.reshape(d_ff, d)
    m1 = _matmul("ff_out_1", act1, w2[1])
    post1_args = [row(norm_full, 1, 3), row(ml, 1, 5)]

    def f_loss(pid, xv, ov, tv, g, gate):
        err = _f_post(pid, xv, ov, g, gate)[0] - tv
        part = 0.5 * jnp.sum(err * err) / d
        return (err / d, jnp.full((8, LANES), part, F32))

    dy, loss_acc = _rowwise("loss", f_loss, seq, tm, [_t(x3), _t(m1), _t(tgt)], post1_args, [(d, F32)], [(8, LANES)])
    loss = lax.psum(loss_acc[0, 0], ("x", "y", "c"))

    d_norm = [[None] * 4 for _ in range(n_layers)]
    d_ml = [[None] * 6 for _ in range(n_layers)]
    recv = {}

    def cols_blocks(g):
        return jnp.moveaxis(g.reshape(g.shape[0], N_DEV, g.shape[1] // N_DEV), 1, 0)

    def rows_blocks(g):
        return g.reshape(N_DEV, g.shape[0] // N_DEV, g.shape[1])

    def chip_sum(name, blocks):
        (theirs,) = _run_comm(name + "_swap", _swap_comm([blocks]))
        mine = lax.dynamic_index_in_dim(blocks.reshape((4, 2) + blocks.shape[1:]), ci, axis=1, keepdims=False)
        r, cc = blocks.shape[1:]
        return _pair_add(name + "_add", mine.reshape(4 * r, cc), theirs.reshape(4 * r, cc)).reshape(4, r, cc)

    def mlp_bwd(i, dm, z, act, h2, carried=None):
        dw2 = chip_sum(f"ff_out_{i}", rows_blocks(_matmul(f"ff_out_dw_{i}", act, dm, ta=True, outs=(BF16,))))
        dz = _matmul(f"ff_out_dx_{i}", dm, w2[i], tb=True, outs=(BF16,), extras=(z,),
                     epilogue=lambda acc, zt: (acc * (2.0 * jnp.maximum(zt, 0.0)),),
                     comm=None if carried is None else _chips_comm([carried[1]]))
        if carried is not None:
            dz, (recv[carried[0]],) = dz
        dw1, (recv[f"ff_out_{i}"],) = _matmul(f"ff_in_dw_{i}", h2, dz, ta=True, outs=(BF16,), comm=_chips_comm([dw2]))
        dw1 = chip_sum(f"ff_in_{i}", cols_blocks(dw1))
        dh2, (recv[f"ff_in_{i}"],) = _matmul(f"ff_in_dx_{i}", dz, w1[i], tb=True, comm=_chips_comm([dw1]))
        return dh2

    def post_bwd(i, xin, m, args, dout):
        res = _rowwise(f"post_bwd{i}", _bwd_of(_f_post, 2, 1, (0, 1, 2, 3)), seq, tmb, [_t(xin), _t(m), _t(dout)], args,
                       [(d, F32), (d, BF16)], [(1, d), (1, d)])
        d_norm[i][3], d_ml[i][5] = res[2], res[3]
        return res[0], res[1]

    def mid_bwd(i, xin, o, args, dx1, dh2):
        res = _rowwise(f"mid_bwd{i}", _bwd_of(_f_mid, 2, 2, (0, 1, 2, 3, 4, 5, 6)), seq, tmb,
                       [_t(xin), _t(o), _t(dx1), _t(dh2)], args, [(d, F32), (d, BF16)], [(1, d)] * 5)
        d_norm[i][1], d_ml[i][2], d_norm[i][2], d_ml[i][3], d_ml[i][4] = res[2:]
        return res[0], res[1]

    dx3, dm1 = post_bwd(1, x3, m1, post1_args, dy)
    dh2_1 = mlp_bwd(1, dm1, z1, act1, h2_1)
    dx2a, do1 = mid_bwd(1, x2l, o1, mid1_args, dx3, dh2_1)
    dw_go = chip_sum("gm_out", rows_blocks(_matmul("gm_out_dw", gmix, do1, ta=True, outs=(BF16,))))
    dgmix = _matmul("gm_out_dx", do1, w_go, tb=True)
    gm_res = _rowwise("gm_mix_bwd", _bwd_of(f_gm, 2, 1, (0, 1, 2, 3, 4, 5, 6, 7)), seq, tmb,
                      gm_tiled + [_t(dgmix)], gm_full, [(d_gm, BF16), (d_gm, BF16)],
                      [(1, d_gm)] * 4 + [w_sp.shape, bsp_t.shape])
    dzg = jnp.concatenate([gm_res[0], gm_res[1]], axis=1)
    g_gm_b_in = jnp.concatenate([gm_res[2], gm_res[3]], axis=1)
    g_gm_vg, g_gm_vb, g_w_sp = gm_res[4], gm_res[5], gm_res[6]
    g_b_sp = gm_res[7][:, :gm_groups].T
    dh1, (recv["gm_out"],) = _matmul("gm_in_dx", dzg, w_gi, tb=True, comm=_chips_comm([dw_go]))
    dw_gi = chip_sum("gm_in", cols_blocks(_matmul("gm_in_dw", h1, dzg, ta=True, outs=(BF16,))))

    def f_pre_bwd(pid, xv, dh, dxa, g, sh, sc):
        dxv, dg, dsh, dsc = _bwd_of(_f_pre, 1, 1, (0, 1, 2, 3))(pid, xv, dh, g, sh, sc)
        return (dxv + dxa, dg, dsh, dsc)

    res = _rowwise("pre_bwd1", f_pre_bwd, seq, tmb, [_t(x2l), _t(dh1), _t(dx2a)], pre1_args, [(d, F32)], [(1, d)] * 3)
    dx2l = res[0]
    d_norm[1][0], d_ml[1][0], d_ml[1][1] = res[1:]

    dx1, dm0 = post_bwd(0, x1, m0, post0_args, dx2l)
    dh2_0 = mlp_bwd(0, dm0, z0, act0, h2_0, carried=("gm_in", dw_gi))
    dxa, do0 = mid_bwd(0, x2, o0, mid0_args, dx1, dh2_0)
    dw_out = chip_sum("ar_out", rows_blocks(_matmul("ar_out_dw", ar, do0, ta=True, outs=(BF16,))))
    d_ar = _matmul("ar_out_dx", do0, w_out, tb=True)

    dq, dkt, dvt = _attn_bwd(q_r, kv_r, d_ar, nct, tm)
    dq_all = jnp.concatenate([jnp.zeros((n_ctx, attn_w), F32), dq], axis=0)
    dkv_all = jnp.concatenate([dkt, dvt], axis=0).T
    qkv_res = _rowwise("qkv_bwd", _bwd_of(f_qkv, 4, 2, (0, 1, 4, 5)), t_all, tmb, qkv_tiled + [_t(dq_all), _t(dkv_all)],
                       [q_g, k_g, perm], [(attn_w, BF16), (2 * kv_w, BF16)], [q_g.shape, k_g.shape])
    dproj_q, dproj_kv, g_q_g, g_k_g = qkv_res

    rnn_res = _rowwise("rnn_out_bwd", _bwd_of(_f_rnnout, 3, 1, (0, 2)), seq, tmb, rnn_tiled + [_t(d_ar, 0, 1, d_rnn)], [],
                       [(d_rnn, F32), (d_rnn, BF16)])
    zc = jnp.zeros((n_ctx, d_rnn), F32)
    dh_all = to3(jnp.concatenate([zc, rnn_res[0]], axis=0))
    dproj_g = jnp.concatenate([zc.astype(BF16), rnn_res[1]], axis=0)
    da_f, db_f = _scan_bwd("scan_f_bwd", to3(a_f), dh_all, hp_f, nc_scan, False)
    da_b, db_b = _scan_bwd("scan_b_bwd", to3(a_b), dh_all, hp_b, nc_scan, True)
    gate_cts = [_t(a.reshape(t_all, d_rnn)) for a in (da_f, db_f, da_b, db_b)]
    gates_res = _rowwise("gates_bwd", _bwd_of(f_gates, 1, 4, (0, 1, 2, 3, 4, 5)), t_all, tmb, [_t(xc)] + gate_cts,
                         gate_full, [(d_rnn, F32)], [wa3.shape, ba.shape, wx3.shape, bx.shape, lam.shape])
    dxc, g_wa, g_ba, g_wx, g_bx, g_lam = gates_res
    dys = [_shift_rows(dxc, CONV_W // 2 - j, n_ctx) for j in range(CONV_W)]
    dproj_x, g_conv_w, g_conv_b = _rowwise("conv_bwd", _f_conv_bwd, t_all, tmb, [_t(a) for a in xs + dys], [conv_w],
                                           [(d_rnn, BF16)], [conv_w.shape, (1, d_rnn)])
    dproj = jnp.concatenate([dproj_q, dproj_x, dproj_g, dproj_kv], axis=1)
    dh0, (recv["ar_out"],) = _matmul("ar_in_dx", dproj, w_in, tb=True, tm=tm_tok, comm=_chips_comm([dw_out]))
    sq_pack, off_sq = _pack([g_wa, g_wx, g_w_sp])
    g_w_in, (g_sq,) = _matmul("ar_in_dw", h0, dproj, ta=True, outs=(BF16,), comm=_gather_comm([sq_pack]))
    g_w_in = jnp.concatenate([g_w_in[:, :attn_w], g_w_in[:, 3 * attn_w:], g_w_in[:, attn_w:3 * attn_w]], axis=1)
    dw_in = chip_sum("ar_in", cols_blocks(g_w_in))
    (recv["ar_in"],) = _run_comm("ar_in_chips", _chips_comm([dw_in]))

    dxa_all = jnp.concatenate([jnp.zeros((n_ctx, d), F32), dxa], axis=0)
    f_pre0b = functools.partial(_f_pre_ctx, n_ctx_tiles=nctb)

    def f_pre0_bwd(pid, xv, dh, dxp, g, shc, scc, shl, scl):
        grads = _bwd_of(f_pre0b, 1, 1, (0, 1, 2, 3, 4, 5))(pid, xv, dh, g, shc, scc, shl, scl)
        return (grads[0] + dxp,) + tuple(grads[1:])

    res = _rowwise("pre_bwd0", f_pre0_bwd, t_all, tmb, [_t(xtok), _t(dh0), _t(dxa_all)], pre0_args, [(d, F32)], [(1, d)] * 5)
    grad_x = res[0][n_ctx:][None]
    d_norm[0][0], d_mc_shift, d_mc_scale, d_ml[0][0], d_ml[0][1] = res[1:]

    z1d = jnp.zeros((1, d), F32)
    dml = jnp.concatenate([jnp.concatenate(r, axis=0)[None] for r in d_ml], axis=0)
    dmc = jnp.concatenate([jnp.concatenate([d_mc_shift, d_mc_scale] + [z1d] * 4, axis=0)[None],
                           jnp.zeros((n_layers - 1, 6, d), F32)], axis=0)
    g_norm = jnp.concatenate([jnp.concatenate(r, axis=0)[None] for r in d_norm], axis=0)
    small_parts = [dmc, g_norm, g_q_g, g_k_g, g_conv_w, g_conv_b, g_ba, g_bx, g_lam, g_gm_b_in, g_gm_vg, g_gm_vb, g_b_sp]
    small2, off2 = _pack([dml] + small_parts)
    (gs2,) = _run_comm("gather_small_grads", _gather_comm([small2]))
    dml_all = gs2.reshape(N_DEV, -1)[:, :off2[1]].reshape(N_DEV, n_layers, 6 * d)
    summed = _sum_lead("sum_small_grads", gs2).reshape(-1)
    summed_sq = _sum_lead("sum_square_grads", g_sq).reshape(-1)

    def seg2(k, shape):
        return summed[off2[k + 1]:off2[k + 2]].reshape(shape)

    def seg_sq(k, shape):
        return summed_sq[off_sq[k]:off_sq[k + 1]].reshape(shape)

    dmc_sum = seg2(0, (n_layers, 6 * d))
    dmod_rows = jnp.concatenate([jnp.moveaxis(dml_all, 0, 1), dmc_sum[:, None, :],
                                 jnp.zeros((n_layers, MOD_ROWS - N_DEV - 1, 6 * d), F32)], axis=1)
    g_b_mod = _sum_lead("sum_b_mod", jnp.moveaxis(dmod_rows, 1, 0).reshape(MOD_ROWS, n_layers * 6 * d // LANES, LANES))
    g_b_mod = g_b_mod.reshape(n_layers, 6 * d)
    g_w_mod, ds16 = _mod_bwd(c16, w_mod, _my_cols(dmod_rows, me, n6))
    (g_ds,) = _run_comm("gather_dctx", _gather_comm([ds16[N_DEV].reshape(d // LANES, LANES)]))
    ds_ctx = _sum_lead("sum_dctx", g_ds)
    (g_c_ctx,) = _rowwise("silu_bwd", _f_silu_mul, d // LANES, d // LANES, [_t(c_ctx.reshape(d // LANES, LANES)), _t(ds_ctx)],
                          [], [(LANES, F32)])
    g_c_ctx = g_c_ctx.reshape(d)

    grads = {
        'c_ctx': g_c_ctx, 'b_mod': g_b_mod,
        'norm_g': _my_cols(seg2(1, (n_layers, 4, d)), me, d // N_DEV),
        'ar_q_g': seg2(2, ar_q_g.shape), 'ar_k_g': seg2(3, ar_k_g.shape),
        'ar_conv_w': _my_cols(seg2(4, (1, CONV_W, d_rnn)), me, d_rnn // N_DEV),
        'ar_conv_b': seg2(5, ar_conv_b.shape),
        'ar_ba': _my_cols(seg2(6, (1, 2, d_rnn)), me, d_rnn // N_DEV),
        'ar_bx': _my_cols(seg2(7, (1, 2, d_rnn)), me, d_rnn // N_DEV),
        'ar_lambda': _my_cols(seg2(8, (1, 2, d_rnn)), me, d_rnn // N_DEV),
        'gm_b_in': _my_cols(seg2(9, (1, 2 * d_gm)), me, 2 * d_gm // N_DEV),
        'gm_v_g': _my_cols(seg2(10, (1, d_gm)), me, d_gm // N_DEV),
        'gm_v_b': _my_cols(seg2(11, (1, d_gm)), me, d_gm // N_DEV),
        'gm_b_sp': seg2(12, gm_b_sp.shape),
        'ar_wa': seg_sq(0, ar_wa.shape), 'ar_wx': seg_sq(1, ar_wx.shape), 'gm_w_sp': seg_sq(2, gm_w_sp.shape),
    }
    deltas, new_m, new_v = {}, {}, {}

    small_names = list(grads)
    wp, offw = _pack([wts[n] for n in small_names])
    mp, _ = _pack([mom1[n] for n in small_names])
    vp, _ = _pack([mom2[n] for n in small_names])
    gp, _ = _pack([grads[n] for n in small_names])
    dp, mp2, vp2 = _adam_f32("adam_small", gp, wp, mp, vp)
    for k, n in enumerate(small_names):
        for dst, slab in ((deltas, dp), (new_m, mp2), (new_v, vp2)):
            dst[n] = slab.reshape(-1)[offw[k]:offw[k + 1]].reshape(wts[n].shape)

    grads['w_mod'] = g_w_mod
    dw, mw, vw = _adam_f32("adam_w_mod", g_w_mod.reshape(n_layers * d, n6), w_mod.reshape(n_layers * d, n6),
                           m_w_mod.reshape(n_layers * d, n6), v_w_mod.reshape(n_layers * d, n6))
    deltas['w_mod'], new_m['w_mod'], new_v['w_mod'] = (a.reshape(w_mod.shape) for a in (dw, mw, vw))

    received = {
        'w_ff_in': jnp.concatenate([recv[f"ff_in_{i}"] for i in range(n_layers)], axis=1),
        'w_ff_out': jnp.concatenate([recv[f"ff_out_{i}"] for i in range(n_layers)], axis=1),
        'ar_w_in': recv["ar_in"], 'ar_w_out': recv["ar_out"], 'gm_w_in': recv["gm_in"], 'gm_w_out': recv["gm_out"]}
    for n, r in received.items():
        shp = wts[n].shape
        flat = (shp[0] * shp[1], shp[2])
        res = _adam_recv("adam_" + n, r, wts[n].reshape(flat), mom1[n].reshape(flat), mom2[n].reshape(flat))
        grads[n], deltas[n], new_m[n], new_v[n] = (a.reshape(shp) for a in res)

    return (loss, grad_x, *[grads[n] for n in WEIGHTS], *[deltas[n] for n in WEIGHTS],
            *[new_m[n] for n in WEIGHTS], *[new_v[n] for n in WEIGHTS])
```

```python
import functools

import numpy as np
import jax
import jax.numpy as jnp
from jax import lax
from jax.experimental import pallas as pl
from jax.experimental.pallas import tpu as pltpu

F32 = jnp.float32
BF16 = jnp.bfloat16
HIGHEST = lax.Precision.HIGHEST

GRID_W = 64
N_HEADS = 8
N_KV_HEADS = 2
HEAD_DIM = 128
ROPE_THETA = 10000.0
RNN_BLOCK_W = 128
CONV_W = 4
RG_C = 8.0
GM_GROUP_W = 128
CHUNK = 128
EPS = 1e-6
ADAM_LR = 0.001
ADAM_B1 = 0.9
ADAM_B2 = 0.999
ADAM_EPS = 1e-08
ADAM_WD = 0.01
ADAM_STEP = 10

N_DEV = 8
MOD_ROWS = 16
LANES = 128
ROW_TILE = 256
ROW_TILE_BWD = 256
SCAN_BLOCK = 256
VMEM_LIMIT = 56 * 1024 * 1024
MM_TILE_M = 512
MM_TILE_N = 1024
MM_TILE_K = 2048

WEIGHTS = ['c_ctx', 'w_mod', 'b_mod', 'norm_g', 'w_ff_in', 'w_ff_out', 'ar_w_in', 'ar_q_g', 'ar_k_g', 'ar_conv_w',
           'ar_conv_b', 'ar_wa', 'ar_ba', 'ar_wx', 'ar_bx', 'ar_lambda', 'ar_w_out', 'gm_w_in', 'gm_b_in', 'gm_v_g',
           'gm_v_b', 'gm_w_sp', 'gm_b_sp', 'gm_w_out']


def _sds(shape, dtype):
    return jax.ShapeDtypeStruct(tuple(shape), dtype)


def _tile(dim, pref, align):
    t = (min(pref, dim) // align) * align
    while t >= align:
        if dim % t == 0:
            return t
        t -= align
    return dim


def _params(sem):
    return pltpu.CompilerParams(dimension_semantics=sem, vmem_limit_bytes=VMEM_LIMIT)


def _rms(x, g):
    return x * lax.rsqrt(jnp.mean(x * x, axis=-1, keepdims=True) + EPS) * g


def _gelu(x):
    return 0.5 * x * (1.0 + jnp.tanh(0.7978845608028654 * (x + 0.044715 * (x * x * x))))


def _sigmoid(x):
    return 0.5 * (jnp.tanh(0.5 * x) + 1.0)


def _log1p_pos(u):
    small = u < 1e-3
    us = jnp.where(small, u, 0.0)
    return jnp.where(small, us * (1.0 - us * (0.5 - us * (1.0 / 3.0))), jnp.log(1.0 + u))


def _softplus(x):
    return jnp.maximum(x, 0.0) + _log1p_pos(jnp.exp(-jnp.abs(x)))


def _expm1(x):
    small = jnp.abs(x) < 0.3
    xs = jnp.where(small, x, 0.0)
    poly = xs * (1.0 + xs * (1.0 / 2 + xs * (1.0 / 6 + xs * (1.0 / 24 + xs * (1.0 / 120 + xs * (1.0 / 720 + xs * (1.0 / 5040)))))))
    return jnp.where(small, poly, jnp.exp(x) - 1.0)


def _f_pre_ctx(pid, x, g, shc, scc, shl, scl, *, n_ctx_tiles):
    is_ctx = pid < n_ctx_tiles
    sh = jnp.where(is_ctx, shc, shl)
    sc = jnp.where(is_ctx, scc, scl)
    return (_rms(x, g) * (1.0 + sc) + sh,)


def _f_pre(pid, x, g, sh, sc):
    return (_rms(x, g) * (1.0 + sc) + sh,)


def _f_mid(pid, x, o, g1, gate, g2, sh, sc):
    x1 = x + gate * _rms(o, g1)
    return (x1, _rms(x1, g2) * (1.0 + sc) + sh)


def _f_post(pid, x, o, g, gate):
    return (x + gate * _rms(o, g),)


def _f_qkv(pid, pq, pkv, cos, sin, q_g, k_g, perm, *, nh, nkv):
    hd = HEAD_DIM

    def norm_rope(xh, g):
        y = _rms(xh, g)
        return y * cos + jnp.dot(y, perm, precision=HIGHEST, preferred_element_type=F32) * sin

    qs = [norm_rope(pq[:, h * hd:(h + 1) * hd], q_g) for h in range(nh)]
    ks = [norm_rope(pkv[:, h * hd:(h + 1) * hd], k_g) for h in range(nkv)]
    return (jnp.concatenate(qs, axis=1), jnp.concatenate(ks + [pkv[:, nkv * hd:]], axis=1))


def _f_gates(pid, x, wa, ba, wx, bx, lam, *, nb):
    w = RNN_BLOCK_W
    outs = []
    for d in range(2):
        ra, ri = [], []
        for n in range(nb):
            xn = x[:, n * w:(n + 1) * w].astype(BF16)
            ra.append(jnp.dot(xn, wa[d * nb + n].astype(BF16), preferred_element_type=F32))
            ri.append(jnp.dot(xn, wx[d * nb + n].astype(BF16), preferred_element_type=F32))
        r = _sigmoid(jnp.concatenate(ra, axis=1) + ba[d:d + 1])
        i = _sigmoid(jnp.concatenate(ri, axis=1) + bx[d:d + 1])
        log_a = -RG_C * r * _softplus(-lam[d:d + 1])
        outs.append(jnp.exp(log_a))
        outs.append(jnp.sqrt(-_expm1(2.0 * log_a)) * (i * x))
    return tuple(outs)


def _f_rnnout(pid, hf, hb, gr):
    return ((hf + hb) * _gelu(gr),)


def _f_gm(pid, zu, zv, bu, bv, v_g, v_b, w_sp, bsp_t, expand, *, n_chunks, groups):
    u = _gelu(zu + bu)
    v = _gelu(zv + bv)
    mu = jnp.mean(v, axis=-1, keepdims=True)
    vc = v - mu
    v = vc * lax.rsqrt(jnp.mean(vc * vc, axis=-1, keepdims=True) + EPS) * v_g + v_b
    bias = jnp.dot(bsp_t, expand, precision=HIGHEST, preferred_element_type=F32)
    outs = []
    for c in range(n_chunks):
        vch = v[c * CHUNK:(c + 1) * CHUNK]
        cols = [jnp.dot(w_sp[g].astype(BF16), vch[:, g * GM_GROUP_W:(g + 1) * GM_GROUP_W].astype(BF16),
                        preferred_element_type=F32) for g in range(groups)]
        outs.append(u[c * CHUNK:(c + 1) * CHUNK] * (jnp.concatenate(cols, axis=1) + bias))
    return (jnp.concatenate(outs, axis=0),)


def _f_silu_mul(pid, c, d):
    return (d * jax.grad(lambda z: jnp.sum(z * _sigmoid(z)))(c),)


def _bwd_of(fn, n_tiled, n_ct, want):
    def bwd(pid, *args):
        tiles = [t.astype(F32) for t in args[:n_tiled]]
        cts = args[n_tiled:n_tiled + n_ct]
        fulls = list(args[n_tiled + n_ct:])
        outs, vjp = jax.vjp(lambda *a: fn(pid, *a), *tiles, *fulls)
        grads = vjp(tuple(ct.astype(o.dtype) for ct, o in zip(cts, outs)))
        return tuple(grads[i] for i in want)
    return bwd


def _rowwise(name, fn, rows, tm, tiled, full, outs, accs=()):
    n_t, n_f, n_o, n_a = len(tiled), len(full), len(outs), len(accs)

    def body(*refs):
        pid = pl.program_id(0)
        vals = [r[...] for r in refs[:n_t + n_f]]
        res = fn(pid, *vals)
        o_refs = refs[n_t + n_f:n_t + n_f + n_o]
        a_refs = refs[n_t + n_f + n_o:]
        for r, v in zip(o_refs, res[:n_o]):
            r[...] = v.astype(r.dtype)
        if n_a:
            @pl.when(pid == 0)
            def _():
                for r in a_refs:
                    r[...] = jnp.zeros_like(r)
            for r, v in zip(a_refs, res[n_o:]):
                r[...] += v.astype(F32)

    assert all(ro % tm == 0 for (_, ro, _, _) in tiled)
    in_specs = [pl.BlockSpec((tm, w), lambda i, ro=ro // tm, cb=cb, last=a.shape[0] // tm - 1: (jnp.clip(i + ro, 0, last), cb))
                for (a, ro, cb, w) in tiled]
    in_specs += [pl.BlockSpec(a.shape, lambda i, nd=a.ndim: (0,) * nd) for a in full]
    out_shape = [_sds((rows, w), dt) for (w, dt) in outs] + [_sds(s, F32) for s in accs]
    out_specs = [pl.BlockSpec((tm, w), lambda i: (i, 0)) for (w, _) in outs]
    out_specs += [pl.BlockSpec(tuple(s), lambda i, nd=len(s): (0,) * nd) for s in accs]
    return pl.pallas_call(body, grid=(rows // tm,), in_specs=in_specs, out_specs=out_specs, out_shape=out_shape, name=name,
                          compiler_params=_params(("arbitrary",)))(*[t[0] for t in tiled], *full)


def _t(a, row_off=0, col_blk=0, width=None):
    return (a, row_off, col_blk, a.shape[1] if width is None else width)


class _Comm:
    def __init__(self, ins, out_shapes, n_sems, start, finish):
        self.ins, self.out_shapes, self.n_sems, self.start, self.finish = list(ins), list(out_shapes), n_sems, start, finish

    def scratch(self):
        return [pltpu.SemaphoreType.DMA((self.n_sems,)), pltpu.SemaphoreType.DMA((self.n_sems,)),
                pltpu.SemaphoreType.DMA((len(self.ins),))]


_ANY = pl.BlockSpec(memory_space=pl.ANY)


class _SemSlice:
    def __init__(self, ref, first):
        self.ref, self.first = ref, first

    @property
    def at(self):
        return self

    def __getitem__(self, k):
        return self.ref.at[self.first + k]


def _both(*comms):
    comms = [cm for cm in comms if cm is not None]
    if len(comms) <= 1:
        return comms[0] if comms else None

    def parts(ins, outs, send, recv, local):
        i0 = o0 = s0 = 0
        for cm in comms:
            ni, no = len(cm.ins), len(cm.out_shapes)
            yield cm, (ins[i0:i0 + ni], outs[o0:o0 + no], _SemSlice(send, s0), _SemSlice(recv, s0), _SemSlice(local, i0))
            i0, o0, s0 = i0 + ni, o0 + no, s0 + cm.n_sems

    def start(*refs):
        for cm, sub in parts(*refs):
            cm.start(*sub)

    def finish(*refs):
        for cm, sub in parts(*refs):
            cm.finish(*sub)

    return _Comm(sum((cm.ins for cm in comms), []), sum((cm.out_shapes for cm in comms), []),
                 sum(cm.n_sems for cm in comms), start, finish)


def _row_block(cols, itemsize):
    return max(16, (1 << 20) // (cols * itemsize))


def _split_refs(refs, n_in, n_out, n_scratch, comm):
    ci, co, cs = (len(comm.ins), len(comm.out_shapes), 3) if comm is not None else (0, 0, 0)
    cuts = np.cumsum([0, n_in, ci, n_out, co, n_scratch, cs])
    return [refs[cuts[i]:cuts[i + 1]] for i in range(6)]


def _matmul(name, a, b, *, ta=False, tb=False, outs=((F32,)), epilogue=None, extras=(), tm=None, tn=None, tk=None, comm=None,
            col_blocks=None):
    m, k = (a.shape[1], a.shape[0]) if ta else a.shape
    n = b.shape[0] if tb else b.shape[1]
    tm = _tile(m, tm or MM_TILE_M, 128 if ta else 16)
    tn = _tile(n if col_blocks is None else n // col_blocks, tn or MM_TILE_N, 128)
    tk = _tile(k, tk or MM_TILE_K, 128 if not ta else 16)
    ni, nj, nk = m // tm, n // tn, k // tk
    n_e, n_o = len(extras), len(outs)
    dims = (((0 if ta else 1,), (1 if tb else 0,)), ((), ()))

    def body(*refs):
        ins, c_ins, o_refs, c_outs, scratch, c_sems = _split_refs(refs, 2 + n_e, n_o, 1 if nk > 1 else 0, comm)
        a_ref, b_ref, e_refs = ins[0], ins[1], ins[2:]
        i, j, kk = pl.program_id(0), pl.program_id(1), pl.program_id(2)
        if comm is not None:
            @pl.when(jnp.logical_and(jnp.logical_and(i == 0, j == 0), kk == 0))
            def _():
                comm.start(c_ins, c_outs, *c_sems)

        def finish(acc):
            res = (acc,) if epilogue is None else epilogue(acc, *[e[...] for e in e_refs])
            for r, v in zip(o_refs, res):
                r[...] = v.astype(r.dtype)

        prod = lax.dot_general(a_ref[...].astype(BF16), b_ref[...].astype(BF16), dims, preferred_element_type=F32)
        if nk == 1:
            finish(prod)
        else:
            acc = scratch[0]

            @pl.when(kk == 0)
            def _():
                acc[...] = prod

            @pl.when(kk > 0)
            def _():
                acc[...] += prod

            @pl.when(kk == nk - 1)
            def _():
                finish(acc[...])
        if comm is not None:
            @pl.when(jnp.logical_and(jnp.logical_and(i == ni - 1, j == nj - 1), kk == nk - 1))
            def _():
                comm.finish(c_ins, c_outs, *c_sems)

    a_spec = pl.BlockSpec((tk, tm), lambda i, j, kk: (kk, i)) if ta else pl.BlockSpec((tm, tk), lambda i, j, kk: (i, kk))
    b_spec = pl.BlockSpec((tn, tk), lambda i, j, kk: (j, kk)) if tb else pl.BlockSpec((tk, tn), lambda i, j, kk: (kk, j))
    mn_spec = pl.BlockSpec((tm, tn), lambda i, j, kk: (i, j))
    c_in, c_out, c_scr = (comm.ins, comm.out_shapes, comm.scratch()) if comm is not None else ([], [], [])
    if col_blocks is None:
        o_spec, o_shape = mn_spec, (m, n)
    else:
        per = n // col_blocks // tn
        o_spec = pl.BlockSpec((None, tm, tn), lambda i, j, kk: (j // per, i, j % per))
        o_shape = (col_blocks, m, n // col_blocks)
    res = pl.pallas_call(body, grid=(ni, nj, nk), in_specs=[a_spec, b_spec] + [mn_spec] * n_e + [_ANY] * len(c_in),
                         out_specs=[o_spec] * n_o + [_ANY] * len(c_out),
                         out_shape=[_sds(o_shape, dt) for dt in outs] + list(c_out),
                         scratch_shapes=([pltpu.VMEM((tm, tn), F32)] if nk > 1 else []) + c_scr, name=name,
                         compiler_params=_params(("arbitrary", "arbitrary", "arbitrary")))(a, b, *extras, *c_in)
    main = res[0] if n_o == 1 else res[:n_o]
    return main if comm is None else (main, res[n_o:])


def _attn_fwd(q, kv, n_ctx_tiles, tq, comm=None):
    t_all = q.shape[0]
    s_len = t_all - n_ctx_tiles * tq
    hd, groups = HEAD_DIM, N_HEADS // N_KV_HEADS
    scale = HEAD_DIM ** -0.5
    nq = s_len // tq

    def body(*refs):
        (q_ref, k_ref, v_ref), c_ins, (o_ref,), c_outs, _, c_sems = _split_refs(refs, 3, 1, 0, comm)
        kh, g, i = pl.program_id(0), pl.program_id(1), pl.program_id(2)
        if comm is not None:
            @pl.when(jnp.logical_and(jnp.logical_and(kh == 0, g == 0), i == 0))
            def _():
                comm.start(c_ins, c_outs, *c_sems)

        s = lax.dot_general(q_ref[...], k_ref[...], (((1,), (1,)), ((), ())), preferred_element_type=F32) * scale
        p = jnp.exp(s - jnp.max(s, axis=-1, keepdims=True))
        l = jnp.sum(p, axis=-1, keepdims=True)
        o = jnp.dot(p.astype(BF16), v_ref[...], preferred_element_type=F32) / l
        o_ref[...] = o.astype(o_ref.dtype)
        if comm is not None:
            @pl.when(jnp.logical_and(jnp.logical_and(kh == N_KV_HEADS - 1, g == groups - 1), i == nq - 1))
            def _():
                comm.finish(c_ins, c_outs, *c_sems)

    c_in, c_out, c_scr = (comm.ins, comm.out_shapes, comm.scratch()) if comm is not None else ([], [], [])
    res = pl.pallas_call(
        body, grid=(N_KV_HEADS, groups, nq),
        in_specs=[pl.BlockSpec((tq, hd), lambda kh, g, i: (i + n_ctx_tiles, kh * groups + g)),
                  pl.BlockSpec((t_all, hd), lambda kh, g, i: (0, kh)),
                  pl.BlockSpec((t_all, hd), lambda kh, g, i: (0, N_KV_HEADS + kh))] + [_ANY] * len(c_in),
        out_specs=[pl.BlockSpec((tq, hd), lambda kh, g, i: (i, kh * groups + g))] + [_ANY] * len(c_out),
        out_shape=[_sds((s_len, N_HEADS * hd), BF16)] + list(c_out), scratch_shapes=c_scr, name="attn_fwd",
        compiler_params=_params(("arbitrary", "arbitrary", "arbitrary")))(q, kv, kv, *c_in)
    return res[0] if comm is None else (res[0], res[1:])


def _attn_bwd(q, kv, d_ar, n_ctx_tiles, tq, comm=None):
    t_all = q.shape[0]
    s_len = t_all - n_ctx_tiles * tq
    hd, groups = HEAD_DIM, N_HEADS // N_KV_HEADS
    scale = HEAD_DIM ** -0.5
    nq = s_len // tq

    def body(*refs):
        (q_ref, k_ref, v_ref, do_ref), c_ins, (dq_ref, dkt_ref, dvt_ref), c_outs, _, c_sems = _split_refs(refs, 4, 3, 0, comm)
        first = jnp.logical_and(pl.program_id(1) == 0, pl.program_id(2) == 0)
        if comm is not None:
            @pl.when(jnp.logical_and(first, pl.program_id(0) == 0))
            def _():
                comm.start(c_ins, c_outs, *c_sems)

        @pl.when(first)
        def _():
            dkt_ref[...] = jnp.zeros_like(dkt_ref)
            dvt_ref[...] = jnp.zeros_like(dvt_ref)

        qv, kk, vv = q_ref[...], k_ref[...], v_ref[...]
        do = do_ref[...].astype(BF16)
        s = lax.dot_general(qv, kk, (((1,), (1,)), ((), ())), preferred_element_type=F32) * scale
        p = jnp.exp(s - jnp.max(s, axis=-1, keepdims=True))
        p = p / jnp.sum(p, axis=-1, keepdims=True)
        dp = lax.dot_general(do, vv, (((1,), (1,)), ((), ())), preferred_element_type=F32)
        ds = (p * (dp - jnp.sum(p * dp, axis=-1, keepdims=True)) * scale).astype(BF16)
        dq_ref[...] = jnp.dot(ds, kk, preferred_element_type=F32)
        dkt_ref[...] += jnp.dot(qv.T, ds, preferred_element_type=F32)
        dvt_ref[...] += jnp.dot(do.T, p.astype(BF16), preferred_element_type=F32)
        if comm is not None:
            last = jnp.logical_and(pl.program_id(1) == groups - 1, pl.program_id(2) == nq - 1)

            @pl.when(jnp.logical_and(last, pl.program_id(0) == N_KV_HEADS - 1))
            def _():
                comm.finish(c_ins, c_outs, *c_sems)

    c_in, c_out, c_scr = (comm.ins, comm.out_shapes, comm.scratch()) if comm is not None else ([], [], [])
    return pl.pallas_call(
        body, grid=(N_KV_HEADS, groups, nq),
        in_specs=[pl.BlockSpec((tq, hd), lambda kh, g, i: (i + n_ctx_tiles, kh * groups + g)),
                  pl.BlockSpec((t_all, hd), lambda kh, g, i: (0, kh)),
                  pl.BlockSpec((t_all, hd), lambda kh, g, i: (0, N_KV_HEADS + kh)),
                  pl.BlockSpec((tq, hd), lambda kh, g, i: (i, kh * groups + g))] + [_ANY] * len(c_in),
        out_specs=[pl.BlockSpec((tq, hd), lambda kh, g, i: (i, kh * groups + g)),
                   pl.BlockSpec((hd, t_all), lambda kh, g, i: (kh, 0)),
                   pl.BlockSpec((hd, t_all), lambda kh, g, i: (kh, 0))] + [_ANY] * len(c_out),
        out_shape=[_sds((s_len, N_HEADS * hd), F32), _sds((N_KV_HEADS * hd, t_all), F32),
                   _sds((N_KV_HEADS * hd, t_all), F32)] + list(c_out),
        scratch_shapes=c_scr, name="attn_bwd",
        compiler_params=_params(("arbitrary", "arbitrary", "arbitrary")))(q, kv, kv, d_ar, *c_in)


def _scan_order(nb, nc, reverse):
    if not reverse:
        return lambda i: i
    return lambda i: jnp.where(i < nc, nc - 1 - i, nb - 1 - (i - nc))


def _scan_fwd(name, a, b, nc, reverse):
    t_all, r, l = a.shape
    tb = SCAN_BLOCK
    nb = t_all // tb
    order = _scan_order(nb, nc, reverse)

    def body(a_ref, b_ref, h_ref, hp_ref, carry):
        @pl.when(pl.program_id(0) == 0)
        def _():
            carry[...] = jnp.zeros_like(carry)

        def step(s, h):
            t = tb - 1 - s if reverse else s
            hp_ref[t] = h
            h = a_ref[t] * h + b_ref[t]
            h_ref[t] = h
            return h

        carry[...] = lax.fori_loop(0, tb, step, carry[...], unroll=8)

    spec = pl.BlockSpec((tb, r, l), lambda i: (order(i), 0, 0))
    return pl.pallas_call(body, grid=(nb,), in_specs=[spec, spec], out_specs=[spec, spec],
                          out_shape=[_sds(a.shape, F32)] * 2, scratch_shapes=[pltpu.VMEM((r, l), F32)], name=name,
                          compiler_params=_params(("arbitrary",)))(a, b)


def _scan_bwd(name, a, dh, hp, nc, reverse):
    t_all, r, l = a.shape
    tb = SCAN_BLOCK
    nb = t_all // tb
    primal = _scan_order(nb, nc, reverse)

    def order(i):
        return primal(nb - 1 - i)

    def body(a_ref, dh_ref, hp_ref, da_ref, db_ref, carry):
        @pl.when(pl.program_id(0) == 0)
        def _():
            carry[...] = jnp.zeros_like(carry)

        def step(s, cr):
            t = s if reverse else tb - 1 - s
            lam = dh_ref[t] + cr
            db_ref[t] = lam
            da_ref[t] = lam * hp_ref[t]
            return a_ref[t] * lam

        carry[...] = lax.fori_loop(0, tb, step, carry[...], unroll=8)

    spec = pl.BlockSpec((tb, r, l), lambda i: (order(i), 0, 0))
    return pl.pallas_call(body, grid=(nb,), in_specs=[spec] * 3, out_specs=[spec, spec],
                          out_shape=[_sds(a.shape, F32)] * 2, scratch_shapes=[pltpu.VMEM((r, l), F32)], name=name,
                          compiler_params=_params(("arbitrary",)))(a, dh, hp)


def _shifted(prev, cur, nxt, k, pid, n_ctx_tiles, n_tiles):
    if k == 0:
        return cur
    tm = cur.shape[0]
    row = lax.broadcasted_iota(jnp.int32, cur.shape, 0)
    if k < 0:
        at_start = jnp.logical_or(pid == 0, pid == n_ctx_tiles)
        edge = jnp.where(at_start, 0.0, pltpu.roll(prev, -k, 0))
        return jnp.where(row < -k, edge, pltpu.roll(cur, -k, 0))
    at_end = jnp.logical_or(pid == n_ctx_tiles - 1, pid == n_tiles - 1)
    edge = jnp.where(at_end, 0.0, pltpu.roll(nxt, tm - k, 0))
    return jnp.where(row >= tm - k, edge, pltpu.roll(cur, tm - k, 0))


def _f_conv(pid, xp, xc, xn, w, b, *, n_ctx_tiles, n_tiles):
    y = b
    for j in range(CONV_W):
        y = y + _shifted(xp, xc, xn, j - CONV_W // 2, pid, n_ctx_tiles, n_tiles) * w[j:j + 1]
    return (y,)


def _f_conv_bwd(pid, xp, xc, xn, dp, dc, dn, w, *, n_ctx_tiles, n_tiles):
    dx = jnp.zeros_like(dc)
    dw = []
    for j in range(CONV_W):
        k = j - CONV_W // 2
        dx = dx + _shifted(dp, dc, dn, -k, pid, n_ctx_tiles, n_tiles) * w[j:j + 1]
        dw.append(jnp.sum(dc * _shifted(xp, xc, xn, k, pid, n_ctx_tiles, n_tiles), axis=0, keepdims=True))
    return (dx, jnp.concatenate(dw, axis=0), jnp.sum(dc, axis=0, keepdims=True))


def _mesh_pos():
    return lax.axis_index("x"), lax.axis_index("y"), lax.axis_index("c")


def _remote(src, dst, send_sems, recv_sems, k, to):
    return pltpu.make_async_remote_copy(src_ref=src, dst_ref=dst, send_sem=send_sems.at[k], recv_sem=recv_sems.at[k],
                                        device_id=to, device_id_type=pl.DeviceIdType.MESH)


def _neighbours():
    x, y, c = _mesh_pos()
    return (x, y, c), (x, y, 1 - c), [(1 - x, y), (x, 1 - y), (1 - x, 1 - y)]


def _gather_comm(arrays):
    n = len(arrays)
    per = 7

    def slot(out, blk):
        return out.at[4 * blk[0] + 2 * blk[1] + blk[2]]

    def start(ins, outs, send, recv, local):
        me, sib, chips = _neighbours()
        for ai in range(n):
            pltpu.make_async_copy(ins[ai], slot(outs[ai], me), local.at[ai]).start()
            _remote(ins[ai], slot(outs[ai], me), send, recv, ai * per, sib).start()
            for j, chip in enumerate(chips):
                _remote(ins[ai], slot(outs[ai], me), send, recv, ai * per + 1 + j, (*chip, me[2])).start()

    def finish(ins, outs, send, recv, local):
        me, sib, chips = _neighbours()
        for ai in range(n):
            for j, chip in enumerate(chips):
                blk = slot(outs[ai], (*chip, me[2]))
                _remote(blk, blk, send, recv, ai * per + 1 + j, me).wait_recv()
                _remote(blk, blk, send, recv, ai * per + 4 + j, sib).start()
        for ai in range(n):
            blk = slot(outs[ai], sib)
            _remote(blk, blk, send, recv, ai * per, me).wait_recv()
            for j, chip in enumerate(chips):
                blk = slot(outs[ai], (*chip, 1 - me[2]))
                _remote(blk, blk, send, recv, ai * per + 4 + j, me).wait_recv()
            for k in range(per):
                _remote(ins[ai], slot(outs[ai], me), send, recv, ai * per + k, sib).wait_send()
            pltpu.make_async_copy(ins[ai], slot(outs[ai], me), local.at[ai]).wait()

    return _Comm(arrays, [_sds((N_DEV,) + a.shape, a.dtype) for a in arrays], n * per, start, finish)


def _swap_comm(arrays):
    n = len(arrays)

    def start(ins, outs, send, recv, local):
        me, sib, _ = _neighbours()
        for ai in range(n):
            for q in range(4):
                _remote(ins[ai].at[2 * q + 1 - me[2]], outs[ai].at[q], send, recv, ai * 4 + q, sib).start()

    def finish(ins, outs, send, recv, local):
        me, sib, _ = _neighbours()
        for ai in range(n):
            for q in range(4):
                cp = _remote(ins[ai].at[q], outs[ai].at[q], send, recv, ai * 4 + q, sib)
                cp.wait_recv()
                cp.wait_send()

    return _Comm(arrays, [_sds((4,) + a.shape[1:], a.dtype) for a in arrays], n * 4, start, finish)


def _chips_comm(arrays):
    n = len(arrays)

    def start(ins, outs, send, recv, local):
        me, _, chips = _neighbours()
        mine = 2 * me[0] + me[1]
        for ai in range(n):
            pltpu.make_async_copy(ins[ai].at[mine], outs[ai].at[mine], local.at[ai]).start()
            for j, chip in enumerate(chips):
                _remote(ins[ai].at[2 * chip[0] + chip[1]], outs[ai].at[mine], send, recv, ai * 3 + j, (*chip, me[2])).start()

    def finish(ins, outs, send, recv, local):
        me, _, chips = _neighbours()
        mine = 2 * me[0] + me[1]
        for ai in range(n):
            for j, chip in enumerate(chips):
                theirs = 2 * chip[0] + chip[1]
                cp = _remote(ins[ai].at[theirs], outs[ai].at[theirs], send, recv, ai * 3 + j, (*chip, me[2]))
                cp.wait_recv()
                cp.wait_send()
            pltpu.make_async_copy(ins[ai].at[mine], outs[ai].at[mine], local.at[ai]).wait()

    return _Comm(arrays, [_sds(a.shape, a.dtype) for a in arrays], n * 3, start, finish)


def _run_comm(name, comm):
    n_in, n_out = len(comm.ins), len(comm.out_shapes)

    def body(*refs):
        ins, outs, sems = refs[:n_in], refs[n_in:n_in + n_out], refs[n_in + n_out:]
        comm.start(ins, outs, *sems)
        comm.finish(ins, outs, *sems)

    return pl.pallas_call(body, in_specs=[_ANY] * n_in, out_specs=[_ANY] * n_out, out_shape=comm.out_shapes, name=name,
                          scratch_shapes=comm.scratch(), compiler_params=pltpu.CompilerParams(has_side_effects=True))(*comm.ins)


def _chip_add(name, blocks, theirs, core):
    _, r, c = blocks.shape
    tr = _tile(r, _row_block(c, 2), 16)

    def body(core_ref, a_ref, b_ref, o_ref):
        o_ref[...] = (a_ref[...].astype(F32) + b_ref[...].astype(F32)).astype(o_ref.dtype)

    spec = pl.BlockSpec((None, tr, c), lambda q, i, core_ref: (q, i, 0))
    grid_spec = pltpu.PrefetchScalarGridSpec(
        num_scalar_prefetch=1, grid=(4, r // tr),
        in_specs=[pl.BlockSpec((None, tr, c), lambda q, i, core_ref: (2 * q + core_ref[0], i, 0)), spec], out_specs=spec)
    return pl.pallas_call(body, grid_spec=grid_spec, out_shape=_sds((4, r, c), blocks.dtype), name=name,
                          compiler_params=_params(("parallel", "parallel")))(jnp.reshape(core, (1,)).astype(jnp.int32), blocks, theirs)


def _sum_lead(name, a):
    n, r, c = a.shape
    tr = _tile(r, _row_block(c, 4 * n // 2), 8)

    def body(a_ref, o_ref):
        acc = a_ref[0]
        for j in range(1, n):
            acc = acc + a_ref[j]
        o_ref[...] = acc

    return pl.pallas_call(body, grid=(r // tr,), in_specs=[pl.BlockSpec((n, tr, c), lambda i: (0, i, 0))],
                          out_specs=pl.BlockSpec((tr, c), lambda i: (i, 0)), out_shape=_sds((r, c), F32), name=name,
                          compiler_params=_params(("parallel",)))(a)


def _adam_math(w, g, m, v):
    m = ADAM_B1 * m + (1.0 - ADAM_B1) * g
    v = ADAM_B2 * v + (1.0 - ADAM_B2) * (g * g)
    m_hat = m / (1.0 - ADAM_B1 ** ADAM_STEP)
    v_hat = v / (1.0 - ADAM_B2 ** ADAM_STEP)
    delta = -ADAM_LR * (m_hat / (jnp.sqrt(v_hat) + ADAM_EPS) + ADAM_WD * w)
    return delta, m, v


def _adam_recv(name, recvs, w, m, v):
    nl = len(recvs)
    n, rl, c = recvs[0].shape
    tr = _tile(rl, _row_block(c, 4), 16)
    per = rl // tr

    def body(*refs):
        g_refs = refs[:nl]
        w_ref, m_ref, v_ref, go_ref, d_ref, mo_ref, vo_ref = refs[nl:]
        for layer in range(nl):
            @pl.when(pl.program_id(0) == layer)
            def _(g_ref=g_refs[layer]):
                g = g_ref[0].astype(F32)
                for j in range(1, n):
                    g = g + g_ref[j].astype(F32)
                delta, m2, v2 = _adam_math(w_ref[...], g, m_ref[...], v_ref[...])
                go_ref[...] = g
                d_ref[...] = delta
                mo_ref[...] = m2
                vo_ref[...] = v2

    g_specs = [pl.BlockSpec((n, tr, c), lambda l, i, layer=layer: (0, jnp.where(l == layer, i, 0), 0)) for layer in range(nl)]
    spec = pl.BlockSpec((tr, c), lambda l, i: (l * per + i, 0))
    return pl.pallas_call(body, grid=(nl, per), in_specs=g_specs + [spec] * 3, out_specs=[spec] * 4,
                          out_shape=[_sds((nl * rl, c), F32)] * 4, name=name,
                          compiler_params=_params(("arbitrary", "arbitrary")))(*recvs, w, m, v)


def _adam_f32(name, g, w, m, v):
    r, c = g.shape
    tr = _tile(r, _row_block(c, 4), 8)

    def body(g_ref, w_ref, m_ref, v_ref, d_ref, mo_ref, vo_ref):
        delta, m2, v2 = _adam_math(w_ref[...], g_ref[...], m_ref[...], v_ref[...])
        d_ref[...] = delta
        mo_ref[...] = m2
        vo_ref[...] = v2

    spec = pl.BlockSpec((tr, c), lambda i: (i, 0))
    return pl.pallas_call(body, grid=(r // tr,), in_specs=[spec] * 4, out_specs=[spec] * 3,
                          out_shape=[_sds((r, c), F32)] * 3, name=name, compiler_params=_params(("parallel",)))(g, w, m, v)


def _mod_fwd(c16, w_mod, b_loc):
    nl, d, n6 = w_mod.shape
    tn = _tile(n6, 512, 128)

    def body(c_ref, w_ref, b_ref, o_ref):
        cv = c_ref[...]
        s = cv * _sigmoid(cv)
        o_ref[0] = jnp.dot(s, w_ref[0], precision=HIGHEST, preferred_element_type=F32) + b_ref[0]

    return pl.pallas_call(
        body, grid=(nl, n6 // tn),
        in_specs=[pl.BlockSpec((MOD_ROWS, d), lambda i, j: (0, 0)), pl.BlockSpec((1, d, tn), lambda i, j: (i, 0, j)),
                  pl.BlockSpec((1, 1, tn), lambda i, j: (i, 0, j))],
        out_specs=pl.BlockSpec((1, MOD_ROWS, tn), lambda i, j: (i, 0, j)), out_shape=_sds((nl, MOD_ROWS, n6), F32),
        name="mod_fwd", compiler_params=_params(("parallel", "parallel")))(c16, w_mod, b_loc)


def _mod_bwd(c16, w_mod, dmod_loc):
    nl, d, n6 = w_mod.shape
    tn = _tile(n6, 512, 128)

    def body(c_ref, w_ref, dm_ref, dw_ref, ds_ref):
        @pl.when(jnp.logical_and(pl.program_id(0) == 0, pl.program_id(1) == 0))
        def _():
            ds_ref[...] = jnp.zeros_like(ds_ref)

        cv = c_ref[...]
        s = cv * _sigmoid(cv)
        dm = dm_ref[0]
        dw_ref[0] = lax.dot_general(s, dm, (((0,), (0,)), ((), ())), precision=HIGHEST, preferred_element_type=F32)
        ds_ref[...] += lax.dot_general(dm, w_ref[0], (((1,), (1,)), ((), ())), precision=HIGHEST, preferred_element_type=F32)

    return pl.pallas_call(
        body, grid=(nl, n6 // tn),
        in_specs=[pl.BlockSpec((MOD_ROWS, d), lambda i, j: (0, 0)), pl.BlockSpec((1, d, tn), lambda i, j: (i, 0, j)),
                  pl.BlockSpec((1, MOD_ROWS, tn), lambda i, j: (i, 0, j))],
        out_specs=[pl.BlockSpec((1, d, tn), lambda i, j: (i, 0, j)), pl.BlockSpec((MOD_ROWS, d), lambda i, j: (0, 0))],
        out_shape=[_sds((nl, d, n6), F32), _sds((MOD_ROWS, d), F32)], name="mod_bwd",
        compiler_params=_params(("arbitrary", "arbitrary")))(c16, w_mod, dmod_loc)


def _pack(parts):
    flat = [p.reshape(-1).astype(F32) for p in parts]
    offs = np.cumsum([0] + [f.shape[0] for f in flat])
    total = int(offs[-1])
    padded = -(-total // (8 * LANES)) * (8 * LANES)
    slab = jnp.concatenate(flat + [jnp.zeros((padded - total,), F32)])
    return slab.reshape(padded // LANES, LANES), [int(o) for o in offs]


def _unshard_cols(seg, lead):
    n = seg.shape[1] // int(np.prod(lead)) if lead else seg.shape[1]
    a = seg.reshape((N_DEV,) + tuple(lead) + (n,))
    a = jnp.moveaxis(a, 0, len(lead))
    return a.reshape(tuple(lead) + (N_DEV * n,))


def _my_cols(a, me, n):
    start = (0,) * (a.ndim - 1) + (me * n,)
    return lax.dynamic_slice(a, start, a.shape[:-1] + (n,))


def _rope_tables(seq, n_ctx):
    rows = seq // GRID_W
    r_idx, c_idx = jnp.meshgrid(jnp.arange(rows), jnp.arange(GRID_W), indexing='ij')
    r_idx = r_idx.reshape(-1).astype(F32)
    c_idx = c_idx.reshape(-1).astype(F32)
    pairs = HEAD_DIM // 4
    freqs = ROPE_THETA ** (-jnp.arange(pairs, dtype=F32) / pairs)
    ang_r, ang_c = r_idx[:, None] * freqs, c_idx[:, None] * freqs
    cos = jnp.concatenate([jnp.cos(ang_r)] * 2 + [jnp.cos(ang_c)] * 2, axis=1)
    sin = jnp.concatenate([-jnp.sin(ang_r), jnp.sin(ang_r), -jnp.sin(ang_c), jnp.sin(ang_c)], axis=1)
    cos = jnp.concatenate([jnp.ones((n_ctx, HEAD_DIM), F32), cos], axis=0)
    sin = jnp.concatenate([jnp.zeros((n_ctx, HEAD_DIM), F32), sin], axis=0)
    lane = np.arange(HEAD_DIM)
    partner = np.where(lane % (2 * pairs) < pairs, lane + pairs, lane - pairs)
    perm = np.zeros((HEAD_DIM, HEAD_DIM), np.float32)
    perm[partner, lane] = 1.0
    return cos, sin, jnp.asarray(perm)


def kernel(x, c, ctx, c_ctx, w_mod, b_mod, norm_g, w_ff_in, w_ff_out, ar_w_in, ar_q_g, ar_k_g, ar_conv_w, ar_conv_b, ar_wa, ar_ba, ar_wx, ar_bx, ar_lambda, ar_w_out, gm_w_in, gm_b_in, gm_v_g, gm_v_b, gm_w_sp, gm_b_sp, gm_w_out, loss_target, m_c_ctx, m_w_mod, m_b_mod, m_norm_g, m_w_ff_in, m_w_ff_out, m_ar_w_in, m_ar_q_g, m_ar_k_g, m_ar_conv_w, m_ar_conv_b, m_ar_wa, m_ar_ba, m_ar_wx, m_ar_bx, m_ar_lambda, m_ar_w_out, m_gm_w_in, m_gm_b_in, m_gm_v_g, m_gm_v_b, m_gm_w_sp, m_gm_b_sp, m_gm_w_out, v_c_ctx, v_w_mod, v_b_mod, v_norm_g, v_w_ff_in, v_w_ff_out, v_ar_w_in, v_ar_q_g, v_ar_k_g, v_ar_conv_w, v_ar_conv_b, v_ar_wa, v_ar_ba, v_ar_wx, v_ar_bx, v_ar_lambda, v_ar_w_out, v_gm_w_in, v_gm_b_in, v_gm_v_g, v_gm_v_b, v_gm_w_sp, v_gm_b_sp, v_gm_w_out):
    given = dict(locals())
    wts = {n: given[n] for n in WEIGHTS}
    mom1 = {n: given["m_" + n] for n in WEIGHTS}
    mom2 = {n: given["v_" + n] for n in WEIGHTS}

    xi, yi, ci = _mesh_pos()
    me = 4 * xi + 2 * yi + ci

    seq, d = x.shape[1], x.shape[2]
    n_ctx = ctx.shape[1]
    t_all = n_ctx + seq
    n_layers = w_mod.shape[0]
    assert n_layers == 2 and ar_w_in.shape[0] == 1 and gm_w_in.shape[0] == 1
    d_ff = w_ff_in.shape[2] * N_DEV
    attn_w, kv_w = N_HEADS * HEAD_DIM, N_KV_HEADS * HEAD_DIM
    rnn_blocks = ar_wa.shape[2]
    d_rnn = rnn_blocks * RNN_BLOCK_W
    gm_groups = gm_w_sp.shape[1]
    d_gm = gm_groups * GM_GROUP_W
    ar_in = ar_w_in.shape[2] * N_DEV
    n6 = w_mod.shape[2]
    tm, tmb = ROW_TILE, ROW_TILE_BWD
    assert attn_w == d_rnn and ar_in == 3 * attn_w + 2 * kv_w and (3 * attn_w) % (2 * kv_w) == 0
    assert n_ctx % tm == 0 and seq % tm == 0 and n_ctx % SCAN_BLOCK == 0 and seq % SCAN_BLOCK == 0 and tm % CHUNK == 0
    nct, nctb = n_ctx // tm, n_ctx // tmb
    kv_blk = (3 * attn_w) // (2 * kv_w)
    lr = d_rnn // LANES

    x2, ctx2, tgt = x[0], ctx[0], loss_target[0]

    def cols_full(g):
        return jnp.moveaxis(g, 0, 1).reshape(g.shape[1], N_DEV * g.shape[2])

    small0, off0 = _pack([c[0], norm_g, ar_conv_w[0], ar_ba[0], ar_bx[0], ar_lambda[0], gm_b_in[0], gm_v_g[0], gm_v_b[0]])
    g_ar_in, gs0 = _run_comm("gather_first", _gather_comm([ar_w_in[0].astype(BF16), small0]))
    gs0 = gs0.reshape(N_DEV, -1)
    w_in = cols_full(g_ar_in)
    split = [attn_w, attn_w + 2 * kv_w, attn_w + 2 * kv_w + d_rnn]
    w_in = jnp.concatenate([w_in[:, :split[0]], w_in[:, split[1]:], w_in[:, split[0]:split[1]]], axis=1)
    w1, w2 = [None] * n_layers, [None] * n_layers

    def seg0(k):
        return gs0[:, off0[k]:off0[k + 1]]

    c_all = seg0(0)
    norm_full = _unshard_cols(seg0(1), (n_layers, 4))
    conv_w = _unshard_cols(seg0(2), (CONV_W,))
    ba, bx, lam = (_unshard_cols(seg0(k), (2,)) for k in (3, 4, 5))
    gm_b_in_f = seg0(6).reshape(1, 2 * d_gm)
    gm_vg, gm_vb = seg0(7).reshape(1, d_gm), seg0(8).reshape(1, d_gm)

    c16 = jnp.concatenate([c_all, c_ctx[None], jnp.zeros((MOD_ROWS - N_DEV - 1, d), F32)], axis=0)
    b_loc = _my_cols(b_mod, me, n6)[:, None, :]
    mod_loc = _mod_fwd(c16, w_mod, b_loc)
    (g_mod,) = _run_comm("gather_mod", _gather_comm([mod_loc]))
    mod_all = jnp.moveaxis(g_mod, 0, 2).reshape(n_layers, MOD_ROWS, N_DEV * n6)
    ml = lax.dynamic_index_in_dim(mod_all, me, axis=1, keepdims=False).reshape(n_layers, 6, d)
    mc = mod_all[:, N_DEV].reshape(n_layers, 6, d)

    def row(a, *idx):
        return a[idx][None]

    cos, sin, perm = _rope_tables(seq, n_ctx)
    wa3 = ar_wa[0].reshape(2 * rnn_blocks, RNN_BLOCK_W, RNN_BLOCK_W)
    wx3 = ar_wx[0].reshape(2 * rnn_blocks, RNN_BLOCK_W, RNN_BLOCK_W)
    conv_b = ar_conv_b
    q_g, k_g = ar_q_g, ar_k_g
    w_sp = gm_w_sp[0]
    bsp_t = jnp.pad(gm_b_sp[0].T, ((0, 0), (0, LANES - gm_groups)))
    expand = np.zeros((LANES, d_gm), np.float32)
    for g in range(gm_groups):
        expand[g, g * GM_GROUP_W:(g + 1) * GM_GROUP_W] = 1.0
    expand = jnp.asarray(expand)

    def relu2(acc):
        r = jnp.maximum(acc, 0.0)
        return (acc, r * r)

    def ff_in_shard(i):
        return w_ff_in[i].astype(BF16)

    def ff_out_shard(i):
        return w_ff_out[i].astype(BF16)

    xtok = jnp.concatenate([ctx2, x2], axis=0)
    pre0_args = [row(norm_full, 0, 0), row(mc, 0, 0), row(mc, 0, 1), row(ml, 0, 0), row(ml, 0, 1)]
    f_pre0 = functools.partial(_f_pre_ctx, n_ctx_tiles=nct)
    (h0,) = _rowwise("pre0", f_pre0, t_all, tm, [_t(xtok)], pre0_args, [(d, BF16)])
    tm_tok = _tile(t_all, 640, 16)
    proj, (g_ar_out,) = _matmul("ar_in", h0, w_in, tm=tm_tok, comm=_gather_comm([ar_w_out[0].astype(BF16)]))
    w_out = g_ar_out.reshape(attn_w + d_rnn, d)
    f_qkv = functools.partial(_f_qkv, nh=N_HEADS, nkv=N_KV_HEADS)
    qkv_tiled = [_t(proj, 0, 0, attn_w), _t(proj, 0, kv_blk, 2 * kv_w), _t(cos), _t(sin)]
    q_r, kv_r = _rowwise("qkv", f_qkv, t_all, tm, qkv_tiled, [q_g, k_g, perm], [(attn_w, BF16), (2 * kv_w, BF16)])
    attn_o, (g_ff_in0, g_ff_out0) = _attn_fwd(q_r, kv_r, nct, tm, comm=_gather_comm([ff_in_shard(0), ff_out_shard(0)]))
    w1[0], w2[0] = cols_full(g_ff_in0), g_ff_out0.reshape(d_ff, d)

    def with_neighbours(a, t, col_blk=0, width=None):
        return [_t(a, -t, col_blk, width), _t(a, 0, col_blk, width), _t(a, t, col_blk, width)]

    f_conv = functools.partial(_f_conv, n_ctx_tiles=nct, n_tiles=t_all // tm)
    (xc,) = _rowwise("conv", f_conv, t_all, tm, with_neighbours(proj, tm, 1, d_rnn), [conv_w, conv_b], [(d_rnn, F32)])
    f_gates = functools.partial(_f_gates, nb=rnn_blocks)
    gate_full = [wa3, ba, wx3, bx, lam]
    a_f, b_f, a_b, b_b = _rowwise("gates", f_gates, t_all, tm, [_t(xc)], gate_full, [(d_rnn, F32)] * 4)

    def to3(a):
        return a.reshape(a.shape[0], lr, LANES)

    nc_scan = n_ctx // SCAN_BLOCK
    h_f, hp_f = _scan_fwd("scan_f", to3(a_f), to3(b_f), nc_scan, False)
    h_b, hp_b = _scan_fwd("scan_b", to3(a_b), to3(b_b), nc_scan, True)
    h_f2, h_b2 = h_f.reshape(t_all, d_rnn), h_b.reshape(t_all, d_rnn)
    rnn_tiled = [_t(h_f2, n_ctx), _t(h_b2, n_ctx), _t(proj, n_ctx, 2, d_rnn)]
    (rnn_o,) = _rowwise("rnn_out", _f_rnnout, seq, tm, rnn_tiled, [], [(d_rnn, BF16)])
    ar = jnp.concatenate([attn_o, rnn_o], axis=1)
    o0 = _matmul("ar_out", ar, w_out)
    mid0_args = [row(norm_full, 0, 1), row(ml, 0, 2), row(norm_full, 0, 2), row(ml, 0, 3), row(ml, 0, 4)]
    x1, h2_0 = _rowwise("mid0", _f_mid, seq, tm, [_t(x2), _t(o0)], mid0_args, [(d, F32), (d, BF16)])
    (z0, act0), (g_gm_in, g_gm_out) = _matmul("ff_in_0", h2_0, w1[0], outs=(F32, BF16), epilogue=relu2,
                                              comm=_gather_comm([gm_w_in[0].astype(BF16), gm_w_out[0].astype(BF16)]))
    w_gi, w_go = cols_full(g_gm_in), g_gm_out.reshape(d_gm, d)
    m0, (g_ff_in1,) = _matmul("ff_out_0", act0, w2[0], comm=_gather_comm([ff_in_shard(1)]))
    w1[1] = cols_full(g_ff_in1)
    post0_args = [row(norm_full, 0, 3), row(ml, 0, 5)]
    (x2l,) = _rowwise("post0", _f_post, seq, tm, [_t(x1), _t(m0)], post0_args, [(d, F32)])

    pre1_args = [row(norm_full, 1, 0), row(ml, 1, 0), row(ml, 1, 1)]
    (h1,) = _rowwise("pre1", _f_pre, seq, tm, [_t(x2l)], pre1_args, [(d, BF16)])
    zg = _matmul("gm_in", h1, w_gi)
    f_gm = functools.partial(_f_gm, n_chunks=tmb // CHUNK, groups=gm_groups)
    gm_full = [gm_b_in_f[:, :d_gm], gm_b_in_f[:, d_gm:], gm_vg, gm_vb, w_sp, bsp_t, expand]
    gm_tiled = [_t(zg, 0, 0, d_gm), _t(zg, 0, 1, d_gm)]
    (gmix,) = _rowwise("gm_mix", f_gm, seq, tmb, gm_tiled, gm_full, [(d_gm, BF16)])
    o1 = _matmul("gm_out", gmix, w_go)
    mid1_args = [row(norm_full, 1, 1), row(ml, 1, 2), row(norm_full, 1, 2), row(ml, 1, 3), row(ml, 1, 4)]
    x3, h2_1 = _rowwise("mid1", _f_mid, seq, tm, [_t(x2l), _t(o1)], mid1_args, [(d, F32), (d, BF16)])
    (z1, act1), (g_ff_out1,) = _matmul("ff_in_1", h2_1, w1[1], outs=(F32, BF16), epilogue=relu2,
                                       comm=_gather_comm([ff_out_shard(1)]))
    w2[1] = g_ff_out1.reshape(d_ff, d)
    m1 = _matmul("ff_out_1", act1, w2[1])
    post1_args = [row(norm_full, 1, 3), row(ml, 1, 5)]

    def f_loss(pid, xv, ov, tv, g, gate):
        err = _f_post(pid, xv, ov, g, gate)[0] - tv
        part = 0.5 * jnp.sum(err * err) / d
        return (err / d, jnp.full((8, LANES), part, F32))

    dy, loss_acc = _rowwise("loss", f_loss, seq, tm, [_t(x3), _t(m1), _t(tgt)], post1_args, [(d, F32)], [(8, LANES)])
    loss = lax.psum(loss_acc[0, 0], ("x", "y", "c"))

    d_norm = [[None] * 4 for _ in range(n_layers)]
    d_ml = [[None] * 6 for _ in range(n_layers)]
    recv = {}

    def cols_blocks(g):
        return jnp.moveaxis(g.reshape(g.shape[0], N_DEV, g.shape[1] // N_DEV), 1, 0)

    def rows_blocks(g):
        return g.reshape(N_DEV, g.shape[0] // N_DEV, g.shape[1])

    chip_sums = {}

    def chip_add(key, blocks, theirs):
        chip_sums[key] = _chip_add(key + "_add", blocks, theirs, ci)

    def mlp_bwd(i, dm, z, act, h2, first_comm=None):
        dw2 = _matmul(f"ff_out_dw_{i}", act, dm, ta=True, outs=(BF16,), comm=first_comm)
        dw2, carried = dw2 if first_comm is not None else (dw2, ())
        blk2 = rows_blocks(dw2)
        dz, (theirs,) = _matmul(f"ff_out_dx_{i}", dm, w2[i], tb=True, outs=(BF16,), extras=(z,),
                                epilogue=lambda acc, zt: (acc * (2.0 * jnp.maximum(zt, 0.0)),), comm=_swap_comm([blk2]))
        chip_add(f"ff_out_{i}", blk2, theirs)
        blk1, (recv[f"ff_out_{i}"],) = _matmul(f"ff_in_dw_{i}", h2, dz, ta=True, outs=(BF16,), col_blocks=N_DEV,
                                              comm=_chips_comm([chip_sums[f"ff_out_{i}"]]))
        dh2, (theirs,) = _matmul(f"ff_in_dx_{i}", dz, w1[i], tb=True, comm=_swap_comm([blk1]))
        chip_add(f"ff_in_{i}", blk1, theirs)
        return dh2, carried

    def post_bwd(i, xin, m, args, dout):
        res = _rowwise(f"post_bwd{i}", _bwd_of(_f_post, 2, 1, (0, 1, 2, 3)), seq, tmb, [_t(xin), _t(m), _t(dout)], args,
                       [(d, F32), (d, BF16)], [(1, d), (1, d)])
        d_norm[i][3], d_ml[i][5] = res[2], res[3]
        return res[0], res[1]

    def mid_bwd(i, xin, o, args, dx1, dh2):
        res = _rowwise(f"mid_bwd{i}", _bwd_of(_f_mid, 2, 2, (0, 1, 2, 3, 4, 5, 6)), seq, tmb,
                       [_t(xin), _t(o), _t(dx1), _t(dh2)], args, [(d, F32), (d, BF16)], [(1, d)] * 5)
        d_norm[i][1], d_ml[i][2], d_norm[i][2], d_ml[i][3], d_ml[i][4] = res[2:]
        return res[0], res[1]

    dx3, dm1 = post_bwd(1, x3, m1, post1_args, dy)
    dh2_1, _ = mlp_bwd(1, dm1, z1, act1, h2_1)
    dx2a, do1 = mid_bwd(1, x2l, o1, mid1_args, dx3, dh2_1)
    blk_go = rows_blocks(_matmul("gm_out_dw", gmix, do1, ta=True, outs=(BF16,)))
    dgmix, (theirs,) = _matmul("gm_out_dx", do1, w_go, tb=True, comm=_swap_comm([blk_go]))
    chip_add("gm_out", blk_go, theirs)
    gm_res = _rowwise("gm_mix_bwd", _bwd_of(f_gm, 2, 1, (0, 1, 2, 3, 4, 5, 6, 7)), seq, tmb,
                      gm_tiled + [_t(dgmix)], gm_full, [(d_gm, BF16), (d_gm, BF16)],
                      [(1, d_gm)] * 4 + [w_sp.shape, bsp_t.shape])
    dzg = jnp.concatenate([gm_res[0], gm_res[1]], axis=1)
    g_gm_b_in = jnp.concatenate([gm_res[2], gm_res[3]], axis=1)
    g_gm_vg, g_gm_vb, g_w_sp = gm_res[4], gm_res[5], gm_res[6]
    g_b_sp = gm_res[7][:, :gm_groups].T
    dh1 = _matmul("gm_in_dx", dzg, w_gi, tb=True)
    blk_gi = _matmul("gm_in_dw", h1, dzg, ta=True, outs=(BF16,), col_blocks=N_DEV)

    def f_pre_bwd(pid, xv, dh, dxa, g, sh, sc):
        dxv, dg, dsh, dsc = _bwd_of(_f_pre, 1, 1, (0, 1, 2, 3))(pid, xv, dh, g, sh, sc)
        return (dxv + dxa, dg, dsh, dsc)

    res = _rowwise("pre_bwd1", f_pre_bwd, seq, tmb, [_t(x2l), _t(dh1), _t(dx2a)], pre1_args, [(d, F32)], [(1, d)] * 3)
    dx2l = res[0]
    d_norm[1][0], d_ml[1][0], d_ml[1][1] = res[1:]

    dx1, dm0 = post_bwd(0, x1, m0, post0_args, dx2l)
    dh2_0, (theirs,) = mlp_bwd(0, dm0, z0, act0, h2_0, first_comm=_swap_comm([blk_gi]))
    chip_add("gm_in", blk_gi, theirs)
    dxa, do0 = mid_bwd(0, x2, o0, mid0_args, dx1, dh2_0)
    blk_out = rows_blocks(_matmul("ar_out_dw", ar, do0, ta=True, outs=(BF16,)))
    d_ar, (theirs,) = _matmul("ar_out_dx", do0, w_out, tb=True, comm=_swap_comm([blk_out]))
    chip_add("ar_out", blk_out, theirs)

    late = ["ff_in_1", "gm_out", "gm_in", "ff_in_0"]
    dq, dkt, dvt, *landed = _attn_bwd(q_r, kv_r, d_ar, nct, tm, comm=_chips_comm([chip_sums[k] for k in late]))
    recv.update(zip(late, landed))
    dq_all = jnp.concatenate([jnp.zeros((n_ctx, attn_w), F32), dq], axis=0)
    dkv_all = jnp.concatenate([dkt, dvt], axis=0).T
    qkv_res = _rowwise("qkv_bwd", _bwd_of(f_qkv, 4, 2, (0, 1, 4, 5)), t_all, tmb, qkv_tiled + [_t(dq_all), _t(dkv_all)],
                       [q_g, k_g, perm], [(attn_w, BF16), (2 * kv_w, BF16)], [q_g.shape, k_g.shape])
    dproj_q, dproj_kv, g_q_g, g_k_g = qkv_res

    rnn_res = _rowwise("rnn_out_bwd", _bwd_of(_f_rnnout, 3, 1, (0, 2)), seq, tmb, rnn_tiled + [_t(d_ar, 0, 1, d_rnn)], [],
                       [(d_rnn, F32), (d_rnn, BF16)])
    zc = jnp.zeros((n_ctx, d_rnn), F32)
    dh_all = to3(jnp.concatenate([zc, rnn_res[0]], axis=0))
    dproj_g = jnp.concatenate([zc.astype(BF16), rnn_res[1]], axis=0)
    da_f, db_f = _scan_bwd("scan_f_bwd", to3(a_f), dh_all, hp_f, nc_scan, False)
    da_b, db_b = _scan_bwd("scan_b_bwd", to3(a_b), dh_all, hp_b, nc_scan, True)
    gate_cts = [_t(a.reshape(t_all, d_rnn)) for a in (da_f, db_f, da_b, db_b)]
    gates_res = _rowwise("gates_bwd", _bwd_of(f_gates, 1, 4, (0, 1, 2, 3, 4, 5)), t_all, tmb, [_t(xc)] + gate_cts,
                         gate_full, [(d_rnn, F32)], [wa3.shape, ba.shape, wx3.shape, bx.shape, lam.shape])
    dxc, g_wa, g_ba, g_wx, g_bx, g_lam = gates_res
    f_conv_b = functools.partial(_f_conv_bwd, n_ctx_tiles=nctb, n_tiles=t_all // tmb)
    conv_tiled = with_neighbours(proj, tmb, 1, d_rnn) + with_neighbours(dxc, tmb)
    dproj_x, g_conv_w, g_conv_b = _rowwise("conv_bwd", f_conv_b, t_all, tmb, conv_tiled, [conv_w],
                                           [(d_rnn, BF16)], [conv_w.shape, (1, d_rnn)])
    dproj = jnp.concatenate([dproj_q, dproj_x, dproj_g, dproj_kv], axis=1)
    dh0, (recv["ar_out"],) = _matmul("ar_in_dx", dproj, w_in, tb=True, tm=tm_tok, comm=_chips_comm([chip_sums["ar_out"]]))
    sq_pack, off_sq = _pack([g_wa, g_wx, g_w_sp])
    g_w_in, (g_sq,) = _matmul("ar_in_dw", h0, dproj, ta=True, outs=(BF16,), comm=_gather_comm([sq_pack]))
    g_w_in = jnp.concatenate([g_w_in[:, :attn_w], g_w_in[:, 3 * attn_w:], g_w_in[:, attn_w:3 * attn_w]], axis=1)
    blk_in = cols_blocks(g_w_in)
    (theirs,) = _run_comm("ar_in_swap", _swap_comm([blk_in]))
    chip_add("ar_in", blk_in, theirs)
    (recv["ar_in"],) = _run_comm("ar_in_chips", _chips_comm([chip_sums["ar_in"]]))

    dxa_all = jnp.concatenate([jnp.zeros((n_ctx, d), F32), dxa], axis=0)
    f_pre0b = functools.partial(_f_pre_ctx, n_ctx_tiles=nctb)

    def f_pre0_bwd(pid, xv, dh, dxp, g, shc, scc, shl, scl):
        grads = _bwd_of(f_pre0b, 1, 1, (0, 1, 2, 3, 4, 5))(pid, xv, dh, g, shc, scc, shl, scl)
        return (grads[0] + dxp,) + tuple(grads[1:])

    res = _rowwise("pre_bwd0", f_pre0_bwd, t_all, tmb, [_t(xtok), _t(dh0), _t(dxa_all)], pre0_args, [(d, F32)], [(1, d)] * 5)
    grad_x = res[0][n_ctx:][None]
    d_norm[0][0], d_mc_shift, d_mc_scale, d_ml[0][0], d_ml[0][1] = res[1:]

    z1d = jnp.zeros((1, d), F32)
    dml = jnp.concatenate([jnp.concatenate(r, axis=0)[None] for r in d_ml], axis=0)
    dmc = jnp.concatenate([jnp.concatenate([d_mc_shift, d_mc_scale] + [z1d] * 4, axis=0)[None],
                           jnp.zeros((n_layers - 1, 6, d), F32)], axis=0)
    g_norm = jnp.concatenate([jnp.concatenate(r, axis=0)[None] for r in d_norm], axis=0)
    small_parts = [dmc, g_norm, g_q_g, g_k_g, g_conv_w, g_conv_b, g_ba, g_bx, g_lam, g_gm_b_in, g_gm_vg, g_gm_vb, g_b_sp]
    small2, off2 = _pack([dml] + small_parts)
    (gs2,) = _run_comm("gather_small_grads", _gather_comm([small2]))
    dml_all = gs2.reshape(N_DEV, -1)[:, :off2[1]].reshape(N_DEV, n_layers, 6 * d)
    summed = _sum_lead("sum_small_grads", gs2).reshape(-1)
    summed_sq = _sum_lead("sum_square_grads", g_sq).reshape(-1)

    def seg2(k, shape):
        return summed[off2[k + 1]:off2[k + 2]].reshape(shape)

    def seg_sq(k, shape):
        return summed_sq[off_sq[k]:off_sq[k + 1]].reshape(shape)

    dmc_sum = seg2(0, (n_layers, 6 * d))
    dmod_rows = jnp.concatenate([jnp.moveaxis(dml_all, 0, 1), dmc_sum[:, None, :],
                                 jnp.zeros((n_layers, MOD_ROWS - N_DEV - 1, 6 * d), F32)], axis=1)
    g_b_mod = _sum_lead("sum_b_mod", jnp.moveaxis(dmod_rows, 1, 0).reshape(MOD_ROWS, n_layers * 6 * d // LANES, LANES))
    g_b_mod = g_b_mod.reshape(n_layers, 6 * d)
    g_w_mod, ds16 = _mod_bwd(c16, w_mod, _my_cols(dmod_rows, me, n6))
    (g_ds,) = _run_comm("gather_dctx", _gather_comm([ds16[N_DEV].reshape(d // LANES, LANES)]))
    ds_ctx = _sum_lead("sum_dctx", g_ds)
    (g_c_ctx,) = _rowwise("silu_bwd", _f_silu_mul, d // LANES, d // LANES, [_t(c_ctx.reshape(d // LANES, LANES)), _t(ds_ctx)],
                          [], [(LANES, F32)])
    g_c_ctx = g_c_ctx.reshape(d)

    grads = {
        'c_ctx': g_c_ctx, 'b_mod': g_b_mod,
        'norm_g': _my_cols(seg2(1, (n_layers, 4, d)), me, d // N_DEV),
        'ar_q_g': seg2(2, ar_q_g.shape), 'ar_k_g': seg2(3, ar_k_g.shape),
        'ar_conv_w': _my_cols(seg2(4, (1, CONV_W, d_rnn)), me, d_rnn // N_DEV),
        'ar_conv_b': seg2(5, ar_conv_b.shape),
        'ar_ba': _my_cols(seg2(6, (1, 2, d_rnn)), me, d_rnn // N_DEV),
        'ar_bx': _my_cols(seg2(7, (1, 2, d_rnn)), me, d_rnn // N_DEV),
        'ar_lambda': _my_cols(seg2(8, (1, 2, d_rnn)), me, d_rnn // N_DEV),
        'gm_b_in': _my_cols(seg2(9, (1, 2 * d_gm)), me, 2 * d_gm // N_DEV),
        'gm_v_g': _my_cols(seg2(10, (1, d_gm)), me, d_gm // N_DEV),
        'gm_v_b': _my_cols(seg2(11, (1, d_gm)), me, d_gm // N_DEV),
        'gm_b_sp': seg2(12, gm_b_sp.shape),
        'ar_wa': seg_sq(0, ar_wa.shape), 'ar_wx': seg_sq(1, ar_wx.shape), 'gm_w_sp': seg_sq(2, gm_w_sp.shape),
    }
    deltas, new_m, new_v = {}, {}, {}

    small_names = list(grads)
    wp, offw = _pack([wts[n] for n in small_names])
    mp, _ = _pack([mom1[n] for n in small_names])
    vp, _ = _pack([mom2[n] for n in small_names])
    gp, _ = _pack([grads[n] for n in small_names])
    dp, mp2, vp2 = _adam_f32("adam_small", gp, wp, mp, vp)
    for k, n in enumerate(small_names):
        for dst, slab in ((deltas, dp), (new_m, mp2), (new_v, vp2)):
            dst[n] = slab.reshape(-1)[offw[k]:offw[k + 1]].reshape(wts[n].shape)

    grads['w_mod'] = g_w_mod
    dw, mw, vw = _adam_f32("adam_w_mod", g_w_mod.reshape(n_layers * d, n6), w_mod.reshape(n_layers * d, n6),
                           m_w_mod.reshape(n_layers * d, n6), v_w_mod.reshape(n_layers * d, n6))
    deltas['w_mod'], new_m['w_mod'], new_v['w_mod'] = (a.reshape(w_mod.shape) for a in (dw, mw, vw))

    received = {
        'w_ff_in': [recv[f"ff_in_{i}"] for i in range(n_layers)], 'w_ff_out': [recv[f"ff_out_{i}"] for i in range(n_layers)],
        'ar_w_in': [recv["ar_in"]], 'ar_w_out': [recv["ar_out"]], 'gm_w_in': [recv["gm_in"]], 'gm_w_out': [recv["gm_out"]]}
    for n, r in received.items():
        shp = wts[n].shape
        flat = (shp[0] * shp[1], shp[2])
        res = _adam_recv("adam_" + n, r, wts[n].reshape(flat), mom1[n].reshape(flat), mom2[n].reshape(flat))
        grads[n], deltas[n], new_m[n], new_v[n] = (a.reshape(shp) for a in res)

    return (loss, grad_x, *[grads[n] for n in WEIGHTS], *[deltas[n] for n in WEIGHTS],
            *[new_m[n] for n in WEIGHTS], *[new_v[n] for n in WEIGHTS])
```

```python
import functools

import numpy as np
import jax
import jax.numpy as jnp
from jax import lax
from jax.experimental import pallas as pl
from jax.experimental.pallas import tpu as pltpu

F32 = jnp.float32
BF16 = jnp.bfloat16
HIGHEST = lax.Precision.HIGHEST

GRID_W = 64
N_HEADS = 8
N_KV_HEADS = 2
HEAD_DIM = 128
ROPE_THETA = 10000.0
RNN_BLOCK_W = 128
CONV_W = 4
RG_C = 8.0
GM_GROUP_W = 128
CHUNK = 128
EPS = 1e-6
ADAM_LR = 0.001
ADAM_B1 = 0.9
ADAM_B2 = 0.999
ADAM_EPS = 1e-08
ADAM_WD = 0.01
ADAM_STEP = 10

N_DEV = 8
MOD_ROWS = 16
LANES = 128
ROW_TILE = 256
ROW_TILE_BWD = 256
GATES_BWD_TILE = 128
SCAN_BLOCK = 256
VMEM_LIMIT = 56 * 1024 * 1024
PACK_ROWS = 512
MM_TILE_M = 1024
MM_TILE_N = 1024
MM_TILE_K = 2048

WEIGHTS = ['c_ctx', 'w_mod', 'b_mod', 'norm_g', 'w_ff_in', 'w_ff_out', 'ar_w_in', 'ar_q_g', 'ar_k_g', 'ar_conv_w',
           'ar_conv_b', 'ar_wa', 'ar_ba', 'ar_wx', 'ar_bx', 'ar_lambda', 'ar_w_out', 'gm_w_in', 'gm_b_in', 'gm_v_g',
           'gm_v_b', 'gm_w_sp', 'gm_b_sp', 'gm_w_out']


def _sds(shape, dtype):
    return jax.ShapeDtypeStruct(tuple(shape), dtype)


def _tile(dim, pref, align):
    t = (min(pref, dim) // align) * align
    while t >= align:
        if dim % t == 0:
            return t
        t -= align
    return dim


def _params(sem):
    return pltpu.CompilerParams(dimension_semantics=sem, vmem_limit_bytes=VMEM_LIMIT)


def _rms(x, g):
    return x * lax.rsqrt(jnp.mean(x * x, axis=-1, keepdims=True) + EPS) * g


def _gelu(x):
    return 0.5 * x * (1.0 + jnp.tanh(0.7978845608028654 * (x + 0.044715 * (x * x * x))))


def _sigmoid(x):
    return 0.5 * (jnp.tanh(0.5 * x) + 1.0)


def _log1p_pos(u):
    small = u < 1e-3
    us = jnp.where(small, u, 0.0)
    return jnp.where(small, us * (1.0 - us * (0.5 - us * (1.0 / 3.0))), jnp.log(1.0 + u))


def _softplus(x):
    return jnp.maximum(x, 0.0) + _log1p_pos(jnp.exp(-jnp.abs(x)))


def _expm1(x):
    small = jnp.abs(x) < 0.3
    xs = jnp.where(small, x, 0.0)
    poly = xs * (1.0 + xs * (1.0 / 2 + xs * (1.0 / 6 + xs * (1.0 / 24 + xs * (1.0 / 120 + xs * (1.0 / 720 + xs * (1.0 / 5040)))))))
    return jnp.where(small, poly, jnp.exp(x) - 1.0)


def _f_pre_ctx(pid, x, g, shc, scc, shl, scl, *, n_ctx_tiles):
    is_ctx = pid < n_ctx_tiles
    sh = jnp.where(is_ctx, shc, shl)
    sc = jnp.where(is_ctx, scc, scl)
    return (_rms(x, g) * (1.0 + sc) + sh,)


def _f_pre(pid, x, g, sh, sc):
    return (_rms(x, g) * (1.0 + sc) + sh,)


def _f_mid(pid, x, o, g1, gate, g2, sh, sc):
    x1 = x + gate * _rms(o, g1)
    return (x1, _rms(x1, g2) * (1.0 + sc) + sh)


def _f_post(pid, x, o, g, gate):
    return (x + gate * _rms(o, g),)


def _f_qkv(pid, pq, pkv, cos, sin, q_g, k_g, perm, *, nh, nkv):
    hd = HEAD_DIM

    def norm_rope(xh, g):
        y = _rms(xh, g)
        return y * cos + jnp.dot(y, perm, precision=HIGHEST, preferred_element_type=F32) * sin

    qs = [norm_rope(pq[:, h * hd:(h + 1) * hd], q_g) for h in range(nh)]
    ks = [norm_rope(pkv[:, h * hd:(h + 1) * hd], k_g) for h in range(nkv)]
    return (jnp.concatenate(qs, axis=1), jnp.concatenate(ks + [pkv[:, nkv * hd:]], axis=1))


def _f_gates(pid, x, wa, ba, wx, bx, lam, *, nb):
    w = RNN_BLOCK_W
    outs = []
    for d in range(2):
        ra, ri = [], []
        for n in range(nb):
            xn = x[:, n * w:(n + 1) * w].astype(BF16)
            ra.append(jnp.dot(xn, wa[d * nb + n].astype(BF16), preferred_element_type=F32))
            ri.append(jnp.dot(xn, wx[d * nb + n].astype(BF16), preferred_element_type=F32))
        r = _sigmoid(jnp.concatenate(ra, axis=1) + ba[d:d + 1])
        i = _sigmoid(jnp.concatenate(ri, axis=1) + bx[d:d + 1])
        log_a = -RG_C * r * _softplus(-lam[d:d + 1])
        outs.append(jnp.exp(log_a))
        outs.append(jnp.sqrt(-_expm1(2.0 * log_a)) * (i * x))
    return tuple(outs)


def _f_rnnout(pid, hf, hb, gr):
    return ((hf + hb) * _gelu(gr),)


def _f_gm(pid, zu, zv, bu, bv, v_g, v_b, w_sp, bsp_t, expand, *, n_chunks, groups):
    u = _gelu(zu + bu)
    v = _gelu(zv + bv)
    mu = jnp.mean(v, axis=-1, keepdims=True)
    vc = v - mu
    v = vc * lax.rsqrt(jnp.mean(vc * vc, axis=-1, keepdims=True) + EPS) * v_g + v_b
    bias = jnp.dot(bsp_t, expand, precision=HIGHEST, preferred_element_type=F32)
    outs = []
    for c in range(n_chunks):
        vch = v[c * CHUNK:(c + 1) * CHUNK]
        cols = [jnp.dot(w_sp[g].astype(BF16), vch[:, g * GM_GROUP_W:(g + 1) * GM_GROUP_W].astype(BF16),
                        preferred_element_type=F32) for g in range(groups)]
        outs.append(u[c * CHUNK:(c + 1) * CHUNK] * (jnp.concatenate(cols, axis=1) + bias))
    return (jnp.concatenate(outs, axis=0),)


def _f_silu_mul(pid, c, d):
    return (d * jax.grad(lambda z: jnp.sum(z * _sigmoid(z)))(c),)


def _bwd_of(fn, n_tiled, n_ct, want):
    def bwd(pid, *args):
        tiles = [t.astype(F32) for t in args[:n_tiled]]
        cts = args[n_tiled:n_tiled + n_ct]
        fulls = list(args[n_tiled + n_ct:])
        outs, vjp = jax.vjp(lambda *a: fn(pid, *a), *tiles, *fulls)
        grads = vjp(tuple(ct.astype(o.dtype) for ct, o in zip(cts, outs)))
        return tuple(grads[i] for i in want)
    return bwd


def _rowwise(name, fn, rows, tm, tiled, full, outs, accs=(), comm=None):
    n_t, n_f, n_o, n_a = len(tiled), len(full), len(outs), len(accs)
    n_tiles = rows // tm

    def body(*refs):
        in_refs, c_ins, res_refs, c_outs, _, c_sems = _split_refs(refs, n_t + n_f, n_o + n_a, 0, comm)
        pid = pl.program_id(0)
        if comm is not None:
            @pl.when(pid == 0)
            def _():
                comm.start(c_ins, c_outs, *c_sems)

        res = fn(pid, *[r[...] for r in in_refs])
        o_refs, a_refs = res_refs[:n_o], res_refs[n_o:]
        for r, v in zip(o_refs, res[:n_o]):
            r[...] = v.astype(r.dtype)
        if n_a:
            @pl.when(pid == 0)
            def _():
                for r in a_refs:
                    r[...] = jnp.zeros_like(r)
            for r, v in zip(a_refs, res[n_o:]):
                r[...] += v.astype(F32)
        if comm is not None:
            @pl.when(pid == n_tiles - 1)
            def _():
                comm.finish(c_ins, c_outs, *c_sems)

    assert all(ro % tm == 0 for (_, ro, _, _) in tiled)
    in_specs = [pl.BlockSpec((tm, w), lambda i, ro=ro // tm, cb=cb, last=a.shape[0] // tm - 1: (jnp.clip(i + ro, 0, last), cb))
                for (a, ro, cb, w) in tiled]
    in_specs += [pl.BlockSpec(a.shape, lambda i, nd=a.ndim: (0,) * nd) for a in full]
    out_shape = [_sds((rows, w), dt) for (w, dt) in outs] + [_sds(s, F32) for s in accs]
    out_specs = [pl.BlockSpec((tm, w), lambda i: (i, 0)) for (w, _) in outs]
    out_specs += [pl.BlockSpec(tuple(s), lambda i, nd=len(s): (0,) * nd) for s in accs]
    c_in, c_out, c_scr = (comm.ins, comm.out_shapes, comm.scratch()) if comm is not None else ([], [], [])
    return pl.pallas_call(body, grid=(n_tiles,), in_specs=in_specs + [_ANY] * len(c_in), out_specs=out_specs + [_ANY] * len(c_out),
                          out_shape=out_shape + list(c_out), scratch_shapes=c_scr, name=name,
                          compiler_params=_params(("arbitrary",)))(*[t[0] for t in tiled], *full, *c_in)


def _t(a, row_off=0, col_blk=0, width=None):
    return (a, row_off, col_blk, a.shape[1] if width is None else width)


class _Comm:
    def __init__(self, ins, out_shapes, n_sems, start, finish):
        self.ins, self.out_shapes, self.n_sems, self.start, self.finish = list(ins), list(out_shapes), n_sems, start, finish

    def scratch(self):
        return [pltpu.SemaphoreType.DMA((self.n_sems,)), pltpu.SemaphoreType.DMA((self.n_sems,)),
                pltpu.SemaphoreType.DMA((len(self.ins),))]


_ANY = pl.BlockSpec(memory_space=pl.ANY)


class _SemSlice:
    def __init__(self, ref, first):
        self.ref, self.first = ref, first

    @property
    def at(self):
        return self

    def __getitem__(self, k):
        return self.ref.at[self.first + k]


def _both(*comms):
    comms = [cm for cm in comms if cm is not None]
    if len(comms) <= 1:
        return comms[0] if comms else None

    def parts(ins, outs, send, recv, local):
        i0 = o0 = s0 = 0
        for cm in comms:
            ni, no = len(cm.ins), len(cm.out_shapes)
            yield cm, (ins[i0:i0 + ni], outs[o0:o0 + no], _SemSlice(send, s0), _SemSlice(recv, s0), _SemSlice(local, i0))
            i0, o0, s0 = i0 + ni, o0 + no, s0 + cm.n_sems

    def start(*refs):
        for cm, sub in parts(*refs):
            cm.start(*sub)

    def finish(*refs):
        for cm, sub in parts(*refs):
            cm.finish(*sub)

    return _Comm(sum((cm.ins for cm in comms), []), sum((cm.out_shapes for cm in comms), []),
                 sum(cm.n_sems for cm in comms), start, finish)


def _row_block(cols, itemsize):
    return max(16, (1 << 20) // (cols * itemsize))


def _split_refs(refs, n_in, n_out, n_scratch, comm):
    ci, co, cs = (len(comm.ins), len(comm.out_shapes), 3) if comm is not None else (0, 0, 0)
    cuts = np.cumsum([0, n_in, ci, n_out, co, n_scratch, cs])
    return [refs[cuts[i]:cuts[i + 1]] for i in range(6)]


def _matmul(name, a, b, *, ta=False, tb=False, outs=((F32,)), epilogue=None, extras=(), tm=None, tn=None, tk=None, comm=None,
            col_blocks=None):
    m, k = (a.shape[1], a.shape[0]) if ta else a.shape
    n = b.shape[0] if tb else b.shape[1]
    tm = _tile(m, tm or MM_TILE_M, 128 if ta else 16)
    tn = _tile(n if col_blocks is None else n // col_blocks, tn or MM_TILE_N, 128)
    tk = _tile(k, tk or MM_TILE_K, 128 if not ta else 16)
    ni, nj, nk = m // tm, n // tn, k // tk
    n_e, n_o = len(extras), len(outs)
    dims = (((0 if ta else 1,), (1 if tb else 0,)), ((), ()))

    def body(*refs):
        ins, c_ins, o_refs, c_outs, scratch, c_sems = _split_refs(refs, 2 + n_e, n_o, 1 if nk > 1 else 0, comm)
        a_ref, b_ref, e_refs = ins[0], ins[1], ins[2:]
        i, j, kk = pl.program_id(0), pl.program_id(1), pl.program_id(2)
        if comm is not None:
            @pl.when(jnp.logical_and(jnp.logical_and(i == 0, j == 0), kk == 0))
            def _():
                comm.start(c_ins, c_outs, *c_sems)

        def finish(acc):
            res = (acc,) if epilogue is None else epilogue(acc, *[e[...] for e in e_refs])
            for r, v in zip(o_refs, res):
                r[...] = v.astype(r.dtype)

        prod = lax.dot_general(a_ref[...].astype(BF16), b_ref[...].astype(BF16), dims, preferred_element_type=F32)
        if nk == 1:
            finish(prod)
        else:
            acc = scratch[0]

            @pl.when(kk == 0)
            def _():
                acc[...] = prod

            @pl.when(kk > 0)
            def _():
                acc[...] += prod

            @pl.when(kk == nk - 1)
            def _():
                finish(acc[...])
        if comm is not None:
            @pl.when(jnp.logical_and(jnp.logical_and(i == ni - 1, j == nj - 1), kk == nk - 1))
            def _():
                comm.finish(c_ins, c_outs, *c_sems)

    a_spec = pl.BlockSpec((tk, tm), lambda i, j, kk: (kk, i)) if ta else pl.BlockSpec((tm, tk), lambda i, j, kk: (i, kk))
    b_spec = pl.BlockSpec((tn, tk), lambda i, j, kk: (j, kk)) if tb else pl.BlockSpec((tk, tn), lambda i, j, kk: (kk, j))
    mn_spec = pl.BlockSpec((tm, tn), lambda i, j, kk: (i, j))
    c_in, c_out, c_scr = (comm.ins, comm.out_shapes, comm.scratch()) if comm is not None else ([], [], [])
    if col_blocks is None:
        o_spec, o_shape = mn_spec, (m, n)
    else:
        per = n // col_blocks // tn
        o_spec = pl.BlockSpec((None, tm, tn), lambda i, j, kk: (j // per, i, j % per))
        o_shape = (col_blocks, m, n // col_blocks)
    res = pl.pallas_call(body, grid=(ni, nj, nk), in_specs=[a_spec, b_spec] + [mn_spec] * n_e + [_ANY] * len(c_in),
                         out_specs=[o_spec] * n_o + [_ANY] * len(c_out),
                         out_shape=[_sds(o_shape, dt) for dt in outs] + list(c_out),
                         scratch_shapes=([pltpu.VMEM((tm, tn), F32)] if nk > 1 else []) + c_scr, name=name,
                         compiler_params=_params(("arbitrary", "arbitrary", "arbitrary")))(a, b, *extras, *c_in)
    main = res[0] if n_o == 1 else res[:n_o]
    return main if comm is None else (main, res[n_o:])


def _attn_fwd(q, kv, n_ctx_tiles, tq, comm=None):
    t_all = q.shape[0]
    s_len = t_all - n_ctx_tiles * tq
    hd, groups = HEAD_DIM, N_HEADS // N_KV_HEADS
    scale = HEAD_DIM ** -0.5
    nq = s_len // tq

    def body(*refs):
        (q_ref, k_ref, v_ref), c_ins, (o_ref,), c_outs, _, c_sems = _split_refs(refs, 3, 1, 0, comm)
        kh, g, i = pl.program_id(0), pl.program_id(1), pl.program_id(2)
        if comm is not None:
            @pl.when(jnp.logical_and(jnp.logical_and(kh == 0, g == 0), i == 0))
            def _():
                comm.start(c_ins, c_outs, *c_sems)

        s = lax.dot_general(q_ref[...], k_ref[...], (((1,), (1,)), ((), ())), preferred_element_type=F32) * scale
        p = jnp.exp(s - jnp.max(s, axis=-1, keepdims=True))
        l = jnp.sum(p, axis=-1, keepdims=True)
        o = jnp.dot(p.astype(BF16), v_ref[...], preferred_element_type=F32) / l
        o_ref[...] = o.astype(o_ref.dtype)
        if comm is not None:
            @pl.when(jnp.logical_and(jnp.logical_and(kh == N_KV_HEADS - 1, g == groups - 1), i == nq - 1))
            def _():
                comm.finish(c_ins, c_outs, *c_sems)

    c_in, c_out, c_scr = (comm.ins, comm.out_shapes, comm.scratch()) if comm is not None else ([], [], [])
    res = pl.pallas_call(
        body, grid=(N_KV_HEADS, groups, nq),
        in_specs=[pl.BlockSpec((tq, hd), lambda kh, g, i: (i + n_ctx_tiles, kh * groups + g)),
                  pl.BlockSpec((t_all, hd), lambda kh, g, i: (0, kh)),
                  pl.BlockSpec((t_all, hd), lambda kh, g, i: (0, N_KV_HEADS + kh))] + [_ANY] * len(c_in),
        out_specs=[pl.BlockSpec((tq, hd), lambda kh, g, i: (i, kh * groups + g))] + [_ANY] * len(c_out),
        out_shape=[_sds((s_len, N_HEADS * hd), BF16)] + list(c_out), scratch_shapes=c_scr, name="attn_fwd",
        compiler_params=_params(("arbitrary", "arbitrary", "arbitrary")))(q, kv, kv, *c_in)
    return res[0] if comm is None else (res[0], res[1:])


def _attn_bwd(q, kv, d_ar, n_ctx_tiles, tq, comm=None):
    t_all = q.shape[0]
    s_len = t_all - n_ctx_tiles * tq
    hd, groups = HEAD_DIM, N_HEADS // N_KV_HEADS
    scale = HEAD_DIM ** -0.5
    nq = s_len // tq

    def body(*refs):
        (q_ref, k_ref, v_ref, do_ref), c_ins, (dq_ref, dkt_ref, dvt_ref), c_outs, _, c_sems = _split_refs(refs, 4, 3, 0, comm)
        first = jnp.logical_and(pl.program_id(1) == 0, pl.program_id(2) == 0)
        if comm is not None:
            @pl.when(jnp.logical_and(first, pl.program_id(0) == 0))
            def _():
                comm.start(c_ins, c_outs, *c_sems)

        @pl.when(first)
        def _():
            dkt_ref[...] = jnp.zeros_like(dkt_ref)
            dvt_ref[...] = jnp.zeros_like(dvt_ref)

        qv, kk, vv = q_ref[...], k_ref[...], v_ref[...]
        do = do_ref[...].astype(BF16)
        s = lax.dot_general(qv, kk, (((1,), (1,)), ((), ())), preferred_element_type=F32) * scale
        p = jnp.exp(s - jnp.max(s, axis=-1, keepdims=True))
        p = p / jnp.sum(p, axis=-1, keepdims=True)
        dp = lax.dot_general(do, vv, (((1,), (1,)), ((), ())), preferred_element_type=F32)
        ds = (p * (dp - jnp.sum(p * dp, axis=-1, keepdims=True)) * scale).astype(BF16)
        dq_ref[...] = jnp.dot(ds, kk, preferred_element_type=F32)
        dkt_ref[...] += jnp.dot(qv.T, ds, preferred_element_type=F32)
        dvt_ref[...] += jnp.dot(do.T, p.astype(BF16), preferred_element_type=F32)
        if comm is not None:
            last = jnp.logical_and(pl.program_id(1) == groups - 1, pl.program_id(2) == nq - 1)

            @pl.when(jnp.logical_and(last, pl.program_id(0) == N_KV_HEADS - 1))
            def _():
                comm.finish(c_ins, c_outs, *c_sems)

    c_in, c_out, c_scr = (comm.ins, comm.out_shapes, comm.scratch()) if comm is not None else ([], [], [])
    return pl.pallas_call(
        body, grid=(N_KV_HEADS, groups, nq),
        in_specs=[pl.BlockSpec((tq, hd), lambda kh, g, i: (i + n_ctx_tiles, kh * groups + g)),
                  pl.BlockSpec((t_all, hd), lambda kh, g, i: (0, kh)),
                  pl.BlockSpec((t_all, hd), lambda kh, g, i: (0, N_KV_HEADS + kh)),
                  pl.BlockSpec((tq, hd), lambda kh, g, i: (i, kh * groups + g))] + [_ANY] * len(c_in),
        out_specs=[pl.BlockSpec((tq, hd), lambda kh, g, i: (i, kh * groups + g)),
                   pl.BlockSpec((hd, t_all), lambda kh, g, i: (kh, 0)),
                   pl.BlockSpec((hd, t_all), lambda kh, g, i: (kh, 0))] + [_ANY] * len(c_out),
        out_shape=[_sds((s_len, N_HEADS * hd), F32), _sds((N_KV_HEADS * hd, t_all), F32),
                   _sds((N_KV_HEADS * hd, t_all), F32)] + list(c_out),
        scratch_shapes=c_scr, name="attn_bwd",
        compiler_params=_params(("arbitrary", "arbitrary", "arbitrary")))(q, kv, kv, d_ar, *c_in)


def _scan_order(nb, nc, reverse):
    if not reverse:
        return lambda i: i
    return lambda i: jnp.where(i < nc, nc - 1 - i, nb - 1 - (i - nc))


def _scan_fwd(name, a, b, nc, reverse):
    t_all, r, l = a.shape
    tb = SCAN_BLOCK
    nb = t_all // tb
    order = _scan_order(nb, nc, reverse)

    def body(a_ref, b_ref, h_ref, hp_ref, carry):
        @pl.when(pl.program_id(0) == 0)
        def _():
            carry[...] = jnp.zeros_like(carry)

        def step(s, h):
            t = tb - 1 - s if reverse else s
            hp_ref[t] = h
            h = a_ref[t] * h + b_ref[t]
            h_ref[t] = h
            return h

        carry[...] = lax.fori_loop(0, tb, step, carry[...], unroll=8)

    spec = pl.BlockSpec((tb, r, l), lambda i: (order(i), 0, 0))
    return pl.pallas_call(body, grid=(nb,), in_specs=[spec, spec], out_specs=[spec, spec],
                          out_shape=[_sds(a.shape, F32)] * 2, scratch_shapes=[pltpu.VMEM((r, l), F32)], name=name,
                          compiler_params=_params(("arbitrary",)))(a, b)


def _scan_bwd(name, a, dh, hp, nc, reverse):
    t_all, r, l = a.shape
    tb = SCAN_BLOCK
    nb = t_all // tb
    primal = _scan_order(nb, nc, reverse)

    def order(i):
        return primal(nb - 1 - i)

    def body(a_ref, dh_ref, hp_ref, da_ref, db_ref, carry):
        @pl.when(pl.program_id(0) == 0)
        def _():
            carry[...] = jnp.zeros_like(carry)

        def step(s, cr):
            t = s if reverse else tb - 1 - s
            lam = dh_ref[t] + cr
            db_ref[t] = lam
            da_ref[t] = lam * hp_ref[t]
            return a_ref[t] * lam

        carry[...] = lax.fori_loop(0, tb, step, carry[...], unroll=8)

    spec = pl.BlockSpec((tb, r, l), lambda i: (order(i), 0, 0))
    return pl.pallas_call(body, grid=(nb,), in_specs=[spec] * 3, out_specs=[spec, spec],
                          out_shape=[_sds(a.shape, F32)] * 2, scratch_shapes=[pltpu.VMEM((r, l), F32)], name=name,
                          compiler_params=_params(("arbitrary",)))(a, dh, hp)


def _shifted(prev, cur, nxt, k, pid, n_ctx_tiles, n_tiles):
    if k == 0:
        return cur
    tm = cur.shape[0]
    row = lax.broadcasted_iota(jnp.int32, cur.shape, 0)
    if k < 0:
        at_start = jnp.logical_or(pid == 0, pid == n_ctx_tiles)
        edge = jnp.where(at_start, 0.0, pltpu.roll(prev, -k, 0))
        return jnp.where(row < -k, edge, pltpu.roll(cur, -k, 0))
    at_end = jnp.logical_or(pid == n_ctx_tiles - 1, pid == n_tiles - 1)
    edge = jnp.where(at_end, 0.0, pltpu.roll(nxt, tm - k, 0))
    return jnp.where(row >= tm - k, edge, pltpu.roll(cur, tm - k, 0))


def _f_conv(pid, xp, xc, xn, w, b, *, n_ctx_tiles, n_tiles):
    y = b
    for j in range(CONV_W):
        y = y + _shifted(xp, xc, xn, j - CONV_W // 2, pid, n_ctx_tiles, n_tiles) * w[j:j + 1]
    return (y,)


def _f_conv_bwd(pid, xp, xc, xn, dp, dc, dn, w, *, n_ctx_tiles, n_tiles):
    dx = jnp.zeros_like(dc)
    dw = []
    for j in range(CONV_W):
        k = j - CONV_W // 2
        dx = dx + _shifted(dp, dc, dn, -k, pid, n_ctx_tiles, n_tiles) * w[j:j + 1]
        dw.append(jnp.sum(dc * _shifted(xp, xc, xn, k, pid, n_ctx_tiles, n_tiles), axis=0, keepdims=True))
    return (dx, jnp.concatenate(dw, axis=0), jnp.sum(dc, axis=0, keepdims=True))


def _mesh_pos():
    return lax.axis_index("x"), lax.axis_index("y"), lax.axis_index("c")


def _remote(src, dst, send_sems, recv_sems, k, to):
    return pltpu.make_async_remote_copy(src_ref=src, dst_ref=dst, send_sem=send_sems.at[k], recv_sem=recv_sems.at[k],
                                        device_id=to, device_id_type=pl.DeviceIdType.MESH)


def _neighbours():
    x, y, c = _mesh_pos()
    return (x, y, c), (x, y, 1 - c), [(1 - x, y), (x, 1 - y), (1 - x, 1 - y)]


def _gather_comm(arrays):
    n = len(arrays)
    per = 7

    def slot(out, blk):
        return out.at[4 * blk[0] + 2 * blk[1] + blk[2]]

    def start(ins, outs, send, recv, local):
        me, sib, chips = _neighbours()
        for ai in range(n):
            pltpu.make_async_copy(ins[ai], slot(outs[ai], me), local.at[ai]).start()
            _remote(ins[ai], slot(outs[ai], me), send, recv, ai * per, sib).start()
            for j, chip in enumerate(chips):
                _remote(ins[ai], slot(outs[ai], me), send, recv, ai * per + 1 + j, (*chip, me[2])).start()

    def finish(ins, outs, send, recv, local):
        me, sib, chips = _neighbours()
        for ai in range(n):
            for j, chip in enumerate(chips):
                blk = slot(outs[ai], (*chip, me[2]))
                _remote(blk, blk, send, recv, ai * per + 1 + j, me).wait_recv()
                _remote(blk, blk, send, recv, ai * per + 4 + j, sib).start()
        for ai in range(n):
            blk = slot(outs[ai], sib)
            _remote(blk, blk, send, recv, ai * per, me).wait_recv()
            for j, chip in enumerate(chips):
                blk = slot(outs[ai], (*chip, 1 - me[2]))
                _remote(blk, blk, send, recv, ai * per + 4 + j, me).wait_recv()
            for k in range(per):
                _remote(ins[ai], slot(outs[ai], me), send, recv, ai * per + k, sib).wait_send()
            pltpu.make_async_copy(ins[ai], slot(outs[ai], me), local.at[ai]).wait()

    return _Comm(arrays, [_sds((N_DEV,) + a.shape, a.dtype) for a in arrays], n * per, start, finish)


def _swap_comm(arrays):
    n = len(arrays)

    def start(ins, outs, send, recv, local):
        me, sib, _ = _neighbours()
        for ai in range(n):
            for q in range(4):
                _remote(ins[ai].at[2 * q + 1 - me[2]], outs[ai].at[q], send, recv, ai * 4 + q, sib).start()

    def finish(ins, outs, send, recv, local):
        me, sib, _ = _neighbours()
        for ai in range(n):
            for q in range(4):
                cp = _remote(ins[ai].at[q], outs[ai].at[q], send, recv, ai * 4 + q, sib)
                cp.wait_recv()
                cp.wait_send()

    return _Comm(arrays, [_sds((4,) + a.shape[1:], a.dtype) for a in arrays], n * 4, start, finish)


def _chips_comm(arrays):
    n = len(arrays)

    def start(ins, outs, send, recv, local):
        me, _, chips = _neighbours()
        mine = 2 * me[0] + me[1]
        for ai in range(n):
            pltpu.make_async_copy(ins[ai].at[mine], outs[ai].at[mine], local.at[ai]).start()
            for j, chip in enumerate(chips):
                _remote(ins[ai].at[2 * chip[0] + chip[1]], outs[ai].at[mine], send, recv, ai * 3 + j, (*chip, me[2])).start()

    def finish(ins, outs, send, recv, local):
        me, _, chips = _neighbours()
        mine = 2 * me[0] + me[1]
        for ai in range(n):
            for j, chip in enumerate(chips):
                theirs = 2 * chip[0] + chip[1]
                cp = _remote(ins[ai].at[theirs], outs[ai].at[theirs], send, recv, ai * 3 + j, (*chip, me[2]))
                cp.wait_recv()
                cp.wait_send()
            pltpu.make_async_copy(ins[ai].at[mine], outs[ai].at[mine], local.at[ai]).wait()

    return _Comm(arrays, [_sds(a.shape, a.dtype) for a in arrays], n * 3, start, finish)


def _run_comm(name, comm):
    n_in, n_out = len(comm.ins), len(comm.out_shapes)

    def body(*refs):
        ins, outs, sems = refs[:n_in], refs[n_in:n_in + n_out], refs[n_in + n_out:]
        comm.start(ins, outs, *sems)
        comm.finish(ins, outs, *sems)

    return pl.pallas_call(body, in_specs=[_ANY] * n_in, out_specs=[_ANY] * n_out, out_shape=comm.out_shapes, name=name,
                          scratch_shapes=comm.scratch(), compiler_params=pltpu.CompilerParams(has_side_effects=True))(*comm.ins)


def _chip_add(name, blocks, theirs, core):
    _, r, c = blocks.shape
    tr = _tile(r, _row_block(c, 2), 16)

    def body(core_ref, a_ref, b_ref, o_ref):
        o_ref[...] = (a_ref[...].astype(F32) + b_ref[...].astype(F32)).astype(o_ref.dtype)

    spec = pl.BlockSpec((None, tr, c), lambda q, i, core_ref: (q, i, 0))
    grid_spec = pltpu.PrefetchScalarGridSpec(
        num_scalar_prefetch=1, grid=(4, r // tr),
        in_specs=[pl.BlockSpec((None, tr, c), lambda q, i, core_ref: (2 * q + core_ref[0], i, 0)), spec], out_specs=spec)
    return pl.pallas_call(body, grid_spec=grid_spec, out_shape=_sds((4, r, c), blocks.dtype), name=name,
                          compiler_params=_params(("parallel", "parallel")))(jnp.reshape(core, (1,)).astype(jnp.int32), blocks, theirs)


def _sum_lead(name, a):
    n, r, c = a.shape
    tr = _tile(r, _row_block(c, 4 * n // 2), 8)

    def body(a_ref, o_ref):
        acc = a_ref[0]
        for j in range(1, n):
            acc = acc + a_ref[j]
        o_ref[...] = acc

    return pl.pallas_call(body, grid=(r // tr,), in_specs=[pl.BlockSpec((n, tr, c), lambda i: (0, i, 0))],
                          out_specs=pl.BlockSpec((tr, c), lambda i: (i, 0)), out_shape=_sds((r, c), F32), name=name,
                          compiler_params=_params(("parallel",)))(a)


def _adam_math(w, g, m, v):
    m = ADAM_B1 * m + (1.0 - ADAM_B1) * g
    v = ADAM_B2 * v + (1.0 - ADAM_B2) * (g * g)
    m_hat = m / (1.0 - ADAM_B1 ** ADAM_STEP)
    v_hat = v / (1.0 - ADAM_B2 ** ADAM_STEP)
    delta = -ADAM_LR * (m_hat / (jnp.sqrt(v_hat) + ADAM_EPS) + ADAM_WD * w)
    return delta, m, v


def _adam_recv(name, recvs, w, m, v):
    nl = len(recvs)
    n, rl, c = recvs[0].shape
    tr = _tile(rl, _row_block(c, 4), 16)
    per = rl // tr

    def body(*refs):
        g_refs = refs[:nl]
        w_ref, m_ref, v_ref, go_ref, d_ref, mo_ref, vo_ref = refs[nl:]
        for layer in range(nl):
            @pl.when(pl.program_id(0) == layer)
            def _(g_ref=g_refs[layer]):
                g = g_ref[0].astype(F32)
                for j in range(1, n):
                    g = g + g_ref[j].astype(F32)
                delta, m2, v2 = _adam_math(w_ref[...], g, m_ref[...], v_ref[...])
                go_ref[...] = g
                d_ref[...] = delta
                mo_ref[...] = m2
                vo_ref[...] = v2

    g_specs = [pl.BlockSpec((n, tr, c), lambda l, i, layer=layer: (0, jnp.where(l == layer, i, 0), 0)) for layer in range(nl)]
    spec = pl.BlockSpec((tr, c), lambda l, i: (l * per + i, 0))
    return pl.pallas_call(body, grid=(nl, per), in_specs=g_specs + [spec] * 3, out_specs=[spec] * 4,
                          out_shape=[_sds((nl * rl, c), F32)] * 4, name=name,
                          compiler_params=_params(("arbitrary", "arbitrary")))(*recvs, w, m, v)


def _adam_f32(name, g, w, m, v, comm=None):
    r, c = g.shape
    tr = _tile(r, _row_block(c, 4), 8)
    steps = r // tr

    def body(*refs):
        (g_ref, w_ref, m_ref, v_ref), c_ins, (d_ref, mo_ref, vo_ref), c_outs, _, c_sems = _split_refs(refs, 4, 3, 0, comm)
        if comm is not None:
            @pl.when(pl.program_id(0) == 0)
            def _():
                comm.start(c_ins, c_outs, *c_sems)

        delta, m2, v2 = _adam_math(w_ref[...], g_ref[...], m_ref[...], v_ref[...])
        d_ref[...] = delta
        mo_ref[...] = m2
        vo_ref[...] = v2
        if comm is not None:
            @pl.when(pl.program_id(0) == steps - 1)
            def _():
                comm.finish(c_ins, c_outs, *c_sems)

    spec = pl.BlockSpec((tr, c), lambda i: (i, 0))
    c_in, c_out, c_scr = (comm.ins, comm.out_shapes, comm.scratch()) if comm is not None else ([], [], [])
    return pl.pallas_call(body, grid=(steps,), in_specs=[spec] * 4 + [_ANY] * len(c_in), out_specs=[spec] * 3 + [_ANY] * len(c_out),
                          out_shape=[_sds((r, c), F32)] * 3 + list(c_out), scratch_shapes=c_scr, name=name,
                          compiler_params=_params(("arbitrary",)))(g, w, m, v, *c_in)


def _mod_fwd(c16, w_mod, b_loc):
    nl, d, n6 = w_mod.shape
    tn = _tile(n6, 512, 128)

    def body(c_ref, w_ref, b_ref, o_ref):
        cv = c_ref[...]
        s = cv * _sigmoid(cv)
        o_ref[0] = jnp.dot(s, w_ref[0], precision=HIGHEST, preferred_element_type=F32) + b_ref[0]

    return pl.pallas_call(
        body, grid=(nl, n6 // tn),
        in_specs=[pl.BlockSpec((MOD_ROWS, d), lambda i, j: (0, 0)), pl.BlockSpec((1, d, tn), lambda i, j: (i, 0, j)),
                  pl.BlockSpec((1, 1, tn), lambda i, j: (i, 0, j))],
        out_specs=pl.BlockSpec((1, MOD_ROWS, tn), lambda i, j: (i, 0, j)), out_shape=_sds((nl, MOD_ROWS, n6), F32),
        name="mod_fwd", compiler_params=_params(("parallel", "parallel")))(c16, w_mod, b_loc)


def _mod_bwd(c16, w_mod, dmod_loc):
    nl, d, n6 = w_mod.shape
    tn = _tile(n6, 512, 128)

    def body(c_ref, w_ref, dm_ref, dw_ref, ds_ref):
        @pl.when(jnp.logical_and(pl.program_id(0) == 0, pl.program_id(1) == 0))
        def _():
            ds_ref[...] = jnp.zeros_like(ds_ref)

        cv = c_ref[...]
        s = cv * _sigmoid(cv)
        dm = dm_ref[0]
        dw_ref[0] = lax.dot_general(s, dm, (((0,), (0,)), ((), ())), precision=HIGHEST, preferred_element_type=F32)
        ds_ref[...] += lax.dot_general(dm, w_ref[0], (((1,), (1,)), ((), ())), precision=HIGHEST, preferred_element_type=F32)

    return pl.pallas_call(
        body, grid=(nl, n6 // tn),
        in_specs=[pl.BlockSpec((MOD_ROWS, d), lambda i, j: (0, 0)), pl.BlockSpec((1, d, tn), lambda i, j: (i, 0, j)),
                  pl.BlockSpec((1, MOD_ROWS, tn), lambda i, j: (i, 0, j))],
        out_specs=[pl.BlockSpec((1, d, tn), lambda i, j: (i, 0, j)), pl.BlockSpec((MOD_ROWS, d), lambda i, j: (0, 0))],
        out_shape=[_sds((nl, d, n6), F32), _sds((MOD_ROWS, d), F32)], name="mod_bwd",
        compiler_params=_params(("arbitrary", "arbitrary")))(c16, w_mod, dmod_loc)


def _pack(parts):
    flat = [p.reshape(-1).astype(F32) for p in parts]
    offs = np.cumsum([0] + [f.shape[0] for f in flat])
    total = int(offs[-1])
    unit = (PACK_ROWS if total > PACK_ROWS * LANES else 8) * LANES
    padded = -(-total // unit) * unit
    slab = jnp.concatenate(flat + [jnp.zeros((padded - total,), F32)])
    return slab.reshape(padded // LANES, LANES), [int(o) for o in offs]


def _unshard_cols(seg, lead):
    n = seg.shape[1] // int(np.prod(lead)) if lead else seg.shape[1]
    a = seg.reshape((N_DEV,) + tuple(lead) + (n,))
    a = jnp.moveaxis(a, 0, len(lead))
    return a.reshape(tuple(lead) + (N_DEV * n,))


def _my_cols(a, me, n):
    start = (0,) * (a.ndim - 1) + (me * n,)
    return lax.dynamic_slice(a, start, a.shape[:-1] + (n,))


def _rope_tables(seq, n_ctx):
    rows = seq // GRID_W
    r_idx, c_idx = jnp.meshgrid(jnp.arange(rows), jnp.arange(GRID_W), indexing='ij')
    r_idx = r_idx.reshape(-1).astype(F32)
    c_idx = c_idx.reshape(-1).astype(F32)
    pairs = HEAD_DIM // 4
    freqs = ROPE_THETA ** (-jnp.arange(pairs, dtype=F32) / pairs)
    ang_r, ang_c = r_idx[:, None] * freqs, c_idx[:, None] * freqs
    cos = jnp.concatenate([jnp.cos(ang_r)] * 2 + [jnp.cos(ang_c)] * 2, axis=1)
    sin = jnp.concatenate([-jnp.sin(ang_r), jnp.sin(ang_r), -jnp.sin(ang_c), jnp.sin(ang_c)], axis=1)
    cos = jnp.concatenate([jnp.ones((n_ctx, HEAD_DIM), F32), cos], axis=0)
    sin = jnp.concatenate([jnp.zeros((n_ctx, HEAD_DIM), F32), sin], axis=0)
    lane = np.arange(HEAD_DIM)
    partner = np.where(lane % (2 * pairs) < pairs, lane + pairs, lane - pairs)
    perm = np.zeros((HEAD_DIM, HEAD_DIM), np.float32)
    perm[partner, lane] = 1.0
    return cos, sin, jnp.asarray(perm)


def kernel(x, c, ctx, c_ctx, w_mod, b_mod, norm_g, w_ff_in, w_ff_out, ar_w_in, ar_q_g, ar_k_g, ar_conv_w, ar_conv_b, ar_wa, ar_ba, ar_wx, ar_bx, ar_lambda, ar_w_out, gm_w_in, gm_b_in, gm_v_g, gm_v_b, gm_w_sp, gm_b_sp, gm_w_out, loss_target, m_c_ctx, m_w_mod, m_b_mod, m_norm_g, m_w_ff_in, m_w_ff_out, m_ar_w_in, m_ar_q_g, m_ar_k_g, m_ar_conv_w, m_ar_conv_b, m_ar_wa, m_ar_ba, m_ar_wx, m_ar_bx, m_ar_lambda, m_ar_w_out, m_gm_w_in, m_gm_b_in, m_gm_v_g, m_gm_v_b, m_gm_w_sp, m_gm_b_sp, m_gm_w_out, v_c_ctx, v_w_mod, v_b_mod, v_norm_g, v_w_ff_in, v_w_ff_out, v_ar_w_in, v_ar_q_g, v_ar_k_g, v_ar_conv_w, v_ar_conv_b, v_ar_wa, v_ar_ba, v_ar_wx, v_ar_bx, v_ar_lambda, v_ar_w_out, v_gm_w_in, v_gm_b_in, v_gm_v_g, v_gm_v_b, v_gm_w_sp, v_gm_b_sp, v_gm_w_out):
    given = dict(locals())
    wts = {n: given[n] for n in WEIGHTS}
    mom1 = {n: given["m_" + n] for n in WEIGHTS}
    mom2 = {n: given["v_" + n] for n in WEIGHTS}

    xi, yi, ci = _mesh_pos()
    me = 4 * xi + 2 * yi + ci

    seq, d = x.shape[1], x.shape[2]
    n_ctx = ctx.shape[1]
    t_all = n_ctx + seq
    n_layers = w_mod.shape[0]
    assert n_layers == 2 and ar_w_in.shape[0] == 1 and gm_w_in.shape[0] == 1
    d_ff = w_ff_in.shape[2] * N_DEV
    attn_w, kv_w = N_HEADS * HEAD_DIM, N_KV_HEADS * HEAD_DIM
    rnn_blocks = ar_wa.shape[2]
    d_rnn = rnn_blocks * RNN_BLOCK_W
    gm_groups = gm_w_sp.shape[1]
    d_gm = gm_groups * GM_GROUP_W
    ar_in = ar_w_in.shape[2] * N_DEV
    n6 = w_mod.shape[2]
    tm, tmb = ROW_TILE, ROW_TILE_BWD
    assert attn_w == d_rnn and ar_in == 3 * attn_w + 2 * kv_w and (3 * attn_w) % (2 * kv_w) == 0
    assert n_ctx % tm == 0 and seq % tm == 0 and n_ctx % SCAN_BLOCK == 0 and seq % SCAN_BLOCK == 0 and tm % CHUNK == 0
    nct, nctb = n_ctx // tm, n_ctx // tmb
    kv_blk = (3 * attn_w) // (2 * kv_w)
    lr = d_rnn // LANES

    x2, ctx2, tgt = x[0], ctx[0], loss_target[0]

    def cols_full(g):
        return jnp.moveaxis(g, 0, 1).reshape(g.shape[1], N_DEV * g.shape[2])

    small0, off0 = _pack([c[0], norm_g, ar_conv_w[0], ar_ba[0], ar_bx[0], ar_lambda[0], gm_b_in[0], gm_v_g[0], gm_v_b[0]])
    g_ar_in, gs0 = _run_comm("gather_first", _gather_comm([ar_w_in[0].astype(BF16), small0]))
    gs0 = gs0.reshape(N_DEV, -1)
    w_in = cols_full(g_ar_in)
    split = [attn_w, attn_w + 2 * kv_w, attn_w + 2 * kv_w + d_rnn]
    w_in = jnp.concatenate([w_in[:, :split[0]], w_in[:, split[1]:], w_in[:, split[0]:split[1]]], axis=1)
    w1, w2 = [None] * n_layers, [None] * n_layers

    def seg0(k):
        return gs0[:, off0[k]:off0[k + 1]]

    c_all = seg0(0)
    norm_full = _unshard_cols(seg0(1), (n_layers, 4))
    conv_w = _unshard_cols(seg0(2), (CONV_W,))
    ba, bx, lam = (_unshard_cols(seg0(k), (2,)) for k in (3, 4, 5))
    gm_b_in_f = seg0(6).reshape(1, 2 * d_gm)
    gm_vg, gm_vb = seg0(7).reshape(1, d_gm), seg0(8).reshape(1, d_gm)

    c16 = jnp.concatenate([c_all, c_ctx[None], jnp.zeros((MOD_ROWS - N_DEV - 1, d), F32)], axis=0)
    b_loc = _my_cols(b_mod, me, n6)[:, None, :]
    mod_loc = _mod_fwd(c16, w_mod, b_loc)
    (g_mod,) = _run_comm("gather_mod", _gather_comm([mod_loc]))
    mod_all = jnp.moveaxis(g_mod, 0, 2).reshape(n_layers, MOD_ROWS, N_DEV * n6)
    ml = lax.dynamic_index_in_dim(mod_all, me, axis=1, keepdims=False).reshape(n_layers, 6, d)
    mc = mod_all[:, N_DEV].reshape(n_layers, 6, d)

    def row(a, *idx):
        return a[idx][None]

    cos, sin, perm = _rope_tables(seq, n_ctx)
    wa3 = ar_wa[0].reshape(2 * rnn_blocks, RNN_BLOCK_W, RNN_BLOCK_W)
    wx3 = ar_wx[0].reshape(2 * rnn_blocks, RNN_BLOCK_W, RNN_BLOCK_W)
    conv_b = ar_conv_b
    q_g, k_g = ar_q_g, ar_k_g
    w_sp = gm_w_sp[0]
    bsp_t = jnp.pad(gm_b_sp[0].T, ((0, 0), (0, LANES - gm_groups)))
    expand = np.zeros((LANES, d_gm), np.float32)
    for g in range(gm_groups):
        expand[g, g * GM_GROUP_W:(g + 1) * GM_GROUP_W] = 1.0
    expand = jnp.asarray(expand)

    def relu2(acc):
        r = jnp.maximum(acc, 0.0)
        return (r * r,)

    def relu2_bwd(acc, act):
        return (acc * (2.0 * jnp.sqrt(act.astype(F32))),)

    def ff_in_shard(i):
        return w_ff_in[i].astype(BF16)

    def ff_out_shard(i):
        return w_ff_out[i].astype(BF16)

    xtok = jnp.concatenate([ctx2, x2], axis=0)
    pre0_args = [row(norm_full, 0, 0), row(mc, 0, 0), row(mc, 0, 1), row(ml, 0, 0), row(ml, 0, 1)]
    f_pre0 = functools.partial(_f_pre_ctx, n_ctx_tiles=nct)
    (h0,) = _rowwise("pre0", f_pre0, t_all, tm, [_t(xtok)], pre0_args, [(d, BF16)])
    tm_tok = _tile(t_all, 640, 16)
    proj, (g_ar_out,) = _matmul("ar_in", h0, w_in, tm=tm_tok, comm=_gather_comm([ar_w_out[0].astype(BF16)]))
    w_out = g_ar_out.reshape(attn_w + d_rnn, d)
    f_qkv = functools.partial(_f_qkv, nh=N_HEADS, nkv=N_KV_HEADS)
    qkv_tiled = [_t(proj, 0, 0, attn_w), _t(proj, 0, kv_blk, 2 * kv_w), _t(cos), _t(sin)]
    q_r, kv_r = _rowwise("qkv", f_qkv, t_all, tm, qkv_tiled, [q_g, k_g, perm], [(attn_w, BF16), (2 * kv_w, BF16)])
    attn_o, (g_ff_in0, g_ff_out0) = _attn_fwd(q_r, kv_r, nct, tm, comm=_gather_comm([ff_in_shard(0), ff_out_shard(0)]))
    w1[0], w2[0] = cols_full(g_ff_in0), g_ff_out0.reshape(d_ff, d)

    def with_neighbours(a, t, col_blk=0, width=None):
        return [_t(a, -t, col_blk, width), _t(a, 0, col_blk, width), _t(a, t, col_blk, width)]

    f_conv = functools.partial(_f_conv, n_ctx_tiles=nct, n_tiles=t_all // tm)
    (xc,) = _rowwise("conv", f_conv, t_all, tm, with_neighbours(proj, tm, 1, d_rnn), [conv_w, conv_b], [(d_rnn, F32)])
    f_gates = functools.partial(_f_gates, nb=rnn_blocks)
    gate_full = [wa3, ba, wx3, bx, lam]
    a_f, b_f, a_b, b_b = _rowwise("gates", f_gates, t_all, tm, [_t(xc)], gate_full, [(d_rnn, F32)] * 4)

    def to3(a):
        return a.reshape(a.shape[0], lr, LANES)

    nc_scan = n_ctx // SCAN_BLOCK
    h_f, hp_f = _scan_fwd("scan_f", to3(a_f), to3(b_f), nc_scan, False)
    h_b, hp_b = _scan_fwd("scan_b", to3(a_b), to3(b_b), nc_scan, True)
    h_f2, h_b2 = h_f.reshape(t_all, d_rnn), h_b.reshape(t_all, d_rnn)
    rnn_tiled = [_t(h_f2, n_ctx), _t(h_b2, n_ctx), _t(proj, n_ctx, 2, d_rnn)]
    (rnn_o,) = _rowwise("rnn_out", _f_rnnout, seq, tm, rnn_tiled, [], [(d_rnn, BF16)])
    ar = jnp.concatenate([attn_o, rnn_o], axis=1)
    o0 = _matmul("ar_out", ar, w_out)
    mid0_args = [row(norm_full, 0, 1), row(ml, 0, 2), row(norm_full, 0, 2), row(ml, 0, 3), row(ml, 0, 4)]
    x1, h2_0 = _rowwise("mid0", _f_mid, seq, tm, [_t(x2), _t(o0)], mid0_args, [(d, F32), (d, BF16)])
    act0, (g_gm_in, g_gm_out) = _matmul("ff_in_0", h2_0, w1[0], outs=(BF16,), epilogue=relu2,
                                              comm=_gather_comm([gm_w_in[0].astype(BF16), gm_w_out[0].astype(BF16)]))
    w_gi, w_go = cols_full(g_gm_in), g_gm_out.reshape(d_gm, d)
    m0, (g_ff_in1,) = _matmul("ff_out_0", act0, w2[0], tm=MM_TILE_M // 2, tk=2 * MM_TILE_K, comm=_gather_comm([ff_in_shard(1)]))
    w1[1] = cols_full(g_ff_in1)
    post0_args = [row(norm_full, 0, 3), row(ml, 0, 5)]
    (x2l,) = _rowwise("post0", _f_post, seq, tm, [_t(x1), _t(m0)], post0_args, [(d, F32)])

    pre1_args = [row(norm_full, 1, 0), row(ml, 1, 0), row(ml, 1, 1)]
    (h1,) = _rowwise("pre1", _f_pre, seq, tm, [_t(x2l)], pre1_args, [(d, BF16)])
    zg = _matmul("gm_in", h1, w_gi)
    f_gm = functools.partial(_f_gm, n_chunks=tmb // CHUNK, groups=gm_groups)
    gm_full = [gm_b_in_f[:, :d_gm], gm_b_in_f[:, d_gm:], gm_vg, gm_vb, w_sp, bsp_t, expand]
    gm_tiled = [_t(zg, 0, 0, d_gm), _t(zg, 0, 1, d_gm)]
    (gmix,) = _rowwise("gm_mix", f_gm, seq, tmb, gm_tiled, gm_full, [(d_gm, BF16)])
    o1 = _matmul("gm_out", gmix, w_go)
    mid1_args = [row(norm_full, 1, 1), row(ml, 1, 2), row(norm_full, 1, 2), row(ml, 1, 3), row(ml, 1, 4)]
    x3, h2_1 = _rowwise("mid1", _f_mid, seq, tm, [_t(x2l), _t(o1)], mid1_args, [(d, F32), (d, BF16)])
    act1, (g_ff_out1,) = _matmul("ff_in_1", h2_1, w1[1], outs=(BF16,), epilogue=relu2,
                                       comm=_gather_comm([ff_out_shard(1)]))
    w2[1] = g_ff_out1.reshape(d_ff, d)
    m1 = _matmul("ff_out_1", act1, w2[1], tm=MM_TILE_M // 2, tk=2 * MM_TILE_K)
    post1_args = [row(norm_full, 1, 3), row(ml, 1, 5)]

    def f_loss(pid, xv, ov, tv, g, gate):
        err = _f_post(pid, xv, ov, g, gate)[0] - tv
        part = 0.5 * jnp.sum(err * err) / d
        return (err / d, jnp.full((8, LANES), part, F32))

    dy, loss_acc = _rowwise("loss", f_loss, seq, tm, [_t(x3), _t(m1), _t(tgt)], post1_args, [(d, F32)], [(8, LANES)])
    loss = lax.psum(loss_acc[0, 0], ("x", "y", "c"))

    d_norm = [[None] * 4 for _ in range(n_layers)]
    d_ml = [[None] * 6 for _ in range(n_layers)]
    recv = {}

    def cols_blocks(g):
        return jnp.moveaxis(g.reshape(g.shape[0], N_DEV, g.shape[1] // N_DEV), 1, 0)

    def rows_blocks(g):
        return g.reshape(N_DEV, g.shape[0] // N_DEV, g.shape[1])

    chip_sums = {}

    def chip_add(key, blocks, theirs):
        chip_sums[key] = _chip_add(key + "_add", blocks, theirs, ci)

    def mlp_bwd(i, dm, act, h2, first_comm=None):
        dw2 = _matmul(f"ff_out_dw_{i}", act, dm, ta=True, outs=(BF16,), comm=first_comm)
        dw2, carried = dw2 if first_comm is not None else (dw2, ())
        blk2 = rows_blocks(dw2)
        dz, (theirs,) = _matmul(f"ff_out_dx_{i}", dm, w2[i], tb=True, outs=(BF16,), extras=(act,), epilogue=relu2_bwd,
                                comm=_swap_comm([blk2]))
        chip_add(f"ff_out_{i}", blk2, theirs)
        blk1, (recv[f"ff_out_{i}"],) = _matmul(f"ff_in_dw_{i}", h2, dz, ta=True, outs=(BF16,), col_blocks=N_DEV,
                                              comm=_chips_comm([chip_sums[f"ff_out_{i}"]]))
        dh2, (theirs,) = _matmul(f"ff_in_dx_{i}", dz, w1[i], tb=True, tm=MM_TILE_M // 2, tk=2 * MM_TILE_K,
                                 comm=_swap_comm([blk1]))
        chip_add(f"ff_in_{i}", blk1, theirs)
        return dh2, carried

    def post_bwd(i, xin, m, args, dout):
        res = _rowwise(f"post_bwd{i}", _bwd_of(_f_post, 2, 1, (0, 1, 2, 3)), seq, tmb, [_t(xin), _t(m), _t(dout)], args,
                       [(d, F32), (d, BF16)], [(1, d), (1, d)])
        d_norm[i][3], d_ml[i][5] = res[2], res[3]
        return res[0], res[1]

    def mid_bwd(i, xin, o, args, dx1, dh2):
        res = _rowwise(f"mid_bwd{i}", _bwd_of(_f_mid, 2, 2, (0, 1, 2, 3, 4, 5, 6)), seq, tmb,
                       [_t(xin), _t(o), _t(dx1), _t(dh2)], args, [(d, F32), (d, BF16)], [(1, d)] * 5)
        d_norm[i][1], d_ml[i][2], d_norm[i][2], d_ml[i][3], d_ml[i][4] = res[2:]
        return res[0], res[1]

    dx3, dm1 = post_bwd(1, x3, m1, post1_args, dy)
    dh2_1, _ = mlp_bwd(1, dm1, act1, h2_1)
    dx2a, do1 = mid_bwd(1, x2l, o1, mid1_args, dx3, dh2_1)
    blk_go = rows_blocks(_matmul("gm_out_dw", gmix, do1, ta=True, outs=(BF16,)))
    dgmix, (theirs,) = _matmul("gm_out_dx", do1, w_go, tb=True, comm=_swap_comm([blk_go]))
    chip_add("gm_out", blk_go, theirs)
    gm_res = _rowwise("gm_mix_bwd", _bwd_of(f_gm, 2, 1, (0, 1, 2, 3, 4, 5, 6, 7)), seq, tmb,
                      gm_tiled + [_t(dgmix)], gm_full, [(d_gm, BF16), (d_gm, BF16)],
                      [(1, d_gm)] * 4 + [w_sp.shape, bsp_t.shape])
    dzg = jnp.concatenate([gm_res[0], gm_res[1]], axis=1)
    g_gm_b_in = jnp.concatenate([gm_res[2], gm_res[3]], axis=1)
    g_gm_vg, g_gm_vb, g_w_sp = gm_res[4], gm_res[5], gm_res[6]
    g_b_sp = gm_res[7][:, :gm_groups].T
    dh1 = _matmul("gm_in_dx", dzg, w_gi, tb=True)
    blk_gi = _matmul("gm_in_dw", h1, dzg, ta=True, outs=(BF16,), col_blocks=N_DEV)

    def f_pre_bwd(pid, xv, dh, dxa, g, sh, sc):
        dxv, dg, dsh, dsc = _bwd_of(_f_pre, 1, 1, (0, 1, 2, 3))(pid, xv, dh, g, sh, sc)
        return (dxv + dxa, dg, dsh, dsc)

    res = _rowwise("pre_bwd1", f_pre_bwd, seq, tmb, [_t(x2l), _t(dh1), _t(dx2a)], pre1_args, [(d, F32)], [(1, d)] * 3)
    dx2l = res[0]
    d_norm[1][0], d_ml[1][0], d_ml[1][1] = res[1:]

    dx1, dm0 = post_bwd(0, x1, m0, post0_args, dx2l)
    dh2_0, (theirs,) = mlp_bwd(0, dm0, act0, h2_0, first_comm=_swap_comm([blk_gi]))
    chip_add("gm_in", blk_gi, theirs)
    dxa, do0 = mid_bwd(0, x2, o0, mid0_args, dx1, dh2_0)
    blk_out = rows_blocks(_matmul("ar_out_dw", ar, do0, ta=True, outs=(BF16,)))
    d_ar, (theirs,) = _matmul("ar_out_dx", do0, w_out, tb=True, comm=_swap_comm([blk_out]))
    chip_add("ar_out", blk_out, theirs)

    late = ["ff_in_1", "gm_out", "gm_in", "ff_in_0"]
    dq, dkt, dvt, *landed = _attn_bwd(q_r, kv_r, d_ar, nct, tm, comm=_chips_comm([chip_sums[k] for k in late]))
    recv.update(zip(late, landed))
    dq_all = jnp.concatenate([jnp.zeros((n_ctx, attn_w), F32), dq], axis=0)
    dkv_all = jnp.concatenate([dkt, dvt], axis=0).T
    qkv_res = _rowwise("qkv_bwd", _bwd_of(f_qkv, 4, 2, (0, 1, 4, 5)), t_all, tmb, qkv_tiled + [_t(dq_all), _t(dkv_all)],
                       [q_g, k_g, perm], [(attn_w, BF16), (2 * kv_w, BF16)], [q_g.shape, k_g.shape])
    dproj_q, dproj_kv, g_q_g, g_k_g = qkv_res

    rnn_res = _rowwise("rnn_out_bwd", _bwd_of(_f_rnnout, 3, 1, (0, 2)), seq, tmb, rnn_tiled + [_t(d_ar, 0, 1, d_rnn)], [],
                       [(d_rnn, F32), (d_rnn, BF16)])
    zc = jnp.zeros((n_ctx, d_rnn), F32)
    dh_all = to3(jnp.concatenate([zc, rnn_res[0]], axis=0))
    dproj_g = jnp.concatenate([zc.astype(BF16), rnn_res[1]], axis=0)
    da_f, db_f = _scan_bwd("scan_f_bwd", to3(a_f), dh_all, hp_f, nc_scan, False)
    da_b, db_b = _scan_bwd("scan_b_bwd", to3(a_b), dh_all, hp_b, nc_scan, True)
    gate_cts = [_t(a.reshape(t_all, d_rnn)) for a in (da_f, db_f, da_b, db_b)]
    gates_res = _rowwise("gates_bwd", _bwd_of(f_gates, 1, 4, (0, 1, 2, 3, 4, 5)), t_all, min(tmb, GATES_BWD_TILE),
                         [_t(xc)] + gate_cts, gate_full, [(d_rnn, F32)], [wa3.shape, ba.shape, wx3.shape, bx.shape, lam.shape])
    dxc, g_wa, g_ba, g_wx, g_bx, g_lam = gates_res
    f_conv_b = functools.partial(_f_conv_bwd, n_ctx_tiles=nctb, n_tiles=t_all // tmb)
    conv_tiled = with_neighbours(proj, tmb, 1, d_rnn) + with_neighbours(dxc, tmb)
    dproj_x, g_conv_w, g_conv_b = _rowwise("conv_bwd", f_conv_b, t_all, tmb, conv_tiled, [conv_w],
                                           [(d_rnn, BF16)], [conv_w.shape, (1, d_rnn)])
    dproj = jnp.concatenate([dproj_q, dproj_x, dproj_g, dproj_kv], axis=1)
    dh0, (recv["ar_out"],) = _matmul("ar_in_dx", dproj, w_in, tb=True, tm=tm_tok, comm=_chips_comm([chip_sums["ar_out"]]))
    sq_pack, off_sq = _pack([g_wa, g_wx, g_w_sp])
    g_w_in, (g_sq,) = _matmul("ar_in_dw", h0, dproj, ta=True, outs=(BF16,), comm=_gather_comm([sq_pack]))
    g_w_in = jnp.concatenate([g_w_in[:, :attn_w], g_w_in[:, 3 * attn_w:], g_w_in[:, attn_w:3 * attn_w]], axis=1)
    blk_in = cols_blocks(g_w_in)

    dxa_all = jnp.concatenate([jnp.zeros((n_ctx, d), F32), dxa], axis=0)
    f_pre0b = functools.partial(_f_pre_ctx, n_ctx_tiles=nctb)

    def f_pre0_bwd(pid, xv, dh, dxp, g, shc, scc, shl, scl):
        grads = _bwd_of(f_pre0b, 1, 1, (0, 1, 2, 3, 4, 5))(pid, xv, dh, g, shc, scc, shl, scl)
        return (grads[0] + dxp,) + tuple(grads[1:])

    res = _rowwise("pre_bwd0", f_pre0_bwd, t_all, tmb, [_t(xtok), _t(dh0), _t(dxa_all)], pre0_args, [(d, F32)], [(1, d)] * 5,
                   comm=_swap_comm([blk_in]))
    grad_x = res[0][n_ctx:][None]
    d_norm[0][0], d_mc_shift, d_mc_scale, d_ml[0][0], d_ml[0][1] = res[1:6]
    chip_add("ar_in", blk_in, res[6])

    z1d = jnp.zeros((1, d), F32)
    dml = jnp.concatenate([jnp.concatenate(r, axis=0)[None] for r in d_ml], axis=0)
    dmc = jnp.concatenate([jnp.concatenate([d_mc_shift, d_mc_scale] + [z1d] * 4, axis=0)[None],
                           jnp.zeros((n_layers - 1, 6, d), F32)], axis=0)
    g_norm = jnp.concatenate([jnp.concatenate(r, axis=0)[None] for r in d_norm], axis=0)
    small_parts = [dmc, g_norm, g_q_g, g_k_g, g_conv_w, g_conv_b, g_ba, g_bx, g_lam, g_gm_b_in, g_gm_vg, g_gm_vb, g_b_sp]
    small2, off2 = _pack([dml] + small_parts)
    (gs2,) = _run_comm("gather_small_grads", _gather_comm([small2]))
    dml_all = gs2.reshape(N_DEV, -1)[:, :off2[1]].reshape(N_DEV, n_layers, 6 * d)
    summed = _sum_lead("sum_small_grads", gs2).reshape(-1)
    summed_sq = _sum_lead("sum_square_grads", g_sq).reshape(-1)

    def seg2(k, shape):
        return summed[off2[k + 1]:off2[k + 2]].reshape(shape)

    def seg_sq(k, shape):
        return summed_sq[off_sq[k]:off_sq[k + 1]].reshape(shape)

    dmc_sum = seg2(0, (n_layers, 6 * d))
    dmod_rows = jnp.concatenate([jnp.moveaxis(dml_all, 0, 1), dmc_sum[:, None, :],
                                 jnp.zeros((n_layers, MOD_ROWS - N_DEV - 1, 6 * d), F32)], axis=1)
    g_b_mod = _sum_lead("sum_b_mod", jnp.moveaxis(dmod_rows, 1, 0).reshape(MOD_ROWS, n_layers * 6 * d // LANES, LANES))
    g_b_mod = g_b_mod.reshape(n_layers, 6 * d)
    g_w_mod, ds16 = _mod_bwd(c16, w_mod, _my_cols(dmod_rows, me, n6))
    (g_ds,) = _run_comm("gather_dctx", _gather_comm([ds16[N_DEV].reshape(d // LANES, LANES)]))
    ds_ctx = _sum_lead("sum_dctx", g_ds)
    (g_c_ctx,) = _rowwise("silu_bwd", _f_silu_mul, d // LANES, d // LANES, [_t(c_ctx.reshape(d // LANES, LANES)), _t(ds_ctx)],
                          [], [(LANES, F32)])
    g_c_ctx = g_c_ctx.reshape(d)

    grads = {
        'c_ctx': g_c_ctx, 'b_mod': g_b_mod,
        'norm_g': _my_cols(seg2(1, (n_layers, 4, d)), me, d // N_DEV),
        'ar_q_g': seg2(2, ar_q_g.shape), 'ar_k_g': seg2(3, ar_k_g.shape),
        'ar_conv_w': _my_cols(seg2(4, (1, CONV_W, d_rnn)), me, d_rnn // N_DEV),
        'ar_conv_b': seg2(5, ar_conv_b.shape),
        'ar_ba': _my_cols(seg2(6, (1, 2, d_rnn)), me, d_rnn // N_DEV),
        'ar_bx': _my_cols(seg2(7, (1, 2, d_rnn)), me, d_rnn // N_DEV),
        'ar_lambda': _my_cols(seg2(8, (1, 2, d_rnn)), me, d_rnn // N_DEV),
        'gm_b_in': _my_cols(seg2(9, (1, 2 * d_gm)), me, 2 * d_gm // N_DEV),
        'gm_v_g': _my_cols(seg2(10, (1, d_gm)), me, d_gm // N_DEV),
        'gm_v_b': _my_cols(seg2(11, (1, d_gm)), me, d_gm // N_DEV),
        'gm_b_sp': seg2(12, gm_b_sp.shape),
        'ar_wa': seg_sq(0, ar_wa.shape), 'ar_wx': seg_sq(1, ar_wx.shape), 'gm_w_sp': seg_sq(2, gm_w_sp.shape),
    }
    deltas, new_m, new_v = {}, {}, {}

    small_names = list(grads)
    wp, offw = _pack([wts[n] for n in small_names])
    mp, _ = _pack([mom1[n] for n in small_names])
    vp, _ = _pack([mom2[n] for n in small_names])
    gp, _ = _pack([grads[n] for n in small_names])
    dp, mp2, vp2 = _adam_f32("adam_small", gp, wp, mp, vp)
    for k, n in enumerate(small_names):
        for dst, slab in ((deltas, dp), (new_m, mp2), (new_v, vp2)):
            dst[n] = slab.reshape(-1)[offw[k]:offw[k + 1]].reshape(wts[n].shape)

    grads['w_mod'] = g_w_mod
    dw, mw, vw, recv["ar_in"] = _adam_f32("adam_w_mod", g_w_mod.reshape(n_layers * d, n6), w_mod.reshape(n_layers * d, n6),
                                          m_w_mod.reshape(n_layers * d, n6), v_w_mod.reshape(n_layers * d, n6),
                                          comm=_chips_comm([chip_sums["ar_in"]]))
    deltas['w_mod'], new_m['w_mod'], new_v['w_mod'] = (a.reshape(w_mod.shape) for a in (dw, mw, vw))

    received = {
        'w_ff_in': [recv[f"ff_in_{i}"] for i in range(n_layers)], 'w_ff_out': [recv[f"ff_out_{i}"] for i in range(n_layers)],
        'ar_w_in': [recv["ar_in"]], 'ar_w_out': [recv["ar_out"]], 'gm_w_in': [recv["gm_in"]], 'gm_w_out': [recv["gm_out"]]}
    for n, r in received.items():
        shp = wts[n].shape
        flat = (shp[0] * shp[1], shp[2])
        res = _adam_recv("adam_" + n, r, wts[n].reshape(flat), mom1[n].reshape(flat), mom2[n].reshape(flat))
        grads[n], deltas[n], new_m[n], new_v[n] = (a.reshape(shp) for a in res)

    return (loss, grad_x, *[grads[n] for n in WEIGHTS], *[deltas[n] for n in WEIGHTS],
            *[new_m[n] for n in WEIGHTS], *[new_v[n] for n in WEIGHTS])
```

```python
import functools

import numpy as np
import jax
import jax.numpy as jnp
from jax import lax
from jax.experimental import pallas as pl
from jax.experimental.pallas import tpu as pltpu

F32 = jnp.float32
BF16 = jnp.bfloat16
HIGHEST = lax.Precision.HIGHEST
LOG2_E = 1.4426950408889634
LN_2 = 0.6931471805599453

GRID_W = 64
N_HEADS = 8
N_KV_HEADS = 2
HEAD_DIM = 128
ROPE_THETA = 10000.0
RNN_BLOCK_W = 128
CONV_W = 4
RG_C = 8.0
GM_GROUP_W = 128
CHUNK = 128
EPS = 1e-6
ADAM_LR = 0.001
ADAM_B1 = 0.9
ADAM_B2 = 0.999
ADAM_EPS = 1e-08
ADAM_WD = 0.01
ADAM_STEP = 10

N_DEV = 8
MOD_ROWS = 16
LANES = 128
ROW_TILE = 256
ROW_TILE_BWD = 256
GATES_BWD_TILE = 128
SCAN_BLOCK = 256
VMEM_LIMIT = 56 * 1024 * 1024
PACK_ROWS = 512
MM_TILE_M = 1024
MM_TILE_N = 1024
MM_TILE_K = 2048

WEIGHTS = ['c_ctx', 'w_mod', 'b_mod', 'norm_g', 'w_ff_in', 'w_ff_out', 'ar_w_in', 'ar_q_g', 'ar_k_g', 'ar_conv_w',
           'ar_conv_b', 'ar_wa', 'ar_ba', 'ar_wx', 'ar_bx', 'ar_lambda', 'ar_w_out', 'gm_w_in', 'gm_b_in', 'gm_v_g',
           'gm_v_b', 'gm_w_sp', 'gm_b_sp', 'gm_w_out']


def _sds(shape, dtype):
    return jax.ShapeDtypeStruct(tuple(shape), dtype)


def _tile(dim, pref, align):
    t = (min(pref, dim) // align) * align
    while t >= align:
        if dim % t == 0:
            return t
        t -= align
    return dim


def _params(sem):
    return pltpu.CompilerParams(dimension_semantics=sem, vmem_limit_bytes=VMEM_LIMIT)


def _rms(x, g):
    return x * lax.rsqrt(jnp.mean(x * x, axis=-1, keepdims=True) + EPS) * g


def _gelu(x):
    return 0.5 * x * (1.0 + jnp.tanh(0.7978845608028654 * (x + 0.044715 * (x * x * x))))


def _sigmoid(x):
    return 0.5 * (jnp.tanh(0.5 * x) + 1.0)


def _log1p_pos(u):
    small = u < 1e-3
    us = jnp.where(small, u, 0.0)
    return jnp.where(small, us * (1.0 - us * (0.5 - us * (1.0 / 3.0))), jnp.log(1.0 + u))


def _softplus(x):
    return jnp.maximum(x, 0.0) + _log1p_pos(jnp.exp(-jnp.abs(x)))


def _expm1(x):
    small = jnp.abs(x) < 0.3
    xs = jnp.where(small, x, 0.0)
    poly = xs * (1.0 + xs * (1.0 / 2 + xs * (1.0 / 6 + xs * (1.0 / 24 + xs * (1.0 / 120 + xs * (1.0 / 720 + xs * (1.0 / 5040)))))))
    return jnp.where(small, poly, jnp.exp(x) - 1.0)


def _f_pre_ctx(pid, xc, xl, g, shc, scc, shl, scl, *, n_ctx_tiles):
    is_ctx = pid < n_ctx_tiles
    x = jnp.where(is_ctx, xc, xl)
    sh = jnp.where(is_ctx, shc, shl)
    sc = jnp.where(is_ctx, scc, scl)
    return (_rms(x, g) * (1.0 + sc) + sh,)


def _f_pre(pid, x, g, sh, sc):
    return (_rms(x, g) * (1.0 + sc) + sh,)


def _f_mid(pid, x, o, g1, gate, g2, sh, sc):
    x1 = x + gate * _rms(o, g1)
    return (x1, _rms(x1, g2) * (1.0 + sc) + sh)


def _f_post(pid, x, o, g, gate):
    return (x + gate * _rms(o, g),)


def _f_qkv(pid, pq, pkv, cos, sin, q_g, k_g, perm, *, nh, nkv):
    hd = HEAD_DIM

    def norm_rope(xh, g):
        y = _rms(xh, g)
        return y * cos + jnp.dot(y, perm, precision=HIGHEST, preferred_element_type=F32) * sin

    qs = [norm_rope(pq[:, h * hd:(h + 1) * hd], q_g) * (HEAD_DIM ** -0.5 * LOG2_E) for h in range(nh)]
    ks = [norm_rope(pkv[:, h * hd:(h + 1) * hd], k_g) for h in range(nkv)]
    return (jnp.concatenate(qs, axis=1), jnp.concatenate(ks + [pkv[:, nkv * hd:]], axis=1))


def _f_gates(pid, x, wa, ba, wx, bx, lam, *, nb):
    w = RNN_BLOCK_W
    outs = []
    for d in range(2):
        ra, ri = [], []
        for n in range(nb):
            xn = x[:, n * w:(n + 1) * w].astype(BF16)
            ra.append(jnp.dot(xn, wa[d * nb + n].astype(BF16), preferred_element_type=F32))
            ri.append(jnp.dot(xn, wx[d * nb + n].astype(BF16), preferred_element_type=F32))
        r = _sigmoid(jnp.concatenate(ra, axis=1) + ba[d:d + 1])
        i = _sigmoid(jnp.concatenate(ri, axis=1) + bx[d:d + 1])
        log_a = -RG_C * r * _softplus(-lam[d:d + 1])
        outs.append(jnp.exp(log_a))
        outs.append(jnp.sqrt(-_expm1(2.0 * log_a)) * (i * x))
    return tuple(outs)


def _f_rnnout(pid, hf, hb, gr):
    return ((hf + hb) * _gelu(gr),)


def _f_gm(pid, zu, zv, bu, bv, v_g, v_b, w_sp, bsp_t, expand, *, n_chunks, groups):
    u = _gelu(zu + bu)
    v = _gelu(zv + bv)
    mu = jnp.mean(v, axis=-1, keepdims=True)
    vc = v - mu
    v = vc * lax.rsqrt(jnp.mean(vc * vc, axis=-1, keepdims=True) + EPS) * v_g + v_b
    bias = jnp.dot(bsp_t, expand, precision=HIGHEST, preferred_element_type=F32)
    outs = []
    for c in range(n_chunks):
        vch = v[c * CHUNK:(c + 1) * CHUNK]
        cols = [jnp.dot(w_sp[g].astype(BF16), vch[:, g * GM_GROUP_W:(g + 1) * GM_GROUP_W].astype(BF16),
                        preferred_element_type=F32) for g in range(groups)]
        outs.append(u[c * CHUNK:(c + 1) * CHUNK] * (jnp.concatenate(cols, axis=1) + bias))
    return (jnp.concatenate(outs, axis=0),)


def _f_silu_mul(pid, c, d):
    return (d * jax.grad(lambda z: jnp.sum(z * _sigmoid(z)))(c),)


def _bwd_of(fn, n_tiled, n_ct, want):
    def bwd(pid, *args):
        tiles = [t.astype(F32) for t in args[:n_tiled]]
        cts = args[n_tiled:n_tiled + n_ct]
        fulls = list(args[n_tiled + n_ct:])
        outs, vjp = jax.vjp(lambda *a: fn(pid, *a), *tiles, *fulls)
        grads = vjp(tuple(ct.astype(o.dtype) for ct, o in zip(cts, outs)))
        return tuple(grads[i] for i in want)
    return bwd


def _rowwise(name, fn, rows, tm, tiled, full, outs, accs=(), comm=None, skip_rows=0):
    n_t, n_f, n_o, n_a = len(tiled), len(full), len(outs), len(accs)
    assert skip_rows % tm == 0
    skip = skip_rows // tm
    n_tiles = rows // tm + skip

    def body(*refs):
        in_refs, c_ins, res_refs, c_outs, _, c_sems = _split_refs(refs, n_t + n_f, n_o + n_a, 0, comm)
        pid = pl.program_id(0)
        if comm is not None:
            @pl.when(pid == 0)
            def _():
                comm.start(c_ins, c_outs, *c_sems)

        res = fn(pid, *[r[...] for r in in_refs])
        o_refs, a_refs = res_refs[:n_o], res_refs[n_o:]
        for r, v in zip(o_refs, res[:n_o]):
            r[...] = v.astype(r.dtype)
        if n_a:
            @pl.when(pid == 0)
            def _():
                for r in a_refs:
                    r[...] = jnp.zeros_like(r)
            for r, v in zip(a_refs, res[n_o:]):
                r[...] += v.astype(F32)
        if comm is not None:
            @pl.when(pid == n_tiles - 1)
            def _():
                comm.finish(c_ins, c_outs, *c_sems)

    assert all(ro % tm == 0 for (_, ro, _, _) in tiled)
    in_specs = [pl.BlockSpec((tm, w), lambda i, ro=ro // tm, cb=cb, last=a.shape[0] // tm - 1: (jnp.clip(i + ro, 0, last), cb))
                for (a, ro, cb, w) in tiled]
    in_specs += [pl.BlockSpec(a.shape, lambda i, nd=a.ndim: (0,) * nd) for a in full]
    out_shape = [_sds((rows, w), dt) for (w, dt) in outs] + [_sds(s, F32) for s in accs]
    out_specs = [pl.BlockSpec((tm, w), lambda i: (jnp.maximum(i - skip, 0), 0)) for (w, _) in outs]
    out_specs += [pl.BlockSpec(tuple(s), lambda i, nd=len(s): (0,) * nd) for s in accs]
    c_in, c_out, c_scr = (comm.ins, comm.out_shapes, comm.scratch()) if comm is not None else ([], [], [])
    return pl.pallas_call(body, grid=(n_tiles,), in_specs=in_specs + [_ANY] * len(c_in), out_specs=out_specs + [_ANY] * len(c_out),
                          out_shape=out_shape + list(c_out), scratch_shapes=c_scr, name=name,
                          compiler_params=_params(("arbitrary",)))(*[t[0] for t in tiled], *full, *c_in)


def _t(a, row_off=0, col_blk=0, width=None):
    return (a, row_off, col_blk, a.shape[1] if width is None else width)


class _Comm:
    def __init__(self, ins, out_shapes, n_sems, start, finish):
        self.ins, self.out_shapes, self.n_sems, self.start, self.finish = list(ins), list(out_shapes), n_sems, start, finish

    def scratch(self):
        return [pltpu.SemaphoreType.DMA((self.n_sems,)), pltpu.SemaphoreType.DMA((self.n_sems,)),
                pltpu.SemaphoreType.DMA((len(self.ins),))]


_ANY = pl.BlockSpec(memory_space=pl.ANY)


class _SemSlice:
    def __init__(self, ref, first):
        self.ref, self.first = ref, first

    @property
    def at(self):
        return self

    def __getitem__(self, k):
        return self.ref.at[self.first + k]


def _both(*comms):
    comms = [cm for cm in comms if cm is not None]
    if len(comms) <= 1:
        return comms[0] if comms else None

    def parts(ins, outs, send, recv, local):
        i0 = o0 = s0 = 0
        for cm in comms:
            ni, no = len(cm.ins), len(cm.out_shapes)
            yield cm, (ins[i0:i0 + ni], outs[o0:o0 + no], _SemSlice(send, s0), _SemSlice(recv, s0), _SemSlice(local, i0))
            i0, o0, s0 = i0 + ni, o0 + no, s0 + cm.n_sems

    def start(*refs):
        for cm, sub in parts(*refs):
            cm.start(*sub)

    def finish(*refs):
        for cm, sub in parts(*refs):
            cm.finish(*sub)

    return _Comm(sum((cm.ins for cm in comms), []), sum((cm.out_shapes for cm in comms), []),
                 sum(cm.n_sems for cm in comms), start, finish)


def _row_block(cols, itemsize):
    return max(16, (1 << 20) // (cols * itemsize))


def _split_refs(refs, n_in, n_out, n_scratch, comm):
    ci, co, cs = (len(comm.ins), len(comm.out_shapes), 3) if comm is not None else (0, 0, 0)
    cuts = np.cumsum([0, n_in, ci, n_out, co, n_scratch, cs])
    return [refs[cuts[i]:cuts[i + 1]] for i in range(6)]


def _matmul(name, a, b, *, ta=False, tb=False, outs=((F32,)), epilogue=None, extras=(), tm=None, tn=None, tk=None, comm=None,
            col_blocks=None):
    m, k = (a.shape[1], a.shape[0]) if ta else a.shape
    n = b.shape[0] if tb else b.shape[1]
    tm = _tile(m, tm or MM_TILE_M, 128 if ta else 16)
    tn = _tile(n if col_blocks is None else n // col_blocks, tn or MM_TILE_N, 128)
    tk = _tile(k, tk or MM_TILE_K, 128 if not ta else 16)
    ni, nj, nk = m // tm, n // tn, k // tk
    n_e, n_o = len(extras), len(outs)
    dims = (((0 if ta else 1,), (1 if tb else 0,)), ((), ()))

    def body(*refs):
        ins, c_ins, o_refs, c_outs, scratch, c_sems = _split_refs(refs, 2 + n_e, n_o, 1 if nk > 1 else 0, comm)
        a_ref, b_ref, e_refs = ins[0], ins[1], ins[2:]
        i, j, kk = pl.program_id(0), pl.program_id(1), pl.program_id(2)
        if comm is not None:
            @pl.when(jnp.logical_and(jnp.logical_and(i == 0, j == 0), kk == 0))
            def _():
                comm.start(c_ins, c_outs, *c_sems)

        def finish(acc):
            res = (acc,) if epilogue is None else epilogue(acc, *[e[...] for e in e_refs])
            for r, v in zip(o_refs, res):
                r[...] = v.astype(r.dtype)

        prod = lax.dot_general(a_ref[...].astype(BF16), b_ref[...].astype(BF16), dims, preferred_element_type=F32)
        if nk == 1:
            finish(prod)
        else:
            acc = scratch[0]

            @pl.when(kk == 0)
            def _():
                acc[...] = prod

            @pl.when(kk > 0)
            def _():
                acc[...] += prod

            @pl.when(kk == nk - 1)
            def _():
                finish(acc[...])
        if comm is not None:
            @pl.when(jnp.logical_and(jnp.logical_and(i == ni - 1, j == nj - 1), kk == nk - 1))
            def _():
                comm.finish(c_ins, c_outs, *c_sems)

    a_spec = pl.BlockSpec((tk, tm), lambda i, j, kk: (kk, i)) if ta else pl.BlockSpec((tm, tk), lambda i, j, kk: (i, kk))
    b_spec = pl.BlockSpec((tn, tk), lambda i, j, kk: (j, kk)) if tb else pl.BlockSpec((tk, tn), lambda i, j, kk: (kk, j))
    mn_spec = pl.BlockSpec((tm, tn), lambda i, j, kk: (i, j))
    c_in, c_out, c_scr = (comm.ins, comm.out_shapes, comm.scratch()) if comm is not None else ([], [], [])
    if col_blocks is None:
        o_spec, o_shape = mn_spec, (m, n)
    else:
        per = n // col_blocks // tn
        o_spec = pl.BlockSpec((None, tm, tn), lambda i, j, kk: (j // per, i, j % per))
        o_shape = (col_blocks, m, n // col_blocks)
    res = pl.pallas_call(body, grid=(ni, nj, nk), in_specs=[a_spec, b_spec] + [mn_spec] * n_e + [_ANY] * len(c_in),
                         out_specs=[o_spec] * n_o + [_ANY] * len(c_out),
                         out_shape=[_sds(o_shape, dt) for dt in outs] + list(c_out),
                         scratch_shapes=([pltpu.VMEM((tm, tn), F32)] if nk > 1 else []) + c_scr, name=name,
                         compiler_params=_params(("arbitrary", "arbitrary", "arbitrary")))(a, b, *extras, *c_in)
    main = res[0] if n_o == 1 else res[:n_o]
    return main if comm is None else (main, res[n_o:])


def _attn_fwd(q, kv, n_ctx_tiles, tq, comm=None):
    t_all = q.shape[0]
    s_len = t_all - n_ctx_tiles * tq
    hd, groups = HEAD_DIM, N_HEADS // N_KV_HEADS
    nq = s_len // tq

    def body(*refs):
        (q_ref, k_ref, v_ref), c_ins, (o_ref,), c_outs, _, c_sems = _split_refs(refs, 3, 1, 0, comm)
        kh, g, i = pl.program_id(0), pl.program_id(1), pl.program_id(2)
        if comm is not None:
            @pl.when(jnp.logical_and(jnp.logical_and(kh == 0, g == 0), i == 0))
            def _():
                comm.start(c_ins, c_outs, *c_sems)

        s = lax.dot_general(q_ref[...], k_ref[...], (((1,), (1,)), ((), ())), preferred_element_type=F32)
        p = jnp.exp2(s - jnp.max(s, axis=-1, keepdims=True))
        l = jnp.sum(p, axis=-1, keepdims=True)
        o = jnp.dot(p.astype(BF16), v_ref[...], preferred_element_type=F32) * (1.0 / l)
        o_ref[...] = o.astype(o_ref.dtype)
        if comm is not None:
            @pl.when(jnp.logical_and(jnp.logical_and(kh == N_KV_HEADS - 1, g == groups - 1), i == nq - 1))
            def _():
                comm.finish(c_ins, c_outs, *c_sems)

    c_in, c_out, c_scr = (comm.ins, comm.out_shapes, comm.scratch()) if comm is not None else ([], [], [])
    res = pl.pallas_call(
        body, grid=(N_KV_HEADS, groups, nq),
        in_specs=[pl.BlockSpec((tq, hd), lambda kh, g, i: (i + n_ctx_tiles, kh * groups + g)),
                  pl.BlockSpec((t_all, hd), lambda kh, g, i: (0, kh)),
                  pl.BlockSpec((t_all, hd), lambda kh, g, i: (0, N_KV_HEADS + kh))] + [_ANY] * len(c_in),
        out_specs=[pl.BlockSpec((tq, hd), lambda kh, g, i: (i, kh * groups + g))] + [_ANY] * len(c_out),
        out_shape=[_sds((s_len, N_HEADS * hd), BF16)] + list(c_out), scratch_shapes=c_scr, name="attn_fwd",
        compiler_params=_params(("arbitrary", "arbitrary", "arbitrary")))(q, kv, kv, *c_in)
    return res[0] if comm is None else (res[0], res[1:])


def _attn_bwd(q, kv, d_ar, n_ctx_tiles, tq, comm=None):
    t_all = q.shape[0]
    s_len = t_all - n_ctx_tiles * tq
    hd, groups = HEAD_DIM, N_HEADS // N_KV_HEADS
    nq = s_len // tq

    def body(*refs):
        (q_ref, k_ref, v_ref, do_ref), c_ins, (dq_ref, dkt_ref, dvt_ref), c_outs, _, c_sems = _split_refs(refs, 4, 3, 0, comm)
        first = jnp.logical_and(pl.program_id(1) == 0, pl.program_id(2) == 0)
        if comm is not None:
            @pl.when(jnp.logical_and(first, pl.program_id(0) == 0))
            def _():
                comm.start(c_ins, c_outs, *c_sems)

        @pl.when(first)
        def _():
            dkt_ref[...] = jnp.zeros_like(dkt_ref)
            dvt_ref[...] = jnp.zeros_like(dvt_ref)

        qv, kk, vv = q_ref[...], k_ref[...], v_ref[...]
        s = lax.dot_general(qv, kk, (((1,), (1,)), ((), ())), preferred_element_type=F32)
        p = jnp.exp2(s - jnp.max(s, axis=-1, keepdims=True))
        inv_l = 1.0 / jnp.sum(p, axis=-1, keepdims=True)
        do = (do_ref[...] * inv_l).astype(BF16)
        dp = lax.dot_general(do, vv, (((1,), (1,)), ((), ())), preferred_element_type=F32)
        ds = (p * (dp - jnp.sum(p * dp, axis=-1, keepdims=True) * inv_l)).astype(BF16)
        dq_ref[...] = jnp.dot(ds, kk, preferred_element_type=F32) * LN_2
        dkt_ref[...] += jnp.dot(qv.T, ds, preferred_element_type=F32)
        dvt_ref[...] += jnp.dot(do.T, p.astype(BF16), preferred_element_type=F32)
        last = jnp.logical_and(pl.program_id(1) == groups - 1, pl.program_id(2) == nq - 1)

        @pl.when(last)
        def _():
            dkt_ref[...] *= LN_2

        if comm is not None:
            @pl.when(jnp.logical_and(last, pl.program_id(0) == N_KV_HEADS - 1))
            def _():
                comm.finish(c_ins, c_outs, *c_sems)

    c_in, c_out, c_scr = (comm.ins, comm.out_shapes, comm.scratch()) if comm is not None else ([], [], [])
    return pl.pallas_call(
        body, grid=(N_KV_HEADS, groups, nq),
        in_specs=[pl.BlockSpec((tq, hd), lambda kh, g, i: (i + n_ctx_tiles, kh * groups + g)),
                  pl.BlockSpec((t_all, hd), lambda kh, g, i: (0, kh)),
                  pl.BlockSpec((t_all, hd), lambda kh, g, i: (0, N_KV_HEADS + kh)),
                  pl.BlockSpec((tq, hd), lambda kh, g, i: (i, kh * groups + g))] + [_ANY] * len(c_in),
        out_specs=[pl.BlockSpec((tq, hd), lambda kh, g, i: (i, kh * groups + g)),
                   pl.BlockSpec((hd, t_all), lambda kh, g, i: (kh, 0)),
                   pl.BlockSpec((hd, t_all), lambda kh, g, i: (kh, 0))] + [_ANY] * len(c_out),
        out_shape=[_sds((s_len, N_HEADS * hd), F32), _sds((N_KV_HEADS * hd, t_all), F32),
                   _sds((N_KV_HEADS * hd, t_all), F32)] + list(c_out),
        scratch_shapes=c_scr, name="attn_bwd",
        compiler_params=_params(("arbitrary", "arbitrary", "arbitrary")))(q, kv, kv, d_ar, *c_in)


def _scan_order(nb, nc, reverse):
    if not reverse:
        return lambda i: i
    return lambda i: jnp.where(i < nc, nc - 1 - i, nb - 1 - (i - nc))


def _scan_fwd(name, a, b, nc, reverse):
    t_all, r, l = a.shape
    tb = SCAN_BLOCK
    nb = t_all // tb
    order = _scan_order(nb, nc, reverse)

    def body(a_ref, b_ref, h_ref, hp_ref, carry):
        @pl.when(pl.program_id(0) == 0)
        def _():
            carry[...] = jnp.zeros_like(carry)

        def step(s, h):
            t = tb - 1 - s if reverse else s
            hp_ref[t] = h
            h = a_ref[t] * h + b_ref[t]
            h_ref[t] = h
            return h

        carry[...] = lax.fori_loop(0, tb, step, carry[...], unroll=8)

    spec = pl.BlockSpec((tb, r, l), lambda i: (order(i), 0, 0))
    return pl.pallas_call(body, grid=(nb,), in_specs=[spec, spec], out_specs=[spec, spec],
                          out_shape=[_sds(a.shape, F32)] * 2, scratch_shapes=[pltpu.VMEM((r, l), F32)], name=name,
                          compiler_params=_params(("arbitrary",)))(a, b)


def _scan_bwd(name, a, dh, hp, nc, reverse):
    t_all, r, l = a.shape
    tb = SCAN_BLOCK
    nb = t_all // tb
    primal = _scan_order(nb, nc, reverse)

    def order(i):
        return primal(nb - 1 - i)

    def body(a_ref, dh_ref, hp_ref, da_ref, db_ref, carry):
        @pl.when(pl.program_id(0) == 0)
        def _():
            carry[...] = jnp.zeros_like(carry)

        def step(s, cr):
            t = s if reverse else tb - 1 - s
            lam = dh_ref[t] + cr
            db_ref[t] = lam
            da_ref[t] = lam * hp_ref[t]
            return a_ref[t] * lam

        carry[...] = lax.fori_loop(0, tb, step, carry[...], unroll=8)

    spec = pl.BlockSpec((tb, r, l), lambda i: (order(i), 0, 0))
    return pl.pallas_call(body, grid=(nb,), in_specs=[spec] * 3, out_specs=[spec, spec],
                          out_shape=[_sds(a.shape, F32)] * 2, scratch_shapes=[pltpu.VMEM((r, l), F32)], name=name,
                          compiler_params=_params(("arbitrary",)))(a, dh, hp)


def _shifted(prev, cur, nxt, k, pid, n_ctx_tiles, n_tiles):
    if k == 0:
        return cur
    tm = cur.shape[0]
    row = lax.broadcasted_iota(jnp.int32, cur.shape, 0)
    if k < 0:
        at_start = jnp.logical_or(pid == 0, pid == n_ctx_tiles)
        edge = jnp.where(at_start, 0.0, pltpu.roll(prev, -k, 0))
        return jnp.where(row < -k, edge, pltpu.roll(cur, -k, 0))
    at_end = jnp.logical_or(pid == n_ctx_tiles - 1, pid == n_tiles - 1)
    edge = jnp.where(at_end, 0.0, pltpu.roll(nxt, tm - k, 0))
    return jnp.where(row >= tm - k, edge, pltpu.roll(cur, tm - k, 0))


def _f_conv(pid, xp, xc, xn, w, b, *, n_ctx_tiles, n_tiles):
    y = b
    for j in range(CONV_W):
        y = y + _shifted(xp, xc, xn, j - CONV_W // 2, pid, n_ctx_tiles, n_tiles) * w[j:j + 1]
    return (y,)


def _f_conv_bwd(pid, xp, xc, xn, dp, dc, dn, w, *, n_ctx_tiles, n_tiles):
    dx = jnp.zeros_like(dc)
    dw = []
    for j in range(CONV_W):
        k = j - CONV_W // 2
        dx = dx + _shifted(dp, dc, dn, -k, pid, n_ctx_tiles, n_tiles) * w[j:j + 1]
        dw.append(jnp.sum(dc * _shifted(xp, xc, xn, k, pid, n_ctx_tiles, n_tiles), axis=0, keepdims=True))
    return (dx, jnp.concatenate(dw, axis=0), jnp.sum(dc, axis=0, keepdims=True))


def _mesh_pos():
    return lax.axis_index("x"), lax.axis_index("y"), lax.axis_index("c")


def _remote(src, dst, send_sems, recv_sems, k, to):
    return pltpu.make_async_remote_copy(src_ref=src, dst_ref=dst, send_sem=send_sems.at[k], recv_sem=recv_sems.at[k],
                                        device_id=to, device_id_type=pl.DeviceIdType.MESH)


def _neighbours():
    x, y, c = _mesh_pos()
    return (x, y, c), (x, y, 1 - c), [(1 - x, y), (x, 1 - y), (1 - x, 1 - y)]


def _gather_comm(arrays):
    n = len(arrays)
    per = 7

    def slot(out, blk):
        return out.at[4 * blk[0] + 2 * blk[1] + blk[2]]

    def start(ins, outs, send, recv, local):
        me, sib, chips = _neighbours()
        for ai in range(n):
            pltpu.make_async_copy(ins[ai], slot(outs[ai], me), local.at[ai]).start()
            _remote(ins[ai], slot(outs[ai], me), send, recv, ai * per, sib).start()
            for j, chip in enumerate(chips):
                _remote(ins[ai], slot(outs[ai], me), send, recv, ai * per + 1 + j, (*chip, me[2])).start()

    def finish(ins, outs, send, recv, local):
        me, sib, chips = _neighbours()
        for ai in range(n):
            for j, chip in enumerate(chips):
                blk = slot(outs[ai], (*chip, me[2]))
                _remote(blk, blk, send, recv, ai * per + 1 + j, me).wait_recv()
                _remote(blk, blk, send, recv, ai * per + 4 + j, sib).start()
        for ai in range(n):
            blk = slot(outs[ai], sib)
            _remote(blk, blk, send, recv, ai * per, me).wait_recv()
            for j, chip in enumerate(chips):
                blk = slot(outs[ai], (*chip, 1 - me[2]))
                _remote(blk, blk, send, recv, ai * per + 4 + j, me).wait_recv()
            for k in range(per):
                _remote(ins[ai], slot(outs[ai], me), send, recv, ai * per + k, sib).wait_send()
            pltpu.make_async_copy(ins[ai], slot(outs[ai], me), local.at[ai]).wait()

    return _Comm(arrays, [_sds((N_DEV,) + a.shape, a.dtype) for a in arrays], n * per, start, finish)


def _swap_comm(arrays):
    n = len(arrays)

    def start(ins, outs, send, recv, local):
        me, sib, _ = _neighbours()
        for ai in range(n):
            for q in range(4):
                _remote(ins[ai].at[2 * q + 1 - me[2]], outs[ai].at[q], send, recv, ai * 4 + q, sib).start()

    def finish(ins, outs, send, recv, local):
        me, sib, _ = _neighbours()
        for ai in range(n):
            for q in range(4):
                cp = _remote(ins[ai].at[q], outs[ai].at[q], send, recv, ai * 4 + q, sib)
                cp.wait_recv()
                cp.wait_send()

    return _Comm(arrays, [_sds((4,) + a.shape[1:], a.dtype) for a in arrays], n * 4, start, finish)


def _chips_comm(arrays):
    n = len(arrays)

    def start(ins, outs, send, recv, local):
        me, _, chips = _neighbours()
        mine = 2 * me[0] + me[1]
        for ai in range(n):
            pltpu.make_async_copy(ins[ai].at[mine], outs[ai].at[mine], local.at[ai]).start()
            for j, chip in enumerate(chips):
                _remote(ins[ai].at[2 * chip[0] + chip[1]], outs[ai].at[mine], send, recv, ai * 3 + j, (*chip, me[2])).start()

    def finish(ins, outs, send, recv, local):
        me, _, chips = _neighbours()
        mine = 2 * me[0] + me[1]
        for ai in range(n):
            for j, chip in enumerate(chips):
                theirs = 2 * chip[0] + chip[1]
                cp = _remote(ins[ai].at[theirs], outs[ai].at[theirs], send, recv, ai * 3 + j, (*chip, me[2]))
                cp.wait_recv()
                cp.wait_send()
            pltpu.make_async_copy(ins[ai].at[mine], outs[ai].at[mine], local.at[ai]).wait()

    return _Comm(arrays, [_sds(a.shape, a.dtype) for a in arrays], n * 3, start, finish)


def _run_comm(name, comm):
    n_in, n_out = len(comm.ins), len(comm.out_shapes)

    def body(*refs):
        ins, outs, sems = refs[:n_in], refs[n_in:n_in + n_out], refs[n_in + n_out:]
        comm.start(ins, outs, *sems)
        comm.finish(ins, outs, *sems)

    return pl.pallas_call(body, in_specs=[_ANY] * n_in, out_specs=[_ANY] * n_out, out_shape=comm.out_shapes, name=name,
                          scratch_shapes=comm.scratch(), compiler_params=pltpu.CompilerParams(has_side_effects=True))(*comm.ins)


def _chip_add(name, blocks, theirs, core):
    _, r, c = blocks.shape
    tr = _tile(r, _row_block(c, 2), 16)

    def body(core_ref, a_ref, b_ref, o_ref):
        o_ref[...] = (a_ref[...].astype(F32) + b_ref[...].astype(F32)).astype(o_ref.dtype)

    spec = pl.BlockSpec((None, tr, c), lambda q, i, core_ref: (q, i, 0))
    grid_spec = pltpu.PrefetchScalarGridSpec(
        num_scalar_prefetch=1, grid=(4, r // tr),
        in_specs=[pl.BlockSpec((None, tr, c), lambda q, i, core_ref: (2 * q + core_ref[0], i, 0)), spec], out_specs=spec)
    return pl.pallas_call(body, grid_spec=grid_spec, out_shape=_sds((4, r, c), blocks.dtype), name=name,
                          compiler_params=_params(("parallel", "parallel")))(jnp.reshape(core, (1,)).astype(jnp.int32), blocks, theirs)


def _sum_lead(name, a):
    n, r, c = a.shape
    tr = _tile(r, _row_block(c, 4 * n // 2), 8)

    def body(a_ref, o_ref):
        acc = a_ref[0]
        for j in range(1, n):
            acc = acc + a_ref[j]
        o_ref[...] = acc

    return pl.pallas_call(body, grid=(r // tr,), in_specs=[pl.BlockSpec((n, tr, c), lambda i: (0, i, 0))],
                          out_specs=pl.BlockSpec((tr, c), lambda i: (i, 0)), out_shape=_sds((r, c), F32), name=name,
                          compiler_params=_params(("parallel",)))(a)


def _adam_math(w, g, m, v):
    m = ADAM_B1 * m + (1.0 - ADAM_B1) * g
    v = ADAM_B2 * v + (1.0 - ADAM_B2) * (g * g)
    m_hat = m / (1.0 - ADAM_B1 ** ADAM_STEP)
    v_hat = v / (1.0 - ADAM_B2 ** ADAM_STEP)
    delta = -ADAM_LR * (m_hat / (jnp.sqrt(v_hat) + ADAM_EPS) + ADAM_WD * w)
    return delta, m, v


def _adam_recv(name, recvs, w, m, v):
    nl = len(recvs)
    n, rl, c = recvs[0].shape
    tr = _tile(rl, _row_block(c, 4), 16)
    per = rl // tr

    def body(*refs):
        g_refs = refs[:nl]
        w_ref, m_ref, v_ref, go_ref, d_ref, mo_ref, vo_ref = refs[nl:]
        for layer in range(nl):
            @pl.when(pl.program_id(0) == layer)
            def _(g_ref=g_refs[layer]):
                g = g_ref[0].astype(F32)
                for j in range(1, n):
                    g = g + g_ref[j].astype(F32)
                delta, m2, v2 = _adam_math(w_ref[...], g, m_ref[...], v_ref[...])
                go_ref[...] = g
                d_ref[...] = delta
                mo_ref[...] = m2
                vo_ref[...] = v2

    g_specs = [pl.BlockSpec((n, tr, c), lambda l, i, layer=layer: (0, jnp.where(l == layer, i, 0), 0)) for layer in range(nl)]
    spec = pl.BlockSpec((tr, c), lambda l, i: (l * per + i, 0))
    return pl.pallas_call(body, grid=(nl, per), in_specs=g_specs + [spec] * 3, out_specs=[spec] * 4,
                          out_shape=[_sds((nl * rl, c), F32)] * 4, name=name,
                          compiler_params=_params(("arbitrary", "arbitrary")))(*recvs, w, m, v)


def _adam_f32(name, g, w, m, v, comm=None):
    r, c = g.shape
    tr = _tile(r, _row_block(c, 4), 8)
    steps = r // tr

    def body(*refs):
        (g_ref, w_ref, m_ref, v_ref), c_ins, (d_ref, mo_ref, vo_ref), c_outs, _, c_sems = _split_refs(refs, 4, 3, 0, comm)
        if comm is not None:
            @pl.when(pl.program_id(0) == 0)
            def _():
                comm.start(c_ins, c_outs, *c_sems)

        delta, m2, v2 = _adam_math(w_ref[...], g_ref[...], m_ref[...], v_ref[...])
        d_ref[...] = delta
        mo_ref[...] = m2
        vo_ref[...] = v2
        if comm is not None:
            @pl.when(pl.program_id(0) == steps - 1)
            def _():
                comm.finish(c_ins, c_outs, *c_sems)

    spec = pl.BlockSpec((tr, c), lambda i: (i, 0))
    c_in, c_out, c_scr = (comm.ins, comm.out_shapes, comm.scratch()) if comm is not None else ([], [], [])
    return pl.pallas_call(body, grid=(steps,), in_specs=[spec] * 4 + [_ANY] * len(c_in), out_specs=[spec] * 3 + [_ANY] * len(c_out),
                          out_shape=[_sds((r, c), F32)] * 3 + list(c_out), scratch_shapes=c_scr, name=name,
                          compiler_params=_params(("arbitrary",)))(g, w, m, v, *c_in)


def _mod_fwd(c16, w_mod, b_loc):
    nl, d, n6 = w_mod.shape
    tn = _tile(n6, 512, 128)

    def body(c_ref, w_ref, b_ref, o_ref):
        cv = c_ref[...]
        s = cv * _sigmoid(cv)
        o_ref[0] = jnp.dot(s, w_ref[0], precision=HIGHEST, preferred_element_type=F32) + b_ref[0]

    return pl.pallas_call(
        body, grid=(nl, n6 // tn),
        in_specs=[pl.BlockSpec((MOD_ROWS, d), lambda i, j: (0, 0)), pl.BlockSpec((1, d, tn), lambda i, j: (i, 0, j)),
                  pl.BlockSpec((1, 1, tn), lambda i, j: (i, 0, j))],
        out_specs=pl.BlockSpec((1, MOD_ROWS, tn), lambda i, j: (i, 0, j)), out_shape=_sds((nl, MOD_ROWS, n6), F32),
        name="mod_fwd", compiler_params=_params(("parallel", "parallel")))(c16, w_mod, b_loc)


def _mod_bwd(c16, w_mod, dmod_loc):
    nl, d, n6 = w_mod.shape
    tn = _tile(n6, 512, 128)

    def body(c_ref, w_ref, dm_ref, dw_ref, ds_ref):
        @pl.when(jnp.logical_and(pl.program_id(0) == 0, pl.program_id(1) == 0))
        def _():
            ds_ref[...] = jnp.zeros_like(ds_ref)

        cv = c_ref[...]
        s = cv * _sigmoid(cv)
        dm = dm_ref[0]
        dw_ref[0] = lax.dot_general(s, dm, (((0,), (0,)), ((), ())), precision=HIGHEST, preferred_element_type=F32)
        ds_ref[...] += lax.dot_general(dm, w_ref[0], (((1,), (1,)), ((), ())), precision=HIGHEST, preferred_element_type=F32)

    return pl.pallas_call(
        body, grid=(nl, n6 // tn),
        in_specs=[pl.BlockSpec((MOD_ROWS, d), lambda i, j: (0, 0)), pl.BlockSpec((1, d, tn), lambda i, j: (i, 0, j)),
                  pl.BlockSpec((1, MOD_ROWS, tn), lambda i, j: (i, 0, j))],
        out_specs=[pl.BlockSpec((1, d, tn), lambda i, j: (i, 0, j)), pl.BlockSpec((MOD_ROWS, d), lambda i, j: (0, 0))],
        out_shape=[_sds((nl, d, n6), F32), _sds((MOD_ROWS, d), F32)], name="mod_bwd",
        compiler_params=_params(("arbitrary", "arbitrary")))(c16, w_mod, dmod_loc)


def _pack(parts):
    flat = [p.reshape(-1).astype(F32) for p in parts]
    offs = np.cumsum([0] + [f.shape[0] for f in flat])
    total = int(offs[-1])
    unit = (PACK_ROWS if total > PACK_ROWS * LANES else 8) * LANES
    padded = -(-total // unit) * unit
    slab = jnp.concatenate(flat + [jnp.zeros((padded - total,), F32)])
    return slab.reshape(padded // LANES, LANES), [int(o) for o in offs]


def _unshard_cols(seg, lead):
    n = seg.shape[1] // int(np.prod(lead)) if lead else seg.shape[1]
    a = seg.reshape((N_DEV,) + tuple(lead) + (n,))
    a = jnp.moveaxis(a, 0, len(lead))
    return a.reshape(tuple(lead) + (N_DEV * n,))


def _my_cols(a, me, n):
    start = (0,) * (a.ndim - 1) + (me * n,)
    return lax.dynamic_slice(a, start, a.shape[:-1] + (n,))


def _rope_tables(seq, n_ctx):
    rows = seq // GRID_W
    r_idx, c_idx = jnp.meshgrid(jnp.arange(rows), jnp.arange(GRID_W), indexing='ij')
    r_idx = r_idx.reshape(-1).astype(F32)
    c_idx = c_idx.reshape(-1).astype(F32)
    pairs = HEAD_DIM // 4
    freqs = ROPE_THETA ** (-jnp.arange(pairs, dtype=F32) / pairs)
    ang_r, ang_c = r_idx[:, None] * freqs, c_idx[:, None] * freqs
    cos = jnp.concatenate([jnp.cos(ang_r)] * 2 + [jnp.cos(ang_c)] * 2, axis=1)
    sin = jnp.concatenate([-jnp.sin(ang_r), jnp.sin(ang_r), -jnp.sin(ang_c), jnp.sin(ang_c)], axis=1)
    cos = jnp.concatenate([jnp.ones((n_ctx, HEAD_DIM), F32), cos], axis=0)
    sin = jnp.concatenate([jnp.zeros((n_ctx, HEAD_DIM), F32), sin], axis=0)
    lane = np.arange(HEAD_DIM)
    partner = np.where(lane % (2 * pairs) < pairs, lane + pairs, lane - pairs)
    perm = np.zeros((HEAD_DIM, HEAD_DIM), np.float32)
    perm[partner, lane] = 1.0
    return cos, sin, jnp.asarray(perm)


def kernel(x, c, ctx, c_ctx, w_mod, b_mod, norm_g, w_ff_in, w_ff_out, ar_w_in, ar_q_g, ar_k_g, ar_conv_w, ar_conv_b, ar_wa, ar_ba, ar_wx, ar_bx, ar_lambda, ar_w_out, gm_w_in, gm_b_in, gm_v_g, gm_v_b, gm_w_sp, gm_b_sp, gm_w_out, loss_target, m_c_ctx, m_w_mod, m_b_mod, m_norm_g, m_w_ff_in, m_w_ff_out, m_ar_w_in, m_ar_q_g, m_ar_k_g, m_ar_conv_w, m_ar_conv_b, m_ar_wa, m_ar_ba, m_ar_wx, m_ar_bx, m_ar_lambda, m_ar_w_out, m_gm_w_in, m_gm_b_in, m_gm_v_g, m_gm_v_b, m_gm_w_sp, m_gm_b_sp, m_gm_w_out, v_c_ctx, v_w_mod, v_b_mod, v_norm_g, v_w_ff_in, v_w_ff_out, v_ar_w_in, v_ar_q_g, v_ar_k_g, v_ar_conv_w, v_ar_conv_b, v_ar_wa, v_ar_ba, v_ar_wx, v_ar_bx, v_ar_lambda, v_ar_w_out, v_gm_w_in, v_gm_b_in, v_gm_v_g, v_gm_v_b, v_gm_w_sp, v_gm_b_sp, v_gm_w_out):
    given = dict(locals())
    wts = {n: given[n] for n in WEIGHTS}
    mom1 = {n: given["m_" + n] for n in WEIGHTS}
    mom2 = {n: given["v_" + n] for n in WEIGHTS}

    xi, yi, ci = _mesh_pos()
    me = 4 * xi + 2 * yi + ci

    seq, d = x.shape[1], x.shape[2]
    n_ctx = ctx.shape[1]
    t_all = n_ctx + seq
    n_layers = w_mod.shape[0]
    assert n_layers == 2 and ar_w_in.shape[0] == 1 and gm_w_in.shape[0] == 1
    d_ff = w_ff_in.shape[2] * N_DEV
    attn_w, kv_w = N_HEADS * HEAD_DIM, N_KV_HEADS * HEAD_DIM
    rnn_blocks = ar_wa.shape[2]
    d_rnn = rnn_blocks * RNN_BLOCK_W
    gm_groups = gm_w_sp.shape[1]
    d_gm = gm_groups * GM_GROUP_W
    ar_in = ar_w_in.shape[2] * N_DEV
    n6 = w_mod.shape[2]
    tm, tmb = ROW_TILE, ROW_TILE_BWD
    assert attn_w == d_rnn and ar_in == 3 * attn_w + 2 * kv_w and (3 * attn_w) % (2 * kv_w) == 0
    assert n_ctx % tm == 0 and seq % tm == 0 and n_ctx % SCAN_BLOCK == 0 and seq % SCAN_BLOCK == 0 and tm % CHUNK == 0
    nct, nctb = n_ctx // tm, n_ctx // tmb
    kv_blk = (3 * attn_w) // (2 * kv_w)
    lr = d_rnn // LANES

    x2, ctx2, tgt = x[0], ctx[0], loss_target[0]

    def cols_full(g):
        return jnp.moveaxis(g, 0, 1).reshape(g.shape[1], N_DEV * g.shape[2])

    small0, off0 = _pack([c[0], norm_g, ar_conv_w[0], ar_ba[0], ar_bx[0], ar_lambda[0], gm_b_in[0], gm_v_g[0], gm_v_b[0]])
    g_ar_in, gs0 = _run_comm("gather_first", _gather_comm([ar_w_in[0].astype(BF16), small0]))
    gs0 = gs0.reshape(N_DEV, -1)
    w_in = cols_full(g_ar_in)
    split = [attn_w, attn_w + 2 * kv_w, attn_w + 2 * kv_w + d_rnn]
    w_in = jnp.concatenate([w_in[:, :split[0]], w_in[:, split[1]:], w_in[:, split[0]:split[1]]], axis=1)
    w1, w2 = [None] * n_layers, [None] * n_layers

    def seg0(k):
        return gs0[:, off0[k]:off0[k + 1]]

    c_all = seg0(0)
    norm_full = _unshard_cols(seg0(1), (n_layers, 4))
    conv_w = _unshard_cols(seg0(2), (CONV_W,))
    ba, bx, lam = (_unshard_cols(seg0(k), (2,)) for k in (3, 4, 5))
    gm_b_in_f = seg0(6).reshape(1, 2 * d_gm)
    gm_vg, gm_vb = seg0(7).reshape(1, d_gm), seg0(8).reshape(1, d_gm)

    c16 = jnp.concatenate([c_all, c_ctx[None], jnp.zeros((MOD_ROWS - N_DEV - 1, d), F32)], axis=0)
    b_loc = _my_cols(b_mod, me, n6)[:, None, :]
    mod_loc = _mod_fwd(c16, w_mod, b_loc)
    (g_mod,) = _run_comm("gather_mod", _gather_comm([mod_loc]))
    mod_all = jnp.moveaxis(g_mod, 0, 2).reshape(n_layers, MOD_ROWS, N_DEV * n6)
    ml = lax.dynamic_index_in_dim(mod_all, me, axis=1, keepdims=False).reshape(n_layers, 6, d)
    mc = mod_all[:, N_DEV].reshape(n_layers, 6, d)

    def row(a, *idx):
        return a[idx][None]

    cos, sin, perm = _rope_tables(seq, n_ctx)
    wa3 = ar_wa[0].reshape(2 * rnn_blocks, RNN_BLOCK_W, RNN_BLOCK_W)
    wx3 = ar_wx[0].reshape(2 * rnn_blocks, RNN_BLOCK_W, RNN_BLOCK_W)
    conv_b = ar_conv_b
    q_g, k_g = ar_q_g, ar_k_g
    w_sp = gm_w_sp[0]
    bsp_t = jnp.pad(gm_b_sp[0].T, ((0, 0), (0, LANES - gm_groups)))
    expand = np.zeros((LANES, d_gm), np.float32)
    for g in range(gm_groups):
        expand[g, g * GM_GROUP_W:(g + 1) * GM_GROUP_W] = 1.0
    expand = jnp.asarray(expand)

    def relu2(acc):
        r = jnp.maximum(acc, 0.0)
        return (r * r,)

    def relu2_bwd(acc, act):
        return (acc * (2.0 * jnp.sqrt(act.astype(F32))),)

    def ff_in_shard(i):
        return w_ff_in[i].astype(BF16)

    def ff_out_shard(i):
        return w_ff_out[i].astype(BF16)

    tokens = [_t(ctx2), _t(x2, -n_ctx)]
    pre0_args = [row(norm_full, 0, 0), row(mc, 0, 0), row(mc, 0, 1), row(ml, 0, 0), row(ml, 0, 1)]
    f_pre0 = functools.partial(_f_pre_ctx, n_ctx_tiles=nct)
    (h0,) = _rowwise("pre0", f_pre0, t_all, tm, tokens, pre0_args, [(d, BF16)])
    tm_tok = _tile(t_all, 640, 16)
    proj, (g_ar_out,) = _matmul("ar_in", h0, w_in, tm=tm_tok, comm=_gather_comm([ar_w_out[0].astype(BF16)]))
    w_out = g_ar_out.reshape(attn_w + d_rnn, d)
    f_qkv = functools.partial(_f_qkv, nh=N_HEADS, nkv=N_KV_HEADS)
    qkv_tiled = [_t(proj, 0, 0, attn_w), _t(proj, 0, kv_blk, 2 * kv_w), _t(cos), _t(sin)]
    q_r, kv_r = _rowwise("qkv", f_qkv, t_all, tm, qkv_tiled, [q_g, k_g, perm], [(attn_w, BF16), (2 * kv_w, BF16)])
    attn_o, (g_ff_in0, g_ff_out0) = _attn_fwd(q_r, kv_r, nct, tm, comm=_gather_comm([ff_in_shard(0), ff_out_shard(0)]))
    w1[0], w2[0] = cols_full(g_ff_in0), g_ff_out0.reshape(d_ff, d)

    def with_neighbours(a, t, col_blk=0, width=None):
        return [_t(a, -t, col_blk, width), _t(a, 0, col_blk, width), _t(a, t, col_blk, width)]

    f_conv = functools.partial(_f_conv, n_ctx_tiles=nct, n_tiles=t_all // tm)
    (xc,) = _rowwise("conv", f_conv, t_all, tm, with_neighbours(proj, tm, 1, d_rnn), [conv_w, conv_b], [(d_rnn, F32)])
    f_gates = functools.partial(_f_gates, nb=rnn_blocks)
    gate_full = [wa3, ba, wx3, bx, lam]
    a_f, b_f, a_b, b_b = _rowwise("gates", f_gates, t_all, tm, [_t(xc)], gate_full, [(d_rnn, F32)] * 4)

    def to3(a):
        return a.reshape(a.shape[0], lr, LANES)

    nc_scan = n_ctx // SCAN_BLOCK
    h_f, hp_f = _scan_fwd("scan_f", to3(a_f), to3(b_f), nc_scan, False)
    h_b, hp_b = _scan_fwd("scan_b", to3(a_b), to3(b_b), nc_scan, True)
    h_f2, h_b2 = h_f.reshape(t_all, d_rnn), h_b.reshape(t_all, d_rnn)
    rnn_tiled = [_t(h_f2, n_ctx), _t(h_b2, n_ctx), _t(proj, n_ctx, 2, d_rnn)]
    (rnn_o,) = _rowwise("rnn_out", _f_rnnout, seq, tm, rnn_tiled, [], [(d_rnn, BF16)])
    ar = jnp.concatenate([attn_o, rnn_o], axis=1)
    o0 = _matmul("ar_out", ar, w_out)
    mid0_args = [row(norm_full, 0, 1), row(ml, 0, 2), row(norm_full, 0, 2), row(ml, 0, 3), row(ml, 0, 4)]
    x1, h2_0 = _rowwise("mid0", _f_mid, seq, tm, [_t(x2), _t(o0)], mid0_args, [(d, F32), (d, BF16)])
    act0, (g_gm_in, g_gm_out) = _matmul("ff_in_0", h2_0, w1[0], outs=(BF16,), epilogue=relu2,
                                              comm=_gather_comm([gm_w_in[0].astype(BF16), gm_w_out[0].astype(BF16)]))
    w_gi, w_go = cols_full(g_gm_in), g_gm_out.reshape(d_gm, d)
    m0, (g_ff_in1,) = _matmul("ff_out_0", act0, w2[0], tm=MM_TILE_M // 2, tk=2 * MM_TILE_K, comm=_gather_comm([ff_in_shard(1)]))
    w1[1] = cols_full(g_ff_in1)
    post0_args = [row(norm_full, 0, 3), row(ml, 0, 5)]
    (x2l,) = _rowwise("post0", _f_post, seq, tm, [_t(x1), _t(m0)], post0_args, [(d, F32)])

    pre1_args = [row(norm_full, 1, 0), row(ml, 1, 0), row(ml, 1, 1)]
    (h1,) = _rowwise("pre1", _f_pre, seq, tm, [_t(x2l)], pre1_args, [(d, BF16)])
    zg = _matmul("gm_in", h1, w_gi)
    f_gm = functools.partial(_f_gm, n_chunks=tmb // CHUNK, groups=gm_groups)
    gm_full = [gm_b_in_f[:, :d_gm], gm_b_in_f[:, d_gm:], gm_vg, gm_vb, w_sp, bsp_t, expand]
    gm_tiled = [_t(zg, 0, 0, d_gm), _t(zg, 0, 1, d_gm)]
    (gmix,) = _rowwise("gm_mix", f_gm, seq, tmb, gm_tiled, gm_full, [(d_gm, BF16)])
    o1 = _matmul("gm_out", gmix, w_go)
    mid1_args = [row(norm_full, 1, 1), row(ml, 1, 2), row(norm_full, 1, 2), row(ml, 1, 3), row(ml, 1, 4)]
    x3, h2_1 = _rowwise("mid1", _f_mid, seq, tm, [_t(x2l), _t(o1)], mid1_args, [(d, F32), (d, BF16)])
    act1, (g_ff_out1,) = _matmul("ff_in_1", h2_1, w1[1], outs=(BF16,), epilogue=relu2,
                                       comm=_gather_comm([ff_out_shard(1)]))
    w2[1] = g_ff_out1.reshape(d_ff, d)
    m1 = _matmul("ff_out_1", act1, w2[1], tm=MM_TILE_M // 2, tk=2 * MM_TILE_K)
    post1_args = [row(norm_full, 1, 3), row(ml, 1, 5)]

    def f_loss(pid, xv, ov, tv, g, gate):
        err = _f_post(pid, xv, ov, g, gate)[0] - tv
        part = 0.5 * jnp.sum(err * err) / d
        return (err / d, jnp.full((8, LANES), part, F32))

    dy, loss_acc = _rowwise("loss", f_loss, seq, tm, [_t(x3), _t(m1), _t(tgt)], post1_args, [(d, F32)], [(8, LANES)])
    loss = lax.psum(loss_acc[0, 0], ("x", "y", "c"))

    d_norm = [[None] * 4 for _ in range(n_layers)]
    d_ml = [[None] * 6 for _ in range(n_layers)]
    recv = {}

    def cols_blocks(g):
        return jnp.moveaxis(g.reshape(g.shape[0], N_DEV, g.shape[1] // N_DEV), 1, 0)

    def rows_blocks(g):
        return g.reshape(N_DEV, g.shape[0] // N_DEV, g.shape[1])

    chip_sums = {}

    def chip_add(key, blocks, theirs):
        chip_sums[key] = _chip_add(key + "_add", blocks, theirs, ci)

    def mlp_bwd(i, dm, act, h2, first_comm=None):
        dw2 = _matmul(f"ff_out_dw_{i}", act, dm, ta=True, outs=(BF16,), comm=first_comm)
        dw2, carried = dw2 if first_comm is not None else (dw2, ())
        blk2 = rows_blocks(dw2)
        dz, (theirs,) = _matmul(f"ff_out_dx_{i}", dm, w2[i], tb=True, outs=(BF16,), extras=(act,), epilogue=relu2_bwd,
                                comm=_swap_comm([blk2]))
        chip_add(f"ff_out_{i}", blk2, theirs)
        blk1, (recv[f"ff_out_{i}"],) = _matmul(f"ff_in_dw_{i}", h2, dz, ta=True, outs=(BF16,), col_blocks=N_DEV,
                                              comm=_chips_comm([chip_sums[f"ff_out_{i}"]]))
        dh2, (theirs,) = _matmul(f"ff_in_dx_{i}", dz, w1[i], tb=True, tm=MM_TILE_M // 2, tk=2 * MM_TILE_K,
                                 comm=_swap_comm([blk1]))
        chip_add(f"ff_in_{i}", blk1, theirs)
        return dh2, carried

    def post_bwd(i, xin, m, args, dout):
        res = _rowwise(f"post_bwd{i}", _bwd_of(_f_post, 2, 1, (0, 1, 2, 3)), seq, tmb, [_t(xin), _t(m), _t(dout)], args,
                       [(d, F32), (d, BF16)], [(1, d), (1, d)])
        d_norm[i][3], d_ml[i][5] = res[2], res[3]
        return res[0], res[1]

    def mid_bwd(i, xin, o, args, dx1, dh2):
        res = _rowwise(f"mid_bwd{i}", _bwd_of(_f_mid, 2, 2, (0, 1, 2, 3, 4, 5, 6)), seq, tmb,
                       [_t(xin), _t(o), _t(dx1), _t(dh2)], args, [(d, F32), (d, BF16)], [(1, d)] * 5)
        d_norm[i][1], d_ml[i][2], d_norm[i][2], d_ml[i][3], d_ml[i][4] = res[2:]
        return res[0], res[1]

    dx3, dm1 = post_bwd(1, x3, m1, post1_args, dy)
    dh2_1, _ = mlp_bwd(1, dm1, act1, h2_1)
    dx2a, do1 = mid_bwd(1, x2l, o1, mid1_args, dx3, dh2_1)
    blk_go = rows_blocks(_matmul("gm_out_dw", gmix, do1, ta=True, outs=(BF16,)))
    dgmix, (theirs,) = _matmul("gm_out_dx", do1, w_go, tb=True, comm=_swap_comm([blk_go]))
    chip_add("gm_out", blk_go, theirs)
    gm_res = _rowwise("gm_mix_bwd", _bwd_of(f_gm, 2, 1, (0, 1, 2, 3, 4, 5, 6, 7)), seq, tmb,
                      gm_tiled + [_t(dgmix)], gm_full, [(d_gm, BF16), (d_gm, BF16)],
                      [(1, d_gm)] * 4 + [w_sp.shape, bsp_t.shape])
    dzg = jnp.concatenate([gm_res[0], gm_res[1]], axis=1)
    g_gm_b_in = jnp.concatenate([gm_res[2], gm_res[3]], axis=1)
    g_gm_vg, g_gm_vb, g_w_sp = gm_res[4], gm_res[5], gm_res[6]
    g_b_sp = gm_res[7][:, :gm_groups].T
    dh1 = _matmul("gm_in_dx", dzg, w_gi, tb=True)
    blk_gi = _matmul("gm_in_dw", h1, dzg, ta=True, outs=(BF16,), col_blocks=N_DEV)

    def f_pre_bwd(pid, xv, dh, dxa, g, sh, sc):
        dxv, dg, dsh, dsc = _bwd_of(_f_pre, 1, 1, (0, 1, 2, 3))(pid, xv, dh, g, sh, sc)
        return (dxv + dxa, dg, dsh, dsc)

    res = _rowwise("pre_bwd1", f_pre_bwd, seq, tmb, [_t(x2l), _t(dh1), _t(dx2a)], pre1_args, [(d, F32)], [(1, d)] * 3)
    dx2l = res[0]
    d_norm[1][0], d_ml[1][0], d_ml[1][1] = res[1:]

    dx1, dm0 = post_bwd(0, x1, m0, post0_args, dx2l)
    dh2_0, (theirs,) = mlp_bwd(0, dm0, act0, h2_0, first_comm=_swap_comm([blk_gi]))
    chip_add("gm_in", blk_gi, theirs)
    dxa, do0 = mid_bwd(0, x2, o0, mid0_args, dx1, dh2_0)
    blk_out = rows_blocks(_matmul("ar_out_dw", ar, do0, ta=True, outs=(BF16,)))
    d_ar, (theirs,) = _matmul("ar_out_dx", do0, w_out, tb=True, comm=_swap_comm([blk_out]))
    chip_add("ar_out", blk_out, theirs)

    late = ["ff_in_1", "gm_out", "gm_in", "ff_in_0"]
    dq, dkt, dvt, *landed = _attn_bwd(q_r, kv_r, d_ar, nct, tm, comm=_chips_comm([chip_sums[k] for k in late]))
    recv.update(zip(late, landed))
    dkv_all = jnp.concatenate([dkt, dvt], axis=0).T

    def f_qkv_bwd(pid, pq, pkv, cos_t, sin_t, dq_t, dkv_t, *fulls):
        dq_t = jnp.where(pid < nctb, 0.0, dq_t)
        return _bwd_of(f_qkv, 4, 2, (0, 1, 4, 5))(pid, pq, pkv, cos_t, sin_t, dq_t, dkv_t, *fulls)

    qkv_res = _rowwise("qkv_bwd", f_qkv_bwd, t_all, tmb, qkv_tiled + [_t(dq, -n_ctx), _t(dkv_all)],
                       [q_g, k_g, perm], [(attn_w, BF16), (2 * kv_w, BF16)], [q_g.shape, k_g.shape])
    dproj_q, dproj_kv, g_q_g, g_k_g = qkv_res

    rnn_res = _rowwise("rnn_out_bwd", _bwd_of(_f_rnnout, 3, 1, (0, 2)), seq, tmb, rnn_tiled + [_t(d_ar, 0, 1, d_rnn)], [],
                       [(d_rnn, F32), (d_rnn, BF16)])
    zc = jnp.zeros((n_ctx, d_rnn), F32)
    dh_all = to3(jnp.concatenate([zc, rnn_res[0]], axis=0))
    dproj_g = jnp.concatenate([zc.astype(BF16), rnn_res[1]], axis=0)
    da_f, db_f = _scan_bwd("scan_f_bwd", to3(a_f), dh_all, hp_f, nc_scan, False)
    da_b, db_b = _scan_bwd("scan_b_bwd", to3(a_b), dh_all, hp_b, nc_scan, True)
    gate_cts = [_t(a.reshape(t_all, d_rnn)) for a in (da_f, db_f, da_b, db_b)]
    gates_res = _rowwise("gates_bwd", _bwd_of(f_gates, 1, 4, (0, 1, 2, 3, 4, 5)), t_all, min(tmb, GATES_BWD_TILE),
                         [_t(xc)] + gate_cts, gate_full, [(d_rnn, F32)], [wa3.shape, ba.shape, wx3.shape, bx.shape, lam.shape])
    dxc, g_wa, g_ba, g_wx, g_bx, g_lam = gates_res
    f_conv_b = functools.partial(_f_conv_bwd, n_ctx_tiles=nctb, n_tiles=t_all // tmb)
    conv_tiled = with_neighbours(proj, tmb, 1, d_rnn) + with_neighbours(dxc, tmb)
    dproj_x, g_conv_w, g_conv_b = _rowwise("conv_bwd", f_conv_b, t_all, tmb, conv_tiled, [conv_w],
                                           [(d_rnn, BF16)], [conv_w.shape, (1, d_rnn)])
    dproj = jnp.concatenate([dproj_q, dproj_x, dproj_g, dproj_kv], axis=1)
    dh0, (recv["ar_out"],) = _matmul("ar_in_dx", dproj, w_in, tb=True, tm=tm_tok, comm=_chips_comm([chip_sums["ar_out"]]))
    sq_pack, off_sq = _pack([g_wa, g_wx, g_w_sp])
    g_w_in, (g_sq,) = _matmul("ar_in_dw", h0, dproj, ta=True, outs=(BF16,), comm=_gather_comm([sq_pack]))
    g_w_in = jnp.concatenate([g_w_in[:, :attn_w], g_w_in[:, 3 * attn_w:], g_w_in[:, attn_w:3 * attn_w]], axis=1)
    blk_in = cols_blocks(g_w_in)

    f_pre0b = functools.partial(_f_pre_ctx, n_ctx_tiles=nctb)

    def f_pre0_bwd(pid, xcv, xlv, dh, dxp, g, shc, scc, shl, scl):
        grads = _bwd_of(f_pre0b, 2, 1, (1, 2, 3, 4, 5, 6))(pid, xcv, xlv, dh, g, shc, scc, shl, scl)
        return (grads[0] + dxp,) + tuple(grads[1:])

    res = _rowwise("pre_bwd0", f_pre0_bwd, seq, tmb, tokens + [_t(dh0), _t(dxa, -n_ctx)], pre0_args, [(d, F32)], [(1, d)] * 5,
                   comm=_swap_comm([blk_in]), skip_rows=n_ctx)
    grad_x = res[0][None]
    d_norm[0][0], d_mc_shift, d_mc_scale, d_ml[0][0], d_ml[0][1] = res[1:6]
    chip_add("ar_in", blk_in, res[6])

    z1d = jnp.zeros((1, d), F32)
    dml = jnp.concatenate([jnp.concatenate(r, axis=0)[None] for r in d_ml], axis=0)
    dmc = jnp.concatenate([jnp.concatenate([d_mc_shift, d_mc_scale] + [z1d] * 4, axis=0)[None],
                           jnp.zeros((n_layers - 1, 6, d), F32)], axis=0)
    g_norm = jnp.concatenate([jnp.concatenate(r, axis=0)[None] for r in d_norm], axis=0)
    small_parts = [dmc, g_norm, g_q_g, g_k_g, g_conv_w, g_conv_b, g_ba, g_bx, g_lam, g_gm_b_in, g_gm_vg, g_gm_vb, g_b_sp]
    small2, off2 = _pack([dml] + small_parts)
    (gs2,) = _run_comm("gather_small_grads", _gather_comm([small2]))
    dml_all = gs2.reshape(N_DEV, -1)[:, :off2[1]].reshape(N_DEV, n_layers, 6 * d)
    summed = _sum_lead("sum_small_grads", gs2).reshape(-1)
    summed_sq = _sum_lead("sum_square_grads", g_sq).reshape(-1)

    def seg2(k, shape):
        return summed[off2[k + 1]:off2[k + 2]].reshape(shape)

    def seg_sq(k, shape):
        return summed_sq[off_sq[k]:off_sq[k + 1]].reshape(shape)

    dmc_sum = seg2(0, (n_layers, 6 * d))
    dmod_rows = jnp.concatenate([jnp.moveaxis(dml_all, 0, 1), dmc_sum[:, None, :],
                                 jnp.zeros((n_layers, MOD_ROWS - N_DEV - 1, 6 * d), F32)], axis=1)
    g_b_mod = _sum_lead("sum_b_mod", jnp.moveaxis(dmod_rows, 1, 0).reshape(MOD_ROWS, n_layers * 6 * d // LANES, LANES))
    g_b_mod = g_b_mod.reshape(n_layers, 6 * d)
    g_w_mod, ds16 = _mod_bwd(c16, w_mod, _my_cols(dmod_rows, me, n6))
    (g_ds,) = _run_comm("gather_dctx", _gather_comm([ds16[N_DEV].reshape(d // LANES, LANES)]))
    ds_ctx = _sum_lead("sum_dctx", g_ds)
    (g_c_ctx,) = _rowwise("silu_bwd", _f_silu_mul, d // LANES, d // LANES, [_t(c_ctx.reshape(d // LANES, LANES)), _t(ds_ctx)],
                          [], [(LANES, F32)])
    g_c_ctx = g_c_ctx.reshape(d)

    grads = {
        'c_ctx': g_c_ctx, 'b_mod': g_b_mod,
        'norm_g': _my_cols(seg2(1, (n_layers, 4, d)), me, d // N_DEV),
        'ar_q_g': seg2(2, ar_q_g.shape), 'ar_k_g': seg2(3, ar_k_g.shape),
        'ar_conv_w': _my_cols(seg2(4, (1, CONV_W, d_rnn)), me, d_rnn // N_DEV),
        'ar_conv_b': seg2(5, ar_conv_b.shape),
        'ar_ba': _my_cols(seg2(6, (1, 2, d_rnn)), me, d_rnn // N_DEV),
        'ar_bx': _my_cols(seg2(7, (1, 2, d_rnn)), me, d_rnn // N_DEV),
        'ar_lambda': _my_cols(seg2(8, (1, 2, d_rnn)), me, d_rnn // N_DEV),
        'gm_b_in': _my_cols(seg2(9, (1, 2 * d_gm)), me, 2 * d_gm // N_DEV),
        'gm_v_g': _my_cols(seg2(10, (1, d_gm)), me, d_gm // N_DEV),
        'gm_v_b': _my_cols(seg2(11, (1, d_gm)), me, d_gm // N_DEV),
        'gm_b_sp': seg2(12, gm_b_sp.shape),
        'ar_wa': seg_sq(0, ar_wa.shape), 'ar_wx': seg_sq(1, ar_wx.shape), 'gm_w_sp': seg_sq(2, gm_w_sp.shape),
    }
    deltas, new_m, new_v = {}, {}, {}

    small_names = list(grads)
    wp, offw = _pack([wts[n] for n in small_names])
    mp, _ = _pack([mom1[n] for n in small_names])
    vp, _ = _pack([mom2[n] for n in small_names])
    gp, _ = _pack([grads[n] for n in small_names])
    dp, mp2, vp2, recv["ar_in"] = _adam_f32("adam_small", gp, wp, mp, vp, comm=_chips_comm([chip_sums["ar_in"]]))
    for k, n in enumerate(small_names):
        for dst, slab in ((deltas, dp), (new_m, mp2), (new_v, vp2)):
            dst[n] = slab.reshape(-1)[offw[k]:offw[k + 1]].reshape(wts[n].shape)

    grads['w_mod'] = g_w_mod
    dw, mw, vw = _adam_f32("adam_w_mod", g_w_mod.reshape(n_layers * d, n6), w_mod.reshape(n_layers * d, n6),
                           m_w_mod.reshape(n_layers * d, n6), v_w_mod.reshape(n_layers * d, n6))
    deltas['w_mod'], new_m['w_mod'], new_v['w_mod'] = (a.reshape(w_mod.shape) for a in (dw, mw, vw))

    received = {
        'w_ff_in': [recv[f"ff_in_{i}"] for i in range(n_layers)], 'w_ff_out': [recv[f"ff_out_{i}"] for i in range(n_layers)],
        'ar_w_in': [recv["ar_in"]], 'ar_w_out': [recv["ar_out"]], 'gm_w_in': [recv["gm_in"]], 'gm_w_out': [recv["gm_out"]]}
    for n, r in received.items():
        shp = wts[n].shape
        flat = (shp[0] * shp[1], shp[2])
        res = _adam_recv("adam_" + n, r, wts[n].reshape(flat), mom1[n].reshape(flat), mom2[n].reshape(flat))
        grads[n], deltas[n], new_m[n], new_v[n] = (a.reshape(shp) for a in res)

    return (loss, grad_x, *[grads[n] for n in WEIGHTS], *[deltas[n] for n in WEIGHTS],
            *[new_m[n] for n in WEIGHTS], *[new_v[n] for n in WEIGHTS])
```

```python
import functools

import numpy as np
import jax
import jax.numpy as jnp
from jax import lax
from jax.experimental import pallas as pl
from jax.experimental.pallas import tpu as pltpu

F32 = jnp.float32
BF16 = jnp.bfloat16
HIGHEST = lax.Precision.HIGHEST
LOG2_E = 1.4426950408889634
LN_2 = 0.6931471805599453

GRID_W = 64
N_HEADS = 8
N_KV_HEADS = 2
HEAD_DIM = 128
ROPE_THETA = 10000.0
RNN_BLOCK_W = 128
CONV_W = 4
RG_C = 8.0
GM_GROUP_W = 128
CHUNK = 128
EPS = 1e-6
ADAM_LR = 0.001
ADAM_B1 = 0.9
ADAM_B2 = 0.999
ADAM_EPS = 1e-08
ADAM_WD = 0.01
ADAM_STEP = 10

N_DEV = 8
MOD_ROWS = 16
LANES = 128
ROW_TILE = 256
ROW_TILE_BWD = 256
ATTN_BWD_HEADS_PER_STEP = 4
ATTN_HEADS_PER_STEP = 4
GATES_BWD_TILE = 128
SCAN_BLOCK = 256
VMEM_LIMIT = 56 * 1024 * 1024
PACK_ROWS = 512
MM_TILE_M = 1024
MM_TILE_N = 1024
MM_TILE_K = 2048

WEIGHTS = ['c_ctx', 'w_mod', 'b_mod', 'norm_g', 'w_ff_in', 'w_ff_out', 'ar_w_in', 'ar_q_g', 'ar_k_g', 'ar_conv_w',
           'ar_conv_b', 'ar_wa', 'ar_ba', 'ar_wx', 'ar_bx', 'ar_lambda', 'ar_w_out', 'gm_w_in', 'gm_b_in', 'gm_v_g',
           'gm_v_b', 'gm_w_sp', 'gm_b_sp', 'gm_w_out']


def _sds(shape, dtype):
    return jax.ShapeDtypeStruct(tuple(shape), dtype)


def _tile(dim, pref, align):
    t = (min(pref, dim) // align) * align
    while t >= align:
        if dim % t == 0:
            return t
        t -= align
    return dim


def _params(sem):
    return pltpu.CompilerParams(dimension_semantics=sem, vmem_limit_bytes=VMEM_LIMIT)


def _rms(x, g):
    return x * lax.rsqrt(jnp.mean(x * x, axis=-1, keepdims=True) + EPS) * g


def _gelu(x):
    return 0.5 * x * (1.0 + jnp.tanh(0.7978845608028654 * (x + 0.044715 * (x * x * x))))


def _sigmoid(x):
    return 0.5 * (jnp.tanh(0.5 * x) + 1.0)


def _log1p_pos(u):
    small = u < 1e-3
    us = jnp.where(small, u, 0.0)
    return jnp.where(small, us * (1.0 - us * (0.5 - us * (1.0 / 3.0))), jnp.log(1.0 + u))


def _softplus(x):
    return jnp.maximum(x, 0.0) + _log1p_pos(jnp.exp(-jnp.abs(x)))


def _expm1(x):
    small = jnp.abs(x) < 0.3
    xs = jnp.where(small, x, 0.0)
    poly = xs * (1.0 + xs * (1.0 / 2 + xs * (1.0 / 6 + xs * (1.0 / 24 + xs * (1.0 / 120 + xs * (1.0 / 720 + xs * (1.0 / 5040)))))))
    return jnp.where(small, poly, jnp.exp(x) - 1.0)


def _f_pre_ctx(pid, xc, xl, g, shc, scc, shl, scl, *, n_ctx_tiles):
    is_ctx = pid < n_ctx_tiles
    x = jnp.where(is_ctx, xc, xl)
    sh = jnp.where(is_ctx, shc, shl)
    sc = jnp.where(is_ctx, scc, scl)
    return (_rms(x, g) * (1.0 + sc) + sh,)


def _f_pre(pid, x, g, sh, sc):
    return (_rms(x, g) * (1.0 + sc) + sh,)


def _f_mid(pid, x, o, g1, gate, g2, sh, sc):
    x1 = x + gate * _rms(o, g1)
    return (x1, _rms(x1, g2) * (1.0 + sc) + sh)


def _f_post(pid, x, o, g, gate):
    return (x + gate * _rms(o, g),)


def _f_qkv(pid, pq, pkv, cos, sin, q_g, k_g, perm, *, nh, nkv):
    hd = HEAD_DIM

    def norm_rope(xh, g):
        y = _rms(xh, g)
        return y * cos + jnp.dot(y, perm, precision=HIGHEST, preferred_element_type=F32) * sin

    qs = [norm_rope(pq[:, h * hd:(h + 1) * hd], q_g) * (HEAD_DIM ** -0.5 * LOG2_E) for h in range(nh)]
    ks = [norm_rope(pkv[:, h * hd:(h + 1) * hd], k_g) for h in range(nkv)]
    return (jnp.concatenate(qs, axis=1), jnp.concatenate(ks + [pkv[:, nkv * hd:]], axis=1))


def _f_gates(pid, x, wa, ba, wx, bx, lam, *, nb):
    w = RNN_BLOCK_W
    outs = []
    for d in range(2):
        ra, ri = [], []
        for n in range(nb):
            xn = x[:, n * w:(n + 1) * w].astype(BF16)
            ra.append(jnp.dot(xn, wa[d * nb + n].astype(BF16), preferred_element_type=F32))
            ri.append(jnp.dot(xn, wx[d * nb + n].astype(BF16), preferred_element_type=F32))
        r = _sigmoid(jnp.concatenate(ra, axis=1) + ba[d:d + 1])
        i = _sigmoid(jnp.concatenate(ri, axis=1) + bx[d:d + 1])
        log_a = -RG_C * r * _softplus(-lam[d:d + 1])
        outs.append(jnp.exp(log_a))
        outs.append(jnp.sqrt(-_expm1(2.0 * log_a)) * (i * x))
    return tuple(outs)


def _f_rnnout(pid, hf, hb, gr):
    return ((hf + hb) * _gelu(gr),)


def _f_gm(pid, zu, zv, bu, bv, v_g, v_b, w_sp, bsp_t, expand, *, n_chunks, groups):
    u = _gelu(zu + bu)
    v = _gelu(zv + bv)
    mu = jnp.mean(v, axis=-1, keepdims=True)
    vc = v - mu
    v = vc * lax.rsqrt(jnp.mean(vc * vc, axis=-1, keepdims=True) + EPS) * v_g + v_b
    bias = jnp.dot(bsp_t, expand, precision=HIGHEST, preferred_element_type=F32)
    outs = []
    for c in range(n_chunks):
        vch = v[c * CHUNK:(c + 1) * CHUNK]
        cols = [jnp.dot(w_sp[g].astype(BF16), vch[:, g * GM_GROUP_W:(g + 1) * GM_GROUP_W].astype(BF16),
                        preferred_element_type=F32) for g in range(groups)]
        outs.append(u[c * CHUNK:(c + 1) * CHUNK] * (jnp.concatenate(cols, axis=1) + bias))
    return (jnp.concatenate(outs, axis=0),)


def _f_silu_mul(pid, c, d):
    return (d * jax.grad(lambda z: jnp.sum(z * _sigmoid(z)))(c),)


def _bwd_of(fn, n_tiled, n_ct, want):
    def bwd(pid, *args):
        tiles = [t.astype(F32) for t in args[:n_tiled]]
        cts = args[n_tiled:n_tiled + n_ct]
        fulls = list(args[n_tiled + n_ct:])
        outs, vjp = jax.vjp(lambda *a: fn(pid, *a), *tiles, *fulls)
        grads = vjp(tuple(ct.astype(o.dtype) for ct, o in zip(cts, outs)))
        return tuple(grads[i] for i in want)
    return bwd


def _rowwise(name, fn, rows, tm, tiled, full, outs, accs=(), comm=None, skip_rows=0):
    n_t, n_f, n_o, n_a = len(tiled), len(full), len(outs), len(accs)
    assert skip_rows % tm == 0
    skip = skip_rows // tm
    n_tiles = rows // tm + skip

    def body(*refs):
        in_refs, c_ins, res_refs, c_outs, _, c_sems = _split_refs(refs, n_t + n_f, n_o + n_a, 0, comm)
        pid = pl.program_id(0)
        if comm is not None:
            @pl.when(pid == 0)
            def _():
                comm.start(c_ins, c_outs, *c_sems)

        res = fn(pid, *[r[...] for r in in_refs])
        o_refs, a_refs = res_refs[:n_o], res_refs[n_o:]
        for r, v in zip(o_refs, res[:n_o]):
            r[...] = v.astype(r.dtype)
        if n_a:
            @pl.when(pid == 0)
            def _():
                for r in a_refs:
                    r[...] = jnp.zeros_like(r)
            for r, v in zip(a_refs, res[n_o:]):
                r[...] += v.astype(F32)
        if comm is not None:
            @pl.when(pid == n_tiles - 1)
            def _():
                comm.finish(c_ins, c_outs, *c_sems)

    assert all(ro % tm == 0 for (_, ro, _, _) in tiled)
    in_specs = [pl.BlockSpec((tm, w), lambda i, ro=ro // tm, cb=cb, last=a.shape[0] // tm - 1: (jnp.clip(i + ro, 0, last), cb))
                for (a, ro, cb, w) in tiled]
    in_specs += [pl.BlockSpec(a.shape, lambda i, nd=a.ndim: (0,) * nd) for a in full]
    out_shape = [_sds((rows, w), dt) for (w, dt) in outs] + [_sds(s, F32) for s in accs]
    out_specs = [pl.BlockSpec((tm, w), lambda i: (jnp.maximum(i - skip, 0), 0)) for (w, _) in outs]
    out_specs += [pl.BlockSpec(tuple(s), lambda i, nd=len(s): (0,) * nd) for s in accs]
    c_in, c_out, c_scr = (comm.ins, comm.out_shapes, comm.scratch()) if comm is not None else ([], [], [])
    return pl.pallas_call(body, grid=(n_tiles,), in_specs=in_specs + [_ANY] * len(c_in), out_specs=out_specs + [_ANY] * len(c_out),
                          out_shape=out_shape + list(c_out), scratch_shapes=c_scr, name=name,
                          compiler_params=_params(("arbitrary",)))(*[t[0] for t in tiled], *full, *c_in)


def _t(a, row_off=0, col_blk=0, width=None):
    return (a, row_off, col_blk, a.shape[1] if width is None else width)


class _Comm:
    def __init__(self, ins, out_shapes, n_sems, start, finish):
        self.ins, self.out_shapes, self.n_sems, self.start, self.finish = list(ins), list(out_shapes), n_sems, start, finish

    def scratch(self):
        return [pltpu.SemaphoreType.DMA((self.n_sems,)), pltpu.SemaphoreType.DMA((self.n_sems,)),
                pltpu.SemaphoreType.DMA((len(self.ins),))]


_ANY = pl.BlockSpec(memory_space=pl.ANY)


class _SemSlice:
    def __init__(self, ref, first):
        self.ref, self.first = ref, first

    @property
    def at(self):
        return self

    def __getitem__(self, k):
        return self.ref.at[self.first + k]


def _both(*comms):
    comms = [cm for cm in comms if cm is not None]
    if len(comms) <= 1:
        return comms[0] if comms else None

    def parts(ins, outs, send, recv, local):
        i0 = o0 = s0 = 0
        for cm in comms:
            ni, no = len(cm.ins), len(cm.out_shapes)
            yield cm, (ins[i0:i0 + ni], outs[o0:o0 + no], _SemSlice(send, s0), _SemSlice(recv, s0), _SemSlice(local, i0))
            i0, o0, s0 = i0 + ni, o0 + no, s0 + cm.n_sems

    def start(*refs):
        for cm, sub in parts(*refs):
            cm.start(*sub)

    def finish(*refs):
        for cm, sub in parts(*refs):
            cm.finish(*sub)

    return _Comm(sum((cm.ins for cm in comms), []), sum((cm.out_shapes for cm in comms), []),
                 sum(cm.n_sems for cm in comms), start, finish)


def _row_block(cols, itemsize):
    return max(16, (1 << 20) // (cols * itemsize))


def _split_refs(refs, n_in, n_out, n_scratch, comm):
    ci, co, cs = (len(comm.ins), len(comm.out_shapes), 3) if comm is not None else (0, 0, 0)
    cuts = np.cumsum([0, n_in, ci, n_out, co, n_scratch, cs])
    return [refs[cuts[i]:cuts[i + 1]] for i in range(6)]


def _matmul(name, a, b, *, ta=False, tb=False, outs=((F32,)), epilogue=None, extras=(), tm=None, tn=None, tk=None, comm=None,
            col_blocks=None):
    m, k = (a.shape[1], a.shape[0]) if ta else a.shape
    n = b.shape[0] if tb else b.shape[1]
    tm = _tile(m, tm or MM_TILE_M, 128 if ta else 16)
    tn = _tile(n if col_blocks is None else n // col_blocks, tn or MM_TILE_N, 128)
    tk = _tile(k, tk or MM_TILE_K, 128 if not ta else 16)
    ni, nj, nk = m // tm, n // tn, k // tk
    n_e, n_o = len(extras), len(outs)
    dims = (((0 if ta else 1,), (1 if tb else 0,)), ((), ()))

    def body(*refs):
        ins, c_ins, o_refs, c_outs, scratch, c_sems = _split_refs(refs, 2 + n_e, n_o, 1 if nk > 1 else 0, comm)
        a_ref, b_ref, e_refs = ins[0], ins[1], ins[2:]
        i, j, kk = pl.program_id(0), pl.program_id(1), pl.program_id(2)
        if comm is not None:
            @pl.when(jnp.logical_and(jnp.logical_and(i == 0, j == 0), kk == 0))
            def _():
                comm.start(c_ins, c_outs, *c_sems)

        def finish(acc):
            res = (acc,) if epilogue is None else epilogue(acc, *[e[...] for e in e_refs])
            for r, v in zip(o_refs, res):
                r[...] = v.astype(r.dtype)

        prod = lax.dot_general(a_ref[...].astype(BF16), b_ref[...].astype(BF16), dims, preferred_element_type=F32)
        if nk == 1:
            finish(prod)
        else:
            acc = scratch[0]

            @pl.when(kk == 0)
            def _():
                acc[...] = prod

            @pl.when(kk > 0)
            def _():
                acc[...] += prod

            @pl.when(kk == nk - 1)
            def _():
                finish(acc[...])
        if comm is not None:
            @pl.when(jnp.logical_and(jnp.logical_and(i == ni - 1, j == nj - 1), kk == nk - 1))
            def _():
                comm.finish(c_ins, c_outs, *c_sems)

    a_spec = pl.BlockSpec((tk, tm), lambda i, j, kk: (kk, i)) if ta else pl.BlockSpec((tm, tk), lambda i, j, kk: (i, kk))
    b_spec = pl.BlockSpec((tn, tk), lambda i, j, kk: (j, kk)) if tb else pl.BlockSpec((tk, tn), lambda i, j, kk: (kk, j))
    mn_spec = pl.BlockSpec((tm, tn), lambda i, j, kk: (i, j))
    c_in, c_out, c_scr = (comm.ins, comm.out_shapes, comm.scratch()) if comm is not None else ([], [], [])
    if col_blocks is None:
        o_spec, o_shape = mn_spec, (m, n)
    else:
        per = n // col_blocks // tn
        o_spec = pl.BlockSpec((None, tm, tn), lambda i, j, kk: (j // per, i, j % per))
        o_shape = (col_blocks, m, n // col_blocks)
    res = pl.pallas_call(body, grid=(ni, nj, nk), in_specs=[a_spec, b_spec] + [mn_spec] * n_e + [_ANY] * len(c_in),
                         out_specs=[o_spec] * n_o + [_ANY] * len(c_out),
                         out_shape=[_sds(o_shape, dt) for dt in outs] + list(c_out),
                         scratch_shapes=([pltpu.VMEM((tm, tn), F32)] if nk > 1 else []) + c_scr, name=name,
                         compiler_params=_params(("arbitrary", "arbitrary", "arbitrary")))(a, b, *extras, *c_in)
    main = res[0] if n_o == 1 else res[:n_o]
    return main if comm is None else (main, res[n_o:])


def _attn_fwd(q, kv, n_ctx_tiles, tq, comm=None):
    t_all = q.shape[0]
    s_len = t_all - n_ctx_tiles * tq
    hd, groups = HEAD_DIM, N_HEADS // N_KV_HEADS
    hps = ATTN_HEADS_PER_STEP
    assert groups % hps == 0
    gsteps = groups // hps
    nq = s_len // tq

    def body(*refs):
        (q_ref, k_ref, v_ref), c_ins, (o_ref,), c_outs, _, c_sems = _split_refs(refs, 3, 1, 0, comm)
        kh, g, i = pl.program_id(0), pl.program_id(1), pl.program_id(2)
        if comm is not None:
            @pl.when(jnp.logical_and(jnp.logical_and(kh == 0, g == 0), i == 0))
            def _():
                comm.start(c_ins, c_outs, *c_sems)

        kk, vv = k_ref[...], v_ref[...]
        for h in range(hps):
            qh = q_ref[:, h * hd:(h + 1) * hd]
            s = lax.dot_general(qh, kk, (((1,), (1,)), ((), ())), preferred_element_type=F32)
            p = jnp.exp2(s - jnp.max(s, axis=-1, keepdims=True))
            l = jnp.sum(p, axis=-1, keepdims=True)
            o = jnp.dot(p.astype(BF16), vv, preferred_element_type=F32) * (1.0 / l)
            o_ref[:, h * hd:(h + 1) * hd] = o.astype(o_ref.dtype)
        if comm is not None:
            @pl.when(jnp.logical_and(jnp.logical_and(kh == N_KV_HEADS - 1, g == gsteps - 1), i == nq - 1))
            def _():
                comm.finish(c_ins, c_outs, *c_sems)

    c_in, c_out, c_scr = (comm.ins, comm.out_shapes, comm.scratch()) if comm is not None else ([], [], [])
    res = pl.pallas_call(
        body, grid=(N_KV_HEADS, gsteps, nq),
        in_specs=[pl.BlockSpec((tq, hps * hd), lambda kh, g, i: (i + n_ctx_tiles, kh * gsteps + g)),
                  pl.BlockSpec((t_all, hd), lambda kh, g, i: (0, kh)),
                  pl.BlockSpec((t_all, hd), lambda kh, g, i: (0, N_KV_HEADS + kh))] + [_ANY] * len(c_in),
        out_specs=[pl.BlockSpec((tq, hps * hd), lambda kh, g, i: (i, kh * gsteps + g))] + [_ANY] * len(c_out),
        out_shape=[_sds((s_len, N_HEADS * hd), BF16)] + list(c_out), scratch_shapes=c_scr, name="attn_fwd",
        compiler_params=_params(("arbitrary", "arbitrary", "arbitrary")))(q, kv, kv, *c_in)
    return res[0] if comm is None else (res[0], res[1:])


def _attn_bwd(q, kv, d_ar, n_ctx_tiles, tq, comm=None):
    t_all = q.shape[0]
    s_len = t_all - n_ctx_tiles * tq
    hd, groups = HEAD_DIM, N_HEADS // N_KV_HEADS
    hps = ATTN_BWD_HEADS_PER_STEP
    assert groups % hps == 0
    gsteps = groups // hps
    nq = s_len // tq

    def body(*refs):
        (q_ref, k_ref, v_ref, do_ref), c_ins, (dq_ref, dkt_ref, dvt_ref), c_outs, _, c_sems = _split_refs(refs, 4, 3, 0, comm)
        first = jnp.logical_and(pl.program_id(1) == 0, pl.program_id(2) == 0)
        if comm is not None:
            @pl.when(jnp.logical_and(first, pl.program_id(0) == 0))
            def _():
                comm.start(c_ins, c_outs, *c_sems)

        @pl.when(first)
        def _():
            dkt_ref[...] = jnp.zeros_like(dkt_ref)
            dvt_ref[...] = jnp.zeros_like(dvt_ref)

        kk, vv = k_ref[...], v_ref[...]
        for h in range(hps):
            cols = slice(h * hd, (h + 1) * hd)
            qv = q_ref[:, cols]
            s = lax.dot_general(qv, kk, (((1,), (1,)), ((), ())), preferred_element_type=F32)
            p = jnp.exp2(s - jnp.max(s, axis=-1, keepdims=True))
            inv_l = 1.0 / jnp.sum(p, axis=-1, keepdims=True)
            do = (do_ref[:, cols] * inv_l).astype(BF16)
            dp = lax.dot_general(do, vv, (((1,), (1,)), ((), ())), preferred_element_type=F32)
            ds = (p * (dp - jnp.sum(p * dp, axis=-1, keepdims=True) * inv_l)).astype(BF16)
            dq_ref[:, cols] = jnp.dot(ds, kk, preferred_element_type=F32) * LN_2
            dkt_ref[...] += jnp.dot(qv.T, ds, preferred_element_type=F32)
            dvt_ref[...] += jnp.dot(do.T, p.astype(BF16), preferred_element_type=F32)
        last = jnp.logical_and(pl.program_id(1) == gsteps - 1, pl.program_id(2) == nq - 1)

        @pl.when(last)
        def _():
            dkt_ref[...] *= LN_2

        if comm is not None:
            @pl.when(jnp.logical_and(last, pl.program_id(0) == N_KV_HEADS - 1))
            def _():
                comm.finish(c_ins, c_outs, *c_sems)

    c_in, c_out, c_scr = (comm.ins, comm.out_shapes, comm.scratch()) if comm is not None else ([], [], [])
    return pl.pallas_call(
        body, grid=(N_KV_HEADS, gsteps, nq),
        in_specs=[pl.BlockSpec((tq, hps * hd), lambda kh, g, i: (i + n_ctx_tiles, kh * gsteps + g)),
                  pl.BlockSpec((t_all, hd), lambda kh, g, i: (0, kh)),
                  pl.BlockSpec((t_all, hd), lambda kh, g, i: (0, N_KV_HEADS + kh)),
                  pl.BlockSpec((tq, hps * hd), lambda kh, g, i: (i, kh * gsteps + g))] + [_ANY] * len(c_in),
        out_specs=[pl.BlockSpec((tq, hps * hd), lambda kh, g, i: (i, kh * gsteps + g)),
                   pl.BlockSpec((hd, t_all), lambda kh, g, i: (kh, 0)),
                   pl.BlockSpec((hd, t_all), lambda kh, g, i: (kh, 0))] + [_ANY] * len(c_out),
        out_shape=[_sds((s_len, N_HEADS * hd), F32), _sds((N_KV_HEADS * hd, t_all), F32),
                   _sds((N_KV_HEADS * hd, t_all), F32)] + list(c_out),
        scratch_shapes=c_scr, name="attn_bwd",
        compiler_params=_params(("arbitrary", "arbitrary", "arbitrary")))(q, kv, kv, d_ar, *c_in)


def _scan_order(nb, nc, reverse):
    if not reverse:
        return lambda i: i
    return lambda i: jnp.where(i < nc, nc - 1 - i, nb - 1 - (i - nc))


def _scan_fwd(name, a, b, nc, reverse):
    t_all, r, l = a.shape
    tb = SCAN_BLOCK
    nb = t_all // tb
    order = _scan_order(nb, nc, reverse)

    def body(a_ref, b_ref, h_ref, hp_ref, carry):
        @pl.when(pl.program_id(0) == 0)
        def _():
            carry[...] = jnp.zeros_like(carry)

        def step(s, h):
            t = tb - 1 - s if reverse else s
            hp_ref[t] = h
            h = a_ref[t] * h + b_ref[t]
            h_ref[t] = h
            return h

        carry[...] = lax.fori_loop(0, tb, step, carry[...], unroll=8)

    spec = pl.BlockSpec((tb, r, l), lambda i: (order(i), 0, 0))
    return pl.pallas_call(body, grid=(nb,), in_specs=[spec, spec], out_specs=[spec, spec],
                          out_shape=[_sds(a.shape, F32)] * 2, scratch_shapes=[pltpu.VMEM((r, l), F32)], name=name,
                          compiler_params=_params(("arbitrary",)))(a, b)


def _scan_bwd(name, a, dh, hp, nc, reverse):
    t_all, r, l = a.shape
    tb = SCAN_BLOCK
    nb = t_all // tb
    primal = _scan_order(nb, nc, reverse)

    def order(i):
        return primal(nb - 1 - i)

    def body(a_ref, dh_ref, hp_ref, da_ref, db_ref, carry):
        @pl.when(pl.program_id(0) == 0)
        def _():
            carry[...] = jnp.zeros_like(carry)

        def step(s, cr):
            t = s if reverse else tb - 1 - s
            lam = dh_ref[t] + cr
            db_ref[t] = lam
            da_ref[t] = lam * hp_ref[t]
            return a_ref[t] * lam

        carry[...] = lax.fori_loop(0, tb, step, carry[...], unroll=8)

    spec = pl.BlockSpec((tb, r, l), lambda i: (order(i), 0, 0))
    return pl.pallas_call(body, grid=(nb,), in_specs=[spec] * 3, out_specs=[spec, spec],
                          out_shape=[_sds(a.shape, F32)] * 2, scratch_shapes=[pltpu.VMEM((r, l), F32)], name=name,
                          compiler_params=_params(("arbitrary",)))(a, dh, hp)


def _shifted(prev, cur, nxt, k, pid, n_ctx_tiles, n_tiles):
    if k == 0:
        return cur
    tm = cur.shape[0]
    row = lax.broadcasted_iota(jnp.int32, cur.shape, 0)
    if k < 0:
        at_start = jnp.logical_or(pid == 0, pid == n_ctx_tiles)
        edge = jnp.where(at_start, 0.0, pltpu.roll(prev, -k, 0))
        return jnp.where(row < -k, edge, pltpu.roll(cur, -k, 0))
    at_end = jnp.logical_or(pid == n_ctx_tiles - 1, pid == n_tiles - 1)
    edge = jnp.where(at_end, 0.0, pltpu.roll(nxt, tm - k, 0))
    return jnp.where(row >= tm - k, edge, pltpu.roll(cur, tm - k, 0))


def _f_conv(pid, xp, xc, xn, w, b, *, n_ctx_tiles, n_tiles):
    y = b
    for j in range(CONV_W):
        y = y + _shifted(xp, xc, xn, j - CONV_W // 2, pid, n_ctx_tiles, n_tiles) * w[j:j + 1]
    return (y,)


def _f_conv_bwd(pid, xp, xc, xn, dp, dc, dn, w, *, n_ctx_tiles, n_tiles):
    dx = jnp.zeros_like(dc)
    dw = []
    for j in range(CONV_W):
        k = j - CONV_W // 2
        dx = dx + _shifted(dp, dc, dn, -k, pid, n_ctx_tiles, n_tiles) * w[j:j + 1]
        dw.append(jnp.sum(dc * _shifted(xp, xc, xn, k, pid, n_ctx_tiles, n_tiles), axis=0, keepdims=True))
    return (dx, jnp.concatenate(dw, axis=0), jnp.sum(dc, axis=0, keepdims=True))


def _mesh_pos():
    return lax.axis_index("x"), lax.axis_index("y"), lax.axis_index("c")


def _remote(src, dst, send_sems, recv_sems, k, to):
    return pltpu.make_async_remote_copy(src_ref=src, dst_ref=dst, send_sem=send_sems.at[k], recv_sem=recv_sems.at[k],
                                        device_id=to, device_id_type=pl.DeviceIdType.MESH)


def _neighbours():
    x, y, c = _mesh_pos()
    return (x, y, c), (x, y, 1 - c), [(1 - x, y), (x, 1 - y), (1 - x, 1 - y)]


def _gather_comm(arrays):
    n = len(arrays)
    per = 7

    def slot(out, blk):
        return out.at[4 * blk[0] + 2 * blk[1] + blk[2]]

    def start(ins, outs, send, recv, local):
        me, sib, chips = _neighbours()
        for ai in range(n):
            pltpu.make_async_copy(ins[ai], slot(outs[ai], me), local.at[ai]).start()
            _remote(ins[ai], slot(outs[ai], me), send, recv, ai * per, sib).start()
            for j, chip in enumerate(chips):
                _remote(ins[ai], slot(outs[ai], me), send, recv, ai * per + 1 + j, (*chip, me[2])).start()

    def finish(ins, outs, send, recv, local):
        me, sib, chips = _neighbours()
        for ai in range(n):
            for j, chip in enumerate(chips):
                blk = slot(outs[ai], (*chip, me[2]))
                _remote(blk, blk, send, recv, ai * per + 1 + j, me).wait_recv()
                _remote(blk, blk, send, recv, ai * per + 4 + j, sib).start()
        for ai in range(n):
            blk = slot(outs[ai], sib)
            _remote(blk, blk, send, recv, ai * per, me).wait_recv()
            for j, chip in enumerate(chips):
                blk = slot(outs[ai], (*chip, 1 - me[2]))
                _remote(blk, blk, send, recv, ai * per + 4 + j, me).wait_recv()
            for k in range(per):
                _remote(ins[ai], slot(outs[ai], me), send, recv, ai * per + k, sib).wait_send()
            pltpu.make_async_copy(ins[ai], slot(outs[ai], me), local.at[ai]).wait()

    return _Comm(arrays, [_sds((N_DEV,) + a.shape, a.dtype) for a in arrays], n * per, start, finish)


def _swap_comm(arrays):
    n = len(arrays)

    def start(ins, outs, send, recv, local):
        me, sib, _ = _neighbours()
        for ai in range(n):
            for q in range(4):
                _remote(ins[ai].at[2 * q + 1 - me[2]], outs[ai].at[q], send, recv, ai * 4 + q, sib).start()

    def finish(ins, outs, send, recv, local):
        me, sib, _ = _neighbours()
        for ai in range(n):
            for q in range(4):
                cp = _remote(ins[ai].at[q], outs[ai].at[q], send, recv, ai * 4 + q, sib)
                cp.wait_recv()
                cp.wait_send()

    return _Comm(arrays, [_sds((4,) + a.shape[1:], a.dtype) for a in arrays], n * 4, start, finish)


def _chips_comm(arrays):
    n = len(arrays)

    def start(ins, outs, send, recv, local):
        me, _, chips = _neighbours()
        mine = 2 * me[0] + me[1]
        for ai in range(n):
            pltpu.make_async_copy(ins[ai].at[mine], outs[ai].at[mine], local.at[ai]).start()
            for j, chip in enumerate(chips):
                _remote(ins[ai].at[2 * chip[0] + chip[1]], outs[ai].at[mine], send, recv, ai * 3 + j, (*chip, me[2])).start()

    def finish(ins, outs, send, recv, local):
        me, _, chips = _neighbours()
        mine = 2 * me[0] + me[1]
        for ai in range(n):
            for j, chip in enumerate(chips):
                theirs = 2 * chip[0] + chip[1]
                cp = _remote(ins[ai].at[theirs], outs[ai].at[theirs], send, recv, ai * 3 + j, (*chip, me[2]))
                cp.wait_recv()
                cp.wait_send()
            pltpu.make_async_copy(ins[ai].at[mine], outs[ai].at[mine], local.at[ai]).wait()

    return _Comm(arrays, [_sds(a.shape, a.dtype) for a in arrays], n * 3, start, finish)


def _run_comm(name, comm):
    n_in, n_out = len(comm.ins), len(comm.out_shapes)

    def body(*refs):
        ins, outs, sems = refs[:n_in], refs[n_in:n_in + n_out], refs[n_in + n_out:]
        comm.start(ins, outs, *sems)
        comm.finish(ins, outs, *sems)

    return pl.pallas_call(body, in_specs=[_ANY] * n_in, out_specs=[_ANY] * n_out, out_shape=comm.out_shapes, name=name,
                          scratch_shapes=comm.scratch(), compiler_params=pltpu.CompilerParams(has_side_effects=True))(*comm.ins)


def _chip_add(name, blocks, theirs, core):
    _, r, c = blocks.shape
    tr = _tile(r, _row_block(c, 2), 16)

    def body(core_ref, a_ref, b_ref, o_ref):
        o_ref[...] = (a_ref[...].astype(F32) + b_ref[...].astype(F32)).astype(o_ref.dtype)

    spec = pl.BlockSpec((None, tr, c), lambda q, i, core_ref: (q, i, 0))
    grid_spec = pltpu.PrefetchScalarGridSpec(
        num_scalar_prefetch=1, grid=(4, r // tr),
        in_specs=[pl.BlockSpec((None, tr, c), lambda q, i, core_ref: (2 * q + core_ref[0], i, 0)), spec], out_specs=spec)
    return pl.pallas_call(body, grid_spec=grid_spec, out_shape=_sds((4, r, c), blocks.dtype), name=name,
                          compiler_params=_params(("parallel", "parallel")))(jnp.reshape(core, (1,)).astype(jnp.int32), blocks, theirs)


def _sum_lead(name, a):
    n, r, c = a.shape
    tr = _tile(r, _row_block(c, 4 * n // 2), 16)

    def body(a_ref, o_ref):
        acc = a_ref[0].astype(F32)
        for j in range(1, n):
            acc = acc + a_ref[j].astype(F32)
        o_ref[...] = acc

    return pl.pallas_call(body, grid=(r // tr,), in_specs=[pl.BlockSpec((n, tr, c), lambda i: (0, i, 0))],
                          out_specs=pl.BlockSpec((tr, c), lambda i: (i, 0)), out_shape=_sds((r, c), F32), name=name,
                          compiler_params=_params(("parallel",)))(a)


def _adam_math(w, g, m, v):
    m = ADAM_B1 * m + (1.0 - ADAM_B1) * g
    v = ADAM_B2 * v + (1.0 - ADAM_B2) * (g * g)
    m_hat = m / (1.0 - ADAM_B1 ** ADAM_STEP)
    v_hat = v / (1.0 - ADAM_B2 ** ADAM_STEP)
    delta = -ADAM_LR * (m_hat / (jnp.sqrt(v_hat) + ADAM_EPS) + ADAM_WD * w)
    return delta, m, v


def _adam_recv(name, recvs, w, m, v):
    nl = len(recvs)
    n, rl, c = recvs[0].shape
    tr = _tile(rl, _row_block(c, 4), 16)
    per = rl // tr

    def body(*refs):
        g_refs = refs[:nl]
        w_ref, m_ref, v_ref, go_ref, d_ref, mo_ref, vo_ref = refs[nl:]
        for layer in range(nl):
            @pl.when(pl.program_id(0) == layer)
            def _(g_ref=g_refs[layer]):
                g = g_ref[0].astype(F32)
                for j in range(1, n):
                    g = g + g_ref[j].astype(F32)
                delta, m2, v2 = _adam_math(w_ref[...], g, m_ref[...], v_ref[...])
                go_ref[...] = g
                d_ref[...] = delta
                mo_ref[...] = m2
                vo_ref[...] = v2

    g_specs = [pl.BlockSpec((n, tr, c), lambda l, i, layer=layer: (0, jnp.where(l == layer, i, 0), 0)) for layer in range(nl)]
    spec = pl.BlockSpec((tr, c), lambda l, i: (l * per + i, 0))
    return pl.pallas_call(body, grid=(nl, per), in_specs=g_specs + [spec] * 3, out_specs=[spec] * 4,
                          out_shape=[_sds((nl * rl, c), F32)] * 4, name=name,
                          compiler_params=_params(("arbitrary", "arbitrary")))(*recvs, w, m, v)


def _adam_f32(name, g, w, m, v, comm=None):
    r, c = g.shape
    tr = _tile(r, _row_block(c, 4), 8)
    steps = r // tr

    def body(*refs):
        (g_ref, w_ref, m_ref, v_ref), c_ins, (d_ref, mo_ref, vo_ref), c_outs, _, c_sems = _split_refs(refs, 4, 3, 0, comm)
        if comm is not None:
            @pl.when(pl.program_id(0) == 0)
            def _():
                comm.start(c_ins, c_outs, *c_sems)

        delta, m2, v2 = _adam_math(w_ref[...], g_ref[...], m_ref[...], v_ref[...])
        d_ref[...] = delta
        mo_ref[...] = m2
        vo_ref[...] = v2
        if comm is not None:
            @pl.when(pl.program_id(0) == steps - 1)
            def _():
                comm.finish(c_ins, c_outs, *c_sems)

    spec = pl.BlockSpec((tr, c), lambda i: (i, 0))
    c_in, c_out, c_scr = (comm.ins, comm.out_shapes, comm.scratch()) if comm is not None else ([], [], [])
    return pl.pallas_call(body, grid=(steps,), in_specs=[spec] * 4 + [_ANY] * len(c_in), out_specs=[spec] * 3 + [_ANY] * len(c_out),
                          out_shape=[_sds((r, c), F32)] * 3 + list(c_out), scratch_shapes=c_scr, name=name,
                          compiler_params=_params(("arbitrary",)))(g, w, m, v, *c_in)


def _mod_fwd(c16, w_mod, b_loc):
    nl, d, n6 = w_mod.shape
    tn = _tile(n6, 512, 128)

    def body(c_ref, w_ref, b_ref, o_ref):
        cv = c_ref[...]
        s = cv * _sigmoid(cv)
        o_ref[0] = jnp.dot(s, w_ref[0], precision=HIGHEST, preferred_element_type=F32) + b_ref[0]

    return pl.pallas_call(
        body, grid=(nl, n6 // tn),
        in_specs=[pl.BlockSpec((MOD_ROWS, d), lambda i, j: (0, 0)), pl.BlockSpec((1, d, tn), lambda i, j: (i, 0, j)),
                  pl.BlockSpec((1, 1, tn), lambda i, j: (i, 0, j))],
        out_specs=pl.BlockSpec((1, MOD_ROWS, tn), lambda i, j: (i, 0, j)), out_shape=_sds((nl, MOD_ROWS, n6), F32),
        name="mod_fwd", compiler_params=_params(("parallel", "parallel")))(c16, w_mod, b_loc)


def _mod_bwd(c16, w_mod, dmod_loc):
    nl, d, n6 = w_mod.shape
    tn = _tile(n6, 512, 128)

    def body(c_ref, w_ref, dm_ref, dw_ref, ds_ref):
        @pl.when(jnp.logical_and(pl.program_id(0) == 0, pl.program_id(1) == 0))
        def _():
            ds_ref[...] = jnp.zeros_like(ds_ref)

        cv = c_ref[...]
        s = cv * _sigmoid(cv)
        dm = dm_ref[0]
        dw_ref[0] = lax.dot_general(s, dm, (((0,), (0,)), ((), ())), precision=HIGHEST, preferred_element_type=F32)
        ds_ref[...] += lax.dot_general(dm, w_ref[0], (((1,), (1,)), ((), ())), precision=HIGHEST, preferred_element_type=F32)

    return pl.pallas_call(
        body, grid=(nl, n6 // tn),
        in_specs=[pl.BlockSpec((MOD_ROWS, d), lambda i, j: (0, 0)), pl.BlockSpec((1, d, tn), lambda i, j: (i, 0, j)),
                  pl.BlockSpec((1, MOD_ROWS, tn), lambda i, j: (i, 0, j))],
        out_specs=[pl.BlockSpec((1, d, tn), lambda i, j: (i, 0, j)), pl.BlockSpec((MOD_ROWS, d), lambda i, j: (0, 0))],
        out_shape=[_sds((nl, d, n6), F32), _sds((MOD_ROWS, d), F32)], name="mod_bwd",
        compiler_params=_params(("arbitrary", "arbitrary")))(c16, w_mod, dmod_loc)


def _pack(parts):
    flat = [p.reshape(-1).astype(F32) for p in parts]
    offs = np.cumsum([0] + [f.shape[0] for f in flat])
    total = int(offs[-1])
    unit = (PACK_ROWS if total > PACK_ROWS * LANES else 8) * LANES
    padded = -(-total // unit) * unit
    slab = jnp.concatenate(flat + [jnp.zeros((padded - total,), F32)])
    return slab.reshape(padded // LANES, LANES), [int(o) for o in offs]


def _unshard_cols(seg, lead):
    n = seg.shape[1] // int(np.prod(lead)) if lead else seg.shape[1]
    a = seg.reshape((N_DEV,) + tuple(lead) + (n,))
    a = jnp.moveaxis(a, 0, len(lead))
    return a.reshape(tuple(lead) + (N_DEV * n,))


def _my_cols(a, me, n):
    start = (0,) * (a.ndim - 1) + (me * n,)
    return lax.dynamic_slice(a, start, a.shape[:-1] + (n,))


def _rope_tables(seq, n_ctx):
    rows = seq // GRID_W
    r_idx, c_idx = jnp.meshgrid(jnp.arange(rows), jnp.arange(GRID_W), indexing='ij')
    r_idx = r_idx.reshape(-1).astype(F32)
    c_idx = c_idx.reshape(-1).astype(F32)
    pairs = HEAD_DIM // 4
    freqs = ROPE_THETA ** (-jnp.arange(pairs, dtype=F32) / pairs)
    ang_r, ang_c = r_idx[:, None] * freqs, c_idx[:, None] * freqs
    cos = jnp.concatenate([jnp.cos(ang_r)] * 2 + [jnp.cos(ang_c)] * 2, axis=1)
    sin = jnp.concatenate([-jnp.sin(ang_r), jnp.sin(ang_r), -jnp.sin(ang_c), jnp.sin(ang_c)], axis=1)
    cos = jnp.concatenate([jnp.ones((n_ctx, HEAD_DIM), F32), cos], axis=0)
    sin = jnp.concatenate([jnp.zeros((n_ctx, HEAD_DIM), F32), sin], axis=0)
    lane = np.arange(HEAD_DIM)
    partner = np.where(lane % (2 * pairs) < pairs, lane + pairs, lane - pairs)
    perm = np.zeros((HEAD_DIM, HEAD_DIM), np.float32)
    perm[partner, lane] = 1.0
    return cos, sin, jnp.asarray(perm)


def kernel(x, c, ctx, c_ctx, w_mod, b_mod, norm_g, w_ff_in, w_ff_out, ar_w_in, ar_q_g, ar_k_g, ar_conv_w, ar_conv_b, ar_wa, ar_ba, ar_wx, ar_bx, ar_lambda, ar_w_out, gm_w_in, gm_b_in, gm_v_g, gm_v_b, gm_w_sp, gm_b_sp, gm_w_out, loss_target, m_c_ctx, m_w_mod, m_b_mod, m_norm_g, m_w_ff_in, m_w_ff_out, m_ar_w_in, m_ar_q_g, m_ar_k_g, m_ar_conv_w, m_ar_conv_b, m_ar_wa, m_ar_ba, m_ar_wx, m_ar_bx, m_ar_lambda, m_ar_w_out, m_gm_w_in, m_gm_b_in, m_gm_v_g, m_gm_v_b, m_gm_w_sp, m_gm_b_sp, m_gm_w_out, v_c_ctx, v_w_mod, v_b_mod, v_norm_g, v_w_ff_in, v_w_ff_out, v_ar_w_in, v_ar_q_g, v_ar_k_g, v_ar_conv_w, v_ar_conv_b, v_ar_wa, v_ar_ba, v_ar_wx, v_ar_bx, v_ar_lambda, v_ar_w_out, v_gm_w_in, v_gm_b_in, v_gm_v_g, v_gm_v_b, v_gm_w_sp, v_gm_b_sp, v_gm_w_out):
    given = dict(locals())
    wts = {n: given[n] for n in WEIGHTS}
    mom1 = {n: given["m_" + n] for n in WEIGHTS}
    mom2 = {n: given["v_" + n] for n in WEIGHTS}

    xi, yi, ci = _mesh_pos()
    me = 4 * xi + 2 * yi + ci

    seq, d = x.shape[1], x.shape[2]
    n_ctx = ctx.shape[1]
    t_all = n_ctx + seq
    n_layers = w_mod.shape[0]
    assert n_layers == 2 and ar_w_in.shape[0] == 1 and gm_w_in.shape[0] == 1
    d_ff = w_ff_in.shape[2] * N_DEV
    attn_w, kv_w = N_HEADS * HEAD_DIM, N_KV_HEADS * HEAD_DIM
    rnn_blocks = ar_wa.shape[2]
    d_rnn = rnn_blocks * RNN_BLOCK_W
    gm_groups = gm_w_sp.shape[1]
    d_gm = gm_groups * GM_GROUP_W
    ar_in = ar_w_in.shape[2] * N_DEV
    n6 = w_mod.shape[2]
    tm, tmb = ROW_TILE, ROW_TILE_BWD
    assert attn_w == d_rnn and ar_in == 3 * attn_w + 2 * kv_w and (3 * attn_w) % (2 * kv_w) == 0
    assert n_ctx % tm == 0 and seq % tm == 0 and n_ctx % SCAN_BLOCK == 0 and seq % SCAN_BLOCK == 0 and tm % CHUNK == 0
    nct, nctb = n_ctx // tm, n_ctx // tmb
    kv_blk = (3 * attn_w) // (2 * kv_w)
    lr = d_rnn // LANES

    x2, ctx2, tgt = x[0], ctx[0], loss_target[0]

    def cols_full(g):
        return jnp.moveaxis(g, 0, 1).reshape(g.shape[1], N_DEV * g.shape[2])

    small0, off0 = _pack([c[0], norm_g, ar_conv_w[0], ar_ba[0], ar_bx[0], ar_lambda[0], gm_b_in[0], gm_v_g[0], gm_v_b[0]])
    g_ar_in, gs0 = _run_comm("gather_first", _gather_comm([ar_w_in[0].astype(BF16), small0]))
    gs0 = gs0.reshape(N_DEV, -1)
    w_in = cols_full(g_ar_in)
    split = [attn_w, attn_w + 2 * kv_w, attn_w + 2 * kv_w + d_rnn]
    w_in = jnp.concatenate([w_in[:, :split[0]], w_in[:, split[1]:], w_in[:, split[0]:split[1]]], axis=1)
    w1, w2 = [None] * n_layers, [None] * n_layers

    def seg0(k):
        return gs0[:, off0[k]:off0[k + 1]]

    c_all = seg0(0)
    norm_full = _unshard_cols(seg0(1), (n_layers, 4))
    conv_w = _unshard_cols(seg0(2), (CONV_W,))
    ba, bx, lam = (_unshard_cols(seg0(k), (2,)) for k in (3, 4, 5))
    gm_b_in_f = seg0(6).reshape(1, 2 * d_gm)
    gm_vg, gm_vb = seg0(7).reshape(1, d_gm), seg0(8).reshape(1, d_gm)

    c16 = jnp.concatenate([c_all, c_ctx[None], jnp.zeros((MOD_ROWS - N_DEV - 1, d), F32)], axis=0)
    b_loc = _my_cols(b_mod, me, n6)[:, None, :]
    mod_loc = _mod_fwd(c16, w_mod, b_loc)
    (g_mod,) = _run_comm("gather_mod", _gather_comm([mod_loc]))
    mod_all = jnp.moveaxis(g_mod, 0, 2).reshape(n_layers, MOD_ROWS, N_DEV * n6)
    ml = lax.dynamic_index_in_dim(mod_all, me, axis=1, keepdims=False).reshape(n_layers, 6, d)
    mc = mod_all[:, N_DEV].reshape(n_layers, 6, d)

    def row(a, *idx):
        return a[idx][None]

    cos, sin, perm = _rope_tables(seq, n_ctx)
    wa3 = ar_wa[0].reshape(2 * rnn_blocks, RNN_BLOCK_W, RNN_BLOCK_W)
    wx3 = ar_wx[0].reshape(2 * rnn_blocks, RNN_BLOCK_W, RNN_BLOCK_W)
    conv_b = ar_conv_b
    q_g, k_g = ar_q_g, ar_k_g
    w_sp = gm_w_sp[0]
    bsp_t = jnp.pad(gm_b_sp[0].T, ((0, 0), (0, LANES - gm_groups)))
    expand = np.zeros((LANES, d_gm), np.float32)
    for g in range(gm_groups):
        expand[g, g * GM_GROUP_W:(g + 1) * GM_GROUP_W] = 1.0
    expand = jnp.asarray(expand)

    def relu2(acc):
        r = jnp.maximum(acc, 0.0)
        return (r * r,)

    def relu2_bwd(acc, act):
        return (acc * (2.0 * jnp.sqrt(act.astype(F32))),)

    def ff_in_shard(i):
        return w_ff_in[i].astype(BF16)

    def ff_out_shard(i):
        return w_ff_out[i].astype(BF16)

    tokens = [_t(ctx2), _t(x2, -n_ctx)]
    pre0_args = [row(norm_full, 0, 0), row(mc, 0, 0), row(mc, 0, 1), row(ml, 0, 0), row(ml, 0, 1)]
    f_pre0 = functools.partial(_f_pre_ctx, n_ctx_tiles=nct)
    (h0,) = _rowwise("pre0", f_pre0, t_all, tm, tokens, pre0_args, [(d, BF16)])
    tm_tok = _tile(t_all, 640, 16)
    proj, (g_gm_in,) = _matmul("ar_in", h0, w_in, tm=tm_tok, comm=_gather_comm([gm_w_in[0].astype(BF16)]))
    f_qkv = functools.partial(_f_qkv, nh=N_HEADS, nkv=N_KV_HEADS)
    qkv_tiled = [_t(proj, 0, 0, attn_w), _t(proj, 0, kv_blk, 2 * kv_w), _t(cos), _t(sin)]
    q_r, kv_r = _rowwise("qkv", f_qkv, t_all, tm, qkv_tiled, [q_g, k_g, perm], [(attn_w, BF16), (2 * kv_w, BF16)])
    attn_o, (g_ar_out, g_ff_in0) = _attn_fwd(q_r, kv_r, nct, tm,
                                             comm=_gather_comm([ar_w_out[0].astype(BF16), ff_in_shard(0)]))
    w_out = g_ar_out.reshape(attn_w + d_rnn, d)
    w1[0] = cols_full(g_ff_in0)

    def with_neighbours(a, t, col_blk=0, width=None):
        return [_t(a, -t, col_blk, width), _t(a, 0, col_blk, width), _t(a, t, col_blk, width)]

    f_conv = functools.partial(_f_conv, n_ctx_tiles=nct, n_tiles=t_all // tm)
    (xc,) = _rowwise("conv", f_conv, t_all, tm, with_neighbours(proj, tm, 1, d_rnn), [conv_w, conv_b], [(d_rnn, F32)])
    f_gates = functools.partial(_f_gates, nb=rnn_blocks)
    gate_full = [wa3, ba, wx3, bx, lam]
    a_f, b_f, a_b, b_b, g_gm_out = _rowwise("gates", f_gates, t_all, tm, [_t(xc)], gate_full, [(d_rnn, F32)] * 4,
                                            comm=_gather_comm([gm_w_out[0].astype(BF16)]))

    def to3(a):
        return a.reshape(a.shape[0], lr, LANES)

    nc_scan = n_ctx // SCAN_BLOCK
    h_f, hp_f = _scan_fwd("scan_f", to3(a_f), to3(b_f), nc_scan, False)
    h_b, hp_b = _scan_fwd("scan_b", to3(a_b), to3(b_b), nc_scan, True)
    h_f2, h_b2 = h_f.reshape(t_all, d_rnn), h_b.reshape(t_all, d_rnn)
    rnn_tiled = [_t(h_f2, n_ctx), _t(h_b2, n_ctx), _t(proj, n_ctx, 2, d_rnn)]
    (rnn_o,) = _rowwise("rnn_out", _f_rnnout, seq, tm, rnn_tiled, [], [(d_rnn, BF16)])
    ar = jnp.concatenate([attn_o, rnn_o], axis=1)
    o0 = _matmul("ar_out", ar, w_out)
    mid0_args = [row(norm_full, 0, 1), row(ml, 0, 2), row(norm_full, 0, 2), row(ml, 0, 3), row(ml, 0, 4)]
    x1, h2_0 = _rowwise("mid0", _f_mid, seq, tm, [_t(x2), _t(o0)], mid0_args, [(d, F32), (d, BF16)])
    act0, (g_ff_out0,) = _matmul("ff_in_0", h2_0, w1[0], outs=(BF16,), epilogue=relu2, comm=_gather_comm([ff_out_shard(0)]))
    w2[0] = g_ff_out0.reshape(d_ff, d)
    w_gi, w_go = cols_full(g_gm_in), g_gm_out.reshape(d_gm, d)
    m0, (g_ff_in1,) = _matmul("ff_out_0", act0, w2[0], tm=MM_TILE_M // 2, tk=2 * MM_TILE_K, comm=_gather_comm([ff_in_shard(1)]))
    w1[1] = cols_full(g_ff_in1)
    post0_args = [row(norm_full, 0, 3), row(ml, 0, 5)]
    (x2l,) = _rowwise("post0", _f_post, seq, tm, [_t(x1), _t(m0)], post0_args, [(d, F32)])

    pre1_args = [row(norm_full, 1, 0), row(ml, 1, 0), row(ml, 1, 1)]
    (h1,) = _rowwise("pre1", _f_pre, seq, tm, [_t(x2l)], pre1_args, [(d, BF16)])
    zg = _matmul("gm_in", h1, w_gi)
    f_gm = functools.partial(_f_gm, n_chunks=tmb // CHUNK, groups=gm_groups)
    gm_full = [gm_b_in_f[:, :d_gm], gm_b_in_f[:, d_gm:], gm_vg, gm_vb, w_sp, bsp_t, expand]
    gm_tiled = [_t(zg, 0, 0, d_gm), _t(zg, 0, 1, d_gm)]
    (gmix,) = _rowwise("gm_mix", f_gm, seq, tmb, gm_tiled, gm_full, [(d_gm, BF16)])
    o1 = _matmul("gm_out", gmix, w_go)
    mid1_args = [row(norm_full, 1, 1), row(ml, 1, 2), row(norm_full, 1, 2), row(ml, 1, 3), row(ml, 1, 4)]
    x3, h2_1 = _rowwise("mid1", _f_mid, seq, tm, [_t(x2l), _t(o1)], mid1_args, [(d, F32), (d, BF16)])
    act1, (g_ff_out1,) = _matmul("ff_in_1", h2_1, w1[1], outs=(BF16,), epilogue=relu2,
                                       comm=_gather_comm([ff_out_shard(1)]))
    w2[1] = g_ff_out1.reshape(d_ff, d)
    m1 = _matmul("ff_out_1", act1, w2[1], tm=MM_TILE_M // 2, tk=2 * MM_TILE_K)
    post1_args = [row(norm_full, 1, 3), row(ml, 1, 5)]

    def f_loss(pid, xv, ov, tv, g, gate):
        err = _f_post(pid, xv, ov, g, gate)[0] - tv
        part = 0.5 * jnp.sum(err * err) / d
        return (err / d, jnp.full((8, LANES), part, F32))

    dy, loss_acc = _rowwise("loss", f_loss, seq, tm, [_t(x3), _t(m1), _t(tgt)], post1_args, [(d, F32)], [(8, LANES)])
    loss = lax.psum(loss_acc[0, 0], ("x", "y", "c"))

    d_norm = [[None] * 4 for _ in range(n_layers)]
    d_ml = [[None] * 6 for _ in range(n_layers)]
    recv = {}

    def cols_blocks(g):
        return jnp.moveaxis(g.reshape(g.shape[0], N_DEV, g.shape[1] // N_DEV), 1, 0)

    def rows_blocks(g):
        return g.reshape(N_DEV, g.shape[0] // N_DEV, g.shape[1])

    chip_sums = {}

    def chip_add(key, blocks, theirs):
        chip_sums[key] = _chip_add(key + "_add", blocks, theirs, ci)

    def mlp_bwd(i, dm, act, h2, first_comm=None):
        dw2 = _matmul(f"ff_out_dw_{i}", act, dm, ta=True, outs=(BF16,), comm=first_comm)
        dw2, carried = dw2 if first_comm is not None else (dw2, ())
        blk2 = rows_blocks(dw2)
        dz, (theirs,) = _matmul(f"ff_out_dx_{i}", dm, w2[i], tb=True, outs=(BF16,), extras=(act,), epilogue=relu2_bwd,
                                comm=_swap_comm([blk2]))
        chip_add(f"ff_out_{i}", blk2, theirs)
        blk1, (recv[f"ff_out_{i}"],) = _matmul(f"ff_in_dw_{i}", h2, dz, ta=True, outs=(BF16,), col_blocks=N_DEV,
                                              comm=_chips_comm([chip_sums[f"ff_out_{i}"]]))
        dh2, (theirs,) = _matmul(f"ff_in_dx_{i}", dz, w1[i], tb=True, tm=MM_TILE_M // 2, tk=2 * MM_TILE_K,
                                 comm=_swap_comm([blk1]))
        chip_add(f"ff_in_{i}", blk1, theirs)
        return dh2, carried

    def post_bwd(i, xin, m, args, dout):
        res = _rowwise(f"post_bwd{i}", _bwd_of(_f_post, 2, 1, (0, 1, 2, 3)), seq, tmb, [_t(xin), _t(m), _t(dout)], args,
                       [(d, F32), (d, BF16)], [(1, d), (1, d)])
        d_norm[i][3], d_ml[i][5] = res[2], res[3]
        return res[0], res[1]

    def mid_bwd(i, xin, o, args, dx1, dh2):
        res = _rowwise(f"mid_bwd{i}", _bwd_of(_f_mid, 2, 2, (0, 1, 2, 3, 4, 5, 6)), seq, tmb,
                       [_t(xin), _t(o), _t(dx1), _t(dh2)], args, [(d, F32), (d, BF16)], [(1, d)] * 5)
        d_norm[i][1], d_ml[i][2], d_norm[i][2], d_ml[i][3], d_ml[i][4] = res[2:]
        return res[0], res[1]

    dx3, dm1 = post_bwd(1, x3, m1, post1_args, dy)
    dh2_1, _ = mlp_bwd(1, dm1, act1, h2_1)
    dx2a, do1 = mid_bwd(1, x2l, o1, mid1_args, dx3, dh2_1)
    blk_go = rows_blocks(_matmul("gm_out_dw", gmix, do1, ta=True, outs=(BF16,)))
    dgmix, (theirs,) = _matmul("gm_out_dx", do1, w_go, tb=True, comm=_swap_comm([blk_go]))
    chip_add("gm_out", blk_go, theirs)
    gm_res = _rowwise("gm_mix_bwd", _bwd_of(f_gm, 2, 1, (0, 1, 2, 3, 4, 5, 6, 7)), seq, tmb,
                      gm_tiled + [_t(dgmix)], gm_full, [(d_gm, BF16), (d_gm, BF16)],
                      [(1, d_gm)] * 4 + [w_sp.shape, bsp_t.shape])
    dzg = jnp.concatenate([gm_res[0], gm_res[1]], axis=1)
    g_gm_b_in = jnp.concatenate([gm_res[2], gm_res[3]], axis=1)
    g_gm_vg, g_gm_vb, g_w_sp = gm_res[4], gm_res[5], gm_res[6]
    g_b_sp = gm_res[7][:, :gm_groups].T
    dh1, (recv["gm_out"],) = _matmul("gm_in_dx", dzg, w_gi, tb=True, comm=_chips_comm([chip_sums["gm_out"]]))
    blk_gi = _matmul("gm_in_dw", h1, dzg, ta=True, outs=(BF16,), col_blocks=N_DEV)

    def f_pre_bwd(pid, xv, dh, dxa, g, sh, sc):
        dxv, dg, dsh, dsc = _bwd_of(_f_pre, 1, 1, (0, 1, 2, 3))(pid, xv, dh, g, sh, sc)
        return (dxv + dxa, dg, dsh, dsc)

    res = _rowwise("pre_bwd1", f_pre_bwd, seq, tmb, [_t(x2l), _t(dh1), _t(dx2a)], pre1_args, [(d, F32)], [(1, d)] * 3)
    dx2l = res[0]
    d_norm[1][0], d_ml[1][0], d_ml[1][1] = res[1:]

    dx1, dm0 = post_bwd(0, x1, m0, post0_args, dx2l)
    dh2_0, (theirs,) = mlp_bwd(0, dm0, act0, h2_0, first_comm=_swap_comm([blk_gi]))
    chip_add("gm_in", blk_gi, theirs)
    dxa, do0 = mid_bwd(0, x2, o0, mid0_args, dx1, dh2_0)
    blk_out = rows_blocks(_matmul("ar_out_dw", ar, do0, ta=True, outs=(BF16,)))
    d_ar, (theirs,) = _matmul("ar_out_dx", do0, w_out, tb=True, comm=_swap_comm([blk_out]))
    chip_add("ar_out", blk_out, theirs)

    late = ["ff_in_1", "gm_in", "ff_in_0"]
    dq, dkt, dvt, *landed = _attn_bwd(q_r, kv_r, d_ar, nct, tm, comm=_chips_comm([chip_sums[k] for k in late]))
    recv.update(zip(late, landed))
    dkv_all = jnp.concatenate([dkt, dvt], axis=0).T

    def f_qkv_bwd(pid, pq, pkv, cos_t, sin_t, dq_t, dkv_t, *fulls):
        dq_t = jnp.where(pid < nctb, 0.0, dq_t)
        return _bwd_of(f_qkv, 4, 2, (0, 1, 4, 5))(pid, pq, pkv, cos_t, sin_t, dq_t, dkv_t, *fulls)

    qkv_res = _rowwise("qkv_bwd", f_qkv_bwd, t_all, tmb, qkv_tiled + [_t(dq, -n_ctx), _t(dkv_all)],
                       [q_g, k_g, perm], [(attn_w, BF16), (2 * kv_w, BF16)], [q_g.shape, k_g.shape])
    dproj_q, dproj_kv, g_q_g, g_k_g = qkv_res

    rnn_res = _rowwise("rnn_out_bwd", _bwd_of(_f_rnnout, 3, 1, (0, 2)), seq, tmb, rnn_tiled + [_t(d_ar, 0, 1, d_rnn)], [],
                       [(d_rnn, F32), (d_rnn, BF16)])
    zc = jnp.zeros((n_ctx, d_rnn), F32)
    dh_all = to3(jnp.concatenate([zc, rnn_res[0]], axis=0))
    dproj_g = jnp.concatenate([zc.astype(BF16), rnn_res[1]], axis=0)
    da_f, db_f = _scan_bwd("scan_f_bwd", to3(a_f), dh_all, hp_f, nc_scan, False)
    da_b, db_b = _scan_bwd("scan_b_bwd", to3(a_b), dh_all, hp_b, nc_scan, True)
    gate_cts = [_t(a.reshape(t_all, d_rnn)) for a in (da_f, db_f, da_b, db_b)]
    gates_res = _rowwise("gates_bwd", _bwd_of(f_gates, 1, 4, (0, 1, 2, 3, 4, 5)), t_all, min(tmb, GATES_BWD_TILE),
                         [_t(xc)] + gate_cts, gate_full, [(d_rnn, F32)], [wa3.shape, ba.shape, wx3.shape, bx.shape, lam.shape])
    dxc, g_wa, g_ba, g_wx, g_bx, g_lam = gates_res
    f_conv_b = functools.partial(_f_conv_bwd, n_ctx_tiles=nctb, n_tiles=t_all // tmb)
    conv_tiled = with_neighbours(proj, tmb, 1, d_rnn) + with_neighbours(dxc, tmb)
    dproj_x, g_conv_w, g_conv_b = _rowwise("conv_bwd", f_conv_b, t_all, tmb, conv_tiled, [conv_w],
                                           [(d_rnn, BF16)], [conv_w.shape, (1, d_rnn)])
    dproj = jnp.concatenate([dproj_q, dproj_x, dproj_g, dproj_kv], axis=1)
    dh0, (recv["ar_out"],) = _matmul("ar_in_dx", dproj, w_in, tb=True, tm=tm_tok, comm=_chips_comm([chip_sums["ar_out"]]))
    sq_names = ['ar_wa', 'ar_wx', 'gm_w_sp']

    def stack_sq(parts):
        return jnp.concatenate([p.reshape(-1, LANES) for p in parts], axis=0)

    sq_pack = stack_sq([g_wa, g_wx, g_w_sp]).astype(BF16)
    g_w_in, (g_sq,) = _matmul("ar_in_dw", h0, dproj, ta=True, outs=(BF16,), comm=_gather_comm([sq_pack]))
    g_w_in = jnp.concatenate([g_w_in[:, :attn_w], g_w_in[:, 3 * attn_w:], g_w_in[:, attn_w:3 * attn_w]], axis=1)
    blk_in = cols_blocks(g_w_in)

    f_pre0b = functools.partial(_f_pre_ctx, n_ctx_tiles=nctb)

    def f_pre0_bwd(pid, xcv, xlv, dh, dxp, g, shc, scc, shl, scl):
        grads = _bwd_of(f_pre0b, 2, 1, (1, 2, 3, 4, 5, 6))(pid, xcv, xlv, dh, g, shc, scc, shl, scl)
        return (grads[0] + dxp,) + tuple(grads[1:])

    res = _rowwise("pre_bwd0", f_pre0_bwd, seq, tmb, tokens + [_t(dh0), _t(dxa, -n_ctx)], pre0_args, [(d, F32)], [(1, d)] * 5,
                   comm=_swap_comm([blk_in]), skip_rows=n_ctx)
    grad_x = res[0][None]
    d_norm[0][0], d_mc_shift, d_mc_scale, d_ml[0][0], d_ml[0][1] = res[1:6]
    chip_add("ar_in", blk_in, res[6])

    z1d = jnp.zeros((1, d), F32)
    dml = jnp.concatenate([jnp.concatenate(r, axis=0)[None] for r in d_ml], axis=0)
    dmc = jnp.concatenate([jnp.concatenate([d_mc_shift, d_mc_scale] + [z1d] * 4, axis=0)[None],
                           jnp.zeros((n_layers - 1, 6, d), F32)], axis=0)
    g_norm = jnp.concatenate([jnp.concatenate(r, axis=0)[None] for r in d_norm], axis=0)
    small_parts = [dmc, g_norm, g_q_g, g_k_g, g_conv_w, g_conv_b, g_ba, g_bx, g_lam, g_gm_b_in, g_gm_vg, g_gm_vb, g_b_sp]
    small2, off2 = _pack([dml] + small_parts)
    (gs2,) = _run_comm("gather_small_grads", _gather_comm([small2]))
    dml_all = gs2.reshape(N_DEV, -1)[:, :off2[1]].reshape(N_DEV, n_layers, 6 * d)
    summed = _sum_lead("sum_small_grads", gs2).reshape(-1)
    summed_sq = _sum_lead("sum_square_grads", g_sq)

    def seg2(k, shape):
        return summed[off2[k + 1]:off2[k + 2]].reshape(shape)

    dmc_sum = seg2(0, (n_layers, 6 * d))
    dmod_rows = jnp.concatenate([jnp.moveaxis(dml_all, 0, 1), dmc_sum[:, None, :],
                                 jnp.zeros((n_layers, MOD_ROWS - N_DEV - 1, 6 * d), F32)], axis=1)
    g_b_mod = _sum_lead("sum_b_mod", jnp.moveaxis(dmod_rows, 1, 0).reshape(MOD_ROWS, n_layers * 6 * d // LANES, LANES))
    g_b_mod = g_b_mod.reshape(n_layers, 6 * d)
    g_w_mod, ds16 = _mod_bwd(c16, w_mod, _my_cols(dmod_rows, me, n6))
    (g_ds,) = _run_comm("gather_dctx", _gather_comm([ds16[N_DEV].reshape(d // LANES, LANES)]))
    ds_ctx = _sum_lead("sum_dctx", g_ds)
    (g_c_ctx,) = _rowwise("silu_bwd", _f_silu_mul, d // LANES, d // LANES, [_t(c_ctx.reshape(d // LANES, LANES)), _t(ds_ctx)],
                          [], [(LANES, F32)])
    g_c_ctx = g_c_ctx.reshape(d)

    grads = {
        'c_ctx': g_c_ctx, 'b_mod': g_b_mod,
        'norm_g': _my_cols(seg2(1, (n_layers, 4, d)), me, d // N_DEV),
        'ar_q_g': seg2(2, ar_q_g.shape), 'ar_k_g': seg2(3, ar_k_g.shape),
        'ar_conv_w': _my_cols(seg2(4, (1, CONV_W, d_rnn)), me, d_rnn // N_DEV),
        'ar_conv_b': seg2(5, ar_conv_b.shape),
        'ar_ba': _my_cols(seg2(6, (1, 2, d_rnn)), me, d_rnn // N_DEV),
        'ar_bx': _my_cols(seg2(7, (1, 2, d_rnn)), me, d_rnn // N_DEV),
        'ar_lambda': _my_cols(seg2(8, (1, 2, d_rnn)), me, d_rnn // N_DEV),
        'gm_b_in': _my_cols(seg2(9, (1, 2 * d_gm)), me, 2 * d_gm // N_DEV),
        'gm_v_g': _my_cols(seg2(10, (1, d_gm)), me, d_gm // N_DEV),
        'gm_v_b': _my_cols(seg2(11, (1, d_gm)), me, d_gm // N_DEV),
        'gm_b_sp': seg2(12, gm_b_sp.shape),
    }
    small_names = list(grads)
    deltas, new_m, new_v = {}, {}, {}

    sq_res = (summed_sq,) + tuple(_adam_f32("adam_square", summed_sq, *[stack_sq([src[n] for n in sq_names])
                                                                        for src in (wts, mom1, mom2)]))
    first = 0
    for n in sq_names:
        rows_n = wts[n].size // LANES
        for dst, slab in zip((grads, deltas, new_m, new_v), sq_res):
            dst[n] = slab[first:first + rows_n].reshape(wts[n].shape)
        first += rows_n

    wp, offw = _pack([wts[n] for n in small_names])
    mp, _ = _pack([mom1[n] for n in small_names])
    vp, _ = _pack([mom2[n] for n in small_names])
    gp, _ = _pack([grads[n] for n in small_names])
    dp, mp2, vp2, recv["ar_in"] = _adam_f32("adam_small", gp, wp, mp, vp, comm=_chips_comm([chip_sums["ar_in"]]))
    for k, n in enumerate(small_names):
        for dst, slab in ((deltas, dp), (new_m, mp2), (new_v, vp2)):
            dst[n] = slab.reshape(-1)[offw[k]:offw[k + 1]].reshape(wts[n].shape)

    grads['w_mod'] = g_w_mod
    dw, mw, vw = _adam_f32("adam_w_mod", g_w_mod.reshape(n_layers * d, n6), w_mod.reshape(n_layers * d, n6),
                           m_w_mod.reshape(n_layers * d, n6), v_w_mod.reshape(n_layers * d, n6))
    deltas['w_mod'], new_m['w_mod'], new_v['w_mod'] = (a.reshape(w_mod.shape) for a in (dw, mw, vw))

    received = {
        'w_ff_in': [recv[f"ff_in_{i}"] for i in range(n_layers)], 'w_ff_out': [recv[f"ff_out_{i}"] for i in range(n_layers)],
        'ar_w_in': [recv["ar_in"]], 'ar_w_out': [recv["ar_out"]], 'gm_w_in': [recv["gm_in"]], 'gm_w_out': [recv["gm_out"]]}
    for n, r in received.items():
        shp = wts[n].shape
        flat = (shp[0] * shp[1], shp[2])
        res = _adam_recv("adam_" + n, r, wts[n].reshape(flat), mom1[n].reshape(flat), mom2[n].reshape(flat))
        grads[n], deltas[n], new_m[n], new_v[n] = (a.reshape(shp) for a in res)

    return (loss, grad_x, *[grads[n] for n in WEIGHTS], *[deltas[n] for n in WEIGHTS],
            *[new_m[n] for n in WEIGHTS], *[new_v[n] for n in WEIGHTS])
```

```python
import functools

import numpy as np
import jax
import jax.numpy as jnp
from jax import lax
from jax.experimental import pallas as pl
from jax.experimental.pallas import tpu as pltpu

F32 = jnp.float32
BF16 = jnp.bfloat16
HIGHEST = lax.Precision.HIGHEST
LOG2_E = 1.4426950408889634
LN_2 = 0.6931471805599453

GRID_W = 64
N_HEADS = 8
N_KV_HEADS = 2
HEAD_DIM = 128
ROPE_THETA = 10000.0
RNN_BLOCK_W = 128
CONV_W = 4
RG_C = 8.0
GM_GROUP_W = 128
CHUNK = 128
EPS = 1e-6
ADAM_LR = 0.001
ADAM_B1 = 0.9
ADAM_B2 = 0.999
ADAM_EPS = 1e-08
ADAM_WD = 0.01
ADAM_STEP = 10

N_DEV = 8
MOD_ROWS = 16
LANES = 128
ROW_TILE = 256
ROW_TILE_BWD = 256
ATTN_BWD_HEADS_PER_STEP = 4
ATTN_HEADS_PER_STEP = 4
GATES_BWD_TILE = 128
SCAN_BLOCK = 256
VMEM_LIMIT = 56 * 1024 * 1024
PACK_ROWS = 512
MM_TILE_M = 1024
MM_TILE_N = 1024
MM_TILE_K = 2048

WEIGHTS = ['c_ctx', 'w_mod', 'b_mod', 'norm_g', 'w_ff_in', 'w_ff_out', 'ar_w_in', 'ar_q_g', 'ar_k_g', 'ar_conv_w',
           'ar_conv_b', 'ar_wa', 'ar_ba', 'ar_wx', 'ar_bx', 'ar_lambda', 'ar_w_out', 'gm_w_in', 'gm_b_in', 'gm_v_g',
           'gm_v_b', 'gm_w_sp', 'gm_b_sp', 'gm_w_out']


def _sds(shape, dtype):
    return jax.ShapeDtypeStruct(tuple(shape), dtype)


def _tile(dim, pref, align):
    t = (min(pref, dim) // align) * align
    while t >= align:
        if dim % t == 0:
            return t
        t -= align
    return dim


def _params(sem):
    return pltpu.CompilerParams(dimension_semantics=sem, vmem_limit_bytes=VMEM_LIMIT)


def _rms(x, g):
    return x * lax.rsqrt(jnp.mean(x * x, axis=-1, keepdims=True) + EPS) * g


def _gelu(x):
    return 0.5 * x * (1.0 + jnp.tanh(0.7978845608028654 * (x + 0.044715 * (x * x * x))))


def _sigmoid(x):
    return 0.5 * (jnp.tanh(0.5 * x) + 1.0)


def _log1p_pos(u):
    small = u < 1e-3
    us = jnp.where(small, u, 0.0)
    return jnp.where(small, us * (1.0 - us * (0.5 - us * (1.0 / 3.0))), jnp.log(1.0 + u))


def _softplus(x):
    return jnp.maximum(x, 0.0) + _log1p_pos(jnp.exp(-jnp.abs(x)))


def _f_pre_ctx(pid, xc, xl, g, shc, scc, shl, scl, *, n_ctx_tiles):
    is_ctx = pid < n_ctx_tiles
    x = jnp.where(is_ctx, xc, xl)
    sh = jnp.where(is_ctx, shc, shl)
    sc = jnp.where(is_ctx, scc, scl)
    return (_rms(x, g) * (1.0 + sc) + sh,)


def _f_pre(pid, x, g, sh, sc):
    return (_rms(x, g) * (1.0 + sc) + sh,)


def _f_mid(pid, x, o, g1, gate, g2, sh, sc):
    x1 = x + gate * _rms(o, g1)
    return (x1, _rms(x1, g2) * (1.0 + sc) + sh)


def _f_post(pid, x, o, g, gate):
    return (x + gate * _rms(o, g),)


def _f_qkv(pid, pq, pkv, cos, sin, q_g, k_g, perm, *, nh, nkv):
    hd = HEAD_DIM

    def norm_rope(xh, g):
        y = _rms(xh, g)
        return y * cos + jnp.dot(y, perm, precision=HIGHEST, preferred_element_type=F32) * sin

    qs = [norm_rope(pq[:, h * hd:(h + 1) * hd], q_g) * (HEAD_DIM ** -0.5 * LOG2_E) for h in range(nh)]
    ks = [norm_rope(pkv[:, h * hd:(h + 1) * hd], k_g) for h in range(nkv)]
    return (jnp.concatenate(qs, axis=1), jnp.concatenate(ks + [pkv[:, nkv * hd:]], axis=1))


def _f_gates(pid, x, wa, ba, wx, bx, lam, *, nb):
    w = RNN_BLOCK_W
    outs = []
    for d in range(2):
        ra, ri = [], []
        for n in range(nb):
            xn = x[:, n * w:(n + 1) * w].astype(BF16)
            ra.append(jnp.dot(xn, wa[d * nb + n].astype(BF16), preferred_element_type=F32))
            ri.append(jnp.dot(xn, wx[d * nb + n].astype(BF16), preferred_element_type=F32))
        r = _sigmoid(jnp.concatenate(ra, axis=1) + ba[d:d + 1])
        i = _sigmoid(jnp.concatenate(ri, axis=1) + bx[d:d + 1])
        log_a = -RG_C * r * _softplus(-lam[d:d + 1])
        a = jnp.exp(log_a)
        outs.append(a)
        outs.append(jnp.sqrt(-jnp.tanh(log_a) * (a * a + 1.0)) * (i * x))
    return tuple(outs)


def _f_rnnout(pid, hf, hb, gr):
    return ((hf + hb) * _gelu(gr),)


def _f_gm(pid, zu, zv, bu, bv, v_g, v_b, w_sp, bsp_t, expand, *, n_chunks, groups):
    u = _gelu(zu + bu)
    v = _gelu(zv + bv)
    mu = jnp.mean(v, axis=-1, keepdims=True)
    vc = v - mu
    v = vc * lax.rsqrt(jnp.mean(vc * vc, axis=-1, keepdims=True) + EPS) * v_g + v_b
    bias = jnp.dot(bsp_t, expand, precision=HIGHEST, preferred_element_type=F32)
    outs = []
    for c in range(n_chunks):
        vch = v[c * CHUNK:(c + 1) * CHUNK]
        cols = [jnp.dot(w_sp[g].astype(BF16), vch[:, g * GM_GROUP_W:(g + 1) * GM_GROUP_W].astype(BF16),
                        preferred_element_type=F32) for g in range(groups)]
        outs.append(u[c * CHUNK:(c + 1) * CHUNK] * (jnp.concatenate(cols, axis=1) + bias))
    return (jnp.concatenate(outs, axis=0),)


def _f_silu_mul(pid, c, d):
    return (d * jax.grad(lambda z: jnp.sum(z * _sigmoid(z)))(c),)


def _bwd_of(fn, n_tiled, n_ct, want):
    def bwd(pid, *args):
        tiles = [t.astype(F32) for t in args[:n_tiled]]
        cts = args[n_tiled:n_tiled + n_ct]
        fulls = list(args[n_tiled + n_ct:])
        outs, vjp = jax.vjp(lambda *a: fn(pid, *a), *tiles, *fulls)
        grads = vjp(tuple(ct.astype(o.dtype) for ct, o in zip(cts, outs)))
        return tuple(grads[i] for i in want)
    return bwd


def _rowwise(name, fn, rows, tm, tiled, full, outs, accs=(), comm=None, skip_rows=0):
    n_t, n_f, n_o, n_a = len(tiled), len(full), len(outs), len(accs)
    assert skip_rows % tm == 0
    skip = skip_rows // tm
    n_tiles = rows // tm + skip

    def body(*refs):
        in_refs, c_ins, res_refs, c_outs, _, c_sems = _split_refs(refs, n_t + n_f, n_o + n_a, 0, comm)
        pid = pl.program_id(0)
        if comm is not None:
            @pl.when(pid == 0)
            def _():
                comm.start(c_ins, c_outs, *c_sems)

        res = fn(pid, *[r[...] for r in in_refs])
        o_refs, a_refs = res_refs[:n_o], res_refs[n_o:]
        for r, v in zip(o_refs, res[:n_o]):
            r[...] = v.astype(r.dtype)
        if n_a:
            @pl.when(pid == 0)
            def _():
                for r in a_refs:
                    r[...] = jnp.zeros_like(r)
            for r, v in zip(a_refs, res[n_o:]):
                r[...] += v.astype(F32)
        if comm is not None:
            @pl.when(pid == n_tiles - 1)
            def _():
                comm.finish(c_ins, c_outs, *c_sems)

    assert all(ro % tm == 0 for (_, ro, _, _) in tiled)
    in_specs = [pl.BlockSpec((tm, w), lambda i, ro=ro // tm, cb=cb, last=a.shape[0] // tm - 1: (jnp.clip(i + ro, 0, last), cb))
                for (a, ro, cb, w) in tiled]
    in_specs += [pl.BlockSpec(a.shape, lambda i, nd=a.ndim: (0,) * nd) for a in full]
    out_shape = [_sds((rows, w), dt) for (w, dt) in outs] + [_sds(s, F32) for s in accs]
    out_specs = [pl.BlockSpec((tm, w), lambda i: (jnp.maximum(i - skip, 0), 0)) for (w, _) in outs]
    out_specs += [pl.BlockSpec(tuple(s), lambda i, nd=len(s): (0,) * nd) for s in accs]
    c_in, c_out, c_scr = (comm.ins, comm.out_shapes, comm.scratch()) if comm is not None else ([], [], [])
    return pl.pallas_call(body, grid=(n_tiles,), in_specs=in_specs + [_ANY] * len(c_in), out_specs=out_specs + [_ANY] * len(c_out),
                          out_shape=out_shape + list(c_out), scratch_shapes=c_scr, name=name,
                          compiler_params=_params(("arbitrary",)))(*[t[0] for t in tiled], *full, *c_in)


def _t(a, row_off=0, col_blk=0, width=None):
    return (a, row_off, col_blk, a.shape[1] if width is None else width)


class _Comm:
    def __init__(self, ins, out_shapes, n_sems, start, finish):
        self.ins, self.out_shapes, self.n_sems, self.start, self.finish = list(ins), list(out_shapes), n_sems, start, finish

    def scratch(self):
        return [pltpu.SemaphoreType.DMA((self.n_sems,)), pltpu.SemaphoreType.DMA((self.n_sems,)),
                pltpu.SemaphoreType.DMA((len(self.ins),))]


_ANY = pl.BlockSpec(memory_space=pl.ANY)


class _SemSlice:
    def __init__(self, ref, first):
        self.ref, self.first = ref, first

    @property
    def at(self):
        return self

    def __getitem__(self, k):
        return self.ref.at[self.first + k]


def _both(*comms):
    comms = [cm for cm in comms if cm is not None]
    if len(comms) <= 1:
        return comms[0] if comms else None

    def parts(ins, outs, send, recv, local):
        i0 = o0 = s0 = 0
        for cm in comms:
            ni, no = len(cm.ins), len(cm.out_shapes)
            yield cm, (ins[i0:i0 + ni], outs[o0:o0 + no], _SemSlice(send, s0), _SemSlice(recv, s0), _SemSlice(local, i0))
            i0, o0, s0 = i0 + ni, o0 + no, s0 + cm.n_sems

    def start(*refs):
        for cm, sub in parts(*refs):
            cm.start(*sub)

    def finish(*refs):
        for cm, sub in parts(*refs):
            cm.finish(*sub)

    return _Comm(sum((cm.ins for cm in comms), []), sum((cm.out_shapes for cm in comms), []),
                 sum(cm.n_sems for cm in comms), start, finish)


def _row_block(cols, itemsize):
    return max(16, (1 << 20) // (cols * itemsize))


def _split_refs(refs, n_in, n_out, n_scratch, comm):
    ci, co, cs = (len(comm.ins), len(comm.out_shapes), 3) if comm is not None else (0, 0, 0)
    cuts = np.cumsum([0, n_in, ci, n_out, co, n_scratch, cs])
    return [refs[cuts[i]:cuts[i + 1]] for i in range(6)]


def _matmul(name, a, b, *, ta=False, tb=False, outs=((F32,)), epilogue=None, extras=(), tm=None, tn=None, tk=None, comm=None,
            col_blocks=None):
    m, k = (a.shape[1], a.shape[0]) if ta else a.shape
    n = b.shape[0] if tb else b.shape[1]
    tm = _tile(m, tm or MM_TILE_M, 128 if ta else 16)
    tn = _tile(n if col_blocks is None else n // col_blocks, tn or MM_TILE_N, 128)
    tk = _tile(k, tk or MM_TILE_K, 128 if not ta else 16)
    ni, nj, nk = m // tm, n // tn, k // tk
    n_e, n_o = len(extras), len(outs)
    dims = (((0 if ta else 1,), (1 if tb else 0,)), ((), ()))

    def body(*refs):
        ins, c_ins, o_refs, c_outs, scratch, c_sems = _split_refs(refs, 2 + n_e, n_o, 1 if nk > 1 else 0, comm)
        a_ref, b_ref, e_refs = ins[0], ins[1], ins[2:]
        i, j, kk = pl.program_id(0), pl.program_id(1), pl.program_id(2)
        if comm is not None:
            @pl.when(jnp.logical_and(jnp.logical_and(i == 0, j == 0), kk == 0))
            def _():
                comm.start(c_ins, c_outs, *c_sems)

        def finish(acc):
            res = (acc,) if epilogue is None else epilogue(acc, *[e[...] for e in e_refs])
            for r, v in zip(o_refs, res):
                r[...] = v.astype(r.dtype)

        prod = lax.dot_general(a_ref[...].astype(BF16), b_ref[...].astype(BF16), dims, preferred_element_type=F32)
        if nk == 1:
            finish(prod)
        else:
            acc = scratch[0]

            @pl.when(kk == 0)
            def _():
                acc[...] = prod

            @pl.when(kk > 0)
            def _():
                acc[...] += prod

            @pl.when(kk == nk - 1)
            def _():
                finish(acc[...])
        if comm is not None:
            @pl.when(jnp.logical_and(jnp.logical_and(i == ni - 1, j == nj - 1), kk == nk - 1))
            def _():
                comm.finish(c_ins, c_outs, *c_sems)

    a_spec = pl.BlockSpec((tk, tm), lambda i, j, kk: (kk, i)) if ta else pl.BlockSpec((tm, tk), lambda i, j, kk: (i, kk))
    b_spec = pl.BlockSpec((tn, tk), lambda i, j, kk: (j, kk)) if tb else pl.BlockSpec((tk, tn), lambda i, j, kk: (kk, j))
    mn_spec = pl.BlockSpec((tm, tn), lambda i, j, kk: (i, j))
    c_in, c_out, c_scr = (comm.ins, comm.out_shapes, comm.scratch()) if comm is not None else ([], [], [])
    if col_blocks is None:
        o_spec, o_shape = mn_spec, (m, n)
    else:
        per = n // col_blocks // tn
        o_spec = pl.BlockSpec((None, tm, tn), lambda i, j, kk: (j // per, i, j % per))
        o_shape = (col_blocks, m, n // col_blocks)
    res = pl.pallas_call(body, grid=(ni, nj, nk), in_specs=[a_spec, b_spec] + [mn_spec] * n_e + [_ANY] * len(c_in),
                         out_specs=[o_spec] * n_o + [_ANY] * len(c_out),
                         out_shape=[_sds(o_shape, dt) for dt in outs] + list(c_out),
                         scratch_shapes=([pltpu.VMEM((tm, tn), F32)] if nk > 1 else []) + c_scr, name=name,
                         compiler_params=_params(("arbitrary", "arbitrary", "arbitrary")))(a, b, *extras, *c_in)
    main = res[0] if n_o == 1 else res[:n_o]
    return main if comm is None else (main, res[n_o:])


def _attn_fwd(q, kv, n_ctx_tiles, tq, comm=None):
    t_all = q.shape[0]
    s_len = t_all - n_ctx_tiles * tq
    hd, groups = HEAD_DIM, N_HEADS // N_KV_HEADS
    hps = ATTN_HEADS_PER_STEP
    assert groups % hps == 0
    gsteps = groups // hps
    nq = s_len // tq

    def body(*refs):
        (q_ref, k_ref, v_ref), c_ins, (o_ref,), c_outs, _, c_sems = _split_refs(refs, 3, 1, 0, comm)
        kh, g, i = pl.program_id(0), pl.program_id(1), pl.program_id(2)
        if comm is not None:
            @pl.when(jnp.logical_and(jnp.logical_and(kh == 0, g == 0), i == 0))
            def _():
                comm.start(c_ins, c_outs, *c_sems)

        kk, vv = k_ref[...], v_ref[...]
        for h in range(hps):
            qh = q_ref[:, h * hd:(h + 1) * hd]
            s = lax.dot_general(qh, kk, (((1,), (1,)), ((), ())), preferred_element_type=F32)
            p = jnp.exp2(s - jnp.max(s, axis=-1, keepdims=True))
            l = jnp.sum(p, axis=-1, keepdims=True)
            o = jnp.dot(p.astype(BF16), vv, preferred_element_type=F32) * (1.0 / l)
            o_ref[:, h * hd:(h + 1) * hd] = o.astype(o_ref.dtype)
        if comm is not None:
            @pl.when(jnp.logical_and(jnp.logical_and(kh == N_KV_HEADS - 1, g == gsteps - 1), i == nq - 1))
            def _():
                comm.finish(c_ins, c_outs, *c_sems)

    c_in, c_out, c_scr = (comm.ins, comm.out_shapes, comm.scratch()) if comm is not None else ([], [], [])
    res = pl.pallas_call(
        body, grid=(N_KV_HEADS, gsteps, nq),
        in_specs=[pl.BlockSpec((tq, hps * hd), lambda kh, g, i: (i + n_ctx_tiles, kh * gsteps + g)),
                  pl.BlockSpec((t_all, hd), lambda kh, g, i: (0, kh)),
                  pl.BlockSpec((t_all, hd), lambda kh, g, i: (0, N_KV_HEADS + kh))] + [_ANY] * len(c_in),
        out_specs=[pl.BlockSpec((tq, hps * hd), lambda kh, g, i: (i, kh * gsteps + g))] + [_ANY] * len(c_out),
        out_shape=[_sds((s_len, N_HEADS * hd), BF16)] + list(c_out), scratch_shapes=c_scr, name="attn_fwd",
        compiler_params=_params(("arbitrary", "arbitrary", "arbitrary")))(q, kv, kv, *c_in)
    return res[0] if comm is None else (res[0], res[1:])


def _attn_bwd(q, kv, d_ar, n_ctx_tiles, tq, comm=None):
    t_all = q.shape[0]
    s_len = t_all - n_ctx_tiles * tq
    hd, groups = HEAD_DIM, N_HEADS // N_KV_HEADS
    hps = ATTN_BWD_HEADS_PER_STEP
    assert groups % hps == 0
    gsteps = groups // hps
    nq = s_len // tq

    def body(*refs):
        (q_ref, k_ref, v_ref, do_ref), c_ins, (dq_ref, dkt_ref, dvt_ref), c_outs, _, c_sems = _split_refs(refs, 4, 3, 0, comm)
        first = jnp.logical_and(pl.program_id(1) == 0, pl.program_id(2) == 0)
        if comm is not None:
            @pl.when(jnp.logical_and(first, pl.program_id(0) == 0))
            def _():
                comm.start(c_ins, c_outs, *c_sems)

        @pl.when(first)
        def _():
            dkt_ref[...] = jnp.zeros_like(dkt_ref)
            dvt_ref[...] = jnp.zeros_like(dvt_ref)

        kk, vv = k_ref[...], v_ref[...]
        for h in range(hps):
            cols = slice(h * hd, (h + 1) * hd)
            qv = q_ref[:, cols]
            s = lax.dot_general(qv, kk, (((1,), (1,)), ((), ())), preferred_element_type=F32)
            p = jnp.exp2(s - jnp.max(s, axis=-1, keepdims=True))
            inv_l = 1.0 / jnp.sum(p, axis=-1, keepdims=True)
            do = (do_ref[:, cols] * inv_l).astype(BF16)
            dp = lax.dot_general(do, vv, (((1,), (1,)), ((), ())), preferred_element_type=F32)
            ds = (p * (dp - jnp.sum(p * dp, axis=-1, keepdims=True) * inv_l)).astype(BF16)
            dq_ref[:, cols] = jnp.dot(ds, kk, preferred_element_type=F32) * LN_2
            dkt_ref[...] += jnp.dot(qv.T, ds, preferred_element_type=F32)
            dvt_ref[...] += jnp.dot(do.T, p.astype(BF16), preferred_element_type=F32)
        last = jnp.logical_and(pl.program_id(1) == gsteps - 1, pl.program_id(2) == nq - 1)

        @pl.when(last)
        def _():
            dkt_ref[...] *= LN_2

        if comm is not None:
            @pl.when(jnp.logical_and(last, pl.program_id(0) == N_KV_HEADS - 1))
            def _():
                comm.finish(c_ins, c_outs, *c_sems)

    c_in, c_out, c_scr = (comm.ins, comm.out_shapes, comm.scratch()) if comm is not None else ([], [], [])
    return pl.pallas_call(
        body, grid=(N_KV_HEADS, gsteps, nq),
        in_specs=[pl.BlockSpec((tq, hps * hd), lambda kh, g, i: (i + n_ctx_tiles, kh * gsteps + g)),
                  pl.BlockSpec((t_all, hd), lambda kh, g, i: (0, kh)),
                  pl.BlockSpec((t_all, hd), lambda kh, g, i: (0, N_KV_HEADS + kh)),
                  pl.BlockSpec((tq, hps * hd), lambda kh, g, i: (i, kh * gsteps + g))] + [_ANY] * len(c_in),
        out_specs=[pl.BlockSpec((tq, hps * hd), lambda kh, g, i: (i, kh * gsteps + g)),
                   pl.BlockSpec((hd, t_all), lambda kh, g, i: (kh, 0)),
                   pl.BlockSpec((hd, t_all), lambda kh, g, i: (kh, 0))] + [_ANY] * len(c_out),
        out_shape=[_sds((s_len, N_HEADS * hd), F32), _sds((N_KV_HEADS * hd, t_all), F32),
                   _sds((N_KV_HEADS * hd, t_all), F32)] + list(c_out),
        scratch_shapes=c_scr, name="attn_bwd",
        compiler_params=_params(("arbitrary", "arbitrary", "arbitrary")))(q, kv, kv, d_ar, *c_in)


def _scan_order(nb, nc, reverse):
    if not reverse:
        return lambda i: i
    return lambda i: jnp.where(i < nc, nc - 1 - i, nb - 1 - (i - nc))


def _scan_fwd(name, a, b, nc, reverse):
    t_all, r, l = a.shape
    tb = SCAN_BLOCK
    nb = t_all // tb
    order = _scan_order(nb, nc, reverse)

    def body(a_ref, b_ref, h_ref, hp_ref, carry):
        @pl.when(pl.program_id(0) == 0)
        def _():
            carry[...] = jnp.zeros_like(carry)

        def step(s, h):
            t = tb - 1 - s if reverse else s
            hp_ref[t] = h
            h = a_ref[t] * h + b_ref[t]
            h_ref[t] = h
            return h

        carry[...] = lax.fori_loop(0, tb, step, carry[...], unroll=8)

    spec = pl.BlockSpec((tb, r, l), lambda i: (order(i), 0, 0))
    return pl.pallas_call(body, grid=(nb,), in_specs=[spec, spec], out_specs=[spec, spec],
                          out_shape=[_sds(a.shape, F32)] * 2, scratch_shapes=[pltpu.VMEM((r, l), F32)], name=name,
                          compiler_params=_params(("arbitrary",)))(a, b)


def _scan_bwd(name, a, dh, hp, nc, reverse):
    t_all, r, l = a.shape
    tb = SCAN_BLOCK
    nb = t_all // tb
    primal = _scan_order(nb, nc, reverse)

    def order(i):
        return primal(nb - 1 - i)

    def body(a_ref, dh_ref, hp_ref, da_ref, db_ref, carry):
        @pl.when(pl.program_id(0) == 0)
        def _():
            carry[...] = jnp.zeros_like(carry)

        def step(s, cr):
            t = s if reverse else tb - 1 - s
            lam = dh_ref[t] + cr
            db_ref[t] = lam
            da_ref[t] = lam * hp_ref[t]
            return a_ref[t] * lam

        carry[...] = lax.fori_loop(0, tb, step, carry[...], unroll=8)

    spec = pl.BlockSpec((tb, r, l), lambda i: (order(i), 0, 0))
    return pl.pallas_call(body, grid=(nb,), in_specs=[spec] * 3, out_specs=[spec, spec],
                          out_shape=[_sds(a.shape, F32)] * 2, scratch_shapes=[pltpu.VMEM((r, l), F32)], name=name,
                          compiler_params=_params(("arbitrary",)))(a, dh, hp)


def _shifted(prev, cur, nxt, k, pid, n_ctx_tiles, n_tiles):
    if k == 0:
        return cur
    tm = cur.shape[0]
    row = lax.broadcasted_iota(jnp.int32, cur.shape, 0)
    if k < 0:
        at_start = jnp.logical_or(pid == 0, pid == n_ctx_tiles)
        edge = jnp.where(at_start, 0.0, pltpu.roll(prev, -k, 0))
        return jnp.where(row < -k, edge, pltpu.roll(cur, -k, 0))
    at_end = jnp.logical_or(pid == n_ctx_tiles - 1, pid == n_tiles - 1)
    edge = jnp.where(at_end, 0.0, pltpu.roll(nxt, tm - k, 0))
    return jnp.where(row >= tm - k, edge, pltpu.roll(cur, tm - k, 0))


def _f_conv(pid, xp, xc, xn, w, b, *, n_ctx_tiles, n_tiles):
    y = b
    for j in range(CONV_W):
        y = y + _shifted(xp, xc, xn, j - CONV_W // 2, pid, n_ctx_tiles, n_tiles) * w[j:j + 1]
    return (y,)


def _f_conv_bwd(pid, xp, xc, xn, dp, dc, dn, w, *, n_ctx_tiles, n_tiles):
    dx = jnp.zeros_like(dc)
    dw = []
    for j in range(CONV_W):
        k = j - CONV_W // 2
        dx = dx + _shifted(dp, dc, dn, -k, pid, n_ctx_tiles, n_tiles) * w[j:j + 1]
        dw.append(jnp.sum(dc * _shifted(xp, xc, xn, k, pid, n_ctx_tiles, n_tiles), axis=0, keepdims=True))
    return (dx, jnp.concatenate(dw, axis=0), jnp.sum(dc, axis=0, keepdims=True))


def _mesh_pos():
    return lax.axis_index("x"), lax.axis_index("y"), lax.axis_index("c")


def _remote(src, dst, send_sems, recv_sems, k, to):
    return pltpu.make_async_remote_copy(src_ref=src, dst_ref=dst, send_sem=send_sems.at[k], recv_sem=recv_sems.at[k],
                                        device_id=to, device_id_type=pl.DeviceIdType.MESH)


def _neighbours():
    x, y, c = _mesh_pos()
    return (x, y, c), (x, y, 1 - c), [(1 - x, y), (x, 1 - y), (1 - x, 1 - y)]


def _gather_comm(arrays):
    n = len(arrays)
    per = 7

    def slot(out, blk):
        return out.at[4 * blk[0] + 2 * blk[1] + blk[2]]

    def start(ins, outs, send, recv, local):
        me, sib, chips = _neighbours()
        for ai in range(n):
            pltpu.make_async_copy(ins[ai], slot(outs[ai], me), local.at[ai]).start()
            _remote(ins[ai], slot(outs[ai], me), send, recv, ai * per, sib).start()
            for j, chip in enumerate(chips):
                _remote(ins[ai], slot(outs[ai], me), send, recv, ai * per + 1 + j, (*chip, me[2])).start()

    def finish(ins, outs, send, recv, local):
        me, sib, chips = _neighbours()
        for ai in range(n):
            for j, chip in enumerate(chips):
                blk = slot(outs[ai], (*chip, me[2]))
                _remote(blk, blk, send, recv, ai * per + 1 + j, me).wait_recv()
                _remote(blk, blk, send, recv, ai * per + 4 + j, sib).start()
        for ai in range(n):
            blk = slot(outs[ai], sib)
            _remote(blk, blk, send, recv, ai * per, me).wait_recv()
            for j, chip in enumerate(chips):
                blk = slot(outs[ai], (*chip, 1 - me[2]))
                _remote(blk, blk, send, recv, ai * per + 4 + j, me).wait_recv()
            for k in range(per):
                _remote(ins[ai], slot(outs[ai], me), send, recv, ai * per + k, sib).wait_send()
            pltpu.make_async_copy(ins[ai], slot(outs[ai], me), local.at[ai]).wait()

    return _Comm(arrays, [_sds((N_DEV,) + a.shape, a.dtype) for a in arrays], n * per, start, finish)


def _swap_comm(arrays):
    n = len(arrays)

    def start(ins, outs, send, recv, local):
        me, sib, _ = _neighbours()
        for ai in range(n):
            for q in range(4):
                _remote(ins[ai].at[2 * q + 1 - me[2]], outs[ai].at[q], send, recv, ai * 4 + q, sib).start()

    def finish(ins, outs, send, recv, local):
        me, sib, _ = _neighbours()
        for ai in range(n):
            for q in range(4):
                cp = _remote(ins[ai].at[q], outs[ai].at[q], send, recv, ai * 4 + q, sib)
                cp.wait_recv()
                cp.wait_send()

    return _Comm(arrays, [_sds((4,) + a.shape[1:], a.dtype) for a in arrays], n * 4, start, finish)


def _chips_comm(arrays):
    n = len(arrays)

    def start(ins, outs, send, recv, local):
        me, _, chips = _neighbours()
        mine = 2 * me[0] + me[1]
        for ai in range(n):
            pltpu.make_async_copy(ins[ai].at[mine], outs[ai].at[mine], local.at[ai]).start()
            for j, chip in enumerate(chips):
                _remote(ins[ai].at[2 * chip[0] + chip[1]], outs[ai].at[mine], send, recv, ai * 3 + j, (*chip, me[2])).start()

    def finish(ins, outs, send, recv, local):
        me, _, chips = _neighbours()
        mine = 2 * me[0] + me[1]
        for ai in range(n):
            for j, chip in enumerate(chips):
                theirs = 2 * chip[0] + chip[1]
                cp = _remote(ins[ai].at[theirs], outs[ai].at[theirs], send, recv, ai * 3 + j, (*chip, me[2]))
                cp.wait_recv()
                cp.wait_send()
            pltpu.make_async_copy(ins[ai].at[mine], outs[ai].at[mine], local.at[ai]).wait()

    return _Comm(arrays, [_sds(a.shape, a.dtype) for a in arrays], n * 3, start, finish)


def _run_comm(name, comm):
    n_in, n_out = len(comm.ins), len(comm.out_shapes)

    def body(*refs):
        ins, outs, sems = refs[:n_in], refs[n_in:n_in + n_out], refs[n_in + n_out:]
        comm.start(ins, outs, *sems)
        comm.finish(ins, outs, *sems)

    return pl.pallas_call(body, in_specs=[_ANY] * n_in, out_specs=[_ANY] * n_out, out_shape=comm.out_shapes, name=name,
                          scratch_shapes=comm.scratch(), compiler_params=pltpu.CompilerParams(has_side_effects=True))(*comm.ins)


def _chip_add(name, blocks, theirs, core):
    _, r, c = blocks.shape
    tr = _tile(r, _row_block(c, 2), 16)

    def body(core_ref, a_ref, b_ref, o_ref):
        o_ref[...] = (a_ref[...].astype(F32) + b_ref[...].astype(F32)).astype(o_ref.dtype)

    spec = pl.BlockSpec((None, tr, c), lambda q, i, core_ref: (q, i, 0))
    grid_spec = pltpu.PrefetchScalarGridSpec(
        num_scalar_prefetch=1, grid=(4, r // tr),
        in_specs=[pl.BlockSpec((None, tr, c), lambda q, i, core_ref: (2 * q + core_ref[0], i, 0)), spec], out_specs=spec)
    return pl.pallas_call(body, grid_spec=grid_spec, out_shape=_sds((4, r, c), blocks.dtype), name=name,
                          compiler_params=_params(("parallel", "parallel")))(jnp.reshape(core, (1,)).astype(jnp.int32), blocks, theirs)


def _sum_lead(name, a):
    n, r, c = a.shape
    tr = _tile(r, _row_block(c, 4 * n // 2), 16)

    def body(a_ref, o_ref):
        acc = a_ref[0].astype(F32)
        for j in range(1, n):
            acc = acc + a_ref[j].astype(F32)
        o_ref[...] = acc

    return pl.pallas_call(body, grid=(r // tr,), in_specs=[pl.BlockSpec((n, tr, c), lambda i: (0, i, 0))],
                          out_specs=pl.BlockSpec((tr, c), lambda i: (i, 0)), out_shape=_sds((r, c), F32), name=name,
                          compiler_params=_params(("parallel",)))(a)


def _adam_math(w, g, m, v):
    m = ADAM_B1 * m + (1.0 - ADAM_B1) * g
    v = ADAM_B2 * v + (1.0 - ADAM_B2) * (g * g)
    m_hat = m / (1.0 - ADAM_B1 ** ADAM_STEP)
    v_hat = v / (1.0 - ADAM_B2 ** ADAM_STEP)
    delta = -ADAM_LR * (m_hat / (jnp.sqrt(v_hat) + ADAM_EPS) + ADAM_WD * w)
    return delta, m, v


def _adam_recv(name, recvs, w, m, v):
    nl = len(recvs)
    n, rl, c = recvs[0].shape
    tr = _tile(rl, _row_block(c, 4), 16)
    per = rl // tr

    def body(*refs):
        g_refs = refs[:nl]
        w_ref, m_ref, v_ref, go_ref, d_ref, mo_ref, vo_ref = refs[nl:]
        for layer in range(nl):
            @pl.when(pl.program_id(0) == layer)
            def _(g_ref=g_refs[layer]):
                g = g_ref[0].astype(F32)
                for j in range(1, n):
                    g = g + g_ref[j].astype(F32)
                delta, m2, v2 = _adam_math(w_ref[...], g, m_ref[...], v_ref[...])
                go_ref[...] = g
                d_ref[...] = delta
                mo_ref[...] = m2
                vo_ref[...] = v2

    g_specs = [pl.BlockSpec((n, tr, c), lambda l, i, layer=layer: (0, jnp.where(l == layer, i, 0), 0)) for layer in range(nl)]
    spec = pl.BlockSpec((tr, c), lambda l, i: (l * per + i, 0))
    return pl.pallas_call(body, grid=(nl, per), in_specs=g_specs + [spec] * 3, out_specs=[spec] * 4,
                          out_shape=[_sds((nl * rl, c), F32)] * 4, name=name,
                          compiler_params=_params(("arbitrary", "arbitrary")))(*recvs, w, m, v)


def _adam_f32(name, g, w, m, v, comm=None):
    r, c = g.shape
    tr = _tile(r, _row_block(c, 4), 8)
    steps = r // tr

    def body(*refs):
        (g_ref, w_ref, m_ref, v_ref), c_ins, (d_ref, mo_ref, vo_ref), c_outs, _, c_sems = _split_refs(refs, 4, 3, 0, comm)
        if comm is not None:
            @pl.when(pl.program_id(0) == 0)
            def _():
                comm.start(c_ins, c_outs, *c_sems)

        delta, m2, v2 = _adam_math(w_ref[...], g_ref[...], m_ref[...], v_ref[...])
        d_ref[...] = delta
        mo_ref[...] = m2
        vo_ref[...] = v2
        if comm is not None:
            @pl.when(pl.program_id(0) == steps - 1)
            def _():
                comm.finish(c_ins, c_outs, *c_sems)

    spec = pl.BlockSpec((tr, c), lambda i: (i, 0))
    c_in, c_out, c_scr = (comm.ins, comm.out_shapes, comm.scratch()) if comm is not None else ([], [], [])
    return pl.pallas_call(body, grid=(steps,), in_specs=[spec] * 4 + [_ANY] * len(c_in), out_specs=[spec] * 3 + [_ANY] * len(c_out),
                          out_shape=[_sds((r, c), F32)] * 3 + list(c_out), scratch_shapes=c_scr, name=name,
                          compiler_params=_params(("arbitrary",)))(g, w, m, v, *c_in)


def _mod_fwd(c16, w_mod, b_loc):
    nl, d, n6 = w_mod.shape
    tn = _tile(n6, 512, 128)

    def body(c_ref, w_ref, b_ref, o_ref):
        cv = c_ref[...]
        s = cv * _sigmoid(cv)
        o_ref[0] = jnp.dot(s, w_ref[0], precision=HIGHEST, preferred_element_type=F32) + b_ref[0]

    return pl.pallas_call(
        body, grid=(nl, n6 // tn),
        in_specs=[pl.BlockSpec((MOD_ROWS, d), lambda i, j: (0, 0)), pl.BlockSpec((1, d, tn), lambda i, j: (i, 0, j)),
                  pl.BlockSpec((1, 1, tn), lambda i, j: (i, 0, j))],
        out_specs=pl.BlockSpec((1, MOD_ROWS, tn), lambda i, j: (i, 0, j)), out_shape=_sds((nl, MOD_ROWS, n6), F32),
        name="mod_fwd", compiler_params=_params(("parallel", "parallel")))(c16, w_mod, b_loc)


def _mod_bwd(c16, w_mod, dmod_loc):
    nl, d, n6 = w_mod.shape
    tn = _tile(n6, 512, 128)

    def body(c_ref, w_ref, dm_ref, dw_ref, ds_ref):
        @pl.when(jnp.logical_and(pl.program_id(0) == 0, pl.program_id(1) == 0))
        def _():
            ds_ref[...] = jnp.zeros_like(ds_ref)

        cv = c_ref[...]
        s = cv * _sigmoid(cv)
        dm = dm_ref[0]
        dw_ref[0] = lax.dot_general(s, dm, (((0,), (0,)), ((), ())), precision=HIGHEST, preferred_element_type=F32)
        ds_ref[...] += lax.dot_general(dm, w_ref[0], (((1,), (1,)), ((), ())), precision=HIGHEST, preferred_element_type=F32)

    return pl.pallas_call(
        body, grid=(nl, n6 // tn),
        in_specs=[pl.BlockSpec((MOD_ROWS, d), lambda i, j: (0, 0)), pl.BlockSpec((1, d, tn), lambda i, j: (i, 0, j)),
                  pl.BlockSpec((1, MOD_ROWS, tn), lambda i, j: (i, 0, j))],
        out_specs=[pl.BlockSpec((1, d, tn), lambda i, j: (i, 0, j)), pl.BlockSpec((MOD_ROWS, d), lambda i, j: (0, 0))],
        out_shape=[_sds((nl, d, n6), F32), _sds((MOD_ROWS, d), F32)], name="mod_bwd",
        compiler_params=_params(("arbitrary", "arbitrary")))(c16, w_mod, dmod_loc)


def _pack(parts):
    flat = [p.reshape(-1).astype(F32) for p in parts]
    offs = np.cumsum([0] + [f.shape[0] for f in flat])
    total = int(offs[-1])
    unit = (PACK_ROWS if total > PACK_ROWS * LANES else 8) * LANES
    padded = -(-total // unit) * unit
    slab = jnp.concatenate(flat + [jnp.zeros((padded - total,), F32)])
    return slab.reshape(padded // LANES, LANES), [int(o) for o in offs]


def _unshard_cols(seg, lead):
    n = seg.shape[1] // int(np.prod(lead)) if lead else seg.shape[1]
    a = seg.reshape((N_DEV,) + tuple(lead) + (n,))
    a = jnp.moveaxis(a, 0, len(lead))
    return a.reshape(tuple(lead) + (N_DEV * n,))


def _my_cols(a, me, n):
    start = (0,) * (a.ndim - 1) + (me * n,)
    return lax.dynamic_slice(a, start, a.shape[:-1] + (n,))


def _rope_tables(seq, n_ctx):
    rows = seq // GRID_W
    r_idx, c_idx = jnp.meshgrid(jnp.arange(rows), jnp.arange(GRID_W), indexing='ij')
    r_idx = r_idx.reshape(-1).astype(F32)
    c_idx = c_idx.reshape(-1).astype(F32)
    pairs = HEAD_DIM // 4
    freqs = ROPE_THETA ** (-jnp.arange(pairs, dtype=F32) / pairs)
    ang_r, ang_c = r_idx[:, None] * freqs, c_idx[:, None] * freqs
    cos = jnp.concatenate([jnp.cos(ang_r)] * 2 + [jnp.cos(ang_c)] * 2, axis=1)
    sin = jnp.concatenate([-jnp.sin(ang_r), jnp.sin(ang_r), -jnp.sin(ang_c), jnp.sin(ang_c)], axis=1)
    cos = jnp.concatenate([jnp.ones((n_ctx, HEAD_DIM), F32), cos], axis=0)
    sin = jnp.concatenate([jnp.zeros((n_ctx, HEAD_DIM), F32), sin], axis=0)
    lane = np.arange(HEAD_DIM)
    partner = np.where(lane % (2 * pairs) < pairs, lane + pairs, lane - pairs)
    perm = np.zeros((HEAD_DIM, HEAD_DIM), np.float32)
    perm[partner, lane] = 1.0
    return cos, sin, jnp.asarray(perm)


def kernel(x, c, ctx, c_ctx, w_mod, b_mod, norm_g, w_ff_in, w_ff_out, ar_w_in, ar_q_g, ar_k_g, ar_conv_w, ar_conv_b, ar_wa, ar_ba, ar_wx, ar_bx, ar_lambda, ar_w_out, gm_w_in, gm_b_in, gm_v_g, gm_v_b, gm_w_sp, gm_b_sp, gm_w_out, loss_target, m_c_ctx, m_w_mod, m_b_mod, m_norm_g, m_w_ff_in, m_w_ff_out, m_ar_w_in, m_ar_q_g, m_ar_k_g, m_ar_conv_w, m_ar_conv_b, m_ar_wa, m_ar_ba, m_ar_wx, m_ar_bx, m_ar_lambda, m_ar_w_out, m_gm_w_in, m_gm_b_in, m_gm_v_g, m_gm_v_b, m_gm_w_sp, m_gm_b_sp, m_gm_w_out, v_c_ctx, v_w_mod, v_b_mod, v_norm_g, v_w_ff_in, v_w_ff_out, v_ar_w_in, v_ar_q_g, v_ar_k_g, v_ar_conv_w, v_ar_conv_b, v_ar_wa, v_ar_ba, v_ar_wx, v_ar_bx, v_ar_lambda, v_ar_w_out, v_gm_w_in, v_gm_b_in, v_gm_v_g, v_gm_v_b, v_gm_w_sp, v_gm_b_sp, v_gm_w_out):
    given = dict(locals())
    wts = {n: given[n] for n in WEIGHTS}
    mom1 = {n: given["m_" + n] for n in WEIGHTS}
    mom2 = {n: given["v_" + n] for n in WEIGHTS}

    xi, yi, ci = _mesh_pos()
    me = 4 * xi + 2 * yi + ci

    seq, d = x.shape[1], x.shape[2]
    n_ctx = ctx.shape[1]
    t_all = n_ctx + seq
    n_layers = w_mod.shape[0]
    assert n_layers == 2 and ar_w_in.shape[0] == 1 and gm_w_in.shape[0] == 1
    d_ff = w_ff_in.shape[2] * N_DEV
    attn_w, kv_w = N_HEADS * HEAD_DIM, N_KV_HEADS * HEAD_DIM
    rnn_blocks = ar_wa.shape[2]
    d_rnn = rnn_blocks * RNN_BLOCK_W
    gm_groups = gm_w_sp.shape[1]
    d_gm = gm_groups * GM_GROUP_W
    ar_in = ar_w_in.shape[2] * N_DEV
    n6 = w_mod.shape[2]
    tm, tmb = ROW_TILE, ROW_TILE_BWD
    assert attn_w == d_rnn and ar_in == 3 * attn_w + 2 * kv_w and (3 * attn_w) % (2 * kv_w) == 0
    assert n_ctx % tm == 0 and seq % tm == 0 and n_ctx % SCAN_BLOCK == 0 and seq % SCAN_BLOCK == 0 and tm % CHUNK == 0
    nct, nctb = n_ctx // tm, n_ctx // tmb
    kv_blk = (3 * attn_w) // (2 * kv_w)
    lr = d_rnn // LANES

    x2, ctx2, tgt = x[0], ctx[0], loss_target[0]

    def cols_full(g):
        return jnp.moveaxis(g, 0, 1).reshape(g.shape[1], N_DEV * g.shape[2])

    small0, off0 = _pack([c[0], norm_g, ar_conv_w[0], ar_ba[0], ar_bx[0], ar_lambda[0], gm_b_in[0], gm_v_g[0], gm_v_b[0]])
    g_ar_in, gs0 = _run_comm("gather_first", _gather_comm([ar_w_in[0].astype(BF16), small0]))
    gs0 = gs0.reshape(N_DEV, -1)
    w_in = cols_full(g_ar_in)
    split = [attn_w, attn_w + 2 * kv_w, attn_w + 2 * kv_w + d_rnn]
    w_in = jnp.concatenate([w_in[:, :split[0]], w_in[:, split[1]:], w_in[:, split[0]:split[1]]], axis=1)
    w1, w2 = [None] * n_layers, [None] * n_layers

    def seg0(k):
        return gs0[:, off0[k]:off0[k + 1]]

    c_all = seg0(0)
    norm_full = _unshard_cols(seg0(1), (n_layers, 4))
    conv_w = _unshard_cols(seg0(2), (CONV_W,))
    ba, bx, lam = (_unshard_cols(seg0(k), (2,)) for k in (3, 4, 5))
    gm_b_in_f = seg0(6).reshape(1, 2 * d_gm)
    gm_vg, gm_vb = seg0(7).reshape(1, d_gm), seg0(8).reshape(1, d_gm)

    c16 = jnp.concatenate([c_all, c_ctx[None], jnp.zeros((MOD_ROWS - N_DEV - 1, d), F32)], axis=0)
    b_loc = _my_cols(b_mod, me, n6)[:, None, :]
    mod_loc = _mod_fwd(c16, w_mod, b_loc)
    (g_mod,) = _run_comm("gather_mod", _gather_comm([mod_loc]))
    mod_all = jnp.moveaxis(g_mod, 0, 2).reshape(n_layers, MOD_ROWS, N_DEV * n6)
    ml = lax.dynamic_index_in_dim(mod_all, me, axis=1, keepdims=False).reshape(n_layers, 6, d)
    mc = mod_all[:, N_DEV].reshape(n_layers, 6, d)

    def row(a, *idx):
        return a[idx][None]

    cos, sin, perm = _rope_tables(seq, n_ctx)
    wa3 = ar_wa[0].reshape(2 * rnn_blocks, RNN_BLOCK_W, RNN_BLOCK_W)
    wx3 = ar_wx[0].reshape(2 * rnn_blocks, RNN_BLOCK_W, RNN_BLOCK_W)
    conv_b = ar_conv_b
    q_g, k_g = ar_q_g, ar_k_g
    w_sp = gm_w_sp[0]
    bsp_t = jnp.pad(gm_b_sp[0].T, ((0, 0), (0, LANES - gm_groups)))
    expand = np.zeros((LANES, d_gm), np.float32)
    for g in range(gm_groups):
        expand[g, g * GM_GROUP_W:(g + 1) * GM_GROUP_W] = 1.0
    expand = jnp.asarray(expand)

    def relu2(acc):
        r = jnp.maximum(acc, 0.0)
        return (r * r,)

    def relu2_bwd(acc, act):
        return (acc * (2.0 * jnp.sqrt(act.astype(F32))),)

    def ff_in_shard(i):
        return w_ff_in[i].astype(BF16)

    def ff_out_shard(i):
        return w_ff_out[i].astype(BF16)

    tokens = [_t(ctx2), _t(x2, -n_ctx)]
    pre0_args = [row(norm_full, 0, 0), row(mc, 0, 0), row(mc, 0, 1), row(ml, 0, 0), row(ml, 0, 1)]
    f_pre0 = functools.partial(_f_pre_ctx, n_ctx_tiles=nct)
    (h0,) = _rowwise("pre0", f_pre0, t_all, tm, tokens, pre0_args, [(d, BF16)])
    tm_tok = _tile(t_all, 640, 16)
    proj, (g_gm_in,) = _matmul("ar_in", h0, w_in, tm=tm_tok, comm=_gather_comm([gm_w_in[0].astype(BF16)]))
    f_qkv = functools.partial(_f_qkv, nh=N_HEADS, nkv=N_KV_HEADS)
    qkv_tiled = [_t(proj, 0, 0, attn_w), _t(proj, 0, kv_blk, 2 * kv_w), _t(cos), _t(sin)]
    q_r, kv_r = _rowwise("qkv", f_qkv, t_all, tm, qkv_tiled, [q_g, k_g, perm], [(attn_w, BF16), (2 * kv_w, BF16)])
    attn_o, (g_ar_out, g_ff_in0) = _attn_fwd(q_r, kv_r, nct, tm,
                                             comm=_gather_comm([ar_w_out[0].astype(BF16), ff_in_shard(0)]))
    w_out = g_ar_out.reshape(attn_w + d_rnn, d)
    w1[0] = cols_full(g_ff_in0)

    def with_neighbours(a, t, col_blk=0, width=None):
        return [_t(a, -t, col_blk, width), _t(a, 0, col_blk, width), _t(a, t, col_blk, width)]

    f_conv = functools.partial(_f_conv, n_ctx_tiles=nct, n_tiles=t_all // tm)
    (xc,) = _rowwise("conv", f_conv, t_all, tm, with_neighbours(proj, tm, 1, d_rnn), [conv_w, conv_b], [(d_rnn, F32)])
    f_gates = functools.partial(_f_gates, nb=rnn_blocks)
    gate_full = [wa3, ba, wx3, bx, lam]
    a_f, b_f, a_b, b_b, g_gm_out = _rowwise("gates", f_gates, t_all, tm, [_t(xc)], gate_full, [(d_rnn, F32)] * 4,
                                            comm=_gather_comm([gm_w_out[0].astype(BF16)]))

    def to3(a):
        return a.reshape(a.shape[0], lr, LANES)

    nc_scan = n_ctx // SCAN_BLOCK
    h_f, hp_f = _scan_fwd("scan_f", to3(a_f), to3(b_f), nc_scan, False)
    h_b, hp_b = _scan_fwd("scan_b", to3(a_b), to3(b_b), nc_scan, True)
    h_f2, h_b2 = h_f.reshape(t_all, d_rnn), h_b.reshape(t_all, d_rnn)
    rnn_tiled = [_t(h_f2, n_ctx), _t(h_b2, n_ctx), _t(proj, n_ctx, 2, d_rnn)]
    (rnn_o,) = _rowwise("rnn_out", _f_rnnout, seq, tm, rnn_tiled, [], [(d_rnn, BF16)])
    ar = jnp.concatenate([attn_o, rnn_o], axis=1)
    o0 = _matmul("ar_out", ar, w_out)
    mid0_args = [row(norm_full, 0, 1), row(ml, 0, 2), row(norm_full, 0, 2), row(ml, 0, 3), row(ml, 0, 4)]
    x1, h2_0 = _rowwise("mid0", _f_mid, seq, tm, [_t(x2), _t(o0)], mid0_args, [(d, F32), (d, BF16)])
    act0, (g_ff_out0,) = _matmul("ff_in_0", h2_0, w1[0], outs=(BF16,), epilogue=relu2, comm=_gather_comm([ff_out_shard(0)]))
    w2[0] = g_ff_out0.reshape(d_ff, d)
    w_gi, w_go = cols_full(g_gm_in), g_gm_out.reshape(d_gm, d)
    m0, (g_ff_in1,) = _matmul("ff_out_0", act0, w2[0], tm=MM_TILE_M // 2, tk=2 * MM_TILE_K, comm=_gather_comm([ff_in_shard(1)]))
    w1[1] = cols_full(g_ff_in1)
    post0_args = [row(norm_full, 0, 3), row(ml, 0, 5)]
    (x2l,) = _rowwise("post0", _f_post, seq, tm, [_t(x1), _t(m0)], post0_args, [(d, F32)])

    pre1_args = [row(norm_full, 1, 0), row(ml, 1, 0), row(ml, 1, 1)]
    (h1,) = _rowwise("pre1", _f_pre, seq, tm, [_t(x2l)], pre1_args, [(d, BF16)])
    zg = _matmul("gm_in", h1, w_gi)
    f_gm = functools.partial(_f_gm, n_chunks=tmb // CHUNK, groups=gm_groups)
    gm_full = [gm_b_in_f[:, :d_gm], gm_b_in_f[:, d_gm:], gm_vg, gm_vb, w_sp, bsp_t, expand]
    gm_tiled = [_t(zg, 0, 0, d_gm), _t(zg, 0, 1, d_gm)]
    (gmix,) = _rowwise("gm_mix", f_gm, seq, tmb, gm_tiled, gm_full, [(d_gm, BF16)])
    o1 = _matmul("gm_out", gmix, w_go)
    mid1_args = [row(norm_full, 1, 1), row(ml, 1, 2), row(norm_full, 1, 2), row(ml, 1, 3), row(ml, 1, 4)]
    x3, h2_1 = _rowwise("mid1", _f_mid, seq, tm, [_t(x2l), _t(o1)], mid1_args, [(d, F32), (d, BF16)])
    act1, (g_ff_out1,) = _matmul("ff_in_1", h2_1, w1[1], outs=(BF16,), epilogue=relu2,
                                       comm=_gather_comm([ff_out_shard(1)]))
    w2[1] = g_ff_out1.reshape(d_ff, d)
    m1 = _matmul("ff_out_1", act1, w2[1], tm=MM_TILE_M // 2, tk=2 * MM_TILE_K)
    post1_args = [row(norm_full, 1, 3), row(ml, 1, 5)]

    def f_loss(pid, xv, ov, tv, g, gate):
        err = _f_post(pid, xv, ov, g, gate)[0] - tv
        part = 0.5 * jnp.sum(err * err) / d
        return (err / d, jnp.full((8, LANES), part, F32))

    dy, loss_acc = _rowwise("loss", f_loss, seq, tm, [_t(x3), _t(m1), _t(tgt)], post1_args, [(d, F32)], [(8, LANES)])
    loss = lax.psum(loss_acc[0, 0], ("x", "y", "c"))

    d_norm = [[None] * 4 for _ in range(n_layers)]
    d_ml = [[None] * 6 for _ in range(n_layers)]
    recv = {}

    def cols_blocks(g):
        return jnp.moveaxis(g.reshape(g.shape[0], N_DEV, g.shape[1] // N_DEV), 1, 0)

    def rows_blocks(g):
        return g.reshape(N_DEV, g.shape[0] // N_DEV, g.shape[1])

    chip_sums = {}

    def chip_add(key, blocks, theirs):
        chip_sums[key] = _chip_add(key + "_add", blocks, theirs, ci)

    def mlp_bwd(i, dm, act, h2, first_comm=None):
        dw2 = _matmul(f"ff_out_dw_{i}", act, dm, ta=True, outs=(BF16,), comm=first_comm)
        dw2, carried = dw2 if first_comm is not None else (dw2, ())
        blk2 = rows_blocks(dw2)
        dz, (theirs,) = _matmul(f"ff_out_dx_{i}", dm, w2[i], tb=True, outs=(BF16,), extras=(act,), epilogue=relu2_bwd,
                                comm=_swap_comm([blk2]))
        chip_add(f"ff_out_{i}", blk2, theirs)
        blk1, (recv[f"ff_out_{i}"],) = _matmul(f"ff_in_dw_{i}", h2, dz, ta=True, outs=(BF16,), col_blocks=N_DEV,
                                              comm=_chips_comm([chip_sums[f"ff_out_{i}"]]))
        dh2, (theirs,) = _matmul(f"ff_in_dx_{i}", dz, w1[i], tb=True, tm=MM_TILE_M // 2, tk=2 * MM_TILE_K,
                                 comm=_swap_comm([blk1]))
        chip_add(f"ff_in_{i}", blk1, theirs)
        return dh2, carried

    def post_bwd(i, xin, m, args, dout):
        res = _rowwise(f"post_bwd{i}", _bwd_of(_f_post, 2, 1, (0, 1, 2, 3)), seq, tmb, [_t(xin), _t(m), _t(dout)], args,
                       [(d, F32), (d, BF16)], [(1, d), (1, d)])
        d_norm[i][3], d_ml[i][5] = res[2], res[3]
        return res[0], res[1]

    def mid_bwd(i, xin, o, args, dx1, dh2):
        res = _rowwise(f"mid_bwd{i}", _bwd_of(_f_mid, 2, 2, (0, 1, 2, 3, 4, 5, 6)), seq, tmb,
                       [_t(xin), _t(o), _t(dx1), _t(dh2)], args, [(d, F32), (d, BF16)], [(1, d)] * 5)
        d_norm[i][1], d_ml[i][2], d_norm[i][2], d_ml[i][3], d_ml[i][4] = res[2:]
        return res[0], res[1]

    dx3, dm1 = post_bwd(1, x3, m1, post1_args, dy)
    dh2_1, _ = mlp_bwd(1, dm1, act1, h2_1)
    dx2a, do1 = mid_bwd(1, x2l, o1, mid1_args, dx3, dh2_1)
    blk_go = rows_blocks(_matmul("gm_out_dw", gmix, do1, ta=True, outs=(BF16,)))
    dgmix, (theirs,) = _matmul("gm_out_dx", do1, w_go, tb=True, comm=_swap_comm([blk_go]))
    chip_add("gm_out", blk_go, theirs)
    gm_res = _rowwise("gm_mix_bwd", _bwd_of(f_gm, 2, 1, (0, 1, 2, 3, 4, 5, 6, 7)), seq, tmb,
                      gm_tiled + [_t(dgmix)], gm_full, [(d_gm, BF16), (d_gm, BF16)],
                      [(1, d_gm)] * 4 + [w_sp.shape, bsp_t.shape])
    dzg = jnp.concatenate([gm_res[0], gm_res[1]], axis=1)
    g_gm_b_in = jnp.concatenate([gm_res[2], gm_res[3]], axis=1)
    g_gm_vg, g_gm_vb, g_w_sp = gm_res[4], gm_res[5], gm_res[6]
    g_b_sp = gm_res[7][:, :gm_groups].T
    dh1, (recv["gm_out"],) = _matmul("gm_in_dx", dzg, w_gi, tb=True, comm=_chips_comm([chip_sums["gm_out"]]))
    blk_gi = _matmul("gm_in_dw", h1, dzg, ta=True, outs=(BF16,), col_blocks=N_DEV)

    def f_pre_bwd(pid, xv, dh, dxa, g, sh, sc):
        dxv, dg, dsh, dsc = _bwd_of(_f_pre, 1, 1, (0, 1, 2, 3))(pid, xv, dh, g, sh, sc)
        return (dxv + dxa, dg, dsh, dsc)

    res = _rowwise("pre_bwd1", f_pre_bwd, seq, tmb, [_t(x2l), _t(dh1), _t(dx2a)], pre1_args, [(d, F32)], [(1, d)] * 3)
    dx2l = res[0]
    d_norm[1][0], d_ml[1][0], d_ml[1][1] = res[1:]

    dx1, dm0 = post_bwd(0, x1, m0, post0_args, dx2l)
    dh2_0, (theirs,) = mlp_bwd(0, dm0, act0, h2_0, first_comm=_swap_comm([blk_gi]))
    chip_add("gm_in", blk_gi, theirs)
    dxa, do0 = mid_bwd(0, x2, o0, mid0_args, dx1, dh2_0)
    blk_out = rows_blocks(_matmul("ar_out_dw", ar, do0, ta=True, outs=(BF16,)))
    d_ar, (theirs,) = _matmul("ar_out_dx", do0, w_out, tb=True, comm=_swap_comm([blk_out]))
    chip_add("ar_out", blk_out, theirs)

    late = ["ff_in_1", "gm_in", "ff_in_0"]
    dq, dkt, dvt, *landed = _attn_bwd(q_r, kv_r, d_ar, nct, tm, comm=_chips_comm([chip_sums[k] for k in late]))
    recv.update(zip(late, landed))
    dkv_all = jnp.concatenate([dkt, dvt], axis=0).T

    def f_qkv_bwd(pid, pq, pkv, cos_t, sin_t, dq_t, dkv_t, *fulls):
        dq_t = jnp.where(pid < nctb, 0.0, dq_t)
        return _bwd_of(f_qkv, 4, 2, (0, 1, 4, 5))(pid, pq, pkv, cos_t, sin_t, dq_t, dkv_t, *fulls)

    qkv_res = _rowwise("qkv_bwd", f_qkv_bwd, t_all, tmb, qkv_tiled + [_t(dq, -n_ctx), _t(dkv_all)],
                       [q_g, k_g, perm], [(attn_w, BF16), (2 * kv_w, BF16)], [q_g.shape, k_g.shape])
    dproj_q, dproj_kv, g_q_g, g_k_g = qkv_res

    rnn_res = _rowwise("rnn_out_bwd", _bwd_of(_f_rnnout, 3, 1, (0, 2)), seq, tmb, rnn_tiled + [_t(d_ar, 0, 1, d_rnn)], [],
                       [(d_rnn, F32), (d_rnn, BF16)])
    zc = jnp.zeros((n_ctx, d_rnn), F32)
    dh_all = to3(jnp.concatenate([zc, rnn_res[0]], axis=0))
    dproj_g = jnp.concatenate([zc.astype(BF16), rnn_res[1]], axis=0)
    da_f, db_f = _scan_bwd("scan_f_bwd", to3(a_f), dh_all, hp_f, nc_scan, False)
    da_b, db_b = _scan_bwd("scan_b_bwd", to3(a_b), dh_all, hp_b, nc_scan, True)
    gate_cts = [_t(a.reshape(t_all, d_rnn)) for a in (da_f, db_f, da_b, db_b)]
    gates_res = _rowwise("gates_bwd", _bwd_of(f_gates, 1, 4, (0, 1, 2, 3, 4, 5)), t_all, min(tmb, GATES_BWD_TILE),
                         [_t(xc)] + gate_cts, gate_full, [(d_rnn, F32)], [wa3.shape, ba.shape, wx3.shape, bx.shape, lam.shape])
    dxc, g_wa, g_ba, g_wx, g_bx, g_lam = gates_res
    f_conv_b = functools.partial(_f_conv_bwd, n_ctx_tiles=nctb, n_tiles=t_all // tmb)
    conv_tiled = with_neighbours(proj, tmb, 1, d_rnn) + with_neighbours(dxc, tmb)
    dproj_x, g_conv_w, g_conv_b = _rowwise("conv_bwd", f_conv_b, t_all, tmb, conv_tiled, [conv_w],
                                           [(d_rnn, BF16)], [conv_w.shape, (1, d_rnn)])
    dproj = jnp.concatenate([dproj_q, dproj_x, dproj_g, dproj_kv], axis=1)
    sq_names = ['ar_wa', 'ar_wx', 'gm_w_sp']

    def stack_sq(parts):
        return jnp.concatenate([p.reshape(-1, LANES) for p in parts], axis=0)

    sq_pack = stack_sq([g_wa, g_wx, g_w_sp]).astype(BF16)
    g_w_in, (g_sq,) = _matmul("ar_in_dw", h0, dproj, ta=True, outs=(BF16,), comm=_gather_comm([sq_pack]))
    g_w_in = jnp.concatenate([g_w_in[:, :attn_w], g_w_in[:, 3 * attn_w:], g_w_in[:, attn_w:3 * attn_w]], axis=1)
    blk_in = cols_blocks(g_w_in)
    dh0, (recv["ar_out"], theirs) = _matmul("ar_in_dx", dproj, w_in, tb=True, tm=tm_tok,
                                            comm=_both(_chips_comm([chip_sums["ar_out"]]), _swap_comm([blk_in])))
    chip_add("ar_in", blk_in, theirs)

    f_pre0b = functools.partial(_f_pre_ctx, n_ctx_tiles=nctb)

    def f_pre0_bwd(pid, xcv, xlv, dh, dxp, g, shc, scc, shl, scl):
        grads = _bwd_of(f_pre0b, 2, 1, (1, 2, 3, 4, 5, 6))(pid, xcv, xlv, dh, g, shc, scc, shl, scl)
        return (grads[0] + dxp,) + tuple(grads[1:])

    res = _rowwise("pre_bwd0", f_pre0_bwd, seq, tmb, tokens + [_t(dh0), _t(dxa, -n_ctx)], pre0_args, [(d, F32)], [(1, d)] * 5,
                   comm=_chips_comm([chip_sums["ar_in"]]), skip_rows=n_ctx)
    grad_x = res[0][None]
    d_norm[0][0], d_mc_shift, d_mc_scale, d_ml[0][0], d_ml[0][1] = res[1:6]
    recv["ar_in"] = res[6]

    z1d = jnp.zeros((1, d), F32)
    dml = jnp.concatenate([jnp.concatenate(r, axis=0)[None] for r in d_ml], axis=0)
    dmc = jnp.concatenate([jnp.concatenate([d_mc_shift, d_mc_scale] + [z1d] * 4, axis=0)[None],
                           jnp.zeros((n_layers - 1, 6, d), F32)], axis=0)
    g_norm = jnp.concatenate([jnp.concatenate(r, axis=0)[None] for r in d_norm], axis=0)
    small_parts = [dmc, g_norm, g_q_g, g_k_g, g_conv_w, g_conv_b, g_ba, g_bx, g_lam, g_gm_b_in, g_gm_vg, g_gm_vb, g_b_sp]
    small2, off2 = _pack([dml] + small_parts)
    (gs2,) = _run_comm("gather_small_grads", _gather_comm([small2]))
    dml_all = gs2.reshape(N_DEV, -1)[:, :off2[1]].reshape(N_DEV, n_layers, 6 * d)
    summed = _sum_lead("sum_small_grads", gs2).reshape(-1)
    summed_sq = _sum_lead("sum_square_grads", g_sq)

    def seg2(k, shape):
        return summed[off2[k + 1]:off2[k + 2]].reshape(shape)

    dmc_sum = seg2(0, (n_layers, 6 * d))
    dmod_rows = jnp.concatenate([jnp.moveaxis(dml_all, 0, 1), dmc_sum[:, None, :],
                                 jnp.zeros((n_layers, MOD_ROWS - N_DEV - 1, 6 * d), F32)], axis=1)
    g_b_mod = _sum_lead("sum_b_mod", jnp.moveaxis(dmod_rows, 1, 0).reshape(MOD_ROWS, n_layers * 6 * d // LANES, LANES))
    g_b_mod = g_b_mod.reshape(n_layers, 6 * d)
    g_w_mod, ds16 = _mod_bwd(c16, w_mod, _my_cols(dmod_rows, me, n6))
    (g_ds,) = _run_comm("gather_dctx", _gather_comm([ds16[N_DEV].reshape(d // LANES, LANES)]))
    ds_ctx = _sum_lead("sum_dctx", g_ds)
    (g_c_ctx,) = _rowwise("silu_bwd", _f_silu_mul, d // LANES, d // LANES, [_t(c_ctx.reshape(d // LANES, LANES)), _t(ds_ctx)],
                          [], [(LANES, F32)])
    g_c_ctx = g_c_ctx.reshape(d)

    grads = {
        'c_ctx': g_c_ctx, 'b_mod': g_b_mod,
        'norm_g': _my_cols(seg2(1, (n_layers, 4, d)), me, d // N_DEV),
        'ar_q_g': seg2(2, ar_q_g.shape), 'ar_k_g': seg2(3, ar_k_g.shape),
        'ar_conv_w': _my_cols(seg2(4, (1, CONV_W, d_rnn)), me, d_rnn // N_DEV),
        'ar_conv_b': seg2(5, ar_conv_b.shape),
        'ar_ba': _my_cols(seg2(6, (1, 2, d_rnn)), me, d_rnn // N_DEV),
        'ar_bx': _my_cols(seg2(7, (1, 2, d_rnn)), me, d_rnn // N_DEV),
        'ar_lambda': _my_cols(seg2(8, (1, 2, d_rnn)), me, d_rnn // N_DEV),
        'gm_b_in': _my_cols(seg2(9, (1, 2 * d_gm)), me, 2 * d_gm // N_DEV),
        'gm_v_g': _my_cols(seg2(10, (1, d_gm)), me, d_gm // N_DEV),
        'gm_v_b': _my_cols(seg2(11, (1, d_gm)), me, d_gm // N_DEV),
        'gm_b_sp': seg2(12, gm_b_sp.shape),
    }
    small_names = list(grads)
    deltas, new_m, new_v = {}, {}, {}

    sq_res = (summed_sq,) + tuple(_adam_f32("adam_square", summed_sq, *[stack_sq([src[n] for n in sq_names])
                                                                        for src in (wts, mom1, mom2)]))
    first = 0
    for n in sq_names:
        rows_n = wts[n].size // LANES
        for dst, slab in zip((grads, deltas, new_m, new_v), sq_res):
            dst[n] = slab[first:first + rows_n].reshape(wts[n].shape)
        first += rows_n

    wp, offw = _pack([wts[n] for n in small_names])
    mp, _ = _pack([mom1[n] for n in small_names])
    vp, _ = _pack([mom2[n] for n in small_names])
    gp, _ = _pack([grads[n] for n in small_names])
    dp, mp2, vp2 = _adam_f32("adam_small", gp, wp, mp, vp)
    for k, n in enumerate(small_names):
        for dst, slab in ((deltas, dp), (new_m, mp2), (new_v, vp2)):
            dst[n] = slab.reshape(-1)[offw[k]:offw[k + 1]].reshape(wts[n].shape)

    grads['w_mod'] = g_w_mod
    dw, mw, vw = _adam_f32("adam_w_mod", g_w_mod.reshape(n_layers * d, n6), w_mod.reshape(n_layers * d, n6),
                           m_w_mod.reshape(n_layers * d, n6), v_w_mod.reshape(n_layers * d, n6))
    deltas['w_mod'], new_m['w_mod'], new_v['w_mod'] = (a.reshape(w_mod.shape) for a in (dw, mw, vw))

    received = {
        'w_ff_in': [recv[f"ff_in_{i}"] for i in range(n_layers)], 'w_ff_out': [recv[f"ff_out_{i}"] for i in range(n_layers)],
        'ar_w_in': [recv["ar_in"]], 'ar_w_out': [recv["ar_out"]], 'gm_w_in': [recv["gm_in"]], 'gm_w_out': [recv["gm_out"]]}
    for n, r in received.items():
        shp = wts[n].shape
        flat = (shp[0] * shp[1], shp[2])
        res = _adam_recv("adam_" + n, r, wts[n].reshape(flat), mom1[n].reshape(flat), mom2[n].reshape(flat))
        grads[n], deltas[n], new_m[n], new_v[n] = (a.reshape(shp) for a in res)

    return (loss, grad_x, *[grads[n] for n in WEIGHTS], *[deltas[n] for n in WEIGHTS],
            *[new_m[n] for n in WEIGHTS], *[new_v[n] for n in WEIGHTS])
```

```python
import functools

import numpy as np
import jax
import jax.numpy as jnp
from jax import lax
from jax.experimental import pallas as pl
from jax.experimental.pallas import tpu as pltpu

F32 = jnp.float32
BF16 = jnp.bfloat16
HIGHEST = lax.Precision.HIGHEST
LOG2_E = 1.4426950408889634
LN_2 = 0.6931471805599453

GRID_W = 64
N_HEADS = 8
N_KV_HEADS = 2
HEAD_DIM = 128
ROPE_THETA = 10000.0
RNN_BLOCK_W = 128
CONV_W = 4
RG_C = 8.0
GM_GROUP_W = 128
CHUNK = 128
EPS = 1e-6
ADAM_LR = 0.001
ADAM_B1 = 0.9
ADAM_B2 = 0.999
ADAM_EPS = 1e-08
ADAM_WD = 0.01
ADAM_STEP = 10

N_DEV = 8
MOD_ROWS = 16
LANES = 128
ROW_TILE = 256
ROW_TILE_BWD = 256
ATTN_BWD_HEADS_PER_STEP = 4
ATTN_HEADS_PER_STEP = 4
GATES_BWD_TILE = 128
SCAN_BLOCK = 256
VMEM_LIMIT = 56 * 1024 * 1024
PACK_ROWS = 512
MM_TILE_M = 1024
MM_TILE_N = 1024
MM_TILE_K = 2048

WEIGHTS = ['c_ctx', 'w_mod', 'b_mod', 'norm_g', 'w_ff_in', 'w_ff_out', 'ar_w_in', 'ar_q_g', 'ar_k_g', 'ar_conv_w',
           'ar_conv_b', 'ar_wa', 'ar_ba', 'ar_wx', 'ar_bx', 'ar_lambda', 'ar_w_out', 'gm_w_in', 'gm_b_in', 'gm_v_g',
           'gm_v_b', 'gm_w_sp', 'gm_b_sp', 'gm_w_out']


def _sds(shape, dtype):
    return jax.ShapeDtypeStruct(tuple(shape), dtype)


def _tile(dim, pref, align):
    t = (min(pref, dim) // align) * align
    while t >= align:
        if dim % t == 0:
            return t
        t -= align
    return dim


def _params(sem):
    return pltpu.CompilerParams(dimension_semantics=sem, vmem_limit_bytes=VMEM_LIMIT)


def _rms(x, g):
    return x * lax.rsqrt(jnp.mean(x * x, axis=-1, keepdims=True) + EPS) * g


def _gelu(x):
    return 0.5 * x * (1.0 + jnp.tanh(0.7978845608028654 * (x + 0.044715 * (x * x * x))))


def _sigmoid(x):
    return 0.5 * (jnp.tanh(0.5 * x) + 1.0)


def _log1p_pos(u):
    small = u < 1e-3
    us = jnp.where(small, u, 0.0)
    return jnp.where(small, us * (1.0 - us * (0.5 - us * (1.0 / 3.0))), jnp.log(1.0 + u))


def _softplus(x):
    return jnp.maximum(x, 0.0) + _log1p_pos(jnp.exp(-jnp.abs(x)))


def _f_pre_ctx(pid, xc, xl, g, shc, scc, shl, scl, *, n_ctx_tiles):
    is_ctx = pid < n_ctx_tiles
    x = jnp.where(is_ctx, xc, xl)
    sh = jnp.where(is_ctx, shc, shl)
    sc = jnp.where(is_ctx, scc, scl)
    return (_rms(x, g) * (1.0 + sc) + sh,)


def _f_pre(pid, x, g, sh, sc):
    return (_rms(x, g) * (1.0 + sc) + sh,)


def _f_mid(pid, x, o, g1, gate, g2, sh, sc):
    x1 = x + gate * _rms(o, g1)
    return (x1, _rms(x1, g2) * (1.0 + sc) + sh)


def _f_post(pid, x, o, g, gate):
    return (x + gate * _rms(o, g),)


def _f_qkv(pid, pq, pkv, cos, sin, q_g, k_g, perm, *, nh, nkv):
    hd = HEAD_DIM

    def norm_rope(xh, g):
        y = _rms(xh, g)
        return y * cos + jnp.dot(y, perm, precision=HIGHEST, preferred_element_type=F32) * sin

    qs = [norm_rope(pq[:, h * hd:(h + 1) * hd], q_g) * (HEAD_DIM ** -0.5 * LOG2_E) for h in range(nh)]
    ks = [norm_rope(pkv[:, h * hd:(h + 1) * hd], k_g) for h in range(nkv)]
    return (jnp.concatenate(qs, axis=1), jnp.concatenate(ks + [pkv[:, nkv * hd:]], axis=1))


def _f_gates(pid, x, wa, ba, wx, bx, lam, *, nb):
    w = RNN_BLOCK_W
    cols = [[] for _ in range(4)]
    for n in range(nb):
        blk = slice(n * w, (n + 1) * w)
        xn = x[:, blk]
        xb = xn.astype(BF16)
        for d in range(2):
            r = _sigmoid(jnp.dot(xb, wa[d * nb + n].astype(BF16), preferred_element_type=F32) + ba[d:d + 1, blk])
            i = _sigmoid(jnp.dot(xb, wx[d * nb + n].astype(BF16), preferred_element_type=F32) + bx[d:d + 1, blk])
            log_a = -RG_C * r * _softplus(-lam[d:d + 1, blk])
            a = jnp.exp(log_a)
            cols[2 * d].append(a)
            cols[2 * d + 1].append(jnp.sqrt(-jnp.tanh(log_a) * (a * a + 1.0)) * (i * xn))
    return tuple(jnp.concatenate(c, axis=1) for c in cols)


def _f_rnnout(pid, hf, hb, gr):
    return ((hf + hb) * _gelu(gr),)


def _f_gm(pid, zu, zv, bu, bv, v_g, v_b, w_sp, bsp_t, expand, *, n_chunks, groups):
    u = _gelu(zu + bu)
    v = _gelu(zv + bv)
    mu = jnp.mean(v, axis=-1, keepdims=True)
    vc = v - mu
    v = vc * lax.rsqrt(jnp.mean(vc * vc, axis=-1, keepdims=True) + EPS) * v_g + v_b
    bias = jnp.dot(bsp_t, expand, precision=HIGHEST, preferred_element_type=F32)
    outs = []
    for c in range(n_chunks):
        vch = v[c * CHUNK:(c + 1) * CHUNK]
        cols = [jnp.dot(w_sp[g].astype(BF16), vch[:, g * GM_GROUP_W:(g + 1) * GM_GROUP_W].astype(BF16),
                        preferred_element_type=F32) for g in range(groups)]
        outs.append(u[c * CHUNK:(c + 1) * CHUNK] * (jnp.concatenate(cols, axis=1) + bias))
    return (jnp.concatenate(outs, axis=0),)


def _f_silu_mul(pid, c, d):
    return (d * jax.grad(lambda z: jnp.sum(z * _sigmoid(z)))(c),)


def _bwd_of(fn, n_tiled, n_ct, want):
    def bwd(pid, *args):
        tiles = [t.astype(F32) for t in args[:n_tiled]]
        cts = args[n_tiled:n_tiled + n_ct]
        fulls = list(args[n_tiled + n_ct:])
        outs, vjp = jax.vjp(lambda *a: fn(pid, *a), *tiles, *fulls)
        grads = vjp(tuple(ct.astype(o.dtype) for ct, o in zip(cts, outs)))
        return tuple(grads[i] for i in want)
    return bwd


def _rowwise(name, fn, rows, tm, tiled, full, outs, accs=(), comm=None, skip_rows=0):
    n_t, n_f, n_o, n_a = len(tiled), len(full), len(outs), len(accs)
    assert skip_rows % tm == 0
    skip = skip_rows // tm
    n_tiles = rows // tm + skip

    def body(*refs):
        in_refs, c_ins, res_refs, c_outs, _, c_sems = _split_refs(refs, n_t + n_f, n_o + n_a, 0, comm)
        pid = pl.program_id(0)
        if comm is not None:
            @pl.when(pid == 0)
            def _():
                comm.start(c_ins, c_outs, *c_sems)

        res = fn(pid, *[r[...] for r in in_refs])
        o_refs, a_refs = res_refs[:n_o], res_refs[n_o:]
        for r, v in zip(o_refs, res[:n_o]):
            r[...] = v.astype(r.dtype)
        if n_a:
            @pl.when(pid == 0)
            def _():
                for r in a_refs:
                    r[...] = jnp.zeros_like(r)
            for r, v in zip(a_refs, res[n_o:]):
                r[...] += v.astype(F32)
        if comm is not None:
            @pl.when(pid == n_tiles - 1)
            def _():
                comm.finish(c_ins, c_outs, *c_sems)

    assert all(ro % tm == 0 for (_, ro, _, _) in tiled)
    in_specs = [pl.BlockSpec((tm, w), lambda i, ro=ro // tm, cb=cb, last=a.shape[0] // tm - 1: (jnp.clip(i + ro, 0, last), cb))
                for (a, ro, cb, w) in tiled]
    in_specs += [pl.BlockSpec(a.shape, lambda i, nd=a.ndim: (0,) * nd) for a in full]
    out_shape = [_sds((rows, w), dt) for (w, dt) in outs] + [_sds(s, F32) for s in accs]
    out_specs = [pl.BlockSpec((tm, w), lambda i: (jnp.maximum(i - skip, 0), 0)) for (w, _) in outs]
    out_specs += [pl.BlockSpec(tuple(s), lambda i, nd=len(s): (0,) * nd) for s in accs]
    c_in, c_out, c_scr = (comm.ins, comm.out_shapes, comm.scratch()) if comm is not None else ([], [], [])
    return pl.pallas_call(body, grid=(n_tiles,), in_specs=in_specs + [_ANY] * len(c_in), out_specs=out_specs + [_ANY] * len(c_out),
                          out_shape=out_shape + list(c_out), scratch_shapes=c_scr, name=name,
                          compiler_params=_params(("arbitrary",)))(*[t[0] for t in tiled], *full, *c_in)


def _t(a, row_off=0, col_blk=0, width=None):
    return (a, row_off, col_blk, a.shape[1] if width is None else width)


class _Comm:
    def __init__(self, ins, out_shapes, n_sems, start, finish):
        self.ins, self.out_shapes, self.n_sems, self.start, self.finish = list(ins), list(out_shapes), n_sems, start, finish

    def scratch(self):
        return [pltpu.SemaphoreType.DMA((self.n_sems,)), pltpu.SemaphoreType.DMA((self.n_sems,)),
                pltpu.SemaphoreType.DMA((len(self.ins),))]


_ANY = pl.BlockSpec(memory_space=pl.ANY)


class _SemSlice:
    def __init__(self, ref, first):
        self.ref, self.first = ref, first

    @property
    def at(self):
        return self

    def __getitem__(self, k):
        return self.ref.at[self.first + k]


def _both(*comms):
    comms = [cm for cm in comms if cm is not None]
    if len(comms) <= 1:
        return comms[0] if comms else None

    def parts(ins, outs, send, recv, local):
        i0 = o0 = s0 = 0
        for cm in comms:
            ni, no = len(cm.ins), len(cm.out_shapes)
            yield cm, (ins[i0:i0 + ni], outs[o0:o0 + no], _SemSlice(send, s0), _SemSlice(recv, s0), _SemSlice(local, i0))
            i0, o0, s0 = i0 + ni, o0 + no, s0 + cm.n_sems

    def start(*refs):
        for cm, sub in parts(*refs):
            cm.start(*sub)

    def finish(*refs):
        for cm, sub in parts(*refs):
            cm.finish(*sub)

    return _Comm(sum((cm.ins for cm in comms), []), sum((cm.out_shapes for cm in comms), []),
                 sum(cm.n_sems for cm in comms), start, finish)


def _row_block(cols, itemsize):
    return max(16, (1 << 20) // (cols * itemsize))


def _split_refs(refs, n_in, n_out, n_scratch, comm):
    ci, co, cs = (len(comm.ins), len(comm.out_shapes), 3) if comm is not None else (0, 0, 0)
    cuts = np.cumsum([0, n_in, ci, n_out, co, n_scratch, cs])
    return [refs[cuts[i]:cuts[i + 1]] for i in range(6)]


def _matmul(name, a, b, *, ta=False, tb=False, outs=((F32,)), epilogue=None, extras=(), tm=None, tn=None, tk=None, comm=None,
            col_blocks=None):
    m, k = (a.shape[1], a.shape[0]) if ta else a.shape
    n = b.shape[0] if tb else b.shape[1]
    tm = _tile(m, tm or MM_TILE_M, 128 if ta else 16)
    tn = _tile(n if col_blocks is None else n // col_blocks, tn or MM_TILE_N, 128)
    tk = _tile(k, tk or MM_TILE_K, 128 if not ta else 16)
    ni, nj, nk = m // tm, n // tn, k // tk
    n_e, n_o = len(extras), len(outs)
    dims = (((0 if ta else 1,), (1 if tb else 0,)), ((), ()))

    def body(*refs):
        ins, c_ins, o_refs, c_outs, scratch, c_sems = _split_refs(refs, 2 + n_e, n_o, 1 if nk > 1 else 0, comm)
        a_ref, b_ref, e_refs = ins[0], ins[1], ins[2:]
        i, j, kk = pl.program_id(0), pl.program_id(1), pl.program_id(2)
        if comm is not None:
            @pl.when(jnp.logical_and(jnp.logical_and(i == 0, j == 0), kk == 0))
            def _():
                comm.start(c_ins, c_outs, *c_sems)

        def finish(acc):
            res = (acc,) if epilogue is None else epilogue(acc, *[e[...] for e in e_refs])
            for r, v in zip(o_refs, res):
                r[...] = v.astype(r.dtype)

        prod = lax.dot_general(a_ref[...].astype(BF16), b_ref[...].astype(BF16), dims, preferred_element_type=F32)
        if nk == 1:
            finish(prod)
        else:
            acc = scratch[0]

            @pl.when(kk == 0)
            def _():
                acc[...] = prod

            @pl.when(kk > 0)
            def _():
                acc[...] += prod

            @pl.when(kk == nk - 1)
            def _():
                finish(acc[...])
        if comm is not None:
            @pl.when(jnp.logical_and(jnp.logical_and(i == ni - 1, j == nj - 1), kk == nk - 1))
            def _():
                comm.finish(c_ins, c_outs, *c_sems)

    a_spec = pl.BlockSpec((tk, tm), lambda i, j, kk: (kk, i)) if ta else pl.BlockSpec((tm, tk), lambda i, j, kk: (i, kk))
    b_spec = pl.BlockSpec((tn, tk), lambda i, j, kk: (j, kk)) if tb else pl.BlockSpec((tk, tn), lambda i, j, kk: (kk, j))
    mn_spec = pl.BlockSpec((tm, tn), lambda i, j, kk: (i, j))
    c_in, c_out, c_scr = (comm.ins, comm.out_shapes, comm.scratch()) if comm is not None else ([], [], [])
    if col_blocks is None:
        o_spec, o_shape = mn_spec, (m, n)
    else:
        per = n // col_blocks // tn
        o_spec = pl.BlockSpec((None, tm, tn), lambda i, j, kk: (j // per, i, j % per))
        o_shape = (col_blocks, m, n // col_blocks)
    res = pl.pallas_call(body, grid=(ni, nj, nk), in_specs=[a_spec, b_spec] + [mn_spec] * n_e + [_ANY] * len(c_in),
                         out_specs=[o_spec] * n_o + [_ANY] * len(c_out),
                         out_shape=[_sds(o_shape, dt) for dt in outs] + list(c_out),
                         scratch_shapes=([pltpu.VMEM((tm, tn), F32)] if nk > 1 else []) + c_scr, name=name,
                         compiler_params=_params(("arbitrary", "arbitrary", "arbitrary")))(a, b, *extras, *c_in)
    main = res[0] if n_o == 1 else res[:n_o]
    return main if comm is None else (main, res[n_o:])


def _attn_fwd(q, kv, n_ctx_tiles, tq, comm=None):
    t_all = q.shape[0]
    s_len = t_all - n_ctx_tiles * tq
    hd, groups = HEAD_DIM, N_HEADS // N_KV_HEADS
    hps = ATTN_HEADS_PER_STEP
    assert groups % hps == 0
    gsteps = groups // hps
    nq = s_len // tq

    def body(*refs):
        (q_ref, k_ref, v_ref), c_ins, (o_ref,), c_outs, _, c_sems = _split_refs(refs, 3, 1, 0, comm)
        kh, g, i = pl.program_id(0), pl.program_id(1), pl.program_id(2)
        if comm is not None:
            @pl.when(jnp.logical_and(jnp.logical_and(kh == 0, g == 0), i == 0))
            def _():
                comm.start(c_ins, c_outs, *c_sems)

        kk, vv = k_ref[...], v_ref[...]
        for h in range(hps):
            qh = q_ref[:, h * hd:(h + 1) * hd]
            s = lax.dot_general(qh, kk, (((1,), (1,)), ((), ())), preferred_element_type=F32)
            p = jnp.exp2(s - jnp.max(s, axis=-1, keepdims=True))
            l = jnp.sum(p, axis=-1, keepdims=True)
            o = jnp.dot(p.astype(BF16), vv, preferred_element_type=F32) * (1.0 / l)
            o_ref[:, h * hd:(h + 1) * hd] = o.astype(o_ref.dtype)
        if comm is not None:
            @pl.when(jnp.logical_and(jnp.logical_and(kh == N_KV_HEADS - 1, g == gsteps - 1), i == nq - 1))
            def _():
                comm.finish(c_ins, c_outs, *c_sems)

    c_in, c_out, c_scr = (comm.ins, comm.out_shapes, comm.scratch()) if comm is not None else ([], [], [])
    res = pl.pallas_call(
        body, grid=(N_KV_HEADS, gsteps, nq),
        in_specs=[pl.BlockSpec((tq, hps * hd), lambda kh, g, i: (i + n_ctx_tiles, kh * gsteps + g)),
                  pl.BlockSpec((t_all, hd), lambda kh, g, i: (0, kh)),
                  pl.BlockSpec((t_all, hd), lambda kh, g, i: (0, N_KV_HEADS + kh))] + [_ANY] * len(c_in),
        out_specs=[pl.BlockSpec((tq, hps * hd), lambda kh, g, i: (i, kh * gsteps + g))] + [_ANY] * len(c_out),
        out_shape=[_sds((s_len, N_HEADS * hd), BF16)] + list(c_out), scratch_shapes=c_scr, name="attn_fwd",
        compiler_params=_params(("arbitrary", "arbitrary", "arbitrary")))(q, kv, kv, *c_in)
    return res[0] if comm is None else (res[0], res[1:])


def _attn_bwd(q, kv, d_ar, n_ctx_tiles, tq, comm=None):
    t_all = q.shape[0]
    s_len = t_all - n_ctx_tiles * tq
    hd, groups = HEAD_DIM, N_HEADS // N_KV_HEADS
    hps = ATTN_BWD_HEADS_PER_STEP
    assert groups % hps == 0
    gsteps = groups // hps
    nq = s_len // tq

    def body(*refs):
        (q_ref, k_ref, v_ref, do_ref), c_ins, (dq_ref, dkt_ref, dvt_ref), c_outs, _, c_sems = _split_refs(refs, 4, 3, 0, comm)
        first = jnp.logical_and(pl.program_id(1) == 0, pl.program_id(2) == 0)
        if comm is not None:
            @pl.when(jnp.logical_and(first, pl.program_id(0) == 0))
            def _():
                comm.start(c_ins, c_outs, *c_sems)

        @pl.when(first)
        def _():
            dkt_ref[...] = jnp.zeros_like(dkt_ref)
            dvt_ref[...] = jnp.zeros_like(dvt_ref)

        kk, vv = k_ref[...], v_ref[...]
        for h in range(hps):
            cols = slice(h * hd, (h + 1) * hd)
            qv = q_ref[:, cols]
            s = lax.dot_general(qv, kk, (((1,), (1,)), ((), ())), preferred_element_type=F32)
            p = jnp.exp2(s - jnp.max(s, axis=-1, keepdims=True))
            inv_l = 1.0 / jnp.sum(p, axis=-1, keepdims=True)
            do = (do_ref[:, cols] * inv_l).astype(BF16)
            dp = lax.dot_general(do, vv, (((1,), (1,)), ((), ())), preferred_element_type=F32)
            ds = (p * (dp - jnp.sum(p * dp, axis=-1, keepdims=True) * inv_l)).astype(BF16)
            dq_ref[:, cols] = jnp.dot(ds, kk, preferred_element_type=F32) * LN_2
            dkt_ref[...] += jnp.dot(qv.T, ds, preferred_element_type=F32)
            dvt_ref[...] += jnp.dot(do.T, p.astype(BF16), preferred_element_type=F32)
        last = jnp.logical_and(pl.program_id(1) == gsteps - 1, pl.program_id(2) == nq - 1)

        @pl.when(last)
        def _():
            dkt_ref[...] *= LN_2

        if comm is not None:
            @pl.when(jnp.logical_and(last, pl.program_id(0) == N_KV_HEADS - 1))
            def _():
                comm.finish(c_ins, c_outs, *c_sems)

    c_in, c_out, c_scr = (comm.ins, comm.out_shapes, comm.scratch()) if comm is not None else ([], [], [])
    return pl.pallas_call(
        body, grid=(N_KV_HEADS, gsteps, nq),
        in_specs=[pl.BlockSpec((tq, hps * hd), lambda kh, g, i: (i + n_ctx_tiles, kh * gsteps + g)),
                  pl.BlockSpec((t_all, hd), lambda kh, g, i: (0, kh)),
                  pl.BlockSpec((t_all, hd), lambda kh, g, i: (0, N_KV_HEADS + kh)),
                  pl.BlockSpec((tq, hps * hd), lambda kh, g, i: (i, kh * gsteps + g))] + [_ANY] * len(c_in),
        out_specs=[pl.BlockSpec((tq, hps * hd), lambda kh, g, i: (i, kh * gsteps + g)),
                   pl.BlockSpec((hd, t_all), lambda kh, g, i: (kh, 0)),
                   pl.BlockSpec((hd, t_all), lambda kh, g, i: (kh, 0))] + [_ANY] * len(c_out),
        out_shape=[_sds((s_len, N_HEADS * hd), F32), _sds((N_KV_HEADS * hd, t_all), F32),
                   _sds((N_KV_HEADS * hd, t_all), F32)] + list(c_out),
        scratch_shapes=c_scr, name="attn_bwd",
        compiler_params=_params(("arbitrary", "arbitrary", "arbitrary")))(q, kv, kv, d_ar, *c_in)


def _scan_order(nb, nc, reverse):
    if not reverse:
        return lambda i: i
    return lambda i: jnp.where(i < nc, nc - 1 - i, nb - 1 - (i - nc))


def _scan_fwd(name, a, b, nc, reverse):
    t_all, r, l = a.shape
    tb = SCAN_BLOCK
    nb = t_all // tb
    order = _scan_order(nb, nc, reverse)

    def body(a_ref, b_ref, h_ref, hp_ref, carry):
        @pl.when(pl.program_id(0) == 0)
        def _():
            carry[...] = jnp.zeros_like(carry)

        def step(s, h):
            t = tb - 1 - s if reverse else s
            hp_ref[t] = h
            h = a_ref[t] * h + b_ref[t]
            h_ref[t] = h
            return h

        carry[...] = lax.fori_loop(0, tb, step, carry[...], unroll=8)

    spec = pl.BlockSpec((tb, r, l), lambda i: (order(i), 0, 0))
    return pl.pallas_call(body, grid=(nb,), in_specs=[spec, spec], out_specs=[spec, spec],
                          out_shape=[_sds(a.shape, F32)] * 2, scratch_shapes=[pltpu.VMEM((r, l), F32)], name=name,
                          compiler_params=_params(("arbitrary",)))(a, b)


def _scan_bwd(name, a, dh, hp, nc, reverse):
    t_all, r, l = a.shape
    tb = SCAN_BLOCK
    nb = t_all // tb
    primal = _scan_order(nb, nc, reverse)

    def order(i):
        return primal(nb - 1 - i)

    def body(a_ref, dh_ref, hp_ref, da_ref, db_ref, carry):
        @pl.when(pl.program_id(0) == 0)
        def _():
            carry[...] = jnp.zeros_like(carry)

        def step(s, cr):
            t = s if reverse else tb - 1 - s
            lam = dh_ref[t] + cr
            db_ref[t] = lam
            da_ref[t] = lam * hp_ref[t]
            return a_ref[t] * lam

        carry[...] = lax.fori_loop(0, tb, step, carry[...], unroll=8)

    spec = pl.BlockSpec((tb, r, l), lambda i: (order(i), 0, 0))
    return pl.pallas_call(body, grid=(nb,), in_specs=[spec] * 3, out_specs=[spec, spec],
                          out_shape=[_sds(a.shape, F32)] * 2, scratch_shapes=[pltpu.VMEM((r, l), F32)], name=name,
                          compiler_params=_params(("arbitrary",)))(a, dh, hp)


def _shifted(prev, cur, nxt, k, pid, n_ctx_tiles, n_tiles):
    if k == 0:
        return cur
    tm = cur.shape[0]
    row = lax.broadcasted_iota(jnp.int32, cur.shape, 0)
    if k < 0:
        at_start = jnp.logical_or(pid == 0, pid == n_ctx_tiles)
        edge = jnp.where(at_start, 0.0, pltpu.roll(prev, -k, 0))
        return jnp.where(row < -k, edge, pltpu.roll(cur, -k, 0))
    at_end = jnp.logical_or(pid == n_ctx_tiles - 1, pid == n_tiles - 1)
    edge = jnp.where(at_end, 0.0, pltpu.roll(nxt, tm - k, 0))
    return jnp.where(row >= tm - k, edge, pltpu.roll(cur, tm - k, 0))


def _f_conv(pid, xp, xc, xn, w, b, *, n_ctx_tiles, n_tiles):
    y = b
    for j in range(CONV_W):
        y = y + _shifted(xp, xc, xn, j - CONV_W // 2, pid, n_ctx_tiles, n_tiles) * w[j:j + 1]
    return (y,)


def _f_conv_bwd(pid, xp, xc, xn, dp, dc, dn, w, *, n_ctx_tiles, n_tiles):
    dx = jnp.zeros_like(dc)
    dw = []
    for j in range(CONV_W):
        k = j - CONV_W // 2
        dx = dx + _shifted(dp, dc, dn, -k, pid, n_ctx_tiles, n_tiles) * w[j:j + 1]
        dw.append(jnp.sum(dc * _shifted(xp, xc, xn, k, pid, n_ctx_tiles, n_tiles), axis=0, keepdims=True))
    return (dx, jnp.concatenate(dw, axis=0), jnp.sum(dc, axis=0, keepdims=True))


def _mesh_pos():
    return lax.axis_index("x"), lax.axis_index("y"), lax.axis_index("c")


def _remote(src, dst, send_sems, recv_sems, k, to):
    return pltpu.make_async_remote_copy(src_ref=src, dst_ref=dst, send_sem=send_sems.at[k], recv_sem=recv_sems.at[k],
                                        device_id=to, device_id_type=pl.DeviceIdType.MESH)


def _neighbours():
    x, y, c = _mesh_pos()
    return (x, y, c), (x, y, 1 - c), [(1 - x, y), (x, 1 - y), (1 - x, 1 - y)]


def _gather_comm(arrays):
    n = len(arrays)
    per = 7

    def slot(out, blk):
        return out.at[4 * blk[0] + 2 * blk[1] + blk[2]]

    def start(ins, outs, send, recv, local):
        me, sib, chips = _neighbours()
        for ai in range(n):
            pltpu.make_async_copy(ins[ai], slot(outs[ai], me), local.at[ai]).start()
            _remote(ins[ai], slot(outs[ai], me), send, recv, ai * per, sib).start()
            for j, chip in enumerate(chips):
                _remote(ins[ai], slot(outs[ai], me), send, recv, ai * per + 1 + j, (*chip, me[2])).start()

    def finish(ins, outs, send, recv, local):
        me, sib, chips = _neighbours()
        for ai in range(n):
            for j, chip in enumerate(chips):
                blk = slot(outs[ai], (*chip, me[2]))
                _remote(blk, blk, send, recv, ai * per + 1 + j, me).wait_recv()
                _remote(blk, blk, send, recv, ai * per + 4 + j, sib).start()
        for ai in range(n):
            blk = slot(outs[ai], sib)
            _remote(blk, blk, send, recv, ai * per, me).wait_recv()
            for j, chip in enumerate(chips):
                blk = slot(outs[ai], (*chip, 1 - me[2]))
                _remote(blk, blk, send, recv, ai * per + 4 + j, me).wait_recv()
            for k in range(per):
                _remote(ins[ai], slot(outs[ai], me), send, recv, ai * per + k, sib).wait_send()
            pltpu.make_async_copy(ins[ai], slot(outs[ai], me), local.at[ai]).wait()

    return _Comm(arrays, [_sds((N_DEV,) + a.shape, a.dtype) for a in arrays], n * per, start, finish)


def _swap_comm(arrays):
    n = len(arrays)

    def start(ins, outs, send, recv, local):
        me, sib, _ = _neighbours()
        for ai in range(n):
            for q in range(4):
                _remote(ins[ai].at[2 * q + 1 - me[2]], outs[ai].at[q], send, recv, ai * 4 + q, sib).start()

    def finish(ins, outs, send, recv, local):
        me, sib, _ = _neighbours()
        for ai in range(n):
            for q in range(4):
                cp = _remote(ins[ai].at[q], outs[ai].at[q], send, recv, ai * 4 + q, sib)
                cp.wait_recv()
                cp.wait_send()

    return _Comm(arrays, [_sds((4,) + a.shape[1:], a.dtype) for a in arrays], n * 4, start, finish)


def _chips_comm(arrays):
    n = len(arrays)

    def start(ins, outs, send, recv, local):
        me, _, chips = _neighbours()
        mine = 2 * me[0] + me[1]
        for ai in range(n):
            pltpu.make_async_copy(ins[ai].at[mine], outs[ai].at[mine], local.at[ai]).start()
            for j, chip in enumerate(chips):
                _remote(ins[ai].at[2 * chip[0] + chip[1]], outs[ai].at[mine], send, recv, ai * 3 + j, (*chip, me[2])).start()

    def finish(ins, outs, send, recv, local):
        me, _, chips = _neighbours()
        mine = 2 * me[0] + me[1]
        for ai in range(n):
            for j, chip in enumerate(chips):
                theirs = 2 * chip[0] + chip[1]
                cp = _remote(ins[ai].at[theirs], outs[ai].at[theirs], send, recv, ai * 3 + j, (*chip, me[2]))
                cp.wait_recv()
                cp.wait_send()
            pltpu.make_async_copy(ins[ai].at[mine], outs[ai].at[mine], local.at[ai]).wait()

    return _Comm(arrays, [_sds(a.shape, a.dtype) for a in arrays], n * 3, start, finish)


def _run_comm(name, comm):
    n_in, n_out = len(comm.ins), len(comm.out_shapes)

    def body(*refs):
        ins, outs, sems = refs[:n_in], refs[n_in:n_in + n_out], refs[n_in + n_out:]
        comm.start(ins, outs, *sems)
        comm.finish(ins, outs, *sems)

    return pl.pallas_call(body, in_specs=[_ANY] * n_in, out_specs=[_ANY] * n_out, out_shape=comm.out_shapes, name=name,
                          scratch_shapes=comm.scratch(), compiler_params=pltpu.CompilerParams(has_side_effects=True))(*comm.ins)


def _chip_add(name, blocks, theirs, core):
    _, r, c = blocks.shape
    tr = _tile(r, _row_block(c, 2), 16)

    def body(core_ref, a_ref, b_ref, o_ref):
        o_ref[...] = (a_ref[...].astype(F32) + b_ref[...].astype(F32)).astype(o_ref.dtype)

    spec = pl.BlockSpec((None, tr, c), lambda q, i, core_ref: (q, i, 0))
    grid_spec = pltpu.PrefetchScalarGridSpec(
        num_scalar_prefetch=1, grid=(4, r // tr),
        in_specs=[pl.BlockSpec((None, tr, c), lambda q, i, core_ref: (2 * q + core_ref[0], i, 0)), spec], out_specs=spec)
    return pl.pallas_call(body, grid_spec=grid_spec, out_shape=_sds((4, r, c), blocks.dtype), name=name,
                          compiler_params=_params(("parallel", "parallel")))(jnp.reshape(core, (1,)).astype(jnp.int32), blocks, theirs)


def _sum_lead(name, a):
    n, r, c = a.shape
    tr = _tile(r, _row_block(c, 4 * n // 2), 16)

    def body(a_ref, o_ref):
        acc = a_ref[0].astype(F32)
        for j in range(1, n):
            acc = acc + a_ref[j].astype(F32)
        o_ref[...] = acc

    return pl.pallas_call(body, grid=(r // tr,), in_specs=[pl.BlockSpec((n, tr, c), lambda i: (0, i, 0))],
                          out_specs=pl.BlockSpec((tr, c), lambda i: (i, 0)), out_shape=_sds((r, c), F32), name=name,
                          compiler_params=_params(("parallel",)))(a)


def _adam_math(w, g, m, v):
    m = ADAM_B1 * m + (1.0 - ADAM_B1) * g
    v = ADAM_B2 * v + (1.0 - ADAM_B2) * (g * g)
    m_hat = m / (1.0 - ADAM_B1 ** ADAM_STEP)
    v_hat = v / (1.0 - ADAM_B2 ** ADAM_STEP)
    delta = -ADAM_LR * (m_hat / (jnp.sqrt(v_hat) + ADAM_EPS) + ADAM_WD * w)
    return delta, m, v


def _adam_recv(name, recvs, w, m, v):
    nl = len(recvs)
    n, rl, c = recvs[0].shape
    tr = _tile(rl, _row_block(c, 4), 16)
    per = rl // tr

    def body(*refs):
        g_refs = refs[:nl]
        w_ref, m_ref, v_ref, go_ref, d_ref, mo_ref, vo_ref = refs[nl:]
        for layer in range(nl):
            @pl.when(pl.program_id(0) == layer)
            def _(g_ref=g_refs[layer]):
                g = g_ref[0].astype(F32)
                for j in range(1, n):
                    g = g + g_ref[j].astype(F32)
                delta, m2, v2 = _adam_math(w_ref[...], g, m_ref[...], v_ref[...])
                go_ref[...] = g
                d_ref[...] = delta
                mo_ref[...] = m2
                vo_ref[...] = v2

    g_specs = [pl.BlockSpec((n, tr, c), lambda l, i, layer=layer: (0, jnp.where(l == layer, i, 0), 0)) for layer in range(nl)]
    spec = pl.BlockSpec((tr, c), lambda l, i: (l * per + i, 0))
    return pl.pallas_call(body, grid=(nl, per), in_specs=g_specs + [spec] * 3, out_specs=[spec] * 4,
                          out_shape=[_sds((nl * rl, c), F32)] * 4, name=name,
                          compiler_params=_params(("arbitrary", "arbitrary")))(*recvs, w, m, v)


def _adam_f32(name, g, w, m, v, comm=None):
    r, c = g.shape
    tr = _tile(r, _row_block(c, 4), 8)
    steps = r // tr

    def body(*refs):
        (g_ref, w_ref, m_ref, v_ref), c_ins, (d_ref, mo_ref, vo_ref), c_outs, _, c_sems = _split_refs(refs, 4, 3, 0, comm)
        if comm is not None:
            @pl.when(pl.program_id(0) == 0)
            def _():
                comm.start(c_ins, c_outs, *c_sems)

        delta, m2, v2 = _adam_math(w_ref[...], g_ref[...], m_ref[...], v_ref[...])
        d_ref[...] = delta
        mo_ref[...] = m2
        vo_ref[...] = v2
        if comm is not None:
            @pl.when(pl.program_id(0) == steps - 1)
            def _():
                comm.finish(c_ins, c_outs, *c_sems)

    spec = pl.BlockSpec((tr, c), lambda i: (i, 0))
    c_in, c_out, c_scr = (comm.ins, comm.out_shapes, comm.scratch()) if comm is not None else ([], [], [])
    return pl.pallas_call(body, grid=(steps,), in_specs=[spec] * 4 + [_ANY] * len(c_in), out_specs=[spec] * 3 + [_ANY] * len(c_out),
                          out_shape=[_sds((r, c), F32)] * 3 + list(c_out), scratch_shapes=c_scr, name=name,
                          compiler_params=_params(("arbitrary",)))(g, w, m, v, *c_in)


def _mod_fwd(c16, w_mod, b_loc):
    nl, d, n6 = w_mod.shape
    tn = _tile(n6, 512, 128)

    def body(c_ref, w_ref, b_ref, o_ref):
        cv = c_ref[...]
        s = cv * _sigmoid(cv)
        o_ref[0] = jnp.dot(s, w_ref[0], precision=HIGHEST, preferred_element_type=F32) + b_ref[0]

    return pl.pallas_call(
        body, grid=(nl, n6 // tn),
        in_specs=[pl.BlockSpec((MOD_ROWS, d), lambda i, j: (0, 0)), pl.BlockSpec((1, d, tn), lambda i, j: (i, 0, j)),
                  pl.BlockSpec((1, 1, tn), lambda i, j: (i, 0, j))],
        out_specs=pl.BlockSpec((1, MOD_ROWS, tn), lambda i, j: (i, 0, j)), out_shape=_sds((nl, MOD_ROWS, n6), F32),
        name="mod_fwd", compiler_params=_params(("parallel", "parallel")))(c16, w_mod, b_loc)


def _mod_bwd(c16, w_mod, dmod_loc):
    nl, d, n6 = w_mod.shape
    tn = _tile(n6, 512, 128)

    def body(c_ref, w_ref, dm_ref, dw_ref, ds_ref):
        @pl.when(jnp.logical_and(pl.program_id(0) == 0, pl.program_id(1) == 0))
        def _():
            ds_ref[...] = jnp.zeros_like(ds_ref)

        cv = c_ref[...]
        s = cv * _sigmoid(cv)
        dm = dm_ref[0]
        dw_ref[0] = lax.dot_general(s, dm, (((0,), (0,)), ((), ())), precision=HIGHEST, preferred_element_type=F32)
        ds_ref[...] += lax.dot_general(dm, w_ref[0], (((1,), (1,)), ((), ())), precision=HIGHEST, preferred_element_type=F32)

    return pl.pallas_call(
        body, grid=(nl, n6 // tn),
        in_specs=[pl.BlockSpec((MOD_ROWS, d), lambda i, j: (0, 0)), pl.BlockSpec((1, d, tn), lambda i, j: (i, 0, j)),
                  pl.BlockSpec((1, MOD_ROWS, tn), lambda i, j: (i, 0, j))],
        out_specs=[pl.BlockSpec((1, d, tn), lambda i, j: (i, 0, j)), pl.BlockSpec((MOD_ROWS, d), lambda i, j: (0, 0))],
        out_shape=[_sds((nl, d, n6), F32), _sds((MOD_ROWS, d), F32)], name="mod_bwd",
        compiler_params=_params(("arbitrary", "arbitrary")))(c16, w_mod, dmod_loc)


def _pack(parts):
    flat = [p.reshape(-1).astype(F32) for p in parts]
    offs = np.cumsum([0] + [f.shape[0] for f in flat])
    total = int(offs[-1])
    unit = (PACK_ROWS if total > PACK_ROWS * LANES else 8) * LANES
    padded = -(-total // unit) * unit
    slab = jnp.concatenate(flat + [jnp.zeros((padded - total,), F32)])
    return slab.reshape(padded // LANES, LANES), [int(o) for o in offs]


def _unshard_cols(seg, lead):
    n = seg.shape[1] // int(np.prod(lead)) if lead else seg.shape[1]
    a = seg.reshape((N_DEV,) + tuple(lead) + (n,))
    a = jnp.moveaxis(a, 0, len(lead))
    return a.reshape(tuple(lead) + (N_DEV * n,))


def _my_cols(a, me, n):
    start = (0,) * (a.ndim - 1) + (me * n,)
    return lax.dynamic_slice(a, start, a.shape[:-1] + (n,))


def _rope_tables(seq, n_ctx):
    rows = seq // GRID_W
    r_idx, c_idx = jnp.meshgrid(jnp.arange(rows), jnp.arange(GRID_W), indexing='ij')
    r_idx = r_idx.reshape(-1).astype(F32)
    c_idx = c_idx.reshape(-1).astype(F32)
    pairs = HEAD_DIM // 4
    freqs = ROPE_THETA ** (-jnp.arange(pairs, dtype=F32) / pairs)
    ang_r, ang_c = r_idx[:, None] * freqs, c_idx[:, None] * freqs
    cos = jnp.concatenate([jnp.cos(ang_r)] * 2 + [jnp.cos(ang_c)] * 2, axis=1)
    sin = jnp.concatenate([-jnp.sin(ang_r), jnp.sin(ang_r), -jnp.sin(ang_c), jnp.sin(ang_c)], axis=1)
    cos = jnp.concatenate([jnp.ones((n_ctx, HEAD_DIM), F32), cos], axis=0)
    sin = jnp.concatenate([jnp.zeros((n_ctx, HEAD_DIM), F32), sin], axis=0)
    lane = np.arange(HEAD_DIM)
    partner = np.where(lane % (2 * pairs) < pairs, lane + pairs, lane - pairs)
    perm = np.zeros((HEAD_DIM, HEAD_DIM), np.float32)
    perm[partner, lane] = 1.0
    return cos, sin, jnp.asarray(perm)


def kernel(x, c, ctx, c_ctx, w_mod, b_mod, norm_g, w_ff_in, w_ff_out, ar_w_in, ar_q_g, ar_k_g, ar_conv_w, ar_conv_b, ar_wa, ar_ba, ar_wx, ar_bx, ar_lambda, ar_w_out, gm_w_in, gm_b_in, gm_v_g, gm_v_b, gm_w_sp, gm_b_sp, gm_w_out, loss_target, m_c_ctx, m_w_mod, m_b_mod, m_norm_g, m_w_ff_in, m_w_ff_out, m_ar_w_in, m_ar_q_g, m_ar_k_g, m_ar_conv_w, m_ar_conv_b, m_ar_wa, m_ar_ba, m_ar_wx, m_ar_bx, m_ar_lambda, m_ar_w_out, m_gm_w_in, m_gm_b_in, m_gm_v_g, m_gm_v_b, m_gm_w_sp, m_gm_b_sp, m_gm_w_out, v_c_ctx, v_w_mod, v_b_mod, v_norm_g, v_w_ff_in, v_w_ff_out, v_ar_w_in, v_ar_q_g, v_ar_k_g, v_ar_conv_w, v_ar_conv_b, v_ar_wa, v_ar_ba, v_ar_wx, v_ar_bx, v_ar_lambda, v_ar_w_out, v_gm_w_in, v_gm_b_in, v_gm_v_g, v_gm_v_b, v_gm_w_sp, v_gm_b_sp, v_gm_w_out):
    given = dict(locals())
    wts = {n: given[n] for n in WEIGHTS}
    mom1 = {n: given["m_" + n] for n in WEIGHTS}
    mom2 = {n: given["v_" + n] for n in WEIGHTS}

    xi, yi, ci = _mesh_pos()
    me = 4 * xi + 2 * yi + ci

    seq, d = x.shape[1], x.shape[2]
    n_ctx = ctx.shape[1]
    t_all = n_ctx + seq
    n_layers = w_mod.shape[0]
    assert n_layers == 2 and ar_w_in.shape[0] == 1 and gm_w_in.shape[0] == 1
    d_ff = w_ff_in.shape[2] * N_DEV
    attn_w, kv_w = N_HEADS * HEAD_DIM, N_KV_HEADS * HEAD_DIM
    rnn_blocks = ar_wa.shape[2]
    d_rnn = rnn_blocks * RNN_BLOCK_W
    gm_groups = gm_w_sp.shape[1]
    d_gm = gm_groups * GM_GROUP_W
    ar_in = ar_w_in.shape[2] * N_DEV
    n6 = w_mod.shape[2]
    tm, tmb = ROW_TILE, ROW_TILE_BWD
    assert attn_w == d_rnn and ar_in == 3 * attn_w + 2 * kv_w and (3 * attn_w) % (2 * kv_w) == 0
    assert n_ctx % tm == 0 and seq % tm == 0 and n_ctx % SCAN_BLOCK == 0 and seq % SCAN_BLOCK == 0 and tm % CHUNK == 0
    nct, nctb = n_ctx // tm, n_ctx // tmb
    kv_blk = (3 * attn_w) // (2 * kv_w)
    lr = d_rnn // LANES

    x2, ctx2, tgt = x[0], ctx[0], loss_target[0]

    def cols_full(g):
        return jnp.moveaxis(g, 0, 1).reshape(g.shape[1], N_DEV * g.shape[2])

    small0, off0 = _pack([c[0], norm_g, ar_conv_w[0], ar_ba[0], ar_bx[0], ar_lambda[0], gm_b_in[0], gm_v_g[0], gm_v_b[0]])
    g_ar_in, gs0 = _run_comm("gather_first", _gather_comm([ar_w_in[0].astype(BF16), small0]))
    gs0 = gs0.reshape(N_DEV, -1)
    w_in = cols_full(g_ar_in)
    split = [attn_w, attn_w + 2 * kv_w, attn_w + 2 * kv_w + d_rnn]
    w_in = jnp.concatenate([w_in[:, :split[0]], w_in[:, split[1]:], w_in[:, split[0]:split[1]]], axis=1)
    w1, w2 = [None] * n_layers, [None] * n_layers

    def seg0(k):
        return gs0[:, off0[k]:off0[k + 1]]

    c_all = seg0(0)
    norm_full = _unshard_cols(seg0(1), (n_layers, 4))
    conv_w = _unshard_cols(seg0(2), (CONV_W,))
    ba, bx, lam = (_unshard_cols(seg0(k), (2,)) for k in (3, 4, 5))
    gm_b_in_f = seg0(6).reshape(1, 2 * d_gm)
    gm_vg, gm_vb = seg0(7).reshape(1, d_gm), seg0(8).reshape(1, d_gm)

    c16 = jnp.concatenate([c_all, c_ctx[None], jnp.zeros((MOD_ROWS - N_DEV - 1, d), F32)], axis=0)
    b_loc = _my_cols(b_mod, me, n6)[:, None, :]
    mod_loc = _mod_fwd(c16, w_mod, b_loc)
    (g_mod,) = _run_comm("gather_mod", _gather_comm([mod_loc]))
    mod_all = jnp.moveaxis(g_mod, 0, 2).reshape(n_layers, MOD_ROWS, N_DEV * n6)
    ml = lax.dynamic_index_in_dim(mod_all, me, axis=1, keepdims=False).reshape(n_layers, 6, d)
    mc = mod_all[:, N_DEV].reshape(n_layers, 6, d)

    def row(a, *idx):
        return a[idx][None]

    cos, sin, perm = _rope_tables(seq, n_ctx)
    wa3 = ar_wa[0].reshape(2 * rnn_blocks, RNN_BLOCK_W, RNN_BLOCK_W)
    wx3 = ar_wx[0].reshape(2 * rnn_blocks, RNN_BLOCK_W, RNN_BLOCK_W)
    conv_b = ar_conv_b
    q_g, k_g = ar_q_g, ar_k_g
    w_sp = gm_w_sp[0]
    bsp_t = jnp.pad(gm_b_sp[0].T, ((0, 0), (0, LANES - gm_groups)))
    expand = np.zeros((LANES, d_gm), np.float32)
    for g in range(gm_groups):
        expand[g, g * GM_GROUP_W:(g + 1) * GM_GROUP_W] = 1.0
    expand = jnp.asarray(expand)

    def relu2(acc):
        r = jnp.maximum(acc, 0.0)
        return (r * r,)

    def relu2_bwd(acc, act):
        return (acc * (2.0 * jnp.sqrt(act.astype(F32))),)

    def ff_in_shard(i):
        return w_ff_in[i].astype(BF16)

    def ff_out_shard(i):
        return w_ff_out[i].astype(BF16)

    tokens = [_t(ctx2), _t(x2, -n_ctx)]
    pre0_args = [row(norm_full, 0, 0), row(mc, 0, 0), row(mc, 0, 1), row(ml, 0, 0), row(ml, 0, 1)]
    f_pre0 = functools.partial(_f_pre_ctx, n_ctx_tiles=nct)
    (h0,) = _rowwise("pre0", f_pre0, t_all, tm, tokens, pre0_args, [(d, BF16)])
    tm_tok = _tile(t_all, 640, 16)
    proj, (g_gm_in,) = _matmul("ar_in", h0, w_in, tm=tm_tok, comm=_gather_comm([gm_w_in[0].astype(BF16)]))
    f_qkv = functools.partial(_f_qkv, nh=N_HEADS, nkv=N_KV_HEADS)
    qkv_tiled = [_t(proj, 0, 0, attn_w), _t(proj, 0, kv_blk, 2 * kv_w), _t(cos), _t(sin)]
    q_r, kv_r = _rowwise("qkv", f_qkv, t_all, tm, qkv_tiled, [q_g, k_g, perm], [(attn_w, BF16), (2 * kv_w, BF16)])
    attn_o, (g_ar_out, g_ff_in0) = _attn_fwd(q_r, kv_r, nct, tm,
                                             comm=_gather_comm([ar_w_out[0].astype(BF16), ff_in_shard(0)]))
    w_out = g_ar_out.reshape(attn_w + d_rnn, d)
    w1[0] = cols_full(g_ff_in0)

    def with_neighbours(a, t, col_blk=0, width=None):
        return [_t(a, -t, col_blk, width), _t(a, 0, col_blk, width), _t(a, t, col_blk, width)]

    f_conv = functools.partial(_f_conv, n_ctx_tiles=nct, n_tiles=t_all // tm)
    (xc,) = _rowwise("conv", f_conv, t_all, tm, with_neighbours(proj, tm, 1, d_rnn), [conv_w, conv_b], [(d_rnn, F32)])
    f_gates = functools.partial(_f_gates, nb=rnn_blocks)
    gate_full = [wa3, ba, wx3, bx, lam]
    a_f, b_f, a_b, b_b, g_gm_out = _rowwise("gates", f_gates, t_all, tm, [_t(xc)], gate_full, [(d_rnn, F32)] * 4,
                                            comm=_gather_comm([gm_w_out[0].astype(BF16)]))

    def to3(a):
        return a.reshape(a.shape[0], lr, LANES)

    nc_scan = n_ctx // SCAN_BLOCK
    h_f, hp_f = _scan_fwd("scan_f", to3(a_f), to3(b_f), nc_scan, False)
    h_b, hp_b = _scan_fwd("scan_b", to3(a_b), to3(b_b), nc_scan, True)
    h_f2, h_b2 = h_f.reshape(t_all, d_rnn), h_b.reshape(t_all, d_rnn)
    rnn_tiled = [_t(h_f2, n_ctx), _t(h_b2, n_ctx), _t(proj, n_ctx, 2, d_rnn)]
    (rnn_o,) = _rowwise("rnn_out", _f_rnnout, seq, tm, rnn_tiled, [], [(d_rnn, BF16)])
    ar = jnp.concatenate([attn_o, rnn_o], axis=1)
    o0 = _matmul("ar_out", ar, w_out)
    mid0_args = [row(norm_full, 0, 1), row(ml, 0, 2), row(norm_full, 0, 2), row(ml, 0, 3), row(ml, 0, 4)]
    x1, h2_0 = _rowwise("mid0", _f_mid, seq, tm, [_t(x2), _t(o0)], mid0_args, [(d, F32), (d, BF16)])
    act0, (g_ff_out0,) = _matmul("ff_in_0", h2_0, w1[0], outs=(BF16,), epilogue=relu2, comm=_gather_comm([ff_out_shard(0)]))
    w2[0] = g_ff_out0.reshape(d_ff, d)
    w_gi, w_go = cols_full(g_gm_in), g_gm_out.reshape(d_gm, d)
    m0, (g_ff_in1,) = _matmul("ff_out_0", act0, w2[0], tm=MM_TILE_M // 2, tk=2 * MM_TILE_K, comm=_gather_comm([ff_in_shard(1)]))
    w1[1] = cols_full(g_ff_in1)
    post0_args = [row(norm_full, 0, 3), row(ml, 0, 5)]
    (x2l,) = _rowwise("post0", _f_post, seq, tm, [_t(x1), _t(m0)], post0_args, [(d, F32)])

    pre1_args = [row(norm_full, 1, 0), row(ml, 1, 0), row(ml, 1, 1)]
    (h1,) = _rowwise("pre1", _f_pre, seq, tm, [_t(x2l)], pre1_args, [(d, BF16)])
    zg = _matmul("gm_in", h1, w_gi)
    f_gm = functools.partial(_f_gm, n_chunks=tmb // CHUNK, groups=gm_groups)
    gm_full = [gm_b_in_f[:, :d_gm], gm_b_in_f[:, d_gm:], gm_vg, gm_vb, w_sp, bsp_t, expand]
    gm_tiled = [_t(zg, 0, 0, d_gm), _t(zg, 0, 1, d_gm)]
    (gmix,) = _rowwise("gm_mix", f_gm, seq, tmb, gm_tiled, gm_full, [(d_gm, BF16)])
    o1 = _matmul("gm_out", gmix, w_go)
    mid1_args = [row(norm_full, 1, 1), row(ml, 1, 2), row(norm_full, 1, 2), row(ml, 1, 3), row(ml, 1, 4)]
    x3, h2_1 = _rowwise("mid1", _f_mid, seq, tm, [_t(x2l), _t(o1)], mid1_args, [(d, F32), (d, BF16)])
    act1, (g_ff_out1,) = _matmul("ff_in_1", h2_1, w1[1], outs=(BF16,), epilogue=relu2,
                                       comm=_gather_comm([ff_out_shard(1)]))
    w2[1] = g_ff_out1.reshape(d_ff, d)
    m1 = _matmul("ff_out_1", act1, w2[1], tm=MM_TILE_M // 2, tk=2 * MM_TILE_K)
    post1_args = [row(norm_full, 1, 3), row(ml, 1, 5)]

    def f_loss(pid, xv, ov, tv, g, gate):
        (y,), vjp = jax.vjp(lambda *a: _f_post(pid, *a), xv, ov, g, gate)
        err = y - tv
        part = 0.5 * jnp.sum(err * err) / d
        dxv, dov, dg, dgate = vjp((err / d,))
        return (dxv, dov, jnp.full((8, LANES), part, F32), dg, dgate)

    dx3, dm1, loss_acc, dg_last, dgate_last = _rowwise("loss", f_loss, seq, tmb, [_t(x3), _t(m1), _t(tgt)], post1_args,
                                                       [(d, F32), (d, BF16)], [(8, LANES), (1, d), (1, d)])
    loss = lax.psum(loss_acc[0, 0], ("x", "y", "c"))

    d_norm = [[None] * 4 for _ in range(n_layers)]
    d_ml = [[None] * 6 for _ in range(n_layers)]
    recv = {}

    def cols_blocks(g):
        return jnp.moveaxis(g.reshape(g.shape[0], N_DEV, g.shape[1] // N_DEV), 1, 0)

    def rows_blocks(g):
        return g.reshape(N_DEV, g.shape[0] // N_DEV, g.shape[1])

    chip_sums = {}

    def chip_add(key, blocks, theirs):
        chip_sums[key] = _chip_add(key + "_add", blocks, theirs, ci)

    def mlp_bwd(i, dm, act, h2, first_comm=None):
        dw2 = _matmul(f"ff_out_dw_{i}", act, dm, ta=True, outs=(BF16,), comm=first_comm)
        dw2, carried = dw2 if first_comm is not None else (dw2, ())
        blk2 = rows_blocks(dw2)
        dz, (theirs,) = _matmul(f"ff_out_dx_{i}", dm, w2[i], tb=True, outs=(BF16,), extras=(act,), epilogue=relu2_bwd,
                                comm=_swap_comm([blk2]))
        chip_add(f"ff_out_{i}", blk2, theirs)
        blk1, (recv[f"ff_out_{i}"],) = _matmul(f"ff_in_dw_{i}", h2, dz, ta=True, outs=(BF16,), col_blocks=N_DEV,
                                              comm=_chips_comm([chip_sums[f"ff_out_{i}"]]))
        dh2, (theirs,) = _matmul(f"ff_in_dx_{i}", dz, w1[i], tb=True, tm=MM_TILE_M // 2, tk=2 * MM_TILE_K,
                                 comm=_swap_comm([blk1]))
        chip_add(f"ff_in_{i}", blk1, theirs)
        return dh2, carried

    def mid_bwd(i, xin, o, args, dx1, dh2):
        res = _rowwise(f"mid_bwd{i}", _bwd_of(_f_mid, 2, 2, (0, 1, 2, 3, 4, 5, 6)), seq, tmb,
                       [_t(xin), _t(o), _t(dx1), _t(dh2)], args, [(d, F32), (d, BF16)], [(1, d)] * 5)
        d_norm[i][1], d_ml[i][2], d_norm[i][2], d_ml[i][3], d_ml[i][4] = res[2:]
        return res[0], res[1]

    d_norm[1][3], d_ml[1][5] = dg_last, dgate_last
    dh2_1, _ = mlp_bwd(1, dm1, act1, h2_1)
    dx2a, do1 = mid_bwd(1, x2l, o1, mid1_args, dx3, dh2_1)
    blk_go = rows_blocks(_matmul("gm_out_dw", gmix, do1, ta=True, outs=(BF16,)))
    dgmix, (theirs,) = _matmul("gm_out_dx", do1, w_go, tb=True, comm=_swap_comm([blk_go]))
    chip_add("gm_out", blk_go, theirs)
    gm_res = _rowwise("gm_mix_bwd", _bwd_of(f_gm, 2, 1, (0, 1, 2, 3, 4, 5, 6, 7)), seq, tmb,
                      gm_tiled + [_t(dgmix)], gm_full, [(d_gm, BF16), (d_gm, BF16)],
                      [(1, d_gm)] * 4 + [w_sp.shape, bsp_t.shape])
    dzg = jnp.concatenate([gm_res[0], gm_res[1]], axis=1)
    g_gm_b_in = jnp.concatenate([gm_res[2], gm_res[3]], axis=1)
    g_gm_vg, g_gm_vb, g_w_sp = gm_res[4], gm_res[5], gm_res[6]
    g_b_sp = gm_res[7][:, :gm_groups].T
    dh1, (recv["gm_out"],) = _matmul("gm_in_dx", dzg, w_gi, tb=True, comm=_chips_comm([chip_sums["gm_out"]]))
    blk_gi = _matmul("gm_in_dw", h1, dzg, ta=True, outs=(BF16,), col_blocks=N_DEV)

    def f_between(pid, xv, dh, dxa, x1v, m0v, g, sh, sc, g3, gate):
        dxv, dg, dsh, dsc = _bwd_of(_f_pre, 1, 1, (0, 1, 2, 3))(pid, xv, dh, g, sh, sc)
        dx1v, dm0v, dg3, dgate = _bwd_of(_f_post, 2, 1, (0, 1, 2, 3))(pid, x1v, m0v, dxv + dxa, g3, gate)
        return (dx1v, dm0v, dg, dsh, dsc, dg3, dgate)

    res = _rowwise("between_bwd", f_between, seq, tmb, [_t(x2l), _t(dh1), _t(dx2a), _t(x1), _t(m0)], pre1_args + post0_args,
                   [(d, F32), (d, BF16)], [(1, d)] * 5)
    dx1, dm0 = res[0], res[1]
    d_norm[1][0], d_ml[1][0], d_ml[1][1], d_norm[0][3], d_ml[0][5] = res[2:]

    dh2_0, (theirs,) = mlp_bwd(0, dm0, act0, h2_0, first_comm=_swap_comm([blk_gi]))
    chip_add("gm_in", blk_gi, theirs)
    dxa, do0 = mid_bwd(0, x2, o0, mid0_args, dx1, dh2_0)
    blk_out = rows_blocks(_matmul("ar_out_dw", ar, do0, ta=True, outs=(BF16,)))
    d_ar, (theirs,) = _matmul("ar_out_dx", do0, w_out, tb=True, comm=_swap_comm([blk_out]))
    chip_add("ar_out", blk_out, theirs)

    late = ["ff_in_1", "gm_in", "ff_in_0"]
    dq, dkt, dvt, *landed = _attn_bwd(q_r, kv_r, d_ar, nct, tm, comm=_chips_comm([chip_sums[k] for k in late]))
    recv.update(zip(late, landed))
    dkv_all = jnp.concatenate([dkt, dvt], axis=0).T

    def f_qkv_bwd(pid, pq, pkv, cos_t, sin_t, dq_t, dkv_t, *fulls):
        dq_t = jnp.where(pid < nctb, 0.0, dq_t)
        return _bwd_of(f_qkv, 4, 2, (0, 1, 4, 5))(pid, pq, pkv, cos_t, sin_t, dq_t, dkv_t, *fulls)

    qkv_res = _rowwise("qkv_bwd", f_qkv_bwd, t_all, tmb, qkv_tiled + [_t(dq, -n_ctx), _t(dkv_all)],
                       [q_g, k_g, perm], [(attn_w, BF16), (2 * kv_w, BF16)], [q_g.shape, k_g.shape])
    dproj_q, dproj_kv, g_q_g, g_k_g = qkv_res

    rnn_res = _rowwise("rnn_out_bwd", _bwd_of(_f_rnnout, 3, 1, (0, 2)), seq, tmb, rnn_tiled + [_t(d_ar, 0, 1, d_rnn)], [],
                       [(d_rnn, F32), (d_rnn, BF16)])
    zc = jnp.zeros((n_ctx, d_rnn), F32)
    dh_all = to3(jnp.concatenate([zc, rnn_res[0]], axis=0))
    dproj_g = jnp.concatenate([zc.astype(BF16), rnn_res[1]], axis=0)
    da_f, db_f = _scan_bwd("scan_f_bwd", to3(a_f), dh_all, hp_f, nc_scan, False)
    da_b, db_b = _scan_bwd("scan_b_bwd", to3(a_b), dh_all, hp_b, nc_scan, True)
    gate_cts = [_t(a.reshape(t_all, d_rnn)) for a in (da_f, db_f, da_b, db_b)]
    gates_res = _rowwise("gates_bwd", _bwd_of(f_gates, 1, 4, (0, 1, 2, 3, 4, 5)), t_all, min(tmb, GATES_BWD_TILE),
                         [_t(xc)] + gate_cts, gate_full, [(d_rnn, F32)], [wa3.shape, ba.shape, wx3.shape, bx.shape, lam.shape])
    dxc, g_wa, g_ba, g_wx, g_bx, g_lam = gates_res
    f_conv_b = functools.partial(_f_conv_bwd, n_ctx_tiles=nctb, n_tiles=t_all // tmb)
    conv_tiled = with_neighbours(proj, tmb, 1, d_rnn) + with_neighbours(dxc, tmb)
    dproj_x, g_conv_w, g_conv_b = _rowwise("conv_bwd", f_conv_b, t_all, tmb, conv_tiled, [conv_w],
                                           [(d_rnn, BF16)], [conv_w.shape, (1, d_rnn)])
    dproj = jnp.concatenate([dproj_q, dproj_x, dproj_g, dproj_kv], axis=1)
    sq_names = ['ar_wa', 'ar_wx', 'gm_w_sp']

    def stack_sq(parts):
        return jnp.concatenate([p.reshape(-1, LANES) for p in parts], axis=0)

    sq_pack = stack_sq([g_wa, g_wx, g_w_sp]).astype(BF16)
    g_w_in, (g_sq,) = _matmul("ar_in_dw", h0, dproj, ta=True, outs=(BF16,), comm=_gather_comm([sq_pack]))
    g_w_in = jnp.concatenate([g_w_in[:, :attn_w], g_w_in[:, 3 * attn_w:], g_w_in[:, attn_w:3 * attn_w]], axis=1)
    blk_in = cols_blocks(g_w_in)
    dh0, (recv["ar_out"], theirs) = _matmul("ar_in_dx", dproj, w_in, tb=True, tm=tm_tok,
                                            comm=_both(_chips_comm([chip_sums["ar_out"]]), _swap_comm([blk_in])))
    chip_add("ar_in", blk_in, theirs)

    f_pre0b = functools.partial(_f_pre_ctx, n_ctx_tiles=nctb)

    def f_pre0_bwd(pid, xcv, xlv, dh, dxp, g, shc, scc, shl, scl):
        grads = _bwd_of(f_pre0b, 2, 1, (1, 2, 3, 4, 5, 6))(pid, xcv, xlv, dh, g, shc, scc, shl, scl)
        return (grads[0] + dxp,) + tuple(grads[1:])

    res = _rowwise("pre_bwd0", f_pre0_bwd, seq, tmb, tokens + [_t(dh0), _t(dxa, -n_ctx)], pre0_args, [(d, F32)], [(1, d)] * 5,
                   comm=_chips_comm([chip_sums["ar_in"]]), skip_rows=n_ctx)
    grad_x = res[0][None]
    d_norm[0][0], d_mc_shift, d_mc_scale, d_ml[0][0], d_ml[0][1] = res[1:6]
    recv["ar_in"] = res[6]

    z1d = jnp.zeros((1, d), F32)
    dml = jnp.concatenate([jnp.concatenate(r, axis=0)[None] for r in d_ml], axis=0)
    dmc = jnp.concatenate([jnp.concatenate([d_mc_shift, d_mc_scale] + [z1d] * 4, axis=0)[None],
                           jnp.zeros((n_layers - 1, 6, d), F32)], axis=0)
    g_norm = jnp.concatenate([jnp.concatenate(r, axis=0)[None] for r in d_norm], axis=0)
    small_parts = [dmc, g_norm, g_q_g, g_k_g, g_conv_w, g_conv_b, g_ba, g_bx, g_lam, g_gm_b_in, g_gm_vg, g_gm_vb, g_b_sp]
    small2, off2 = _pack([dml] + small_parts)
    (gs2,) = _run_comm("gather_small_grads", _gather_comm([small2]))
    dml_all = gs2.reshape(N_DEV, -1)[:, :off2[1]].reshape(N_DEV, n_layers, 6 * d)
    summed = _sum_lead("sum_small_grads", gs2).reshape(-1)
    summed_sq = _sum_lead("sum_square_grads", g_sq)

    def seg2(k, shape):
        return summed[off2[k + 1]:off2[k + 2]].reshape(shape)

    dmc_sum = seg2(0, (n_layers, 6 * d))
    dmod_rows = jnp.concatenate([jnp.moveaxis(dml_all, 0, 1), dmc_sum[:, None, :],
                                 jnp.zeros((n_layers, MOD_ROWS - N_DEV - 1, 6 * d), F32)], axis=1)
    g_b_mod = _sum_lead("sum_b_mod", jnp.moveaxis(dmod_rows, 1, 0).reshape(MOD_ROWS, n_layers * 6 * d // LANES, LANES))
    g_b_mod = g_b_mod.reshape(n_layers, 6 * d)
    g_w_mod, ds16 = _mod_bwd(c16, w_mod, _my_cols(dmod_rows, me, n6))
    (g_ds,) = _run_comm("gather_dctx", _gather_comm([ds16[N_DEV].reshape(d // LANES, LANES)]))
    ds_ctx = _sum_lead("sum_dctx", g_ds)
    (g_c_ctx,) = _rowwise("silu_bwd", _f_silu_mul, d // LANES, d // LANES, [_t(c_ctx.reshape(d // LANES, LANES)), _t(ds_ctx)],
                          [], [(LANES, F32)])
    g_c_ctx = g_c_ctx.reshape(d)

    grads = {
        'c_ctx': g_c_ctx, 'b_mod': g_b_mod,
        'norm_g': _my_cols(seg2(1, (n_layers, 4, d)), me, d // N_DEV),
        'ar_q_g': seg2(2, ar_q_g.shape), 'ar_k_g': seg2(3, ar_k_g.shape),
        'ar_conv_w': _my_cols(seg2(4, (1, CONV_W, d_rnn)), me, d_rnn // N_DEV),
        'ar_conv_b': seg2(5, ar_conv_b.shape),
        'ar_ba': _my_cols(seg2(6, (1, 2, d_rnn)), me, d_rnn // N_DEV),
        'ar_bx': _my_cols(seg2(7, (1, 2, d_rnn)), me, d_rnn // N_DEV),
        'ar_lambda': _my_cols(seg2(8, (1, 2, d_rnn)), me, d_rnn // N_DEV),
        'gm_b_in': _my_cols(seg2(9, (1, 2 * d_gm)), me, 2 * d_gm // N_DEV),
        'gm_v_g': _my_cols(seg2(10, (1, d_gm)), me, d_gm // N_DEV),
        'gm_v_b': _my_cols(seg2(11, (1, d_gm)), me, d_gm // N_DEV),
        'gm_b_sp': seg2(12, gm_b_sp.shape),
    }
    small_names = list(grads)
    deltas, new_m, new_v = {}, {}, {}

    sq_res = (summed_sq,) + tuple(_adam_f32("adam_square", summed_sq, *[stack_sq([src[n] for n in sq_names])
                                                                        for src in (wts, mom1, mom2)]))
    first = 0
    for n in sq_names:
        rows_n = wts[n].size // LANES
        for dst, slab in zip((grads, deltas, new_m, new_v), sq_res):
            dst[n] = slab[first:first + rows_n].reshape(wts[n].shape)
        first += rows_n

    wp, offw = _pack([wts[n] for n in small_names])
    mp, _ = _pack([mom1[n] for n in small_names])
    vp, _ = _pack([mom2[n] for n in small_names])
    gp, _ = _pack([grads[n] for n in small_names])
    dp, mp2, vp2 = _adam_f32("adam_small", gp, wp, mp, vp)
    for k, n in enumerate(small_names):
        for dst, slab in ((deltas, dp), (new_m, mp2), (new_v, vp2)):
            dst[n] = slab.reshape(-1)[offw[k]:offw[k + 1]].reshape(wts[n].shape)

    grads['w_mod'] = g_w_mod
    dw, mw, vw = _adam_f32("adam_w_mod", g_w_mod.reshape(n_layers * d, n6), w_mod.reshape(n_layers * d, n6),
                           m_w_mod.reshape(n_layers * d, n6), v_w_mod.reshape(n_layers * d, n6))
    deltas['w_mod'], new_m['w_mod'], new_v['w_mod'] = (a.reshape(w_mod.shape) for a in (dw, mw, vw))

    received = {
        'w_ff_in': [recv[f"ff_in_{i}"] for i in range(n_layers)], 'w_ff_out': [recv[f"ff_out_{i}"] for i in range(n_layers)],
        'ar_w_in': [recv["ar_in"]], 'ar_w_out': [recv["ar_out"]], 'gm_w_in': [recv["gm_in"]], 'gm_w_out': [recv["gm_out"]]}
    for n, r in received.items():
        shp = wts[n].shape
        flat = (shp[0] * shp[1], shp[2])
        res = _adam_recv("adam_" + n, r, wts[n].reshape(flat), mom1[n].reshape(flat), mom2[n].reshape(flat))
        grads[n], deltas[n], new_m[n], new_v[n] = (a.reshape(shp) for a in res)

    return (loss, grad_x, *[grads[n] for n in WEIGHTS], *[deltas[n] for n in WEIGHTS],
            *[new_m[n] for n in WEIGHTS], *[new_v[n] for n in WEIGHTS])
```

```python
import functools

import numpy as np
import jax
import jax.numpy as jnp
from jax import lax
from jax.experimental import pallas as pl
from jax.experimental.pallas import tpu as pltpu

F32 = jnp.float32
BF16 = jnp.bfloat16
HIGHEST = lax.Precision.HIGHEST
LOG2_E = 1.4426950408889634
LN_2 = 0.6931471805599453

GRID_W = 64
N_HEADS = 8
N_KV_HEADS = 2
HEAD_DIM = 128
ROPE_THETA = 10000.0
RNN_BLOCK_W = 128
CONV_W = 4
RG_C = 8.0
GM_GROUP_W = 128
CHUNK = 128
EPS = 1e-6
ADAM_LR = 0.001
ADAM_B1 = 0.9
ADAM_B2 = 0.999
ADAM_EPS = 1e-08
ADAM_WD = 0.01
ADAM_STEP = 10

N_DEV = 8
MOD_ROWS = 16
LANES = 128
ROW_TILE = 256
ROW_TILE_BWD = 256
ATTN_BWD_HEADS_PER_STEP = 4
ATTN_HEADS_PER_STEP = 4
GATES_BWD_TILE = 128
SCAN_BLOCK = 256
VMEM_LIMIT = 56 * 1024 * 1024
PACK_ROWS = 512
MM_TILE_M = 1024
MM_TILE_N = 1024
MM_TILE_K = 2048

WEIGHTS = ['c_ctx', 'w_mod', 'b_mod', 'norm_g', 'w_ff_in', 'w_ff_out', 'ar_w_in', 'ar_q_g', 'ar_k_g', 'ar_conv_w',
           'ar_conv_b', 'ar_wa', 'ar_ba', 'ar_wx', 'ar_bx', 'ar_lambda', 'ar_w_out', 'gm_w_in', 'gm_b_in', 'gm_v_g',
           'gm_v_b', 'gm_w_sp', 'gm_b_sp', 'gm_w_out']


def _sds(shape, dtype):
    return jax.ShapeDtypeStruct(tuple(shape), dtype)


def _tile(dim, pref, align):
    t = (min(pref, dim) // align) * align
    while t >= align:
        if dim % t == 0:
            return t
        t -= align
    return dim


def _params(sem):
    return pltpu.CompilerParams(dimension_semantics=sem, vmem_limit_bytes=VMEM_LIMIT)


def _rms(x, g):
    return x * lax.rsqrt(jnp.mean(x * x, axis=-1, keepdims=True) + EPS) * g


def _gelu(x):
    return 0.5 * x * (1.0 + jnp.tanh(0.7978845608028654 * (x + 0.044715 * (x * x * x))))


def _sigmoid(x):
    return 0.5 * (jnp.tanh(0.5 * x) + 1.0)


def _log1p_pos(u):
    small = u < 1e-3
    us = jnp.where(small, u, 0.0)
    return jnp.where(small, us * (1.0 - us * (0.5 - us * (1.0 / 3.0))), jnp.log(1.0 + u))


def _softplus(x):
    return jnp.maximum(x, 0.0) + _log1p_pos(jnp.exp(-jnp.abs(x)))


def _f_pre_ctx(pid, xc, xl, g, shc, scc, shl, scl, *, n_ctx_tiles):
    is_ctx = pid < n_ctx_tiles
    x = jnp.where(is_ctx, xc, xl)
    sh = jnp.where(is_ctx, shc, shl)
    sc = jnp.where(is_ctx, scc, scl)
    return (_rms(x, g) * (1.0 + sc) + sh,)


def _f_pre(pid, x, g, sh, sc):
    return (_rms(x, g) * (1.0 + sc) + sh,)


def _f_mid(pid, x, o, g1, gate, g2, sh, sc):
    x1 = x + gate * _rms(o, g1)
    return (x1, _rms(x1, g2) * (1.0 + sc) + sh)


def _f_post(pid, x, o, g, gate):
    return (x + gate * _rms(o, g),)


def _f_qkv(pid, pq, pkv, cos, sin, q_g, k_g, perm, *, nh, nkv):
    hd = HEAD_DIM

    def norm_rope(xh, g):
        y = _rms(xh, g)
        return y * cos + jnp.dot(y, perm, precision=HIGHEST, preferred_element_type=F32) * sin

    qs = [norm_rope(pq[:, h * hd:(h + 1) * hd], q_g) * (HEAD_DIM ** -0.5 * LOG2_E) for h in range(nh)]
    ks = [norm_rope(pkv[:, h * hd:(h + 1) * hd], k_g) for h in range(nkv)]
    return (jnp.concatenate(qs, axis=1), jnp.concatenate(ks + [pkv[:, nkv * hd:]], axis=1))


def _f_gates(pid, x, wa, ba, wx, bx, lam, *, nb):
    w = RNN_BLOCK_W
    cols = [[] for _ in range(4)]
    for n in range(nb):
        blk = slice(n * w, (n + 1) * w)
        xn = x[:, blk]
        xb = xn.astype(BF16)
        for d in range(2):
            r = _sigmoid(jnp.dot(xb, wa[d * nb + n].astype(BF16), preferred_element_type=F32) + ba[d:d + 1, blk])
            i = _sigmoid(jnp.dot(xb, wx[d * nb + n].astype(BF16), preferred_element_type=F32) + bx[d:d + 1, blk])
            log_a = -RG_C * r * _softplus(-lam[d:d + 1, blk])
            a = jnp.exp(log_a)
            cols[2 * d].append(a)
            cols[2 * d + 1].append(jnp.sqrt(-jnp.tanh(log_a) * (a * a + 1.0)) * (i * xn))
    return tuple(jnp.concatenate(c, axis=1) for c in cols)


def _f_rnnout(pid, hf, hb, gr):
    return ((hf + hb) * _gelu(gr),)


def _f_gm(pid, zu, zv, bu, bv, v_g, v_b, w_sp, bsp_t, expand, *, n_chunks, groups):
    u = _gelu(zu + bu)
    v = _gelu(zv + bv)
    mu = jnp.mean(v, axis=-1, keepdims=True)
    vc = v - mu
    v = vc * lax.rsqrt(jnp.mean(vc * vc, axis=-1, keepdims=True) + EPS) * v_g + v_b
    bias = jnp.dot(bsp_t, expand, precision=HIGHEST, preferred_element_type=F32)
    outs = []
    for c in range(n_chunks):
        vch = v[c * CHUNK:(c + 1) * CHUNK]
        cols = [jnp.dot(w_sp[g].astype(BF16), vch[:, g * GM_GROUP_W:(g + 1) * GM_GROUP_W].astype(BF16),
                        preferred_element_type=F32) for g in range(groups)]
        outs.append(u[c * CHUNK:(c + 1) * CHUNK] * (jnp.concatenate(cols, axis=1) + bias))
    return (jnp.concatenate(outs, axis=0),)


def _f_silu_mul(pid, c, d):
    return (d * jax.grad(lambda z: jnp.sum(z * _sigmoid(z)))(c),)


def _bwd_of(fn, n_tiled, n_ct, want):
    def bwd(pid, *args):
        tiles = [t.astype(F32) for t in args[:n_tiled]]
        cts = args[n_tiled:n_tiled + n_ct]
        fulls = list(args[n_tiled + n_ct:])
        outs, vjp = jax.vjp(lambda *a: fn(pid, *a), *tiles, *fulls)
        grads = vjp(tuple(ct.astype(o.dtype) for ct, o in zip(cts, outs)))
        return tuple(grads[i] for i in want)
    return bwd


def _rowwise(name, fn, rows, tm, tiled, full, outs, accs=(), comm=None, skip_rows=0):
    n_t, n_f, n_o, n_a = len(tiled), len(full), len(outs), len(accs)
    assert skip_rows % tm == 0
    skip = skip_rows // tm
    n_tiles = rows // tm + skip

    def body(*refs):
        in_refs, c_ins, res_refs, c_outs, _, c_sems = _split_refs(refs, n_t + n_f, n_o + n_a, 0, comm)
        pid = pl.program_id(0)
        if comm is not None:
            @pl.when(pid == 0)
            def _():
                comm.start(c_ins, c_outs, *c_sems)

        res = fn(pid, *[r[...] for r in in_refs])
        o_refs, a_refs = res_refs[:n_o], res_refs[n_o:]
        for r, v in zip(o_refs, res[:n_o]):
            r[...] = v.astype(r.dtype)
        if n_a:
            @pl.when(pid == 0)
            def _():
                for r in a_refs:
                    r[...] = jnp.zeros_like(r)
            for r, v in zip(a_refs, res[n_o:]):
                r[...] += v.astype(F32)
        if comm is not None:
            @pl.when(pid == n_tiles - 1)
            def _():
                comm.finish(c_ins, c_outs, *c_sems)

    assert all(ro % tm == 0 for (_, ro, _, _) in tiled)
    in_specs = [pl.BlockSpec((tm, w), lambda i, ro=ro // tm, cb=cb, last=a.shape[0] // tm - 1: (jnp.clip(i + ro, 0, last), cb))
                for (a, ro, cb, w) in tiled]
    in_specs += [pl.BlockSpec(a.shape, lambda i, nd=a.ndim: (0,) * nd) for a in full]
    out_shape = [_sds((rows, w), dt) for (w, dt) in outs] + [_sds(s, F32) for s in accs]
    out_specs = [pl.BlockSpec((tm, w), lambda i: (jnp.maximum(i - skip, 0), 0)) for (w, _) in outs]
    out_specs += [pl.BlockSpec(tuple(s), lambda i, nd=len(s): (0,) * nd) for s in accs]
    c_in, c_out, c_scr = (comm.ins, comm.out_shapes, comm.scratch()) if comm is not None else ([], [], [])
    return pl.pallas_call(body, grid=(n_tiles,), in_specs=in_specs + [_ANY] * len(c_in), out_specs=out_specs + [_ANY] * len(c_out),
                          out_shape=out_shape + list(c_out), scratch_shapes=c_scr, name=name,
                          compiler_params=_params(("arbitrary",)))(*[t[0] for t in tiled], *full, *c_in)


def _t(a, row_off=0, col_blk=0, width=None):
    return (a, row_off, col_blk, a.shape[1] if width is None else width)


class _Comm:
    def __init__(self, ins, out_shapes, n_sems, start, finish):
        self.ins, self.out_shapes, self.n_sems, self.start, self.finish = list(ins), list(out_shapes), n_sems, start, finish

    def scratch(self):
        return [pltpu.SemaphoreType.DMA((self.n_sems,)), pltpu.SemaphoreType.DMA((self.n_sems,)),
                pltpu.SemaphoreType.DMA((len(self.ins),))]


_ANY = pl.BlockSpec(memory_space=pl.ANY)


class _SemSlice:
    def __init__(self, ref, first):
        self.ref, self.first = ref, first

    @property
    def at(self):
        return self

    def __getitem__(self, k):
        return self.ref.at[self.first + k]


def _both(*comms):
    comms = [cm for cm in comms if cm is not None]
    if len(comms) <= 1:
        return comms[0] if comms else None

    def parts(ins, outs, send, recv, local):
        i0 = o0 = s0 = 0
        for cm in comms:
            ni, no = len(cm.ins), len(cm.out_shapes)
            yield cm, (ins[i0:i0 + ni], outs[o0:o0 + no], _SemSlice(send, s0), _SemSlice(recv, s0), _SemSlice(local, i0))
            i0, o0, s0 = i0 + ni, o0 + no, s0 + cm.n_sems

    def start(*refs):
        for cm, sub in parts(*refs):
            cm.start(*sub)

    def finish(*refs):
        for cm, sub in parts(*refs):
            cm.finish(*sub)

    return _Comm(sum((cm.ins for cm in comms), []), sum((cm.out_shapes for cm in comms), []),
                 sum(cm.n_sems for cm in comms), start, finish)


def _row_block(cols, itemsize):
    return max(16, (1 << 20) // (cols * itemsize))


def _split_refs(refs, n_in, n_out, n_scratch, comm):
    ci, co, cs = (len(comm.ins), len(comm.out_shapes), 3) if comm is not None else (0, 0, 0)
    cuts = np.cumsum([0, n_in, ci, n_out, co, n_scratch, cs])
    return [refs[cuts[i]:cuts[i + 1]] for i in range(6)]


def _matmul(name, a, b, *, ta=False, tb=False, outs=((F32,)), epilogue=None, extras=(), tm=None, tn=None, tk=None, comm=None,
            col_blocks=None):
    m, k = (a.shape[1], a.shape[0]) if ta else a.shape
    n = b.shape[0] if tb else b.shape[1]
    tm = _tile(m, tm or MM_TILE_M, 128 if ta else 16)
    tn = _tile(n if col_blocks is None else n // col_blocks, tn or MM_TILE_N, 128)
    tk = _tile(k, tk or MM_TILE_K, 128 if not ta else 16)
    ni, nj, nk = m // tm, n // tn, k // tk
    n_e, n_o = len(extras), len(outs)
    dims = (((0 if ta else 1,), (1 if tb else 0,)), ((), ()))

    def body(*refs):
        ins, c_ins, o_refs, c_outs, scratch, c_sems = _split_refs(refs, 2 + n_e, n_o, 1 if nk > 1 else 0, comm)
        a_ref, b_ref, e_refs = ins[0], ins[1], ins[2:]
        i, j, kk = pl.program_id(0), pl.program_id(1), pl.program_id(2)
        if comm is not None:
            @pl.when(jnp.logical_and(jnp.logical_and(i == 0, j == 0), kk == 0))
            def _():
                comm.start(c_ins, c_outs, *c_sems)

        def finish(acc):
            res = (acc,) if epilogue is None else epilogue(acc, *[e[...] for e in e_refs])
            for r, v in zip(o_refs, res):
                r[...] = v.astype(r.dtype)

        prod = lax.dot_general(a_ref[...].astype(BF16), b_ref[...].astype(BF16), dims, preferred_element_type=F32)
        if nk == 1:
            finish(prod)
        else:
            acc = scratch[0]

            @pl.when(kk == 0)
            def _():
                acc[...] = prod

            @pl.when(kk > 0)
            def _():
                acc[...] += prod

            @pl.when(kk == nk - 1)
            def _():
                finish(acc[...])
        if comm is not None:
            @pl.when(jnp.logical_and(jnp.logical_and(i == ni - 1, j == nj - 1), kk == nk - 1))
            def _():
                comm.finish(c_ins, c_outs, *c_sems)

    a_spec = pl.BlockSpec((tk, tm), lambda i, j, kk: (kk, i)) if ta else pl.BlockSpec((tm, tk), lambda i, j, kk: (i, kk))
    b_spec = pl.BlockSpec((tn, tk), lambda i, j, kk: (j, kk)) if tb else pl.BlockSpec((tk, tn), lambda i, j, kk: (kk, j))
    mn_spec = pl.BlockSpec((tm, tn), lambda i, j, kk: (i, j))
    c_in, c_out, c_scr = (comm.ins, comm.out_shapes, comm.scratch()) if comm is not None else ([], [], [])
    if col_blocks is None:
        o_spec, o_shape = mn_spec, (m, n)
    else:
        per = n // col_blocks // tn
        o_spec = pl.BlockSpec((None, tm, tn), lambda i, j, kk: (j // per, i, j % per))
        o_shape = (col_blocks, m, n // col_blocks)
    res = pl.pallas_call(body, grid=(ni, nj, nk), in_specs=[a_spec, b_spec] + [mn_spec] * n_e + [_ANY] * len(c_in),
                         out_specs=[o_spec] * n_o + [_ANY] * len(c_out),
                         out_shape=[_sds(o_shape, dt) for dt in outs] + list(c_out),
                         scratch_shapes=([pltpu.VMEM((tm, tn), F32)] if nk > 1 else []) + c_scr, name=name,
                         compiler_params=_params(("arbitrary", "arbitrary", "arbitrary")))(a, b, *extras, *c_in)
    main = res[0] if n_o == 1 else res[:n_o]
    return main if comm is None else (main, res[n_o:])


def _attn_fwd(q, kv, n_ctx_tiles, tq, comm=None):
    t_all = q.shape[0]
    s_len = t_all - n_ctx_tiles * tq
    hd, groups = HEAD_DIM, N_HEADS // N_KV_HEADS
    hps = ATTN_HEADS_PER_STEP
    assert groups % hps == 0
    gsteps = groups // hps
    nq = s_len // tq

    def body(*refs):
        (q_ref, k_ref, v_ref), c_ins, (o_ref,), c_outs, _, c_sems = _split_refs(refs, 3, 1, 0, comm)
        kh, g, i = pl.program_id(0), pl.program_id(1), pl.program_id(2)
        if comm is not None:
            @pl.when(jnp.logical_and(jnp.logical_and(kh == 0, g == 0), i == 0))
            def _():
                comm.start(c_ins, c_outs, *c_sems)

        kk, vv = k_ref[...], v_ref[...]
        for h in range(hps):
            qh = q_ref[:, h * hd:(h + 1) * hd]
            s = lax.dot_general(qh, kk, (((1,), (1,)), ((), ())), preferred_element_type=F32)
            p = jnp.exp2(s - jnp.max(s, axis=-1, keepdims=True))
            l = jnp.sum(p, axis=-1, keepdims=True)
            o = jnp.dot(p.astype(BF16), vv, preferred_element_type=F32) * (1.0 / l)
            o_ref[:, h * hd:(h + 1) * hd] = o.astype(o_ref.dtype)
        if comm is not None:
            @pl.when(jnp.logical_and(jnp.logical_and(kh == N_KV_HEADS - 1, g == gsteps - 1), i == nq - 1))
            def _():
                comm.finish(c_ins, c_outs, *c_sems)

    c_in, c_out, c_scr = (comm.ins, comm.out_shapes, comm.scratch()) if comm is not None else ([], [], [])
    res = pl.pallas_call(
        body, grid=(N_KV_HEADS, gsteps, nq),
        in_specs=[pl.BlockSpec((tq, hps * hd), lambda kh, g, i: (i + n_ctx_tiles, kh * gsteps + g)),
                  pl.BlockSpec((t_all, hd), lambda kh, g, i: (0, kh)),
                  pl.BlockSpec((t_all, hd), lambda kh, g, i: (0, N_KV_HEADS + kh))] + [_ANY] * len(c_in),
        out_specs=[pl.BlockSpec((tq, hps * hd), lambda kh, g, i: (i, kh * gsteps + g))] + [_ANY] * len(c_out),
        out_shape=[_sds((s_len, N_HEADS * hd), BF16)] + list(c_out), scratch_shapes=c_scr, name="attn_fwd",
        compiler_params=_params(("arbitrary", "arbitrary", "arbitrary")))(q, kv, kv, *c_in)
    return res[0] if comm is None else (res[0], res[1:])


def _attn_bwd(q, kv, d_ar, n_ctx_tiles, tq, comm=None):
    t_all = q.shape[0]
    s_len = t_all - n_ctx_tiles * tq
    hd, groups = HEAD_DIM, N_HEADS // N_KV_HEADS
    hps = ATTN_BWD_HEADS_PER_STEP
    assert groups % hps == 0
    gsteps = groups // hps
    nq = s_len // tq

    def body(*refs):
        (q_ref, k_ref, v_ref, do_ref), c_ins, (dq_ref, dkt_ref, dvt_ref), c_outs, _, c_sems = _split_refs(refs, 4, 3, 0, comm)
        first = jnp.logical_and(pl.program_id(1) == 0, pl.program_id(2) == 0)
        if comm is not None:
            @pl.when(jnp.logical_and(first, pl.program_id(0) == 0))
            def _():
                comm.start(c_ins, c_outs, *c_sems)

        @pl.when(first)
        def _():
            dkt_ref[...] = jnp.zeros_like(dkt_ref)
            dvt_ref[...] = jnp.zeros_like(dvt_ref)

        kk, vv = k_ref[...], v_ref[...]
        for h in range(hps):
            cols = slice(h * hd, (h + 1) * hd)
            qv = q_ref[:, cols]
            s = lax.dot_general(qv, kk, (((1,), (1,)), ((), ())), preferred_element_type=F32)
            p = jnp.exp2(s - jnp.max(s, axis=-1, keepdims=True))
            inv_l = 1.0 / jnp.sum(p, axis=-1, keepdims=True)
            do = (do_ref[:, cols] * inv_l).astype(BF16)
            dp = lax.dot_general(do, vv, (((1,), (1,)), ((), ())), preferred_element_type=F32)
            ds = (p * (dp - jnp.sum(p * dp, axis=-1, keepdims=True) * inv_l)).astype(BF16)
            dq_ref[:, cols] = jnp.dot(ds, kk, preferred_element_type=F32) * LN_2
            dkt_ref[...] += jnp.dot(qv.T, ds, preferred_element_type=F32)
            dvt_ref[...] += jnp.dot(do.T, p.astype(BF16), preferred_element_type=F32)
        last = jnp.logical_and(pl.program_id(1) == gsteps - 1, pl.program_id(2) == nq - 1)

        @pl.when(last)
        def _():
            dkt_ref[...] *= LN_2

        if comm is not None:
            @pl.when(jnp.logical_and(last, pl.program_id(0) == N_KV_HEADS - 1))
            def _():
                comm.finish(c_ins, c_outs, *c_sems)

    c_in, c_out, c_scr = (comm.ins, comm.out_shapes, comm.scratch()) if comm is not None else ([], [], [])
    return pl.pallas_call(
        body, grid=(N_KV_HEADS, gsteps, nq),
        in_specs=[pl.BlockSpec((tq, hps * hd), lambda kh, g, i: (i + n_ctx_tiles, kh * gsteps + g)),
                  pl.BlockSpec((t_all, hd), lambda kh, g, i: (0, kh)),
                  pl.BlockSpec((t_all, hd), lambda kh, g, i: (0, N_KV_HEADS + kh)),
                  pl.BlockSpec((tq, hps * hd), lambda kh, g, i: (i, kh * gsteps + g))] + [_ANY] * len(c_in),
        out_specs=[pl.BlockSpec((tq, hps * hd), lambda kh, g, i: (i, kh * gsteps + g)),
                   pl.BlockSpec((hd, t_all), lambda kh, g, i: (kh, 0)),
                   pl.BlockSpec((hd, t_all), lambda kh, g, i: (kh, 0))] + [_ANY] * len(c_out),
        out_shape=[_sds((s_len, N_HEADS * hd), F32), _sds((N_KV_HEADS * hd, t_all), F32),
                   _sds((N_KV_HEADS * hd, t_all), F32)] + list(c_out),
        scratch_shapes=c_scr, name="attn_bwd",
        compiler_params=_params(("arbitrary", "arbitrary", "arbitrary")))(q, kv, kv, d_ar, *c_in)


def _scan_order(nb, nc, reverse):
    if not reverse:
        return lambda i: i
    return lambda i: jnp.where(i < nc, nc - 1 - i, nb - 1 - (i - nc))


def _scan_fwd(name, a, b, nc, reverse):
    t_all, r, l = a.shape
    tb = SCAN_BLOCK
    nb = t_all // tb
    order = _scan_order(nb, nc, reverse)

    def body(a_ref, b_ref, h_ref, hp_ref, carry):
        @pl.when(pl.program_id(0) == 0)
        def _():
            carry[...] = jnp.zeros_like(carry)

        def step(s, h):
            t = tb - 1 - s if reverse else s
            hp_ref[t] = h
            h = a_ref[t] * h + b_ref[t]
            h_ref[t] = h
            return h

        carry[...] = lax.fori_loop(0, tb, step, carry[...], unroll=8)

    spec = pl.BlockSpec((tb, r, l), lambda i: (order(i), 0, 0))
    return pl.pallas_call(body, grid=(nb,), in_specs=[spec, spec], out_specs=[spec, spec],
                          out_shape=[_sds(a.shape, F32)] * 2, scratch_shapes=[pltpu.VMEM((r, l), F32)], name=name,
                          compiler_params=_params(("arbitrary",)))(a, b)


def _scan_bwd(name, a, dh, hp, nc, reverse):
    t_all, r, l = a.shape
    tb = SCAN_BLOCK
    nb = t_all // tb
    primal = _scan_order(nb, nc, reverse)

    def order(i):
        return primal(nb - 1 - i)

    def body(a_ref, dh_ref, hp_ref, da_ref, db_ref, carry):
        @pl.when(pl.program_id(0) == 0)
        def _():
            carry[...] = jnp.zeros_like(carry)

        def step(s, cr):
            t = s if reverse else tb - 1 - s
            lam = dh_ref[t] + cr
            db_ref[t] = lam
            da_ref[t] = lam * hp_ref[t]
            return a_ref[t] * lam

        carry[...] = lax.fori_loop(0, tb, step, carry[...], unroll=8)

    spec = pl.BlockSpec((tb, r, l), lambda i: (order(i), 0, 0))
    return pl.pallas_call(body, grid=(nb,), in_specs=[spec] * 3, out_specs=[spec, spec],
                          out_shape=[_sds(a.shape, F32)] * 2, scratch_shapes=[pltpu.VMEM((r, l), F32)], name=name,
                          compiler_params=_params(("arbitrary",)))(a, dh, hp)


def _shifted(prev, cur, nxt, k, pid, n_ctx_tiles, n_tiles):
    if k == 0:
        return cur
    tm = cur.shape[0]
    row = lax.broadcasted_iota(jnp.int32, cur.shape, 0)
    if k < 0:
        at_start = jnp.logical_or(pid == 0, pid == n_ctx_tiles)
        edge = jnp.where(at_start, 0.0, pltpu.roll(prev, -k, 0))
        return jnp.where(row < -k, edge, pltpu.roll(cur, -k, 0))
    at_end = jnp.logical_or(pid == n_ctx_tiles - 1, pid == n_tiles - 1)
    edge = jnp.where(at_end, 0.0, pltpu.roll(nxt, tm - k, 0))
    return jnp.where(row >= tm - k, edge, pltpu.roll(cur, tm - k, 0))


def _f_conv(pid, xp, xc, xn, w, b, *, n_ctx_tiles, n_tiles):
    y = b
    for j in range(CONV_W):
        y = y + _shifted(xp, xc, xn, j - CONV_W // 2, pid, n_ctx_tiles, n_tiles) * w[j:j + 1]
    return (y,)


def _f_conv_bwd(pid, xp, xc, xn, dp, dc, dn, w, *, n_ctx_tiles, n_tiles):
    dx = jnp.zeros_like(dc)
    dw = []
    for j in range(CONV_W):
        k = j - CONV_W // 2
        dx = dx + _shifted(dp, dc, dn, -k, pid, n_ctx_tiles, n_tiles) * w[j:j + 1]
        dw.append(jnp.sum(dc * _shifted(xp, xc, xn, k, pid, n_ctx_tiles, n_tiles), axis=0, keepdims=True))
    return (dx, jnp.concatenate(dw, axis=0), jnp.sum(dc, axis=0, keepdims=True))


def _mesh_pos():
    return lax.axis_index("x"), lax.axis_index("y"), lax.axis_index("c")


def _remote(src, dst, send_sems, recv_sems, k, to):
    return pltpu.make_async_remote_copy(src_ref=src, dst_ref=dst, send_sem=send_sems.at[k], recv_sem=recv_sems.at[k],
                                        device_id=to, device_id_type=pl.DeviceIdType.MESH)


def _neighbours():
    x, y, c = _mesh_pos()
    return (x, y, c), (x, y, 1 - c), [(1 - x, y), (x, 1 - y), (1 - x, 1 - y)]


def _gather_comm(arrays):
    n = len(arrays)
    per = 7

    def slot(out, blk):
        return out.at[4 * blk[0] + 2 * blk[1] + blk[2]]

    def start(ins, outs, send, recv, local):
        me, sib, chips = _neighbours()
        for ai in range(n):
            pltpu.make_async_copy(ins[ai], slot(outs[ai], me), local.at[ai]).start()
            _remote(ins[ai], slot(outs[ai], me), send, recv, ai * per, sib).start()
            for j, chip in enumerate(chips):
                _remote(ins[ai], slot(outs[ai], me), send, recv, ai * per + 1 + j, (*chip, me[2])).start()

    def finish(ins, outs, send, recv, local):
        me, sib, chips = _neighbours()
        for ai in range(n):
            for j, chip in enumerate(chips):
                blk = slot(outs[ai], (*chip, me[2]))
                _remote(blk, blk, send, recv, ai * per + 1 + j, me).wait_recv()
                _remote(blk, blk, send, recv, ai * per + 4 + j, sib).start()
        for ai in range(n):
            blk = slot(outs[ai], sib)
            _remote(blk, blk, send, recv, ai * per, me).wait_recv()
            for j, chip in enumerate(chips):
                blk = slot(outs[ai], (*chip, 1 - me[2]))
                _remote(blk, blk, send, recv, ai * per + 4 + j, me).wait_recv()
            for k in range(per):
                _remote(ins[ai], slot(outs[ai], me), send, recv, ai * per + k, sib).wait_send()
            pltpu.make_async_copy(ins[ai], slot(outs[ai], me), local.at[ai]).wait()

    return _Comm(arrays, [_sds((N_DEV,) + a.shape, a.dtype) for a in arrays], n * per, start, finish)


def _swap_comm(arrays):
    n = len(arrays)

    def start(ins, outs, send, recv, local):
        me, sib, _ = _neighbours()
        for ai in range(n):
            for q in range(4):
                _remote(ins[ai].at[2 * q + 1 - me[2]], outs[ai].at[q], send, recv, ai * 4 + q, sib).start()

    def finish(ins, outs, send, recv, local):
        me, sib, _ = _neighbours()
        for ai in range(n):
            for q in range(4):
                cp = _remote(ins[ai].at[q], outs[ai].at[q], send, recv, ai * 4 + q, sib)
                cp.wait_recv()
                cp.wait_send()

    return _Comm(arrays, [_sds((4,) + a.shape[1:], a.dtype) for a in arrays], n * 4, start, finish)


def _chips_comm(arrays):
    n = len(arrays)

    def start(ins, outs, send, recv, local):
        me, _, chips = _neighbours()
        mine = 2 * me[0] + me[1]
        for ai in range(n):
            pltpu.make_async_copy(ins[ai].at[mine], outs[ai].at[mine], local.at[ai]).start()
            for j, chip in enumerate(chips):
                _remote(ins[ai].at[2 * chip[0] + chip[1]], outs[ai].at[mine], send, recv, ai * 3 + j, (*chip, me[2])).start()

    def finish(ins, outs, send, recv, local):
        me, _, chips = _neighbours()
        mine = 2 * me[0] + me[1]
        for ai in range(n):
            for j, chip in enumerate(chips):
                theirs = 2 * chip[0] + chip[1]
                cp = _remote(ins[ai].at[theirs], outs[ai].at[theirs], send, recv, ai * 3 + j, (*chip, me[2]))
                cp.wait_recv()
                cp.wait_send()
            pltpu.make_async_copy(ins[ai].at[mine], outs[ai].at[mine], local.at[ai]).wait()

    return _Comm(arrays, [_sds(a.shape, a.dtype) for a in arrays], n * 3, start, finish)


def _run_comm(name, comm):
    n_in, n_out = len(comm.ins), len(comm.out_shapes)

    def body(*refs):
        ins, outs, sems = refs[:n_in], refs[n_in:n_in + n_out], refs[n_in + n_out:]
        comm.start(ins, outs, *sems)
        comm.finish(ins, outs, *sems)

    return pl.pallas_call(body, in_specs=[_ANY] * n_in, out_specs=[_ANY] * n_out, out_shape=comm.out_shapes, name=name,
                          scratch_shapes=comm.scratch(), compiler_params=pltpu.CompilerParams(has_side_effects=True))(*comm.ins)


def _chip_add(name, blocks, theirs, core):
    _, r, c = blocks.shape
    tr = _tile(r, _row_block(c, 2), 16)

    def body(core_ref, a_ref, b_ref, o_ref):
        o_ref[...] = (a_ref[...].astype(F32) + b_ref[...].astype(F32)).astype(o_ref.dtype)

    spec = pl.BlockSpec((None, tr, c), lambda q, i, core_ref: (q, i, 0))
    grid_spec = pltpu.PrefetchScalarGridSpec(
        num_scalar_prefetch=1, grid=(4, r // tr),
        in_specs=[pl.BlockSpec((None, tr, c), lambda q, i, core_ref: (2 * q + core_ref[0], i, 0)), spec], out_specs=spec)
    return pl.pallas_call(body, grid_spec=grid_spec, out_shape=_sds((4, r, c), blocks.dtype), name=name,
                          compiler_params=_params(("parallel", "parallel")))(jnp.reshape(core, (1,)).astype(jnp.int32), blocks, theirs)


def _sum_lead(name, a):
    n, r, c = a.shape
    tr = _tile(r, _row_block(c, 4 * n // 2), 16)

    def body(a_ref, o_ref):
        acc = a_ref[0].astype(F32)
        for j in range(1, n):
            acc = acc + a_ref[j].astype(F32)
        o_ref[...] = acc

    return pl.pallas_call(body, grid=(r // tr,), in_specs=[pl.BlockSpec((n, tr, c), lambda i: (0, i, 0))],
                          out_specs=pl.BlockSpec((tr, c), lambda i: (i, 0)), out_shape=_sds((r, c), F32), name=name,
                          compiler_params=_params(("parallel",)))(a)


def _adam_math(w, g, m, v):
    m = ADAM_B1 * m + (1.0 - ADAM_B1) * g
    v = ADAM_B2 * v + (1.0 - ADAM_B2) * (g * g)
    m_hat = m / (1.0 - ADAM_B1 ** ADAM_STEP)
    v_hat = v / (1.0 - ADAM_B2 ** ADAM_STEP)
    delta = -ADAM_LR * (m_hat / (jnp.sqrt(v_hat) + ADAM_EPS) + ADAM_WD * w)
    return delta, m, v


def _adam_recv(name, recvs, w, m, v):
    nl = len(recvs)
    n, rl, c = recvs[0].shape
    tr = _tile(rl, _row_block(c, 4), 16)
    per = rl // tr

    def body(*refs):
        g_refs = refs[:nl]
        w_ref, m_ref, v_ref, go_ref, d_ref, mo_ref, vo_ref = refs[nl:]
        for layer in range(nl):
            @pl.when(pl.program_id(0) == layer)
            def _(g_ref=g_refs[layer]):
                g = g_ref[0].astype(F32)
                for j in range(1, n):
                    g = g + g_ref[j].astype(F32)
                delta, m2, v2 = _adam_math(w_ref[...], g, m_ref[...], v_ref[...])
                go_ref[...] = g
                d_ref[...] = delta
                mo_ref[...] = m2
                vo_ref[...] = v2

    g_specs = [pl.BlockSpec((n, tr, c), lambda l, i, layer=layer: (0, jnp.where(l == layer, i, 0), 0)) for layer in range(nl)]
    spec = pl.BlockSpec((tr, c), lambda l, i: (l * per + i, 0))
    return pl.pallas_call(body, grid=(nl, per), in_specs=g_specs + [spec] * 3, out_specs=[spec] * 4,
                          out_shape=[_sds((nl * rl, c), F32)] * 4, name=name,
                          compiler_params=_params(("arbitrary", "arbitrary")))(*recvs, w, m, v)


def _adam_f32(name, g, w, m, v, comm=None):
    r, c = g.shape
    tr = _tile(r, _row_block(c, 4), 8)
    steps = r // tr

    def body(*refs):
        (g_ref, w_ref, m_ref, v_ref), c_ins, (d_ref, mo_ref, vo_ref), c_outs, _, c_sems = _split_refs(refs, 4, 3, 0, comm)
        if comm is not None:
            @pl.when(pl.program_id(0) == 0)
            def _():
                comm.start(c_ins, c_outs, *c_sems)

        delta, m2, v2 = _adam_math(w_ref[...], g_ref[...], m_ref[...], v_ref[...])
        d_ref[...] = delta
        mo_ref[...] = m2
        vo_ref[...] = v2
        if comm is not None:
            @pl.when(pl.program_id(0) == steps - 1)
            def _():
                comm.finish(c_ins, c_outs, *c_sems)

    spec = pl.BlockSpec((tr, c), lambda i: (i, 0))
    c_in, c_out, c_scr = (comm.ins, comm.out_shapes, comm.scratch()) if comm is not None else ([], [], [])
    return pl.pallas_call(body, grid=(steps,), in_specs=[spec] * 4 + [_ANY] * len(c_in), out_specs=[spec] * 3 + [_ANY] * len(c_out),
                          out_shape=[_sds((r, c), F32)] * 3 + list(c_out), scratch_shapes=c_scr, name=name,
                          compiler_params=_params(("arbitrary",)))(g, w, m, v, *c_in)


def _mod_fwd(c16, w_mod, b_loc):
    nl, d, n6 = w_mod.shape
    tn = _tile(n6, 512, 128)

    def body(c_ref, w_ref, b_ref, o_ref):
        cv = c_ref[...]
        s = cv * _sigmoid(cv)
        o_ref[0] = jnp.dot(s, w_ref[0], precision=HIGHEST, preferred_element_type=F32) + b_ref[0]

    return pl.pallas_call(
        body, grid=(nl, n6 // tn),
        in_specs=[pl.BlockSpec((MOD_ROWS, d), lambda i, j: (0, 0)), pl.BlockSpec((1, d, tn), lambda i, j: (i, 0, j)),
                  pl.BlockSpec((1, 1, tn), lambda i, j: (i, 0, j))],
        out_specs=pl.BlockSpec((1, MOD_ROWS, tn), lambda i, j: (i, 0, j)), out_shape=_sds((nl, MOD_ROWS, n6), F32),
        name="mod_fwd", compiler_params=_params(("parallel", "parallel")))(c16, w_mod, b_loc)


def _mod_bwd(c16, w_mod, dmod_loc):
    nl, d, n6 = w_mod.shape
    tn = _tile(n6, 512, 128)

    def body(c_ref, w_ref, dm_ref, dw_ref, ds_ref):
        @pl.when(jnp.logical_and(pl.program_id(0) == 0, pl.program_id(1) == 0))
        def _():
            ds_ref[...] = jnp.zeros_like(ds_ref)

        cv = c_ref[...]
        s = cv * _sigmoid(cv)
        dm = dm_ref[0]
        dw_ref[0] = lax.dot_general(s, dm, (((0,), (0,)), ((), ())), precision=HIGHEST, preferred_element_type=F32)
        ds_ref[...] += lax.dot_general(dm, w_ref[0], (((1,), (1,)), ((), ())), precision=HIGHEST, preferred_element_type=F32)

    return pl.pallas_call(
        body, grid=(nl, n6 // tn),
        in_specs=[pl.BlockSpec((MOD_ROWS, d), lambda i, j: (0, 0)), pl.BlockSpec((1, d, tn), lambda i, j: (i, 0, j)),
                  pl.BlockSpec((1, MOD_ROWS, tn), lambda i, j: (i, 0, j))],
        out_specs=[pl.BlockSpec((1, d, tn), lambda i, j: (i, 0, j)), pl.BlockSpec((MOD_ROWS, d), lambda i, j: (0, 0))],
        out_shape=[_sds((nl, d, n6), F32), _sds((MOD_ROWS, d), F32)], name="mod_bwd",
        compiler_params=_params(("arbitrary", "arbitrary")))(c16, w_mod, dmod_loc)


def _pack(parts):
    flat = [p.reshape(-1).astype(F32) for p in parts]
    offs = np.cumsum([0] + [f.shape[0] for f in flat])
    total = int(offs[-1])
    unit = (PACK_ROWS if total > PACK_ROWS * LANES else 8) * LANES
    padded = -(-total // unit) * unit
    slab = jnp.concatenate(flat + [jnp.zeros((padded - total,), F32)])
    return slab.reshape(padded // LANES, LANES), [int(o) for o in offs]


def _unshard_cols(seg, lead):
    n = seg.shape[1] // int(np.prod(lead)) if lead else seg.shape[1]
    a = seg.reshape((N_DEV,) + tuple(lead) + (n,))
    a = jnp.moveaxis(a, 0, len(lead))
    return a.reshape(tuple(lead) + (N_DEV * n,))


def _my_cols(a, me, n):
    start = (0,) * (a.ndim - 1) + (me * n,)
    return lax.dynamic_slice(a, start, a.shape[:-1] + (n,))


def _rope_tables(seq, n_ctx):
    rows = seq // GRID_W
    r_idx, c_idx = jnp.meshgrid(jnp.arange(rows), jnp.arange(GRID_W), indexing='ij')
    r_idx = r_idx.reshape(-1).astype(F32)
    c_idx = c_idx.reshape(-1).astype(F32)
    pairs = HEAD_DIM // 4
    freqs = ROPE_THETA ** (-jnp.arange(pairs, dtype=F32) / pairs)
    ang_r, ang_c = r_idx[:, None] * freqs, c_idx[:, None] * freqs
    cos = jnp.concatenate([jnp.cos(ang_r)] * 2 + [jnp.cos(ang_c)] * 2, axis=1)
    sin = jnp.concatenate([-jnp.sin(ang_r), jnp.sin(ang_r), -jnp.sin(ang_c), jnp.sin(ang_c)], axis=1)
    cos = jnp.concatenate([jnp.ones((n_ctx, HEAD_DIM), F32), cos], axis=0)
    sin = jnp.concatenate([jnp.zeros((n_ctx, HEAD_DIM), F32), sin], axis=0)
    lane = np.arange(HEAD_DIM)
    partner = np.where(lane % (2 * pairs) < pairs, lane + pairs, lane - pairs)
    perm = np.zeros((HEAD_DIM, HEAD_DIM), np.float32)
    perm[partner, lane] = 1.0
    return cos, sin, jnp.asarray(perm)


def kernel(x, c, ctx, c_ctx, w_mod, b_mod, norm_g, w_ff_in, w_ff_out, ar_w_in, ar_q_g, ar_k_g, ar_conv_w, ar_conv_b, ar_wa, ar_ba, ar_wx, ar_bx, ar_lambda, ar_w_out, gm_w_in, gm_b_in, gm_v_g, gm_v_b, gm_w_sp, gm_b_sp, gm_w_out, loss_target, m_c_ctx, m_w_mod, m_b_mod, m_norm_g, m_w_ff_in, m_w_ff_out, m_ar_w_in, m_ar_q_g, m_ar_k_g, m_ar_conv_w, m_ar_conv_b, m_ar_wa, m_ar_ba, m_ar_wx, m_ar_bx, m_ar_lambda, m_ar_w_out, m_gm_w_in, m_gm_b_in, m_gm_v_g, m_gm_v_b, m_gm_w_sp, m_gm_b_sp, m_gm_w_out, v_c_ctx, v_w_mod, v_b_mod, v_norm_g, v_w_ff_in, v_w_ff_out, v_ar_w_in, v_ar_q_g, v_ar_k_g, v_ar_conv_w, v_ar_conv_b, v_ar_wa, v_ar_ba, v_ar_wx, v_ar_bx, v_ar_lambda, v_ar_w_out, v_gm_w_in, v_gm_b_in, v_gm_v_g, v_gm_v_b, v_gm_w_sp, v_gm_b_sp, v_gm_w_out):
    given = dict(locals())
    wts = {n: given[n] for n in WEIGHTS}
    mom1 = {n: given["m_" + n] for n in WEIGHTS}
    mom2 = {n: given["v_" + n] for n in WEIGHTS}

    xi, yi, ci = _mesh_pos()
    me = 4 * xi + 2 * yi + ci

    seq, d = x.shape[1], x.shape[2]
    n_ctx = ctx.shape[1]
    t_all = n_ctx + seq
    n_layers = w_mod.shape[0]
    assert n_layers == 2 and ar_w_in.shape[0] == 1 and gm_w_in.shape[0] == 1
    d_ff = w_ff_in.shape[2] * N_DEV
    attn_w, kv_w = N_HEADS * HEAD_DIM, N_KV_HEADS * HEAD_DIM
    rnn_blocks = ar_wa.shape[2]
    d_rnn = rnn_blocks * RNN_BLOCK_W
    gm_groups = gm_w_sp.shape[1]
    d_gm = gm_groups * GM_GROUP_W
    ar_in = ar_w_in.shape[2] * N_DEV
    n6 = w_mod.shape[2]
    tm, tmb = ROW_TILE, ROW_TILE_BWD
    assert attn_w == d_rnn and ar_in == 3 * attn_w + 2 * kv_w and (3 * attn_w) % (2 * kv_w) == 0
    assert n_ctx % tm == 0 and seq % tm == 0 and n_ctx % SCAN_BLOCK == 0 and seq % SCAN_BLOCK == 0 and tm % CHUNK == 0
    nct, nctb = n_ctx // tm, n_ctx // tmb
    kv_blk = (3 * attn_w) // (2 * kv_w)
    lr = d_rnn // LANES

    x2, ctx2, tgt = x[0], ctx[0], loss_target[0]

    def cols_full(g):
        return jnp.moveaxis(g, 0, 1).reshape(g.shape[1], N_DEV * g.shape[2])

    small0, off0 = _pack([c[0], norm_g, ar_conv_w[0], ar_ba[0], ar_bx[0], ar_lambda[0], gm_b_in[0], gm_v_g[0], gm_v_b[0]])
    (gs0,) = _run_comm("gather_first", _gather_comm([small0]))
    gs0 = gs0.reshape(N_DEV, -1)
    w1, w2 = [None] * n_layers, [None] * n_layers

    def seg0(k):
        return gs0[:, off0[k]:off0[k + 1]]

    c_all = seg0(0)
    norm_full = _unshard_cols(seg0(1), (n_layers, 4))
    conv_w = _unshard_cols(seg0(2), (CONV_W,))
    ba, bx, lam = (_unshard_cols(seg0(k), (2,)) for k in (3, 4, 5))
    gm_b_in_f = seg0(6).reshape(1, 2 * d_gm)
    gm_vg, gm_vb = seg0(7).reshape(1, d_gm), seg0(8).reshape(1, d_gm)

    c16 = jnp.concatenate([c_all, c_ctx[None], jnp.zeros((MOD_ROWS - N_DEV - 1, d), F32)], axis=0)
    b_loc = _my_cols(b_mod, me, n6)[:, None, :]
    mod_loc = _mod_fwd(c16, w_mod, b_loc)
    (g_mod,) = _run_comm("gather_mod", _gather_comm([mod_loc]))
    mod_all = jnp.moveaxis(g_mod, 0, 2).reshape(n_layers, MOD_ROWS, N_DEV * n6)
    ml = lax.dynamic_index_in_dim(mod_all, me, axis=1, keepdims=False).reshape(n_layers, 6, d)
    mc = mod_all[:, N_DEV].reshape(n_layers, 6, d)

    def row(a, *idx):
        return a[idx][None]

    cos, sin, perm = _rope_tables(seq, n_ctx)
    wa3 = ar_wa[0].reshape(2 * rnn_blocks, RNN_BLOCK_W, RNN_BLOCK_W)
    wx3 = ar_wx[0].reshape(2 * rnn_blocks, RNN_BLOCK_W, RNN_BLOCK_W)
    conv_b = ar_conv_b
    q_g, k_g = ar_q_g, ar_k_g
    w_sp = gm_w_sp[0]
    bsp_t = jnp.pad(gm_b_sp[0].T, ((0, 0), (0, LANES - gm_groups)))
    expand = np.zeros((LANES, d_gm), np.float32)
    for g in range(gm_groups):
        expand[g, g * GM_GROUP_W:(g + 1) * GM_GROUP_W] = 1.0
    expand = jnp.asarray(expand)

    def relu2(acc):
        r = jnp.maximum(acc, 0.0)
        return (r * r,)

    def relu2_bwd(acc, act):
        return (acc * (2.0 * jnp.sqrt(act.astype(F32))),)

    def ff_in_shard(i):
        return w_ff_in[i].astype(BF16)

    def ff_out_shard(i):
        return w_ff_out[i].astype(BF16)

    tokens = [_t(ctx2), _t(x2, -n_ctx)]
    pre0_args = [row(norm_full, 0, 0), row(mc, 0, 0), row(mc, 0, 1), row(ml, 0, 0), row(ml, 0, 1)]
    f_pre0 = functools.partial(_f_pre_ctx, n_ctx_tiles=nct)
    h0, g_ar_in = _rowwise("pre0", f_pre0, t_all, tm, tokens, pre0_args, [(d, BF16)], comm=_gather_comm([ar_w_in[0].astype(BF16)]))
    w_in = cols_full(g_ar_in)
    split = [attn_w, attn_w + 2 * kv_w, attn_w + 2 * kv_w + d_rnn]
    w_in = jnp.concatenate([w_in[:, :split[0]], w_in[:, split[1]:], w_in[:, split[0]:split[1]]], axis=1)
    tm_tok = _tile(t_all, 640, 16)
    proj, (g_gm_in,) = _matmul("ar_in", h0, w_in, tm=tm_tok, comm=_gather_comm([gm_w_in[0].astype(BF16)]))
    f_qkv = functools.partial(_f_qkv, nh=N_HEADS, nkv=N_KV_HEADS)
    qkv_tiled = [_t(proj, 0, 0, attn_w), _t(proj, 0, kv_blk, 2 * kv_w), _t(cos), _t(sin)]
    q_r, kv_r = _rowwise("qkv", f_qkv, t_all, tm, qkv_tiled, [q_g, k_g, perm], [(attn_w, BF16), (2 * kv_w, BF16)])
    attn_o, (g_ar_out, g_ff_in0) = _attn_fwd(q_r, kv_r, nct, tm,
                                             comm=_gather_comm([ar_w_out[0].astype(BF16), ff_in_shard(0)]))
    w_out = g_ar_out.reshape(attn_w + d_rnn, d)
    w1[0] = cols_full(g_ff_in0)

    def with_neighbours(a, t, col_blk=0, width=None):
        return [_t(a, -t, col_blk, width), _t(a, 0, col_blk, width), _t(a, t, col_blk, width)]

    f_conv = functools.partial(_f_conv, n_ctx_tiles=nct, n_tiles=t_all // tm)
    (xc,) = _rowwise("conv", f_conv, t_all, tm, with_neighbours(proj, tm, 1, d_rnn), [conv_w, conv_b], [(d_rnn, F32)])
    f_gates = functools.partial(_f_gates, nb=rnn_blocks)
    gate_full = [wa3, ba, wx3, bx, lam]
    a_f, b_f, a_b, b_b, g_gm_out = _rowwise("gates", f_gates, t_all, tm, [_t(xc)], gate_full, [(d_rnn, F32)] * 4,
                                            comm=_gather_comm([gm_w_out[0].astype(BF16)]))

    def to3(a):
        return a.reshape(a.shape[0], lr, LANES)

    nc_scan = n_ctx // SCAN_BLOCK
    h_f, hp_f = _scan_fwd("scan_f", to3(a_f), to3(b_f), nc_scan, False)
    h_b, hp_b = _scan_fwd("scan_b", to3(a_b), to3(b_b), nc_scan, True)
    h_f2, h_b2 = h_f.reshape(t_all, d_rnn), h_b.reshape(t_all, d_rnn)
    rnn_tiled = [_t(h_f2, n_ctx), _t(h_b2, n_ctx), _t(proj, n_ctx, 2, d_rnn)]
    (rnn_o,) = _rowwise("rnn_out", _f_rnnout, seq, tm, rnn_tiled, [], [(d_rnn, BF16)])
    ar = jnp.concatenate([attn_o, rnn_o], axis=1)
    o0 = _matmul("ar_out", ar, w_out)
    mid0_args = [row(norm_full, 0, 1), row(ml, 0, 2), row(norm_full, 0, 2), row(ml, 0, 3), row(ml, 0, 4)]
    x1, h2_0 = _rowwise("mid0", _f_mid, seq, tm, [_t(x2), _t(o0)], mid0_args, [(d, F32), (d, BF16)])
    act0, (g_ff_out0,) = _matmul("ff_in_0", h2_0, w1[0], outs=(BF16,), epilogue=relu2, comm=_gather_comm([ff_out_shard(0)]))
    w2[0] = g_ff_out0.reshape(d_ff, d)
    w_gi, w_go = cols_full(g_gm_in), g_gm_out.reshape(d_gm, d)
    m0, (g_ff_in1,) = _matmul("ff_out_0", act0, w2[0], tm=MM_TILE_M // 2, tk=2 * MM_TILE_K, comm=_gather_comm([ff_in_shard(1)]))
    w1[1] = cols_full(g_ff_in1)
    post0_args = [row(norm_full, 0, 3), row(ml, 0, 5)]
    pre1_args = [row(norm_full, 1, 0), row(ml, 1, 0), row(ml, 1, 1)]

    def f_between_fwd(pid, xv, mv, g3, gate, g, sh, sc):
        (xo,) = _f_post(pid, xv, mv, g3, gate)
        return (xo, _f_pre(pid, xo, g, sh, sc)[0])

    x2l, h1 = _rowwise("between", f_between_fwd, seq, tm, [_t(x1), _t(m0)], post0_args + pre1_args, [(d, F32), (d, BF16)])
    zg = _matmul("gm_in", h1, w_gi)
    f_gm = functools.partial(_f_gm, n_chunks=tmb // CHUNK, groups=gm_groups)
    gm_full = [gm_b_in_f[:, :d_gm], gm_b_in_f[:, d_gm:], gm_vg, gm_vb, w_sp, bsp_t, expand]
    gm_tiled = [_t(zg, 0, 0, d_gm), _t(zg, 0, 1, d_gm)]
    (gmix,) = _rowwise("gm_mix", f_gm, seq, tmb, gm_tiled, gm_full, [(d_gm, BF16)])
    o1 = _matmul("gm_out", gmix, w_go)
    mid1_args = [row(norm_full, 1, 1), row(ml, 1, 2), row(norm_full, 1, 2), row(ml, 1, 3), row(ml, 1, 4)]
    x3, h2_1 = _rowwise("mid1", _f_mid, seq, tm, [_t(x2l), _t(o1)], mid1_args, [(d, F32), (d, BF16)])
    act1, (g_ff_out1,) = _matmul("ff_in_1", h2_1, w1[1], outs=(BF16,), epilogue=relu2,
                                       comm=_gather_comm([ff_out_shard(1)]))
    w2[1] = g_ff_out1.reshape(d_ff, d)
    m1 = _matmul("ff_out_1", act1, w2[1], tm=MM_TILE_M // 2, tk=2 * MM_TILE_K)
    post1_args = [row(norm_full, 1, 3), row(ml, 1, 5)]

    def f_loss(pid, xv, ov, tv, g, gate):
        (y,), vjp = jax.vjp(lambda *a: _f_post(pid, *a), xv, ov, g, gate)
        err = y - tv
        part = 0.5 * jnp.sum(err * err) / d
        dxv, dov, dg, dgate = vjp((err / d,))
        return (dxv, dov, jnp.full((8, LANES), part, F32), dg, dgate)

    dx3, dm1, loss_acc, dg_last, dgate_last = _rowwise("loss", f_loss, seq, tmb, [_t(x3), _t(m1), _t(tgt)], post1_args,
                                                       [(d, F32), (d, BF16)], [(8, LANES), (1, d), (1, d)])
    loss = lax.psum(loss_acc[0, 0], ("x", "y", "c"))

    d_norm = [[None] * 4 for _ in range(n_layers)]
    d_ml = [[None] * 6 for _ in range(n_layers)]
    recv = {}

    def cols_blocks(g):
        return jnp.moveaxis(g.reshape(g.shape[0], N_DEV, g.shape[1] // N_DEV), 1, 0)

    def rows_blocks(g):
        return g.reshape(N_DEV, g.shape[0] // N_DEV, g.shape[1])

    chip_sums = {}

    def chip_add(key, blocks, theirs):
        chip_sums[key] = _chip_add(key + "_add", blocks, theirs, ci)

    def mlp_bwd(i, dm, act, h2, first_comm=None):
        dw2 = _matmul(f"ff_out_dw_{i}", act, dm, ta=True, outs=(BF16,), comm=first_comm)
        dw2, carried = dw2 if first_comm is not None else (dw2, ())
        blk2 = rows_blocks(dw2)
        dz, (theirs,) = _matmul(f"ff_out_dx_{i}", dm, w2[i], tb=True, outs=(BF16,), extras=(act,), epilogue=relu2_bwd,
                                comm=_swap_comm([blk2]))
        chip_add(f"ff_out_{i}", blk2, theirs)
        blk1, (recv[f"ff_out_{i}"],) = _matmul(f"ff_in_dw_{i}", h2, dz, ta=True, outs=(BF16,), col_blocks=N_DEV,
                                              comm=_chips_comm([chip_sums[f"ff_out_{i}"]]))
        dh2, (theirs,) = _matmul(f"ff_in_dx_{i}", dz, w1[i], tb=True, tm=MM_TILE_M // 2, tk=2 * MM_TILE_K,
                                 comm=_swap_comm([blk1]))
        chip_add(f"ff_in_{i}", blk1, theirs)
        return dh2, carried

    def mid_bwd(i, xin, o, args, dx1, dh2):
        res = _rowwise(f"mid_bwd{i}", _bwd_of(_f_mid, 2, 2, (0, 1, 2, 3, 4, 5, 6)), seq, tmb,
                       [_t(xin), _t(o), _t(dx1), _t(dh2)], args, [(d, F32), (d, BF16)], [(1, d)] * 5)
        d_norm[i][1], d_ml[i][2], d_norm[i][2], d_ml[i][3], d_ml[i][4] = res[2:]
        return res[0], res[1]

    d_norm[1][3], d_ml[1][5] = dg_last, dgate_last
    dh2_1, _ = mlp_bwd(1, dm1, act1, h2_1)
    dx2a, do1 = mid_bwd(1, x2l, o1, mid1_args, dx3, dh2_1)
    blk_go = rows_blocks(_matmul("gm_out_dw", gmix, do1, ta=True, outs=(BF16,)))
    dgmix, (theirs,) = _matmul("gm_out_dx", do1, w_go, tb=True, comm=_swap_comm([blk_go]))
    chip_add("gm_out", blk_go, theirs)
    gm_res = _rowwise("gm_mix_bwd", _bwd_of(f_gm, 2, 1, (0, 1, 2, 3, 4, 5, 6, 7)), seq, tmb,
                      gm_tiled + [_t(dgmix)], gm_full, [(d_gm, BF16), (d_gm, BF16)],
                      [(1, d_gm)] * 4 + [w_sp.shape, bsp_t.shape])
    dzg = jnp.concatenate([gm_res[0], gm_res[1]], axis=1)
    g_gm_b_in = jnp.concatenate([gm_res[2], gm_res[3]], axis=1)
    g_gm_vg, g_gm_vb, g_w_sp = gm_res[4], gm_res[5], gm_res[6]
    g_b_sp = gm_res[7][:, :gm_groups].T
    dh1, (recv["gm_out"],) = _matmul("gm_in_dx", dzg, w_gi, tb=True, comm=_chips_comm([chip_sums["gm_out"]]))
    blk_gi = _matmul("gm_in_dw", h1, dzg, ta=True, outs=(BF16,), col_blocks=N_DEV)

    def f_between(pid, xv, dh, dxa, x1v, m0v, g, sh, sc, g3, gate):
        dxv, dg, dsh, dsc = _bwd_of(_f_pre, 1, 1, (0, 1, 2, 3))(pid, xv, dh, g, sh, sc)
        dx1v, dm0v, dg3, dgate = _bwd_of(_f_post, 2, 1, (0, 1, 2, 3))(pid, x1v, m0v, dxv + dxa, g3, gate)
        return (dx1v, dm0v, dg, dsh, dsc, dg3, dgate)

    res = _rowwise("between_bwd", f_between, seq, tmb, [_t(x2l), _t(dh1), _t(dx2a), _t(x1), _t(m0)], pre1_args + post0_args,
                   [(d, F32), (d, BF16)], [(1, d)] * 5)
    dx1, dm0 = res[0], res[1]
    d_norm[1][0], d_ml[1][0], d_ml[1][1], d_norm[0][3], d_ml[0][5] = res[2:]

    dh2_0, (theirs,) = mlp_bwd(0, dm0, act0, h2_0, first_comm=_swap_comm([blk_gi]))
    chip_add("gm_in", blk_gi, theirs)
    dxa, do0 = mid_bwd(0, x2, o0, mid0_args, dx1, dh2_0)
    blk_out = rows_blocks(_matmul("ar_out_dw", ar, do0, ta=True, outs=(BF16,)))
    d_ar, (theirs,) = _matmul("ar_out_dx", do0, w_out, tb=True, comm=_swap_comm([blk_out]))
    chip_add("ar_out", blk_out, theirs)

    late = ["ff_in_1", "gm_in", "ff_in_0"]
    dq, dkt, dvt, *landed = _attn_bwd(q_r, kv_r, d_ar, nct, tm, comm=_chips_comm([chip_sums[k] for k in late]))
    recv.update(zip(late, landed))
    dkv_all = jnp.concatenate([dkt, dvt], axis=0).T

    def f_qkv_bwd(pid, pq, pkv, cos_t, sin_t, dq_t, dkv_t, *fulls):
        dq_t = jnp.where(pid < nctb, 0.0, dq_t)
        return _bwd_of(f_qkv, 4, 2, (0, 1, 4, 5))(pid, pq, pkv, cos_t, sin_t, dq_t, dkv_t, *fulls)

    qkv_res = _rowwise("qkv_bwd", f_qkv_bwd, t_all, tmb, qkv_tiled + [_t(dq, -n_ctx), _t(dkv_all)],
                       [q_g, k_g, perm], [(attn_w, BF16), (2 * kv_w, BF16)], [q_g.shape, k_g.shape])
    dproj_q, dproj_kv, g_q_g, g_k_g = qkv_res

    rnn_res = _rowwise("rnn_out_bwd", _bwd_of(_f_rnnout, 3, 1, (0, 2)), seq, tmb, rnn_tiled + [_t(d_ar, 0, 1, d_rnn)], [],
                       [(d_rnn, F32), (d_rnn, BF16)])
    zc = jnp.zeros((n_ctx, d_rnn), F32)
    dh_all = to3(jnp.concatenate([zc, rnn_res[0]], axis=0))
    dproj_g = jnp.concatenate([zc.astype(BF16), rnn_res[1]], axis=0)
    da_f, db_f = _scan_bwd("scan_f_bwd", to3(a_f), dh_all, hp_f, nc_scan, False)
    da_b, db_b = _scan_bwd("scan_b_bwd", to3(a_b), dh_all, hp_b, nc_scan, True)
    gate_cts = [_t(a.reshape(t_all, d_rnn)) for a in (da_f, db_f, da_b, db_b)]
    gates_res = _rowwise("gates_bwd", _bwd_of(f_gates, 1, 4, (0, 1, 2, 3, 4, 5)), t_all, min(tmb, GATES_BWD_TILE),
                         [_t(xc)] + gate_cts, gate_full, [(d_rnn, F32)], [wa3.shape, ba.shape, wx3.shape, bx.shape, lam.shape])
    dxc, g_wa, g_ba, g_wx, g_bx, g_lam = gates_res
    f_conv_b = functools.partial(_f_conv_bwd, n_ctx_tiles=nctb, n_tiles=t_all // tmb)
    conv_tiled = with_neighbours(proj, tmb, 1, d_rnn) + with_neighbours(dxc, tmb)
    dproj_x, g_conv_w, g_conv_b = _rowwise("conv_bwd", f_conv_b, t_all, tmb, conv_tiled, [conv_w],
                                           [(d_rnn, BF16)], [conv_w.shape, (1, d_rnn)])
    dproj = jnp.concatenate([dproj_q, dproj_x, dproj_g, dproj_kv], axis=1)
    sq_names = ['ar_wa', 'ar_wx', 'gm_w_sp']

    def stack_sq(parts):
        return jnp.concatenate([p.reshape(-1, LANES) for p in parts], axis=0)

    sq_pack = stack_sq([g_wa, g_wx, g_w_sp]).astype(BF16)
    g_w_in, (g_sq,) = _matmul("ar_in_dw", h0, dproj, ta=True, outs=(BF16,), comm=_gather_comm([sq_pack]))
    g_w_in = jnp.concatenate([g_w_in[:, :attn_w], g_w_in[:, 3 * attn_w:], g_w_in[:, attn_w:3 * attn_w]], axis=1)
    blk_in = cols_blocks(g_w_in)
    dh0, (recv["ar_out"], theirs) = _matmul("ar_in_dx", dproj, w_in, tb=True, tm=tm_tok,
                                            comm=_both(_chips_comm([chip_sums["ar_out"]]), _swap_comm([blk_in])))
    chip_add("ar_in", blk_in, theirs)

    f_pre0b = functools.partial(_f_pre_ctx, n_ctx_tiles=nctb)

    def f_pre0_bwd(pid, xcv, xlv, dh, dxp, g, shc, scc, shl, scl):
        grads = _bwd_of(f_pre0b, 2, 1, (1, 2, 3, 4, 5, 6))(pid, xcv, xlv, dh, g, shc, scc, shl, scl)
        return (grads[0] + dxp,) + tuple(grads[1:])

    res = _rowwise("pre_bwd0", f_pre0_bwd, seq, tmb, tokens + [_t(dh0), _t(dxa, -n_ctx)], pre0_args, [(d, F32)], [(1, d)] * 5,
                   comm=_chips_comm([chip_sums["ar_in"]]), skip_rows=n_ctx)
    grad_x = res[0][None]
    d_norm[0][0], d_mc_shift, d_mc_scale, d_ml[0][0], d_ml[0][1] = res[1:6]
    recv["ar_in"] = res[6]

    z1d = jnp.zeros((1, d), F32)
    dml = jnp.concatenate([jnp.concatenate(r, axis=0)[None] for r in d_ml], axis=0)
    dmc = jnp.concatenate([jnp.concatenate([d_mc_shift, d_mc_scale] + [z1d] * 4, axis=0)[None],
                           jnp.zeros((n_layers - 1, 6, d), F32)], axis=0)
    g_norm = jnp.concatenate([jnp.concatenate(r, axis=0)[None] for r in d_norm], axis=0)
    small_parts = [dmc, g_norm, g_q_g, g_k_g, g_conv_w, g_conv_b, g_ba, g_bx, g_lam, g_gm_b_in, g_gm_vg, g_gm_vb, g_b_sp]
    small2, off2 = _pack([dml] + small_parts)
    (gs2,) = _run_comm("gather_small_grads", _gather_comm([small2]))
    dml_all = gs2.reshape(N_DEV, -1)[:, :off2[1]].reshape(N_DEV, n_layers, 6 * d)
    summed = _sum_lead("sum_small_grads", gs2).reshape(-1)
    summed_sq = _sum_lead("sum_square_grads", g_sq)

    def seg2(k, shape):
        return summed[off2[k + 1]:off2[k + 2]].reshape(shape)

    dmc_sum = seg2(0, (n_layers, 6 * d))
    dmod_rows = jnp.concatenate([jnp.moveaxis(dml_all, 0, 1), dmc_sum[:, None, :],
                                 jnp.zeros((n_layers, MOD_ROWS - N_DEV - 1, 6 * d), F32)], axis=1)
    g_b_mod = _sum_lead("sum_b_mod", jnp.moveaxis(dmod_rows, 1, 0).reshape(MOD_ROWS, n_layers * 6 * d // LANES, LANES))
    g_b_mod = g_b_mod.reshape(n_layers, 6 * d)
    g_w_mod, ds16 = _mod_bwd(c16, w_mod, _my_cols(dmod_rows, me, n6))
    (g_ds,) = _run_comm("gather_dctx", _gather_comm([ds16[N_DEV].reshape(d // LANES, LANES)]))
    ds_ctx = _sum_lead("sum_dctx", g_ds)
    (g_c_ctx,) = _rowwise("silu_bwd", _f_silu_mul, d // LANES, d // LANES, [_t(c_ctx.reshape(d // LANES, LANES)), _t(ds_ctx)],
                          [], [(LANES, F32)])
    g_c_ctx = g_c_ctx.reshape(d)

    grads = {
        'c_ctx': g_c_ctx, 'b_mod': g_b_mod,
        'norm_g': _my_cols(seg2(1, (n_layers, 4, d)), me, d // N_DEV),
        'ar_q_g': seg2(2, ar_q_g.shape), 'ar_k_g': seg2(3, ar_k_g.shape),
        'ar_conv_w': _my_cols(seg2(4, (1, CONV_W, d_rnn)), me, d_rnn // N_DEV),
        'ar_conv_b': seg2(5, ar_conv_b.shape),
        'ar_ba': _my_cols(seg2(6, (1, 2, d_rnn)), me, d_rnn // N_DEV),
        'ar_bx': _my_cols(seg2(7, (1, 2, d_rnn)), me, d_rnn // N_DEV),
        'ar_lambda': _my_cols(seg2(8, (1, 2, d_rnn)), me, d_rnn // N_DEV),
        'gm_b_in': _my_cols(seg2(9, (1, 2 * d_gm)), me, 2 * d_gm // N_DEV),
        'gm_v_g': _my_cols(seg2(10, (1, d_gm)), me, d_gm // N_DEV),
        'gm_v_b': _my_cols(seg2(11, (1, d_gm)), me, d_gm // N_DEV),
        'gm_b_sp': seg2(12, gm_b_sp.shape),
    }
    small_names = list(grads)
    deltas, new_m, new_v = {}, {}, {}

    sq_res = (summed_sq,) + tuple(_adam_f32("adam_square", summed_sq, *[stack_sq([src[n] for n in sq_names])
                                                                        for src in (wts, mom1, mom2)]))
    first = 0
    for n in sq_names:
        rows_n = wts[n].size // LANES
        for dst, slab in zip((grads, deltas, new_m, new_v), sq_res):
            dst[n] = slab[first:first + rows_n].reshape(wts[n].shape)
        first += rows_n

    wp, offw = _pack([wts[n] for n in small_names])
    mp, _ = _pack([mom1[n] for n in small_names])
    vp, _ = _pack([mom2[n] for n in small_names])
    gp, _ = _pack([grads[n] for n in small_names])
    dp, mp2, vp2 = _adam_f32("adam_small", gp, wp, mp, vp)
    for k, n in enumerate(small_names):
        for dst, slab in ((deltas, dp), (new_m, mp2), (new_v, vp2)):
            dst[n] = slab.reshape(-1)[offw[k]:offw[k + 1]].reshape(wts[n].shape)

    grads['w_mod'] = g_w_mod
    dw, mw, vw = _adam_f32("adam_w_mod", g_w_mod.reshape(n_layers * d, n6), w_mod.reshape(n_layers * d, n6),
                           m_w_mod.reshape(n_layers * d, n6), v_w_mod.reshape(n_layers * d, n6))
    deltas['w_mod'], new_m['w_mod'], new_v['w_mod'] = (a.reshape(w_mod.shape) for a in (dw, mw, vw))

    received = {
        'w_ff_in': [recv[f"ff_in_{i}"] for i in range(n_layers)], 'w_ff_out': [recv[f"ff_out_{i}"] for i in range(n_layers)],
        'ar_w_in': [recv["ar_in"]], 'ar_w_out': [recv["ar_out"]], 'gm_w_in': [recv["gm_in"]], 'gm_w_out': [recv["gm_out"]]}
    for n, r in received.items():
        shp = wts[n].shape
        flat = (shp[0] * shp[1], shp[2])
        res = _adam_recv("adam_" + n, r, wts[n].reshape(flat), mom1[n].reshape(flat), mom2[n].reshape(flat))
        grads[n], deltas[n], new_m[n], new_v[n] = (a.reshape(shp) for a in res)

    return (loss, grad_x, *[grads[n] for n in WEIGHTS], *[deltas[n] for n in WEIGHTS],
            *[new_m[n] for n in WEIGHTS], *[new_v[n] for n in WEIGHTS])
```

```python
import functools

import numpy as np
import jax
import jax.numpy as jnp
from jax import lax
from jax.experimental import pallas as pl
from jax.experimental.pallas import tpu as pltpu

F32 = jnp.float32
BF16 = jnp.bfloat16
HIGHEST = lax.Precision.HIGHEST
LOG2_E = 1.4426950408889634
LN_2 = 0.6931471805599453

GRID_W = 64
N_HEADS = 8
N_KV_HEADS = 2
HEAD_DIM = 128
ROPE_THETA = 10000.0
RNN_BLOCK_W = 128
CONV_W = 4
RG_C = 8.0
GM_GROUP_W = 128
CHUNK = 128
EPS = 1e-6
ADAM_LR = 0.001
ADAM_B1 = 0.9
ADAM_B2 = 0.999
ADAM_EPS = 1e-08
ADAM_WD = 0.01
ADAM_STEP = 10

N_DEV = 8
MOD_ROWS = 16
LANES = 128
ROW_TILE = 256
ROW_TILE_BWD = 256
ATTN_BWD_HEADS_PER_STEP = 4
ATTN_HEADS_PER_STEP = 4
GATES_BWD_TILE = 128
SCAN_BLOCK = 256
VMEM_LIMIT = 56 * 1024 * 1024
PACK_ROWS = 512
MM_TILE_M = 1024
MM_TILE_N = 1024
MM_TILE_K = 2048

WEIGHTS = ['c_ctx', 'w_mod', 'b_mod', 'norm_g', 'w_ff_in', 'w_ff_out', 'ar_w_in', 'ar_q_g', 'ar_k_g', 'ar_conv_w',
           'ar_conv_b', 'ar_wa', 'ar_ba', 'ar_wx', 'ar_bx', 'ar_lambda', 'ar_w_out', 'gm_w_in', 'gm_b_in', 'gm_v_g',
           'gm_v_b', 'gm_w_sp', 'gm_b_sp', 'gm_w_out']


def _sds(shape, dtype):
    return jax.ShapeDtypeStruct(tuple(shape), dtype)


def _tile(dim, pref, align):
    t = (min(pref, dim) // align) * align
    while t >= align:
        if dim % t == 0:
            return t
        t -= align
    return dim


def _params(sem):
    return pltpu.CompilerParams(dimension_semantics=sem, vmem_limit_bytes=VMEM_LIMIT)


def _rms(x, g):
    return x * lax.rsqrt(jnp.mean(x * x, axis=-1, keepdims=True) + EPS) * g


def _gelu(x):
    return 0.5 * x * (1.0 + jnp.tanh(0.7978845608028654 * (x + 0.044715 * (x * x * x))))


def _sigmoid(x):
    return 0.5 * (jnp.tanh(0.5 * x) + 1.0)


def _log1p_pos(u):
    small = u < 1e-3
    us = jnp.where(small, u, 0.0)
    return jnp.where(small, us * (1.0 - us * (0.5 - us * (1.0 / 3.0))), jnp.log(1.0 + u))


def _softplus(x):
    return jnp.maximum(x, 0.0) + _log1p_pos(jnp.exp(-jnp.abs(x)))


def _f_pre_ctx(pid, xc, xl, g, shc, scc, shl, scl, *, n_ctx_tiles):
    is_ctx = pid < n_ctx_tiles
    x = jnp.where(is_ctx, xc, xl)
    sh = jnp.where(is_ctx, shc, shl)
    sc = jnp.where(is_ctx, scc, scl)
    return (_rms(x, g) * (1.0 + sc) + sh,)


def _f_pre(pid, x, g, sh, sc):
    return (_rms(x, g) * (1.0 + sc) + sh,)


def _f_mid(pid, x, o, g1, gate, g2, sh, sc):
    x1 = x + gate * _rms(o, g1)
    return (x1, _rms(x1, g2) * (1.0 + sc) + sh)


def _f_post(pid, x, o, g, gate):
    return (x + gate * _rms(o, g),)


def _f_qkv(pid, pq, pkv, cos, sin, q_g, k_g, perm, *, nh, nkv):
    hd = HEAD_DIM

    def norm_rope(xh, g):
        y = _rms(xh, g)
        return y * cos + jnp.dot(y, perm, precision=HIGHEST, preferred_element_type=F32) * sin

    qs = [norm_rope(pq[:, h * hd:(h + 1) * hd], q_g) * (HEAD_DIM ** -0.5 * LOG2_E) for h in range(nh)]
    ks = [norm_rope(pkv[:, h * hd:(h + 1) * hd], k_g) for h in range(nkv)]
    return (jnp.concatenate(qs, axis=1), jnp.concatenate(ks + [pkv[:, nkv * hd:]], axis=1))


def _f_gates(pid, x, wa, ba, wx, bx, lam, *, nb):
    w = RNN_BLOCK_W
    cols = [[] for _ in range(4)]
    for n in range(nb):
        blk = slice(n * w, (n + 1) * w)
        xn = x[:, blk]
        xb = xn.astype(BF16)
        for d in range(2):
            r = _sigmoid(jnp.dot(xb, wa[d * nb + n].astype(BF16), preferred_element_type=F32) + ba[d:d + 1, blk])
            i = _sigmoid(jnp.dot(xb, wx[d * nb + n].astype(BF16), preferred_element_type=F32) + bx[d:d + 1, blk])
            log_a = -RG_C * r * _softplus(-lam[d:d + 1, blk])
            a = jnp.exp(log_a)
            cols[2 * d].append(a)
            cols[2 * d + 1].append(jnp.sqrt(-jnp.tanh(log_a) * (a * a + 1.0)) * (i * xn))
    return tuple(jnp.concatenate(c, axis=1) for c in cols)


def _f_rnnout(pid, hf, hb, gr):
    return ((hf + hb) * _gelu(gr),)


def _f_gm(pid, zu, zv, bu, bv, v_g, v_b, w_sp, bsp_t, expand, *, n_chunks, groups):
    u = _gelu(zu + bu)
    v = _gelu(zv + bv)
    mu = jnp.mean(v, axis=-1, keepdims=True)
    vc = v - mu
    v = vc * lax.rsqrt(jnp.mean(vc * vc, axis=-1, keepdims=True) + EPS) * v_g + v_b
    bias = jnp.dot(bsp_t, expand, precision=HIGHEST, preferred_element_type=F32)
    outs = []
    for c in range(n_chunks):
        vch = v[c * CHUNK:(c + 1) * CHUNK]
        cols = [jnp.dot(w_sp[g].astype(BF16), vch[:, g * GM_GROUP_W:(g + 1) * GM_GROUP_W].astype(BF16),
                        preferred_element_type=F32) for g in range(groups)]
        outs.append(u[c * CHUNK:(c + 1) * CHUNK] * (jnp.concatenate(cols, axis=1) + bias))
    return (jnp.concatenate(outs, axis=0),)


def _f_silu_mul(pid, c, d):
    return (d * jax.grad(lambda z: jnp.sum(z * _sigmoid(z)))(c),)


def _bwd_of(fn, n_tiled, n_ct, want):
    def bwd(pid, *args):
        tiles = [t.astype(F32) for t in args[:n_tiled]]
        cts = args[n_tiled:n_tiled + n_ct]
        fulls = list(args[n_tiled + n_ct:])
        outs, vjp = jax.vjp(lambda *a: fn(pid, *a), *tiles, *fulls)
        grads = vjp(tuple(ct.astype(o.dtype) for ct, o in zip(cts, outs)))
        return tuple(grads[i] for i in want)
    return bwd


def _rowwise(name, fn, rows, tm, tiled, full, outs, accs=(), comm=None, skip_rows=0):
    n_t, n_f, n_o, n_a = len(tiled), len(full), len(outs), len(accs)
    assert skip_rows % tm == 0
    skip = skip_rows // tm
    n_tiles = rows // tm + skip

    def body(*refs):
        in_refs, c_ins, res_refs, c_outs, _, c_sems = _split_refs(refs, n_t + n_f, n_o + n_a, 0, comm)
        pid = pl.program_id(0)
        if comm is not None:
            @pl.when(pid == 0)
            def _():
                comm.start(c_ins, c_outs, *c_sems)

        res = fn(pid, *[r[...] for r in in_refs])
        o_refs, a_refs = res_refs[:n_o], res_refs[n_o:]
        for r, v in zip(o_refs, res[:n_o]):
            r[...] = v.astype(r.dtype)
        if n_a:
            @pl.when(pid == 0)
            def _():
                for r in a_refs:
                    r[...] = jnp.zeros_like(r)
            for r, v in zip(a_refs, res[n_o:]):
                r[...] += v.astype(F32)
        if comm is not None:
            @pl.when(pid == n_tiles - 1)
            def _():
                comm.finish(c_ins, c_outs, *c_sems)

    assert all(ro % tm == 0 for (_, ro, _, _) in tiled)
    in_specs = [pl.BlockSpec((tm, w), lambda i, ro=ro // tm, cb=cb, last=a.shape[0] // tm - 1: (jnp.clip(i + ro, 0, last), cb))
                for (a, ro, cb, w) in tiled]
    in_specs += [pl.BlockSpec(a.shape, lambda i, nd=a.ndim: (0,) * nd) for a in full]
    out_shape = [_sds((rows, w), dt) for (w, dt) in outs] + [_sds(s, F32) for s in accs]
    out_specs = [pl.BlockSpec((tm, w), lambda i: (jnp.maximum(i - skip, 0), 0)) for (w, _) in outs]
    out_specs += [pl.BlockSpec(tuple(s), lambda i, nd=len(s): (0,) * nd) for s in accs]
    c_in, c_out, c_scr = (comm.ins, comm.out_shapes, comm.scratch()) if comm is not None else ([], [], [])
    return pl.pallas_call(body, grid=(n_tiles,), in_specs=in_specs + [_ANY] * len(c_in), out_specs=out_specs + [_ANY] * len(c_out),
                          out_shape=out_shape + list(c_out), scratch_shapes=c_scr, name=name,
                          compiler_params=_params(("arbitrary",)))(*[t[0] for t in tiled], *full, *c_in)


def _t(a, row_off=0, col_blk=0, width=None):
    return (a, row_off, col_blk, a.shape[1] if width is None else width)


class _Comm:
    def __init__(self, ins, out_shapes, n_sems, start, finish):
        self.ins, self.out_shapes, self.n_sems, self.start, self.finish = list(ins), list(out_shapes), n_sems, start, finish

    def scratch(self):
        return [pltpu.SemaphoreType.DMA((self.n_sems,)), pltpu.SemaphoreType.DMA((self.n_sems,)),
                pltpu.SemaphoreType.DMA((len(self.ins),))]


_ANY = pl.BlockSpec(memory_space=pl.ANY)


class _SemSlice:
    def __init__(self, ref, first):
        self.ref, self.first = ref, first

    @property
    def at(self):
        return self

    def __getitem__(self, k):
        return self.ref.at[self.first + k]


def _both(*comms):
    comms = [cm for cm in comms if cm is not None]
    if len(comms) <= 1:
        return comms[0] if comms else None

    def parts(ins, outs, send, recv, local):
        i0 = o0 = s0 = 0
        for cm in comms:
            ni, no = len(cm.ins), len(cm.out_shapes)
            yield cm, (ins[i0:i0 + ni], outs[o0:o0 + no], _SemSlice(send, s0), _SemSlice(recv, s0), _SemSlice(local, i0))
            i0, o0, s0 = i0 + ni, o0 + no, s0 + cm.n_sems

    def start(*refs):
        for cm, sub in parts(*refs):
            cm.start(*sub)

    def finish(*refs):
        for cm, sub in parts(*refs):
            cm.finish(*sub)

    return _Comm(sum((cm.ins for cm in comms), []), sum((cm.out_shapes for cm in comms), []),
                 sum(cm.n_sems for cm in comms), start, finish)


def _row_block(cols, itemsize):
    return max(16, (1 << 20) // (cols * itemsize))


def _split_refs(refs, n_in, n_out, n_scratch, comm):
    ci, co, cs = (len(comm.ins), len(comm.out_shapes), 3) if comm is not None else (0, 0, 0)
    cuts = np.cumsum([0, n_in, ci, n_out, co, n_scratch, cs])
    return [refs[cuts[i]:cuts[i + 1]] for i in range(6)]


def _matmul(name, a, b, *, ta=False, tb=False, outs=((F32,)), epilogue=None, extras=(), tm=None, tn=None, tk=None, comm=None,
            col_blocks=None):
    m, k = (a.shape[1], a.shape[0]) if ta else a.shape
    n = b.shape[0] if tb else b.shape[1]
    tm = _tile(m, tm or MM_TILE_M, 128 if ta else 16)
    tn = _tile(n if col_blocks is None else n // col_blocks, tn or MM_TILE_N, 128)
    tk = _tile(k, tk or MM_TILE_K, 128 if not ta else 16)
    ni, nj, nk = m // tm, n // tn, k // tk
    n_e, n_o = len(extras), len(outs)
    dims = (((0 if ta else 1,), (1 if tb else 0,)), ((), ()))

    def body(*refs):
        ins, c_ins, o_refs, c_outs, scratch, c_sems = _split_refs(refs, 2 + n_e, n_o, 1 if nk > 1 else 0, comm)
        a_ref, b_ref, e_refs = ins[0], ins[1], ins[2:]
        i, j, kk = pl.program_id(0), pl.program_id(1), pl.program_id(2)
        if comm is not None:
            @pl.when(jnp.logical_and(jnp.logical_and(i == 0, j == 0), kk == 0))
            def _():
                comm.start(c_ins, c_outs, *c_sems)

        def finish(acc):
            res = (acc,) if epilogue is None else epilogue(acc, *[e[...] for e in e_refs])
            for r, v in zip(o_refs, res):
                r[...] = v.astype(r.dtype)

        prod = lax.dot_general(a_ref[...].astype(BF16), b_ref[...].astype(BF16), dims, preferred_element_type=F32)
        if nk == 1:
            finish(prod)
        else:
            acc = scratch[0]

            @pl.when(kk == 0)
            def _():
                acc[...] = prod

            @pl.when(kk > 0)
            def _():
                acc[...] += prod

            @pl.when(kk == nk - 1)
            def _():
                finish(acc[...])
        if comm is not None:
            @pl.when(jnp.logical_and(jnp.logical_and(i == ni - 1, j == nj - 1), kk == nk - 1))
            def _():
                comm.finish(c_ins, c_outs, *c_sems)

    a_spec = pl.BlockSpec((tk, tm), lambda i, j, kk: (kk, i)) if ta else pl.BlockSpec((tm, tk), lambda i, j, kk: (i, kk))
    b_spec = pl.BlockSpec((tn, tk), lambda i, j, kk: (j, kk)) if tb else pl.BlockSpec((tk, tn), lambda i, j, kk: (kk, j))
    mn_spec = pl.BlockSpec((tm, tn), lambda i, j, kk: (i, j))
    c_in, c_out, c_scr = (comm.ins, comm.out_shapes, comm.scratch()) if comm is not None else ([], [], [])
    if col_blocks is None:
        o_spec, o_shape = mn_spec, (m, n)
    else:
        per = n // col_blocks // tn
        o_spec = pl.BlockSpec((None, tm, tn), lambda i, j, kk: (j // per, i, j % per))
        o_shape = (col_blocks, m, n // col_blocks)
    res = pl.pallas_call(body, grid=(ni, nj, nk), in_specs=[a_spec, b_spec] + [mn_spec] * n_e + [_ANY] * len(c_in),
                         out_specs=[o_spec] * n_o + [_ANY] * len(c_out),
                         out_shape=[_sds(o_shape, dt) for dt in outs] + list(c_out),
                         scratch_shapes=([pltpu.VMEM((tm, tn), F32)] if nk > 1 else []) + c_scr, name=name,
                         compiler_params=_params(("arbitrary", "arbitrary", "arbitrary")))(a, b, *extras, *c_in)
    main = res[0] if n_o == 1 else res[:n_o]
    return main if comm is None else (main, res[n_o:])


def _attn_fwd(q, kv, n_ctx_tiles, tq, comm=None):
    t_all = q.shape[0]
    s_len = t_all - n_ctx_tiles * tq
    hd, groups = HEAD_DIM, N_HEADS // N_KV_HEADS
    hps = ATTN_HEADS_PER_STEP
    assert groups % hps == 0
    gsteps = groups // hps
    nq = s_len // tq

    def body(*refs):
        (q_ref, k_ref, v_ref), c_ins, (o_ref,), c_outs, _, c_sems = _split_refs(refs, 3, 1, 0, comm)
        kh, g, i = pl.program_id(0), pl.program_id(1), pl.program_id(2)
        if comm is not None:
            @pl.when(jnp.logical_and(jnp.logical_and(kh == 0, g == 0), i == 0))
            def _():
                comm.start(c_ins, c_outs, *c_sems)

        kk, vv = k_ref[...], v_ref[...]
        for h in range(hps):
            qh = q_ref[:, h * hd:(h + 1) * hd]
            s = lax.dot_general(qh, kk, (((1,), (1,)), ((), ())), preferred_element_type=F32)
            p = jnp.exp2(s - jnp.max(s, axis=-1, keepdims=True))
            l = jnp.sum(p, axis=-1, keepdims=True)
            o = jnp.dot(p.astype(BF16), vv, preferred_element_type=F32) * (1.0 / l)
            o_ref[:, h * hd:(h + 1) * hd] = o.astype(o_ref.dtype)
        if comm is not None:
            @pl.when(jnp.logical_and(jnp.logical_and(kh == N_KV_HEADS - 1, g == gsteps - 1), i == nq - 1))
            def _():
                comm.finish(c_ins, c_outs, *c_sems)

    c_in, c_out, c_scr = (comm.ins, comm.out_shapes, comm.scratch()) if comm is not None else ([], [], [])
    res = pl.pallas_call(
        body, grid=(N_KV_HEADS, gsteps, nq),
        in_specs=[pl.BlockSpec((tq, hps * hd), lambda kh, g, i: (i + n_ctx_tiles, kh * gsteps + g)),
                  pl.BlockSpec((t_all, hd), lambda kh, g, i: (0, kh)),
                  pl.BlockSpec((t_all, hd), lambda kh, g, i: (0, N_KV_HEADS + kh))] + [_ANY] * len(c_in),
        out_specs=[pl.BlockSpec((tq, hps * hd), lambda kh, g, i: (i, kh * gsteps + g))] + [_ANY] * len(c_out),
        out_shape=[_sds((s_len, N_HEADS * hd), BF16)] + list(c_out), scratch_shapes=c_scr, name="attn_fwd",
        compiler_params=_params(("arbitrary", "arbitrary", "arbitrary")))(q, kv, kv, *c_in)
    return res[0] if comm is None else (res[0], res[1:])


def _attn_bwd(q, kv, d_ar, n_ctx_tiles, tq, comm=None):
    t_all = q.shape[0]
    s_len = t_all - n_ctx_tiles * tq
    hd, groups = HEAD_DIM, N_HEADS // N_KV_HEADS
    hps = ATTN_BWD_HEADS_PER_STEP
    assert groups % hps == 0
    gsteps = groups // hps
    nq = s_len // tq

    def body(*refs):
        (q_ref, k_ref, v_ref, do_ref), c_ins, (dq_ref, dkt_ref, dvt_ref), c_outs, _, c_sems = _split_refs(refs, 4, 3, 0, comm)
        first = jnp.logical_and(pl.program_id(1) == 0, pl.program_id(2) == 0)
        if comm is not None:
            @pl.when(jnp.logical_and(first, pl.program_id(0) == 0))
            def _():
                comm.start(c_ins, c_outs, *c_sems)

        @pl.when(first)
        def _():
            dkt_ref[...] = jnp.zeros_like(dkt_ref)
            dvt_ref[...] = jnp.zeros_like(dvt_ref)

        kk, vv = k_ref[...], v_ref[...]
        for h in range(hps):
            cols = slice(h * hd, (h + 1) * hd)
            qv = q_ref[:, cols]
            s = lax.dot_general(qv, kk, (((1,), (1,)), ((), ())), preferred_element_type=F32)
            p = jnp.exp2(s - jnp.max(s, axis=-1, keepdims=True))
            inv_l = 1.0 / jnp.sum(p, axis=-1, keepdims=True)
            do = (do_ref[:, cols] * inv_l).astype(BF16)
            dp = lax.dot_general(do, vv, (((1,), (1,)), ((), ())), preferred_element_type=F32)
            ds = (p * (dp - jnp.sum(p * dp, axis=-1, keepdims=True) * inv_l)).astype(BF16)
            dq_ref[:, cols] = jnp.dot(ds, kk, preferred_element_type=F32) * LN_2
            dkt_ref[...] += jnp.dot(qv.T, ds, preferred_element_type=F32)
            dvt_ref[...] += jnp.dot(do.T, p.astype(BF16), preferred_element_type=F32)
        last = jnp.logical_and(pl.program_id(1) == gsteps - 1, pl.program_id(2) == nq - 1)

        @pl.when(last)
        def _():
            dkt_ref[...] *= LN_2

        if comm is not None:
            @pl.when(jnp.logical_and(last, pl.program_id(0) == N_KV_HEADS - 1))
            def _():
                comm.finish(c_ins, c_outs, *c_sems)

    c_in, c_out, c_scr = (comm.ins, comm.out_shapes, comm.scratch()) if comm is not None else ([], [], [])
    return pl.pallas_call(
        body, grid=(N_KV_HEADS, gsteps, nq),
        in_specs=[pl.BlockSpec((tq, hps * hd), lambda kh, g, i: (i + n_ctx_tiles, kh * gsteps + g)),
                  pl.BlockSpec((t_all, hd), lambda kh, g, i: (0, kh)),
                  pl.BlockSpec((t_all, hd), lambda kh, g, i: (0, N_KV_HEADS + kh)),
                  pl.BlockSpec((tq, hps * hd), lambda kh, g, i: (i, kh * gsteps + g))] + [_ANY] * len(c_in),
        out_specs=[pl.BlockSpec((tq, hps * hd), lambda kh, g, i: (i, kh * gsteps + g)),
                   pl.BlockSpec((hd, t_all), lambda kh, g, i: (kh, 0)),
                   pl.BlockSpec((hd, t_all), lambda kh, g, i: (kh, 0))] + [_ANY] * len(c_out),
        out_shape=[_sds((s_len, N_HEADS * hd), F32), _sds((N_KV_HEADS * hd, t_all), F32),
                   _sds((N_KV_HEADS * hd, t_all), F32)] + list(c_out),
        scratch_shapes=c_scr, name="attn_bwd",
        compiler_params=_params(("arbitrary", "arbitrary", "arbitrary")))(q, kv, kv, d_ar, *c_in)


def _scan_order(nb, nc, reverse):
    if not reverse:
        return lambda i: i
    return lambda i: jnp.where(i < nc, nc - 1 - i, nb - 1 - (i - nc))


def _scan_both(a_f, b_f, a_b, b_b, nc):
    t_all, r, l = a_f.shape
    tb = SCAN_BLOCK
    nb = t_all // tb
    order_f, order_b = _scan_order(nb, nc, False), _scan_order(nb, nc, True)

    def body(af_ref, bf_ref, ab_ref, bb_ref, hf_ref, hpf_ref, hb_ref, hpb_ref, carry_f, carry_b):
        @pl.when(pl.program_id(0) == 0)
        def _():
            carry_f[...] = jnp.zeros_like(carry_f)
            carry_b[...] = jnp.zeros_like(carry_b)

        def step(s, hs):
            hf, hb = hs
            t = tb - 1 - s
            hpf_ref[s] = hf
            hf = af_ref[s] * hf + bf_ref[s]
            hf_ref[s] = hf
            hpb_ref[t] = hb
            hb = ab_ref[t] * hb + bb_ref[t]
            hb_ref[t] = hb
            return (hf, hb)

        hf, hb = lax.fori_loop(0, tb, step, (carry_f[...], carry_b[...]), unroll=8)
        carry_f[...] = hf
        carry_b[...] = hb

    spec_f = pl.BlockSpec((tb, r, l), lambda i: (order_f(i), 0, 0))
    spec_b = pl.BlockSpec((tb, r, l), lambda i: (order_b(i), 0, 0))
    return pl.pallas_call(body, grid=(nb,), in_specs=[spec_f, spec_f, spec_b, spec_b], out_specs=[spec_f, spec_f, spec_b, spec_b],
                          out_shape=[_sds(a_f.shape, F32)] * 4, scratch_shapes=[pltpu.VMEM((r, l), F32)] * 2, name="scan_both",
                          compiler_params=_params(("arbitrary",)))(a_f, b_f, a_b, b_b)


def _scan_both_bwd(a_f, hp_f, a_b, hp_b, dh, nc):
    t_all, r, l = a_f.shape
    tb = SCAN_BLOCK
    nb = t_all // tb
    primal_f, primal_b = _scan_order(nb, nc, False), _scan_order(nb, nc, True)

    def body(af_ref, hpf_ref, dhf_ref, ab_ref, hpb_ref, dhb_ref, daf_ref, dbf_ref, dab_ref, dbb_ref, carry_f, carry_b):
        @pl.when(pl.program_id(0) == 0)
        def _():
            carry_f[...] = jnp.zeros_like(carry_f)
            carry_b[...] = jnp.zeros_like(carry_b)

        def step(s, cs):
            cf, cb = cs
            t = tb - 1 - s
            lam_f = dhf_ref[t] + cf
            dbf_ref[t] = lam_f
            daf_ref[t] = lam_f * hpf_ref[t]
            lam_b = dhb_ref[s] + cb
            dbb_ref[s] = lam_b
            dab_ref[s] = lam_b * hpb_ref[s]
            return (af_ref[t] * lam_f, ab_ref[s] * lam_b)

        cf, cb = lax.fori_loop(0, tb, step, (carry_f[...], carry_b[...]), unroll=8)
        carry_f[...] = cf
        carry_b[...] = cb

    spec_f = pl.BlockSpec((tb, r, l), lambda i: (primal_f(nb - 1 - i), 0, 0))
    spec_b = pl.BlockSpec((tb, r, l), lambda i: (primal_b(nb - 1 - i), 0, 0))
    return pl.pallas_call(body, grid=(nb,), in_specs=[spec_f] * 3 + [spec_b] * 3, out_specs=[spec_f, spec_f, spec_b, spec_b],
                          out_shape=[_sds(a_f.shape, F32)] * 4, scratch_shapes=[pltpu.VMEM((r, l), F32)] * 2,
                          name="scan_both_bwd", compiler_params=_params(("arbitrary",)))(a_f, hp_f, dh, a_b, hp_b, dh)


def _scan_fwd(name, a, b, nc, reverse):
    t_all, r, l = a.shape
    tb = SCAN_BLOCK
    nb = t_all // tb
    order = _scan_order(nb, nc, reverse)

    def body(a_ref, b_ref, h_ref, hp_ref, carry):
        @pl.when(pl.program_id(0) == 0)
        def _():
            carry[...] = jnp.zeros_like(carry)

        def step(s, h):
            t = tb - 1 - s if reverse else s
            hp_ref[t] = h
            h = a_ref[t] * h + b_ref[t]
            h_ref[t] = h
            return h

        carry[...] = lax.fori_loop(0, tb, step, carry[...], unroll=8)

    spec = pl.BlockSpec((tb, r, l), lambda i: (order(i), 0, 0))
    return pl.pallas_call(body, grid=(nb,), in_specs=[spec, spec], out_specs=[spec, spec],
                          out_shape=[_sds(a.shape, F32)] * 2, scratch_shapes=[pltpu.VMEM((r, l), F32)], name=name,
                          compiler_params=_params(("arbitrary",)))(a, b)


def _scan_bwd(name, a, dh, hp, nc, reverse):
    t_all, r, l = a.shape
    tb = SCAN_BLOCK
    nb = t_all // tb
    primal = _scan_order(nb, nc, reverse)

    def order(i):
        return primal(nb - 1 - i)

    def body(a_ref, dh_ref, hp_ref, da_ref, db_ref, carry):
        @pl.when(pl.program_id(0) == 0)
        def _():
            carry[...] = jnp.zeros_like(carry)

        def step(s, cr):
            t = s if reverse else tb - 1 - s
            lam = dh_ref[t] + cr
            db_ref[t] = lam
            da_ref[t] = lam * hp_ref[t]
            return a_ref[t] * lam

        carry[...] = lax.fori_loop(0, tb, step, carry[...], unroll=8)

    spec = pl.BlockSpec((tb, r, l), lambda i: (order(i), 0, 0))
    return pl.pallas_call(body, grid=(nb,), in_specs=[spec] * 3, out_specs=[spec, spec],
                          out_shape=[_sds(a.shape, F32)] * 2, scratch_shapes=[pltpu.VMEM((r, l), F32)], name=name,
                          compiler_params=_params(("arbitrary",)))(a, dh, hp)


def _shifted(prev, cur, nxt, k, pid, n_ctx_tiles, n_tiles):
    if k == 0:
        return cur
    tm = cur.shape[0]
    row = lax.broadcasted_iota(jnp.int32, cur.shape, 0)
    if k < 0:
        at_start = jnp.logical_or(pid == 0, pid == n_ctx_tiles)
        edge = jnp.where(at_start, 0.0, pltpu.roll(prev, -k, 0))
        return jnp.where(row < -k, edge, pltpu.roll(cur, -k, 0))
    at_end = jnp.logical_or(pid == n_ctx_tiles - 1, pid == n_tiles - 1)
    edge = jnp.where(at_end, 0.0, pltpu.roll(nxt, tm - k, 0))
    return jnp.where(row >= tm - k, edge, pltpu.roll(cur, tm - k, 0))


def _f_conv(pid, xp, xc, xn, w, b, *, n_ctx_tiles, n_tiles):
    y = b
    for j in range(CONV_W):
        y = y + _shifted(xp, xc, xn, j - CONV_W // 2, pid, n_ctx_tiles, n_tiles) * w[j:j + 1]
    return (y,)


def _f_conv_bwd(pid, xp, xc, xn, dp, dc, dn, w, *, n_ctx_tiles, n_tiles):
    dx = jnp.zeros_like(dc)
    dw = []
    for j in range(CONV_W):
        k = j - CONV_W // 2
        dx = dx + _shifted(dp, dc, dn, -k, pid, n_ctx_tiles, n_tiles) * w[j:j + 1]
        dw.append(jnp.sum(dc * _shifted(xp, xc, xn, k, pid, n_ctx_tiles, n_tiles), axis=0, keepdims=True))
    return (dx, jnp.concatenate(dw, axis=0), jnp.sum(dc, axis=0, keepdims=True))


def _mesh_pos():
    return lax.axis_index("x"), lax.axis_index("y"), lax.axis_index("c")


def _remote(src, dst, send_sems, recv_sems, k, to):
    return pltpu.make_async_remote_copy(src_ref=src, dst_ref=dst, send_sem=send_sems.at[k], recv_sem=recv_sems.at[k],
                                        device_id=to, device_id_type=pl.DeviceIdType.MESH)


def _neighbours():
    x, y, c = _mesh_pos()
    return (x, y, c), (x, y, 1 - c), [(1 - x, y), (x, 1 - y), (1 - x, 1 - y)]


def _gather_comm(arrays):
    n = len(arrays)
    per = 7

    def slot(out, blk):
        return out.at[4 * blk[0] + 2 * blk[1] + blk[2]]

    def start(ins, outs, send, recv, local):
        me, sib, chips = _neighbours()
        for ai in range(n):
            pltpu.make_async_copy(ins[ai], slot(outs[ai], me), local.at[ai]).start()
            _remote(ins[ai], slot(outs[ai], me), send, recv, ai * per, sib).start()
            for j, chip in enumerate(chips):
                _remote(ins[ai], slot(outs[ai], me), send, recv, ai * per + 1 + j, (*chip, me[2])).start()

    def finish(ins, outs, send, recv, local):
        me, sib, chips = _neighbours()
        for ai in range(n):
            for j, chip in enumerate(chips):
                blk = slot(outs[ai], (*chip, me[2]))
                _remote(blk, blk, send, recv, ai * per + 1 + j, me).wait_recv()
                _remote(blk, blk, send, recv, ai * per + 4 + j, sib).start()
        for ai in range(n):
            blk = slot(outs[ai], sib)
            _remote(blk, blk, send, recv, ai * per, me).wait_recv()
            for j, chip in enumerate(chips):
                blk = slot(outs[ai], (*chip, 1 - me[2]))
                _remote(blk, blk, send, recv, ai * per + 4 + j, me).wait_recv()
            for k in range(per):
                _remote(ins[ai], slot(outs[ai], me), send, recv, ai * per + k, sib).wait_send()
            pltpu.make_async_copy(ins[ai], slot(outs[ai], me), local.at[ai]).wait()

    return _Comm(arrays, [_sds((N_DEV,) + a.shape, a.dtype) for a in arrays], n * per, start, finish)


def _swap_comm(arrays):
    n = len(arrays)

    def start(ins, outs, send, recv, local):
        me, sib, _ = _neighbours()
        for ai in range(n):
            for q in range(4):
                _remote(ins[ai].at[2 * q + 1 - me[2]], outs[ai].at[q], send, recv, ai * 4 + q, sib).start()

    def finish(ins, outs, send, recv, local):
        me, sib, _ = _neighbours()
        for ai in range(n):
            for q in range(4):
                cp = _remote(ins[ai].at[q], outs[ai].at[q], send, recv, ai * 4 + q, sib)
                cp.wait_recv()
                cp.wait_send()

    return _Comm(arrays, [_sds((4,) + a.shape[1:], a.dtype) for a in arrays], n * 4, start, finish)


def _chips_comm(arrays):
    n = len(arrays)

    def start(ins, outs, send, recv, local):
        me, _, chips = _neighbours()
        mine = 2 * me[0] + me[1]
        for ai in range(n):
            pltpu.make_async_copy(ins[ai].at[mine], outs[ai].at[mine], local.at[ai]).start()
            for j, chip in enumerate(chips):
                _remote(ins[ai].at[2 * chip[0] + chip[1]], outs[ai].at[mine], send, recv, ai * 3 + j, (*chip, me[2])).start()

    def finish(ins, outs, send, recv, local):
        me, _, chips = _neighbours()
        mine = 2 * me[0] + me[1]
        for ai in range(n):
            for j, chip in enumerate(chips):
                theirs = 2 * chip[0] + chip[1]
                cp = _remote(ins[ai].at[theirs], outs[ai].at[theirs], send, recv, ai * 3 + j, (*chip, me[2]))
                cp.wait_recv()
                cp.wait_send()
            pltpu.make_async_copy(ins[ai].at[mine], outs[ai].at[mine], local.at[ai]).wait()

    return _Comm(arrays, [_sds(a.shape, a.dtype) for a in arrays], n * 3, start, finish)


def _run_comm(name, comm):
    n_in, n_out = len(comm.ins), len(comm.out_shapes)

    def body(*refs):
        ins, outs, sems = refs[:n_in], refs[n_in:n_in + n_out], refs[n_in + n_out:]
        comm.start(ins, outs, *sems)
        comm.finish(ins, outs, *sems)

    return pl.pallas_call(body, in_specs=[_ANY] * n_in, out_specs=[_ANY] * n_out, out_shape=comm.out_shapes, name=name,
                          scratch_shapes=comm.scratch(), compiler_params=pltpu.CompilerParams(has_side_effects=True))(*comm.ins)


def _chip_add(name, blocks, theirs, core):
    _, r, c = blocks.shape
    tr = _tile(r, _row_block(c, 2), 16)

    def body(core_ref, a_ref, b_ref, o_ref):
        o_ref[...] = (a_ref[...].astype(F32) + b_ref[...].astype(F32)).astype(o_ref.dtype)

    spec = pl.BlockSpec((None, tr, c), lambda q, i, core_ref: (q, i, 0))
    grid_spec = pltpu.PrefetchScalarGridSpec(
        num_scalar_prefetch=1, grid=(4, r // tr),
        in_specs=[pl.BlockSpec((None, tr, c), lambda q, i, core_ref: (2 * q + core_ref[0], i, 0)), spec], out_specs=spec)
    return pl.pallas_call(body, grid_spec=grid_spec, out_shape=_sds((4, r, c), blocks.dtype), name=name,
                          compiler_params=_params(("parallel", "parallel")))(jnp.reshape(core, (1,)).astype(jnp.int32), blocks, theirs)


def _sum_lead(name, a):
    n, r, c = a.shape
    tr = _tile(r, _row_block(c, 4 * n // 2), 16)

    def body(a_ref, o_ref):
        acc = a_ref[0].astype(F32)
        for j in range(1, n):
            acc = acc + a_ref[j].astype(F32)
        o_ref[...] = acc

    return pl.pallas_call(body, grid=(r // tr,), in_specs=[pl.BlockSpec((n, tr, c), lambda i: (0, i, 0))],
                          out_specs=pl.BlockSpec((tr, c), lambda i: (i, 0)), out_shape=_sds((r, c), F32), name=name,
                          compiler_params=_params(("parallel",)))(a)


def _adam_math(w, g, m, v):
    m = ADAM_B1 * m + (1.0 - ADAM_B1) * g
    v = ADAM_B2 * v + (1.0 - ADAM_B2) * (g * g)
    m_hat = m / (1.0 - ADAM_B1 ** ADAM_STEP)
    v_hat = v / (1.0 - ADAM_B2 ** ADAM_STEP)
    delta = -ADAM_LR * (m_hat / (jnp.sqrt(v_hat) + ADAM_EPS) + ADAM_WD * w)
    return delta, m, v


def _adam_recv(name, recvs, w, m, v):
    nl = len(recvs)
    n, rl, c = recvs[0].shape
    tr = _tile(rl, _row_block(c, 4), 16)
    per = rl // tr

    def body(*refs):
        g_refs = refs[:nl]
        w_ref, m_ref, v_ref, go_ref, d_ref, mo_ref, vo_ref = refs[nl:]
        for layer in range(nl):
            @pl.when(pl.program_id(0) == layer)
            def _(g_ref=g_refs[layer]):
                g = g_ref[0].astype(F32)
                for j in range(1, n):
                    g = g + g_ref[j].astype(F32)
                delta, m2, v2 = _adam_math(w_ref[...], g, m_ref[...], v_ref[...])
                go_ref[...] = g
                d_ref[...] = delta
                mo_ref[...] = m2
                vo_ref[...] = v2

    g_specs = [pl.BlockSpec((n, tr, c), lambda l, i, layer=layer: (0, jnp.where(l == layer, i, 0), 0)) for layer in range(nl)]
    spec = pl.BlockSpec((tr, c), lambda l, i: (l * per + i, 0))
    return pl.pallas_call(body, grid=(nl, per), in_specs=g_specs + [spec] * 3, out_specs=[spec] * 4,
                          out_shape=[_sds((nl * rl, c), F32)] * 4, name=name,
                          compiler_params=_params(("arbitrary", "arbitrary")))(*recvs, w, m, v)


def _adam_f32(name, g, w, m, v, comm=None):
    r, c = g.shape
    tr = _tile(r, _row_block(c, 4), 8)
    steps = r // tr

    def body(*refs):
        (g_ref, w_ref, m_ref, v_ref), c_ins, (d_ref, mo_ref, vo_ref), c_outs, _, c_sems = _split_refs(refs, 4, 3, 0, comm)
        if comm is not None:
            @pl.when(pl.program_id(0) == 0)
            def _():
                comm.start(c_ins, c_outs, *c_sems)

        delta, m2, v2 = _adam_math(w_ref[...], g_ref[...], m_ref[...], v_ref[...])
        d_ref[...] = delta
        mo_ref[...] = m2
        vo_ref[...] = v2
        if comm is not None:
            @pl.when(pl.program_id(0) == steps - 1)
            def _():
                comm.finish(c_ins, c_outs, *c_sems)

    spec = pl.BlockSpec((tr, c), lambda i: (i, 0))
    c_in, c_out, c_scr = (comm.ins, comm.out_shapes, comm.scratch()) if comm is not None else ([], [], [])
    return pl.pallas_call(body, grid=(steps,), in_specs=[spec] * 4 + [_ANY] * len(c_in), out_specs=[spec] * 3 + [_ANY] * len(c_out),
                          out_shape=[_sds((r, c), F32)] * 3 + list(c_out), scratch_shapes=c_scr, name=name,
                          compiler_params=_params(("arbitrary",)))(g, w, m, v, *c_in)


def _mod_fwd(c16, w_mod, b_loc):
    nl, d, n6 = w_mod.shape
    tn = _tile(n6, 512, 128)

    def body(c_ref, w_ref, b_ref, o_ref):
        cv = c_ref[...]
        s = cv * _sigmoid(cv)
        o_ref[0] = jnp.dot(s, w_ref[0], precision=HIGHEST, preferred_element_type=F32) + b_ref[0]

    return pl.pallas_call(
        body, grid=(nl, n6 // tn),
        in_specs=[pl.BlockSpec((MOD_ROWS, d), lambda i, j: (0, 0)), pl.BlockSpec((1, d, tn), lambda i, j: (i, 0, j)),
                  pl.BlockSpec((1, 1, tn), lambda i, j: (i, 0, j))],
        out_specs=pl.BlockSpec((1, MOD_ROWS, tn), lambda i, j: (i, 0, j)), out_shape=_sds((nl, MOD_ROWS, n6), F32),
        name="mod_fwd", compiler_params=_params(("parallel", "parallel")))(c16, w_mod, b_loc)


def _mod_bwd(c16, w_mod, dmod_loc):
    nl, d, n6 = w_mod.shape
    tn = _tile(n6, 512, 128)

    def body(c_ref, w_ref, dm_ref, dw_ref, ds_ref):
        @pl.when(jnp.logical_and(pl.program_id(0) == 0, pl.program_id(1) == 0))
        def _():
            ds_ref[...] = jnp.zeros_like(ds_ref)

        cv = c_ref[...]
        s = cv * _sigmoid(cv)
        dm = dm_ref[0]
        dw_ref[0] = lax.dot_general(s, dm, (((0,), (0,)), ((), ())), precision=HIGHEST, preferred_element_type=F32)
        ds_ref[...] += lax.dot_general(dm, w_ref[0], (((1,), (1,)), ((), ())), precision=HIGHEST, preferred_element_type=F32)

    return pl.pallas_call(
        body, grid=(nl, n6 // tn),
        in_specs=[pl.BlockSpec((MOD_ROWS, d), lambda i, j: (0, 0)), pl.BlockSpec((1, d, tn), lambda i, j: (i, 0, j)),
                  pl.BlockSpec((1, MOD_ROWS, tn), lambda i, j: (i, 0, j))],
        out_specs=[pl.BlockSpec((1, d, tn), lambda i, j: (i, 0, j)), pl.BlockSpec((MOD_ROWS, d), lambda i, j: (0, 0))],
        out_shape=[_sds((nl, d, n6), F32), _sds((MOD_ROWS, d), F32)], name="mod_bwd",
        compiler_params=_params(("arbitrary", "arbitrary")))(c16, w_mod, dmod_loc)


def _pack(parts):
    flat = [p.reshape(-1).astype(F32) for p in parts]
    offs = np.cumsum([0] + [f.shape[0] for f in flat])
    total = int(offs[-1])
    unit = (PACK_ROWS if total > PACK_ROWS * LANES else 8) * LANES
    padded = -(-total // unit) * unit
    slab = jnp.concatenate(flat + [jnp.zeros((padded - total,), F32)])
    return slab.reshape(padded // LANES, LANES), [int(o) for o in offs]


def _unshard_cols(seg, lead):
    n = seg.shape[1] // int(np.prod(lead)) if lead else seg.shape[1]
    a = seg.reshape((N_DEV,) + tuple(lead) + (n,))
    a = jnp.moveaxis(a, 0, len(lead))
    return a.reshape(tuple(lead) + (N_DEV * n,))


def _my_cols(a, me, n):
    start = (0,) * (a.ndim - 1) + (me * n,)
    return lax.dynamic_slice(a, start, a.shape[:-1] + (n,))


def _rope_tables(seq, n_ctx):
    rows = seq // GRID_W
    r_idx, c_idx = jnp.meshgrid(jnp.arange(rows), jnp.arange(GRID_W), indexing='ij')
    r_idx = r_idx.reshape(-1).astype(F32)
    c_idx = c_idx.reshape(-1).astype(F32)
    pairs = HEAD_DIM // 4
    freqs = ROPE_THETA ** (-jnp.arange(pairs, dtype=F32) / pairs)
    ang_r, ang_c = r_idx[:, None] * freqs, c_idx[:, None] * freqs
    cos = jnp.concatenate([jnp.cos(ang_r)] * 2 + [jnp.cos(ang_c)] * 2, axis=1)
    sin = jnp.concatenate([-jnp.sin(ang_r), jnp.sin(ang_r), -jnp.sin(ang_c), jnp.sin(ang_c)], axis=1)
    cos = jnp.concatenate([jnp.ones((n_ctx, HEAD_DIM), F32), cos], axis=0)
    sin = jnp.concatenate([jnp.zeros((n_ctx, HEAD_DIM), F32), sin], axis=0)
    lane = np.arange(HEAD_DIM)
    partner = np.where(lane % (2 * pairs) < pairs, lane + pairs, lane - pairs)
    perm = np.zeros((HEAD_DIM, HEAD_DIM), np.float32)
    perm[partner, lane] = 1.0
    return cos, sin, jnp.asarray(perm)


def kernel(x, c, ctx, c_ctx, w_mod, b_mod, norm_g, w_ff_in, w_ff_out, ar_w_in, ar_q_g, ar_k_g, ar_conv_w, ar_conv_b, ar_wa, ar_ba, ar_wx, ar_bx, ar_lambda, ar_w_out, gm_w_in, gm_b_in, gm_v_g, gm_v_b, gm_w_sp, gm_b_sp, gm_w_out, loss_target, m_c_ctx, m_w_mod, m_b_mod, m_norm_g, m_w_ff_in, m_w_ff_out, m_ar_w_in, m_ar_q_g, m_ar_k_g, m_ar_conv_w, m_ar_conv_b, m_ar_wa, m_ar_ba, m_ar_wx, m_ar_bx, m_ar_lambda, m_ar_w_out, m_gm_w_in, m_gm_b_in, m_gm_v_g, m_gm_v_b, m_gm_w_sp, m_gm_b_sp, m_gm_w_out, v_c_ctx, v_w_mod, v_b_mod, v_norm_g, v_w_ff_in, v_w_ff_out, v_ar_w_in, v_ar_q_g, v_ar_k_g, v_ar_conv_w, v_ar_conv_b, v_ar_wa, v_ar_ba, v_ar_wx, v_ar_bx, v_ar_lambda, v_ar_w_out, v_gm_w_in, v_gm_b_in, v_gm_v_g, v_gm_v_b, v_gm_w_sp, v_gm_b_sp, v_gm_w_out):
    given = dict(locals())
    wts = {n: given[n] for n in WEIGHTS}
    mom1 = {n: given["m_" + n] for n in WEIGHTS}
    mom2 = {n: given["v_" + n] for n in WEIGHTS}

    xi, yi, ci = _mesh_pos()
    me = 4 * xi + 2 * yi + ci

    seq, d = x.shape[1], x.shape[2]
    n_ctx = ctx.shape[1]
    t_all = n_ctx + seq
    n_layers = w_mod.shape[0]
    assert n_layers == 2 and ar_w_in.shape[0] == 1 and gm_w_in.shape[0] == 1
    d_ff = w_ff_in.shape[2] * N_DEV
    attn_w, kv_w = N_HEADS * HEAD_DIM, N_KV_HEADS * HEAD_DIM
    rnn_blocks = ar_wa.shape[2]
    d_rnn = rnn_blocks * RNN_BLOCK_W
    gm_groups = gm_w_sp.shape[1]
    d_gm = gm_groups * GM_GROUP_W
    ar_in = ar_w_in.shape[2] * N_DEV
    n6 = w_mod.shape[2]
    tm, tmb = ROW_TILE, ROW_TILE_BWD
    assert attn_w == d_rnn and ar_in == 3 * attn_w + 2 * kv_w and (3 * attn_w) % (2 * kv_w) == 0
    assert n_ctx % tm == 0 and seq % tm == 0 and n_ctx % SCAN_BLOCK == 0 and seq % SCAN_BLOCK == 0 and tm % CHUNK == 0
    nct, nctb = n_ctx // tm, n_ctx // tmb
    kv_blk = (3 * attn_w) // (2 * kv_w)
    lr = d_rnn // LANES

    x2, ctx2, tgt = x[0], ctx[0], loss_target[0]

    def cols_full(g):
        return jnp.moveaxis(g, 0, 1).reshape(g.shape[1], N_DEV * g.shape[2])

    small0, off0 = _pack([c[0], norm_g, ar_conv_w[0], ar_ba[0], ar_bx[0], ar_lambda[0], gm_b_in[0], gm_v_g[0], gm_v_b[0]])
    (gs0,) = _run_comm("gather_first", _gather_comm([small0]))
    gs0 = gs0.reshape(N_DEV, -1)
    w1, w2 = [None] * n_layers, [None] * n_layers

    def seg0(k):
        return gs0[:, off0[k]:off0[k + 1]]

    c_all = seg0(0)
    norm_full = _unshard_cols(seg0(1), (n_layers, 4))
    conv_w = _unshard_cols(seg0(2), (CONV_W,))
    ba, bx, lam = (_unshard_cols(seg0(k), (2,)) for k in (3, 4, 5))
    gm_b_in_f = seg0(6).reshape(1, 2 * d_gm)
    gm_vg, gm_vb = seg0(7).reshape(1, d_gm), seg0(8).reshape(1, d_gm)

    c16 = jnp.concatenate([c_all, c_ctx[None], jnp.zeros((MOD_ROWS - N_DEV - 1, d), F32)], axis=0)
    b_loc = _my_cols(b_mod, me, n6)[:, None, :]
    mod_loc = _mod_fwd(c16, w_mod, b_loc)
    (g_mod,) = _run_comm("gather_mod", _gather_comm([mod_loc]))
    mod_all = jnp.moveaxis(g_mod, 0, 2).reshape(n_layers, MOD_ROWS, N_DEV * n6)
    ml = lax.dynamic_index_in_dim(mod_all, me, axis=1, keepdims=False).reshape(n_layers, 6, d)
    mc = mod_all[:, N_DEV].reshape(n_layers, 6, d)

    def row(a, *idx):
        return a[idx][None]

    cos, sin, perm = _rope_tables(seq, n_ctx)
    wa3 = ar_wa[0].reshape(2 * rnn_blocks, RNN_BLOCK_W, RNN_BLOCK_W)
    wx3 = ar_wx[0].reshape(2 * rnn_blocks, RNN_BLOCK_W, RNN_BLOCK_W)
    conv_b = ar_conv_b
    q_g, k_g = ar_q_g, ar_k_g
    w_sp = gm_w_sp[0]
    bsp_t = jnp.pad(gm_b_sp[0].T, ((0, 0), (0, LANES - gm_groups)))
    expand = np.zeros((LANES, d_gm), np.float32)
    for g in range(gm_groups):
        expand[g, g * GM_GROUP_W:(g + 1) * GM_GROUP_W] = 1.0
    expand = jnp.asarray(expand)

    def relu2(acc):
        r = jnp.maximum(acc, 0.0)
        return (r * r,)

    def relu2_bwd(acc, act):
        return (acc * (2.0 * jnp.sqrt(act.astype(F32))),)

    def ff_in_shard(i):
        return w_ff_in[i].astype(BF16)

    def ff_out_shard(i):
        return w_ff_out[i].astype(BF16)

    tokens = [_t(ctx2), _t(x2, -n_ctx)]
    pre0_args = [row(norm_full, 0, 0), row(mc, 0, 0), row(mc, 0, 1), row(ml, 0, 0), row(ml, 0, 1)]
    f_pre0 = functools.partial(_f_pre_ctx, n_ctx_tiles=nct)
    h0, g_ar_in = _rowwise("pre0", f_pre0, t_all, tm, tokens, pre0_args, [(d, BF16)], comm=_gather_comm([ar_w_in[0].astype(BF16)]))
    w_in = cols_full(g_ar_in)
    split = [attn_w, attn_w + 2 * kv_w, attn_w + 2 * kv_w + d_rnn]
    w_in = jnp.concatenate([w_in[:, :split[0]], w_in[:, split[1]:], w_in[:, split[0]:split[1]]], axis=1)
    tm_tok = _tile(t_all, 640, 16)
    proj, (g_gm_in,) = _matmul("ar_in", h0, w_in, tm=tm_tok, comm=_gather_comm([gm_w_in[0].astype(BF16)]))
    f_qkv = functools.partial(_f_qkv, nh=N_HEADS, nkv=N_KV_HEADS)
    qkv_tiled = [_t(proj, 0, 0, attn_w), _t(proj, 0, kv_blk, 2 * kv_w), _t(cos), _t(sin)]
    q_r, kv_r = _rowwise("qkv", f_qkv, t_all, tm, qkv_tiled, [q_g, k_g, perm], [(attn_w, BF16), (2 * kv_w, BF16)])
    attn_o, (g_ar_out, g_ff_in0) = _attn_fwd(q_r, kv_r, nct, tm,
                                             comm=_gather_comm([ar_w_out[0].astype(BF16), ff_in_shard(0)]))
    w_out = g_ar_out.reshape(attn_w + d_rnn, d)
    w1[0] = cols_full(g_ff_in0)

    def with_neighbours(a, t, col_blk=0, width=None):
        return [_t(a, -t, col_blk, width), _t(a, 0, col_blk, width), _t(a, t, col_blk, width)]

    f_conv = functools.partial(_f_conv, n_ctx_tiles=nct, n_tiles=t_all // tm)
    (xc,) = _rowwise("conv", f_conv, t_all, tm, with_neighbours(proj, tm, 1, d_rnn), [conv_w, conv_b], [(d_rnn, F32)])
    f_gates = functools.partial(_f_gates, nb=rnn_blocks)
    gate_full = [wa3, ba, wx3, bx, lam]
    a_f, b_f, a_b, b_b, g_gm_out = _rowwise("gates", f_gates, t_all, tm, [_t(xc)], gate_full, [(d_rnn, F32)] * 4,
                                            comm=_gather_comm([gm_w_out[0].astype(BF16)]))

    def to3(a):
        return a.reshape(a.shape[0], lr, LANES)

    nc_scan = n_ctx // SCAN_BLOCK
    h_f, hp_f, h_b, hp_b = _scan_both(to3(a_f), to3(b_f), to3(a_b), to3(b_b), nc_scan)
    h_f2, h_b2 = h_f.reshape(t_all, d_rnn), h_b.reshape(t_all, d_rnn)
    rnn_tiled = [_t(h_f2, n_ctx), _t(h_b2, n_ctx), _t(proj, n_ctx, 2, d_rnn)]
    (rnn_o,) = _rowwise("rnn_out", _f_rnnout, seq, tm, rnn_tiled, [], [(d_rnn, BF16)])
    ar = jnp.concatenate([attn_o, rnn_o], axis=1)
    o0 = _matmul("ar_out", ar, w_out)
    mid0_args = [row(norm_full, 0, 1), row(ml, 0, 2), row(norm_full, 0, 2), row(ml, 0, 3), row(ml, 0, 4)]
    x1, h2_0 = _rowwise("mid0", _f_mid, seq, tm, [_t(x2), _t(o0)], mid0_args, [(d, F32), (d, BF16)])
    act0, (g_ff_out0,) = _matmul("ff_in_0", h2_0, w1[0], outs=(BF16,), epilogue=relu2, comm=_gather_comm([ff_out_shard(0)]))
    w2[0] = g_ff_out0.reshape(d_ff, d)
    w_gi, w_go = cols_full(g_gm_in), g_gm_out.reshape(d_gm, d)
    m0, (g_ff_in1,) = _matmul("ff_out_0", act0, w2[0], tm=MM_TILE_M // 2, tk=2 * MM_TILE_K, comm=_gather_comm([ff_in_shard(1)]))
    w1[1] = cols_full(g_ff_in1)
    post0_args = [row(norm_full, 0, 3), row(ml, 0, 5)]
    pre1_args = [row(norm_full, 1, 0), row(ml, 1, 0), row(ml, 1, 1)]

    def f_between_fwd(pid, xv, mv, g3, gate, g, sh, sc):
        (xo,) = _f_post(pid, xv, mv, g3, gate)
        return (xo, _f_pre(pid, xo, g, sh, sc)[0])

    x2l, h1 = _rowwise("between", f_between_fwd, seq, tm, [_t(x1), _t(m0)], post0_args + pre1_args, [(d, F32), (d, BF16)])
    zg = _matmul("gm_in", h1, w_gi)
    f_gm = functools.partial(_f_gm, n_chunks=tmb // CHUNK, groups=gm_groups)
    gm_full = [gm_b_in_f[:, :d_gm], gm_b_in_f[:, d_gm:], gm_vg, gm_vb, w_sp, bsp_t, expand]
    gm_tiled = [_t(zg, 0, 0, d_gm), _t(zg, 0, 1, d_gm)]
    (gmix,) = _rowwise("gm_mix", f_gm, seq, tmb, gm_tiled, gm_full, [(d_gm, BF16)])
    o1 = _matmul("gm_out", gmix, w_go)
    mid1_args = [row(norm_full, 1, 1), row(ml, 1, 2), row(norm_full, 1, 2), row(ml, 1, 3), row(ml, 1, 4)]
    x3, h2_1 = _rowwise("mid1", _f_mid, seq, tm, [_t(x2l), _t(o1)], mid1_args, [(d, F32), (d, BF16)])
    act1, (g_ff_out1,) = _matmul("ff_in_1", h2_1, w1[1], outs=(BF16,), epilogue=relu2,
                                       comm=_gather_comm([ff_out_shard(1)]))
    w2[1] = g_ff_out1.reshape(d_ff, d)
    m1 = _matmul("ff_out_1", act1, w2[1], tm=MM_TILE_M // 2, tk=2 * MM_TILE_K)
    post1_args = [row(norm_full, 1, 3), row(ml, 1, 5)]

    def f_loss(pid, xv, ov, tv, g, gate):
        (y,), vjp = jax.vjp(lambda *a: _f_post(pid, *a), xv, ov, g, gate)
        err = y - tv
        part = 0.5 * jnp.sum(err * err) / d
        dxv, dov, dg, dgate = vjp((err / d,))
        return (dxv, dov, jnp.full((8, LANES), part, F32), dg, dgate)

    dx3, dm1, loss_acc, dg_last, dgate_last = _rowwise("loss", f_loss, seq, tmb, [_t(x3), _t(m1), _t(tgt)], post1_args,
                                                       [(d, F32), (d, BF16)], [(8, LANES), (1, d), (1, d)])
    loss = lax.psum(loss_acc[0, 0], ("x", "y", "c"))

    d_norm = [[None] * 4 for _ in range(n_layers)]
    d_ml = [[None] * 6 for _ in range(n_layers)]
    recv = {}

    def cols_blocks(g):
        return jnp.moveaxis(g.reshape(g.shape[0], N_DEV, g.shape[1] // N_DEV), 1, 0)

    def rows_blocks(g):
        return g.reshape(N_DEV, g.shape[0] // N_DEV, g.shape[1])

    chip_sums = {}

    def chip_add(key, blocks, theirs):
        chip_sums[key] = _chip_add(key + "_add", blocks, theirs, ci)

    def mlp_bwd(i, dm, act, h2, first_comm=None):
        dw2 = _matmul(f"ff_out_dw_{i}", act, dm, ta=True, outs=(BF16,), comm=first_comm)
        dw2, carried = dw2 if first_comm is not None else (dw2, ())
        blk2 = rows_blocks(dw2)
        dz, (theirs,) = _matmul(f"ff_out_dx_{i}", dm, w2[i], tb=True, outs=(BF16,), extras=(act,), epilogue=relu2_bwd,
                                comm=_swap_comm([blk2]))
        chip_add(f"ff_out_{i}", blk2, theirs)
        blk1, (recv[f"ff_out_{i}"],) = _matmul(f"ff_in_dw_{i}", h2, dz, ta=True, outs=(BF16,), col_blocks=N_DEV,
                                              comm=_chips_comm([chip_sums[f"ff_out_{i}"]]))
        dh2, (theirs,) = _matmul(f"ff_in_dx_{i}", dz, w1[i], tb=True, tm=MM_TILE_M // 2, tk=2 * MM_TILE_K,
                                 comm=_swap_comm([blk1]))
        chip_add(f"ff_in_{i}", blk1, theirs)
        return dh2, carried

    def mid_bwd(i, xin, o, args, dx1, dh2):
        res = _rowwise(f"mid_bwd{i}", _bwd_of(_f_mid, 2, 2, (0, 1, 2, 3, 4, 5, 6)), seq, tmb,
                       [_t(xin), _t(o), _t(dx1), _t(dh2)], args, [(d, F32), (d, BF16)], [(1, d)] * 5)
        d_norm[i][1], d_ml[i][2], d_norm[i][2], d_ml[i][3], d_ml[i][4] = res[2:]
        return res[0], res[1]

    d_norm[1][3], d_ml[1][5] = dg_last, dgate_last
    dh2_1, _ = mlp_bwd(1, dm1, act1, h2_1)
    dx2a, do1 = mid_bwd(1, x2l, o1, mid1_args, dx3, dh2_1)
    blk_go = rows_blocks(_matmul("gm_out_dw", gmix, do1, ta=True, outs=(BF16,)))
    dgmix, (theirs,) = _matmul("gm_out_dx", do1, w_go, tb=True, comm=_swap_comm([blk_go]))
    chip_add("gm_out", blk_go, theirs)
    gm_res = _rowwise("gm_mix_bwd", _bwd_of(f_gm, 2, 1, (0, 1, 2, 3, 4, 5, 6, 7)), seq, tmb,
                      gm_tiled + [_t(dgmix)], gm_full, [(d_gm, BF16), (d_gm, BF16)],
                      [(1, d_gm)] * 4 + [w_sp.shape, bsp_t.shape])
    dzg = jnp.concatenate([gm_res[0], gm_res[1]], axis=1)
    g_gm_b_in = jnp.concatenate([gm_res[2], gm_res[3]], axis=1)
    g_gm_vg, g_gm_vb, g_w_sp = gm_res[4], gm_res[5], gm_res[6]
    g_b_sp = gm_res[7][:, :gm_groups].T
    dh1, (recv["gm_out"],) = _matmul("gm_in_dx", dzg, w_gi, tb=True, comm=_chips_comm([chip_sums["gm_out"]]))
    blk_gi = _matmul("gm_in_dw", h1, dzg, ta=True, outs=(BF16,), col_blocks=N_DEV)

    def f_between(pid, xv, dh, dxa, x1v, m0v, g, sh, sc, g3, gate):
        dxv, dg, dsh, dsc = _bwd_of(_f_pre, 1, 1, (0, 1, 2, 3))(pid, xv, dh, g, sh, sc)
        dx1v, dm0v, dg3, dgate = _bwd_of(_f_post, 2, 1, (0, 1, 2, 3))(pid, x1v, m0v, dxv + dxa, g3, gate)
        return (dx1v, dm0v, dg, dsh, dsc, dg3, dgate)

    res = _rowwise("between_bwd", f_between, seq, tmb, [_t(x2l), _t(dh1), _t(dx2a), _t(x1), _t(m0)], pre1_args + post0_args,
                   [(d, F32), (d, BF16)], [(1, d)] * 5)
    dx1, dm0 = res[0], res[1]
    d_norm[1][0], d_ml[1][0], d_ml[1][1], d_norm[0][3], d_ml[0][5] = res[2:]

    dh2_0, (theirs,) = mlp_bwd(0, dm0, act0, h2_0, first_comm=_swap_comm([blk_gi]))
    chip_add("gm_in", blk_gi, theirs)
    dxa, do0 = mid_bwd(0, x2, o0, mid0_args, dx1, dh2_0)
    blk_out = rows_blocks(_matmul("ar_out_dw", ar, do0, ta=True, outs=(BF16,)))
    d_ar, (theirs,) = _matmul("ar_out_dx", do0, w_out, tb=True, comm=_swap_comm([blk_out]))
    chip_add("ar_out", blk_out, theirs)

    late = ["ff_in_1", "gm_in", "ff_in_0"]
    dq, dkt, dvt, *landed = _attn_bwd(q_r, kv_r, d_ar, nct, tm, comm=_chips_comm([chip_sums[k] for k in late]))
    recv.update(zip(late, landed))
    dkv_all = jnp.concatenate([dkt, dvt], axis=0).T

    def f_qkv_bwd(pid, pq, pkv, cos_t, sin_t, dq_t, dkv_t, *fulls):
        dq_t = jnp.where(pid < nctb, 0.0, dq_t)
        return _bwd_of(f_qkv, 4, 2, (0, 1, 4, 5))(pid, pq, pkv, cos_t, sin_t, dq_t, dkv_t, *fulls)

    qkv_res = _rowwise("qkv_bwd", f_qkv_bwd, t_all, tmb, qkv_tiled + [_t(dq, -n_ctx), _t(dkv_all)],
                       [q_g, k_g, perm], [(attn_w, BF16), (2 * kv_w, BF16)], [q_g.shape, k_g.shape])
    dproj_q, dproj_kv, g_q_g, g_k_g = qkv_res

    rnn_res = _rowwise("rnn_out_bwd", _bwd_of(_f_rnnout, 3, 1, (0, 2)), seq, tmb, rnn_tiled + [_t(d_ar, 0, 1, d_rnn)], [],
                       [(d_rnn, F32), (d_rnn, BF16)])
    zc = jnp.zeros((n_ctx, d_rnn), F32)
    dh_all = to3(jnp.concatenate([zc, rnn_res[0]], axis=0))
    dproj_g = jnp.concatenate([zc.astype(BF16), rnn_res[1]], axis=0)
    da_f, db_f, da_b, db_b = _scan_both_bwd(to3(a_f), hp_f, to3(a_b), hp_b, dh_all, nc_scan)
    gate_cts = [_t(a.reshape(t_all, d_rnn)) for a in (da_f, db_f, da_b, db_b)]
    gates_res = _rowwise("gates_bwd", _bwd_of(f_gates, 1, 4, (0, 1, 2, 3, 4, 5)), t_all, min(tmb, GATES_BWD_TILE),
                         [_t(xc)] + gate_cts, gate_full, [(d_rnn, F32)], [wa3.shape, ba.shape, wx3.shape, bx.shape, lam.shape])
    dxc, g_wa, g_ba, g_wx, g_bx, g_lam = gates_res
    f_conv_b = functools.partial(_f_conv_bwd, n_ctx_tiles=nctb, n_tiles=t_all // tmb)
    conv_tiled = with_neighbours(proj, tmb, 1, d_rnn) + with_neighbours(dxc, tmb)
    dproj_x, g_conv_w, g_conv_b = _rowwise("conv_bwd", f_conv_b, t_all, tmb, conv_tiled, [conv_w],
                                           [(d_rnn, BF16)], [conv_w.shape, (1, d_rnn)])
    dproj = jnp.concatenate([dproj_q, dproj_x, dproj_g, dproj_kv], axis=1)
    sq_names = ['ar_wa', 'ar_wx', 'gm_w_sp']

    def stack_sq(parts):
        return jnp.concatenate([p.reshape(-1, LANES) for p in parts], axis=0)

    sq_pack = stack_sq([g_wa, g_wx, g_w_sp]).astype(BF16)
    g_w_in, (g_sq,) = _matmul("ar_in_dw", h0, dproj, ta=True, outs=(BF16,), comm=_gather_comm([sq_pack]))
    g_w_in = jnp.concatenate([g_w_in[:, :attn_w], g_w_in[:, 3 * attn_w:], g_w_in[:, attn_w:3 * attn_w]], axis=1)
    blk_in = cols_blocks(g_w_in)
    dh0, (recv["ar_out"], theirs) = _matmul("ar_in_dx", dproj, w_in, tb=True, tm=tm_tok,
                                            comm=_both(_chips_comm([chip_sums["ar_out"]]), _swap_comm([blk_in])))
    chip_add("ar_in", blk_in, theirs)

    f_pre0b = functools.partial(_f_pre_ctx, n_ctx_tiles=nctb)

    def f_pre0_bwd(pid, xcv, xlv, dh, dxp, g, shc, scc, shl, scl):
        grads = _bwd_of(f_pre0b, 2, 1, (1, 2, 3, 4, 5, 6))(pid, xcv, xlv, dh, g, shc, scc, shl, scl)
        return (grads[0] + dxp,) + tuple(grads[1:])

    res = _rowwise("pre_bwd0", f_pre0_bwd, seq, tmb, tokens + [_t(dh0), _t(dxa, -n_ctx)], pre0_args, [(d, F32)], [(1, d)] * 5,
                   comm=_chips_comm([chip_sums["ar_in"]]), skip_rows=n_ctx)
    grad_x = res[0][None]
    d_norm[0][0], d_mc_shift, d_mc_scale, d_ml[0][0], d_ml[0][1] = res[1:6]
    recv["ar_in"] = res[6]

    z1d = jnp.zeros((1, d), F32)
    dml = jnp.concatenate([jnp.concatenate(r, axis=0)[None] for r in d_ml], axis=0)
    dmc = jnp.concatenate([jnp.concatenate([d_mc_shift, d_mc_scale] + [z1d] * 4, axis=0)[None],
                           jnp.zeros((n_layers - 1, 6, d), F32)], axis=0)
    g_norm = jnp.concatenate([jnp.concatenate(r, axis=0)[None] for r in d_norm], axis=0)
    small_parts = [dmc, g_norm, g_q_g, g_k_g, g_conv_w, g_conv_b, g_ba, g_bx, g_lam, g_gm_b_in, g_gm_vg, g_gm_vb, g_b_sp]
    small2, off2 = _pack([dml] + small_parts)
    (gs2,) = _run_comm("gather_small_grads", _gather_comm([small2]))
    dml_all = gs2.reshape(N_DEV, -1)[:, :off2[1]].reshape(N_DEV, n_layers, 6 * d)
    summed = _sum_lead("sum_small_grads", gs2).reshape(-1)
    summed_sq = _sum_lead("sum_square_grads", g_sq)

    def seg2(k, shape):
        return summed[off2[k + 1]:off2[k + 2]].reshape(shape)

    dmc_sum = seg2(0, (n_layers, 6 * d))
    dmod_rows = jnp.concatenate([jnp.moveaxis(dml_all, 0, 1), dmc_sum[:, None, :],
                                 jnp.zeros((n_layers, MOD_ROWS - N_DEV - 1, 6 * d), F32)], axis=1)
    g_b_mod = _sum_lead("sum_b_mod", jnp.moveaxis(dmod_rows, 1, 0).reshape(MOD_ROWS, n_layers * 6 * d // LANES, LANES))
    g_b_mod = g_b_mod.reshape(n_layers, 6 * d)
    g_w_mod, ds16 = _mod_bwd(c16, w_mod, _my_cols(dmod_rows, me, n6))
    (g_ds,) = _run_comm("gather_dctx", _gather_comm([ds16[N_DEV].reshape(d // LANES, LANES)]))
    ds_ctx = _sum_lead("sum_dctx", g_ds)
    (g_c_ctx,) = _rowwise("silu_bwd", _f_silu_mul, d // LANES, d // LANES, [_t(c_ctx.reshape(d // LANES, LANES)), _t(ds_ctx)],
                          [], [(LANES, F32)])
    g_c_ctx = g_c_ctx.reshape(d)

    grads = {
        'c_ctx': g_c_ctx, 'b_mod': g_b_mod,
        'norm_g': _my_cols(seg2(1, (n_layers, 4, d)), me, d // N_DEV),
        'ar_q_g': seg2(2, ar_q_g.shape), 'ar_k_g': seg2(3, ar_k_g.shape),
        'ar_conv_w': _my_cols(seg2(4, (1, CONV_W, d_rnn)), me, d_rnn // N_DEV),
        'ar_conv_b': seg2(5, ar_conv_b.shape),
        'ar_ba': _my_cols(seg2(6, (1, 2, d_rnn)), me, d_rnn // N_DEV),
        'ar_bx': _my_cols(seg2(7, (1, 2, d_rnn)), me, d_rnn // N_DEV),
        'ar_lambda': _my_cols(seg2(8, (1, 2, d_rnn)), me, d_rnn // N_DEV),
        'gm_b_in': _my_cols(seg2(9, (1, 2 * d_gm)), me, 2 * d_gm // N_DEV),
        'gm_v_g': _my_cols(seg2(10, (1, d_gm)), me, d_gm // N_DEV),
        'gm_v_b': _my_cols(seg2(11, (1, d_gm)), me, d_gm // N_DEV),
        'gm_b_sp': seg2(12, gm_b_sp.shape),
    }
    small_names = list(grads)
    deltas, new_m, new_v = {}, {}, {}

    sq_res = (summed_sq,) + tuple(_adam_f32("adam_square", summed_sq, *[stack_sq([src[n] for n in sq_names])
                                                                        for src in (wts, mom1, mom2)]))
    first = 0
    for n in sq_names:
        rows_n = wts[n].size // LANES
        for dst, slab in zip((grads, deltas, new_m, new_v), sq_res):
            dst[n] = slab[first:first + rows_n].reshape(wts[n].shape)
        first += rows_n

    wp, offw = _pack([wts[n] for n in small_names])
    mp, _ = _pack([mom1[n] for n in small_names])
    vp, _ = _pack([mom2[n] for n in small_names])
    gp, _ = _pack([grads[n] for n in small_names])
    dp, mp2, vp2 = _adam_f32("adam_small", gp, wp, mp, vp)
    for k, n in enumerate(small_names):
        for dst, slab in ((deltas, dp), (new_m, mp2), (new_v, vp2)):
            dst[n] = slab.reshape(-1)[offw[k]:offw[k + 1]].reshape(wts[n].shape)

    grads['w_mod'] = g_w_mod
    dw, mw, vw = _adam_f32("adam_w_mod", g_w_mod.reshape(n_layers * d, n6), w_mod.reshape(n_layers * d, n6),
                           m_w_mod.reshape(n_layers * d, n6), v_w_mod.reshape(n_layers * d, n6))
    deltas['w_mod'], new_m['w_mod'], new_v['w_mod'] = (a.reshape(w_mod.shape) for a in (dw, mw, vw))

    received = {
        'w_ff_in': [recv[f"ff_in_{i}"] for i in range(n_layers)], 'w_ff_out': [recv[f"ff_out_{i}"] for i in range(n_layers)],
        'ar_w_in': [recv["ar_in"]], 'ar_w_out': [recv["ar_out"]], 'gm_w_in': [recv["gm_in"]], 'gm_w_out': [recv["gm_out"]]}
    for n, r in received.items():
        shp = wts[n].shape
        flat = (shp[0] * shp[1], shp[2])
        res = _adam_recv("adam_" + n, r, wts[n].reshape(flat), mom1[n].reshape(flat), mom2[n].reshape(flat))
        grads[n], deltas[n], new_m[n], new_v[n] = (a.reshape(shp) for a in res)

    return (loss, grad_x, *[grads[n] for n in WEIGHTS], *[deltas[n] for n in WEIGHTS],
            *[new_m[n] for n in WEIGHTS], *[new_v[n] for n in WEIGHTS])
```

```python
import functools

import numpy as np
import jax
import jax.numpy as jnp
from jax import lax
from jax.experimental import pallas as pl
from jax.experimental.pallas import tpu as pltpu

F32 = jnp.float32
BF16 = jnp.bfloat16
HIGHEST = lax.Precision.HIGHEST
LOG2_E = 1.4426950408889634
LN_2 = 0.6931471805599453

GRID_W = 64
N_HEADS = 8
N_KV_HEADS = 2
HEAD_DIM = 128
ROPE_THETA = 10000.0
RNN_BLOCK_W = 128
CONV_W = 4
RG_C = 8.0
GM_GROUP_W = 128
CHUNK = 128
EPS = 1e-6
ADAM_LR = 0.001
ADAM_B1 = 0.9
ADAM_B2 = 0.999
ADAM_EPS = 1e-08
ADAM_WD = 0.01
ADAM_STEP = 10

N_DEV = 8
MOD_ROWS = 16
LANES = 128
ROW_TILE = 256
ROW_TILE_BWD = 256
ATTN_BWD_HEADS_PER_STEP = 4
ATTN_HEADS_PER_STEP = 4
GATES_BWD_TILE = 64
SCAN_BLOCK = 256
VMEM_LIMIT = 56 * 1024 * 1024
PACK_ROWS = 512
MM_TILE_M = 1024
MM_TILE_N = 1024
MM_TILE_K = 2048

WEIGHTS = ['c_ctx', 'w_mod', 'b_mod', 'norm_g', 'w_ff_in', 'w_ff_out', 'ar_w_in', 'ar_q_g', 'ar_k_g', 'ar_conv_w',
           'ar_conv_b', 'ar_wa', 'ar_ba', 'ar_wx', 'ar_bx', 'ar_lambda', 'ar_w_out', 'gm_w_in', 'gm_b_in', 'gm_v_g',
           'gm_v_b', 'gm_w_sp', 'gm_b_sp', 'gm_w_out']


def _sds(shape, dtype):
    return jax.ShapeDtypeStruct(tuple(shape), dtype)


def _tile(dim, pref, align):
    t = (min(pref, dim) // align) * align
    while t >= align:
        if dim % t == 0:
            return t
        t -= align
    return dim


def _params(sem):
    return pltpu.CompilerParams(dimension_semantics=sem, vmem_limit_bytes=VMEM_LIMIT)


def _rms(x, g):
    return x * lax.rsqrt(jnp.mean(x * x, axis=-1, keepdims=True) + EPS) * g


def _gelu(x):
    return 0.5 * x * (1.0 + jnp.tanh(0.7978845608028654 * (x + 0.044715 * (x * x * x))))


def _sigmoid(x):
    return 0.5 * (jnp.tanh(0.5 * x) + 1.0)


def _log1p_pos(u):
    small = u < 1e-3
    us = jnp.where(small, u, 0.0)
    return jnp.where(small, us * (1.0 - us * (0.5 - us * (1.0 / 3.0))), jnp.log(1.0 + u))


def _softplus(x):
    return jnp.maximum(x, 0.0) + _log1p_pos(jnp.exp(-jnp.abs(x)))


def _f_pre_ctx(pid, xc, xl, g, shc, scc, shl, scl, *, n_ctx_tiles):
    is_ctx = pid < n_ctx_tiles
    x = jnp.where(is_ctx, xc, xl)
    sh = jnp.where(is_ctx, shc, shl)
    sc = jnp.where(is_ctx, scc, scl)
    return (_rms(x, g) * (1.0 + sc) + sh,)


def _f_pre(pid, x, g, sh, sc):
    return (_rms(x, g) * (1.0 + sc) + sh,)


def _f_mid(pid, x, o, g1, gate, g2, sh, sc):
    x1 = x + gate * _rms(o, g1)
    return (x1, _rms(x1, g2) * (1.0 + sc) + sh)


def _f_post(pid, x, o, g, gate):
    return (x + gate * _rms(o, g),)


def _f_qkv(pid, pq, pkv, cos, sin, q_g, k_g, perm, *, nh, nkv):
    hd = HEAD_DIM

    def norm_rope(xh, g):
        y = _rms(xh, g)
        return y * cos + jnp.dot(y, perm, precision=HIGHEST, preferred_element_type=F32) * sin

    qs = [norm_rope(pq[:, h * hd:(h + 1) * hd], q_g) * (HEAD_DIM ** -0.5 * LOG2_E) for h in range(nh)]
    ks = [norm_rope(pkv[:, h * hd:(h + 1) * hd], k_g) for h in range(nkv)]
    return (jnp.concatenate(qs, axis=1), jnp.concatenate(ks + [pkv[:, nkv * hd:]], axis=1))


def _f_gates(pid, x, wa, ba, wx, bx, lam, *, nb):
    w = RNN_BLOCK_W
    cols = [[] for _ in range(4)]
    for n in range(nb):
        blk = slice(n * w, (n + 1) * w)
        xn = x[:, blk]
        xb = xn.astype(BF16)
        for d in range(2):
            r = _sigmoid(jnp.dot(xb, wa[d * nb + n].astype(BF16), preferred_element_type=F32) + ba[d:d + 1, blk])
            i = _sigmoid(jnp.dot(xb, wx[d * nb + n].astype(BF16), preferred_element_type=F32) + bx[d:d + 1, blk])
            log_a = -RG_C * r * _softplus(-lam[d:d + 1, blk])
            a = jnp.exp(log_a)
            cols[2 * d].append(a)
            cols[2 * d + 1].append(jnp.sqrt(-jnp.tanh(log_a) * (a * a + 1.0)) * (i * xn))
    return tuple(jnp.concatenate(c, axis=1) for c in cols)


def _f_rnnout(pid, hf, hb, gr):
    return ((hf + hb) * _gelu(gr),)


def _f_gm(pid, zu, zv, bu, bv, v_g, v_b, w_sp, bsp_t, expand, *, n_chunks, groups):
    u = _gelu(zu + bu)
    v = _gelu(zv + bv)
    mu = jnp.mean(v, axis=-1, keepdims=True)
    vc = v - mu
    v = vc * lax.rsqrt(jnp.mean(vc * vc, axis=-1, keepdims=True) + EPS) * v_g + v_b
    bias = jnp.dot(bsp_t, expand, precision=HIGHEST, preferred_element_type=F32)
    outs = []
    for c in range(n_chunks):
        vch = v[c * CHUNK:(c + 1) * CHUNK]
        cols = [jnp.dot(w_sp[g].astype(BF16), vch[:, g * GM_GROUP_W:(g + 1) * GM_GROUP_W].astype(BF16),
                        preferred_element_type=F32) for g in range(groups)]
        outs.append(u[c * CHUNK:(c + 1) * CHUNK] * (jnp.concatenate(cols, axis=1) + bias))
    return (jnp.concatenate(outs, axis=0),)


def _f_silu_mul(pid, c, d):
    return (d * jax.grad(lambda z: jnp.sum(z * _sigmoid(z)))(c),)


def _bwd_of(fn, n_tiled, n_ct, want):
    def bwd(pid, *args):
        tiles = [t.astype(F32) for t in args[:n_tiled]]
        cts = args[n_tiled:n_tiled + n_ct]
        fulls = list(args[n_tiled + n_ct:])
        outs, vjp = jax.vjp(lambda *a: fn(pid, *a), *tiles, *fulls)
        grads = vjp(tuple(ct.astype(o.dtype) for ct, o in zip(cts, outs)))
        return tuple(grads[i] for i in want)
    return bwd


def _rowwise(name, fn, rows, tm, tiled, full, outs, accs=(), comm=None, skip_rows=0):
    n_t, n_f, n_o, n_a = len(tiled), len(full), len(outs), len(accs)
    assert skip_rows % tm == 0
    skip = skip_rows // tm
    n_tiles = rows // tm + skip

    def body(*refs):
        in_refs, c_ins, res_refs, c_outs, _, c_sems = _split_refs(refs, n_t + n_f, n_o + n_a, 0, comm)
        pid = pl.program_id(0)
        if comm is not None:
            @pl.when(pid == 0)
            def _():
                comm.start(c_ins, c_outs, *c_sems)

        res = fn(pid, *[r[...] for r in in_refs])
        o_refs, a_refs = res_refs[:n_o], res_refs[n_o:]
        for r, v in zip(o_refs, res[:n_o]):
            r[...] = v.astype(r.dtype)
        if n_a:
            @pl.when(pid == 0)
            def _():
                for r in a_refs:
                    r[...] = jnp.zeros_like(r)
            for r, v in zip(a_refs, res[n_o:]):
                r[...] += v.astype(F32)
        if comm is not None:
            @pl.when(pid == n_tiles - 1)
            def _():
                comm.finish(c_ins, c_outs, *c_sems)

    assert all(ro % tm == 0 for (_, ro, _, _) in tiled)
    in_specs = [pl.BlockSpec((tm, w), lambda i, ro=ro // tm, cb=cb, last=a.shape[0] // tm - 1: (jnp.clip(i + ro, 0, last), cb))
                for (a, ro, cb, w) in tiled]
    in_specs += [pl.BlockSpec(a.shape, lambda i, nd=a.ndim: (0,) * nd) for a in full]
    out_shape = [_sds((rows, w), dt) for (w, dt) in outs] + [_sds(s, F32) for s in accs]
    out_specs = [pl.BlockSpec((tm, w), lambda i: (jnp.maximum(i - skip, 0), 0)) for (w, _) in outs]
    out_specs += [pl.BlockSpec(tuple(s), lambda i, nd=len(s): (0,) * nd) for s in accs]
    c_in, c_out, c_scr = (comm.ins, comm.out_shapes, comm.scratch()) if comm is not None else ([], [], [])
    return pl.pallas_call(body, grid=(n_tiles,), in_specs=in_specs + [_ANY] * len(c_in), out_specs=out_specs + [_ANY] * len(c_out),
                          out_shape=out_shape + list(c_out), scratch_shapes=c_scr, name=name,
                          compiler_params=_params(("arbitrary",)))(*[t[0] for t in tiled], *full, *c_in)


def _t(a, row_off=0, col_blk=0, width=None):
    return (a, row_off, col_blk, a.shape[1] if width is None else width)


class _Comm:
    def __init__(self, ins, out_shapes, n_sems, start, finish):
        self.ins, self.out_shapes, self.n_sems, self.start, self.finish = list(ins), list(out_shapes), n_sems, start, finish

    def scratch(self):
        return [pltpu.SemaphoreType.DMA((self.n_sems,)), pltpu.SemaphoreType.DMA((self.n_sems,)),
                pltpu.SemaphoreType.DMA((len(self.ins),))]


_ANY = pl.BlockSpec(memory_space=pl.ANY)


class _SemSlice:
    def __init__(self, ref, first):
        self.ref, self.first = ref, first

    @property
    def at(self):
        return self

    def __getitem__(self, k):
        return self.ref.at[self.first + k]


def _both(*comms):
    comms = [cm for cm in comms if cm is not None]
    if len(comms) <= 1:
        return comms[0] if comms else None

    def parts(ins, outs, send, recv, local):
        i0 = o0 = s0 = 0
        for cm in comms:
            ni, no = len(cm.ins), len(cm.out_shapes)
            yield cm, (ins[i0:i0 + ni], outs[o0:o0 + no], _SemSlice(send, s0), _SemSlice(recv, s0), _SemSlice(local, i0))
            i0, o0, s0 = i0 + ni, o0 + no, s0 + cm.n_sems

    def start(*refs):
        for cm, sub in parts(*refs):
            cm.start(*sub)

    def finish(*refs):
        for cm, sub in parts(*refs):
            cm.finish(*sub)

    return _Comm(sum((cm.ins for cm in comms), []), sum((cm.out_shapes for cm in comms), []),
                 sum(cm.n_sems for cm in comms), start, finish)


def _row_block(cols, itemsize):
    return max(16, (1 << 20) // (cols * itemsize))


def _split_refs(refs, n_in, n_out, n_scratch, comm):
    ci, co, cs = (len(comm.ins), len(comm.out_shapes), 3) if comm is not None else (0, 0, 0)
    cuts = np.cumsum([0, n_in, ci, n_out, co, n_scratch, cs])
    return [refs[cuts[i]:cuts[i + 1]] for i in range(6)]


def _matmul(name, a, b, *, ta=False, tb=False, outs=((F32,)), epilogue=None, extras=(), tm=None, tn=None, tk=None, comm=None,
            col_blocks=None):
    m, k = (a.shape[1], a.shape[0]) if ta else a.shape
    n = b.shape[0] if tb else b.shape[1]
    tm = _tile(m, tm or MM_TILE_M, 128 if ta else 16)
    tn = _tile(n if col_blocks is None else n // col_blocks, tn or MM_TILE_N, 128)
    tk = _tile(k, tk or MM_TILE_K, 128 if not ta else 16)
    ni, nj, nk = m // tm, n // tn, k // tk
    n_e, n_o = len(extras), len(outs)
    dims = (((0 if ta else 1,), (1 if tb else 0,)), ((), ()))

    def body(*refs):
        ins, c_ins, o_refs, c_outs, scratch, c_sems = _split_refs(refs, 2 + n_e, n_o, 1 if nk > 1 else 0, comm)
        a_ref, b_ref, e_refs = ins[0], ins[1], ins[2:]
        i, j, kk = pl.program_id(0), pl.program_id(1), pl.program_id(2)
        if comm is not None:
            @pl.when(jnp.logical_and(jnp.logical_and(i == 0, j == 0), kk == 0))
            def _():
                comm.start(c_ins, c_outs, *c_sems)

        def finish(acc):
            res = (acc,) if epilogue is None else epilogue(acc, *[e[...] for e in e_refs])
            for r, v in zip(o_refs, res):
                r[...] = v.astype(r.dtype)

        prod = lax.dot_general(a_ref[...].astype(BF16), b_ref[...].astype(BF16), dims, preferred_element_type=F32)
        if nk == 1:
            finish(prod)
        else:
            acc = scratch[0]

            @pl.when(kk == 0)
            def _():
                acc[...] = prod

            @pl.when(kk > 0)
            def _():
                acc[...] += prod

            @pl.when(kk == nk - 1)
            def _():
                finish(acc[...])
        if comm is not None:
            @pl.when(jnp.logical_and(jnp.logical_and(i == ni - 1, j == nj - 1), kk == nk - 1))
            def _():
                comm.finish(c_ins, c_outs, *c_sems)

    a_spec = pl.BlockSpec((tk, tm), lambda i, j, kk: (kk, i)) if ta else pl.BlockSpec((tm, tk), lambda i, j, kk: (i, kk))
    b_spec = pl.BlockSpec((tn, tk), lambda i, j, kk: (j, kk)) if tb else pl.BlockSpec((tk, tn), lambda i, j, kk: (kk, j))
    mn_spec = pl.BlockSpec((tm, tn), lambda i, j, kk: (i, j))
    c_in, c_out, c_scr = (comm.ins, comm.out_shapes, comm.scratch()) if comm is not None else ([], [], [])
    if col_blocks is None:
        o_spec, o_shape = mn_spec, (m, n)
    else:
        per = n // col_blocks // tn
        o_spec = pl.BlockSpec((None, tm, tn), lambda i, j, kk: (j // per, i, j % per))
        o_shape = (col_blocks, m, n // col_blocks)
    res = pl.pallas_call(body, grid=(ni, nj, nk), in_specs=[a_spec, b_spec] + [mn_spec] * n_e + [_ANY] * len(c_in),
                         out_specs=[o_spec] * n_o + [_ANY] * len(c_out),
                         out_shape=[_sds(o_shape, dt) for dt in outs] + list(c_out),
                         scratch_shapes=([pltpu.VMEM((tm, tn), F32)] if nk > 1 else []) + c_scr, name=name,
                         compiler_params=_params(("arbitrary", "arbitrary", "arbitrary")))(a, b, *extras, *c_in)
    main = res[0] if n_o == 1 else res[:n_o]
    return main if comm is None else (main, res[n_o:])


def _attn_fwd(q, kv, n_ctx_tiles, tq, comm=None):
    t_all = q.shape[0]
    s_len = t_all - n_ctx_tiles * tq
    hd, groups = HEAD_DIM, N_HEADS // N_KV_HEADS
    hps = ATTN_HEADS_PER_STEP
    assert groups % hps == 0
    gsteps = groups // hps
    nq = s_len // tq

    def body(*refs):
        (q_ref, k_ref, v_ref), c_ins, (o_ref,), c_outs, _, c_sems = _split_refs(refs, 3, 1, 0, comm)
        kh, g, i = pl.program_id(0), pl.program_id(1), pl.program_id(2)
        if comm is not None:
            @pl.when(jnp.logical_and(jnp.logical_and(kh == 0, g == 0), i == 0))
            def _():
                comm.start(c_ins, c_outs, *c_sems)

        kk, vv = k_ref[...], v_ref[...]
        for h in range(hps):
            qh = q_ref[:, h * hd:(h + 1) * hd]
            s = lax.dot_general(qh, kk, (((1,), (1,)), ((), ())), preferred_element_type=F32)
            p = jnp.exp2(s - jnp.max(s, axis=-1, keepdims=True))
            l = jnp.sum(p, axis=-1, keepdims=True)
            o = jnp.dot(p.astype(BF16), vv, preferred_element_type=F32) * (1.0 / l)
            o_ref[:, h * hd:(h + 1) * hd] = o.astype(o_ref.dtype)
        if comm is not None:
            @pl.when(jnp.logical_and(jnp.logical_and(kh == N_KV_HEADS - 1, g == gsteps - 1), i == nq - 1))
            def _():
                comm.finish(c_ins, c_outs, *c_sems)

    c_in, c_out, c_scr = (comm.ins, comm.out_shapes, comm.scratch()) if comm is not None else ([], [], [])
    res = pl.pallas_call(
        body, grid=(N_KV_HEADS, gsteps, nq),
        in_specs=[pl.BlockSpec((tq, hps * hd), lambda kh, g, i: (i + n_ctx_tiles, kh * gsteps + g)),
                  pl.BlockSpec((t_all, hd), lambda kh, g, i: (0, kh)),
                  pl.BlockSpec((t_all, hd), lambda kh, g, i: (0, N_KV_HEADS + kh))] + [_ANY] * len(c_in),
        out_specs=[pl.BlockSpec((tq, hps * hd), lambda kh, g, i: (i, kh * gsteps + g))] + [_ANY] * len(c_out),
        out_shape=[_sds((s_len, N_HEADS * hd), BF16)] + list(c_out), scratch_shapes=c_scr, name="attn_fwd",
        compiler_params=_params(("arbitrary", "arbitrary", "arbitrary")))(q, kv, kv, *c_in)
    return res[0] if comm is None else (res[0], res[1:])


def _attn_bwd(q, kv, d_ar, n_ctx_tiles, tq, comm=None):
    t_all = q.shape[0]
    s_len = t_all - n_ctx_tiles * tq
    hd, groups = HEAD_DIM, N_HEADS // N_KV_HEADS
    hps = ATTN_BWD_HEADS_PER_STEP
    assert groups % hps == 0
    gsteps = groups // hps
    nq = s_len // tq

    def body(*refs):
        (q_ref, k_ref, v_ref, do_ref), c_ins, (dq_ref, dkt_ref, dvt_ref), c_outs, _, c_sems = _split_refs(refs, 4, 3, 0, comm)
        first = jnp.logical_and(pl.program_id(1) == 0, pl.program_id(2) == 0)
        if comm is not None:
            @pl.when(jnp.logical_and(first, pl.program_id(0) == 0))
            def _():
                comm.start(c_ins, c_outs, *c_sems)

        @pl.when(first)
        def _():
            dkt_ref[...] = jnp.zeros_like(dkt_ref)
            dvt_ref[...] = jnp.zeros_like(dvt_ref)

        kk, vv = k_ref[...], v_ref[...]
        for h in range(hps):
            cols = slice(h * hd, (h + 1) * hd)
            qv = q_ref[:, cols]
            s = lax.dot_general(qv, kk, (((1,), (1,)), ((), ())), preferred_element_type=F32)
            p = jnp.exp2(s - jnp.max(s, axis=-1, keepdims=True))
            inv_l = 1.0 / jnp.sum(p, axis=-1, keepdims=True)
            do = (do_ref[:, cols] * inv_l).astype(BF16)
            dp = lax.dot_general(do, vv, (((1,), (1,)), ((), ())), preferred_element_type=F32)
            ds = (p * (dp - jnp.sum(p * dp, axis=-1, keepdims=True) * inv_l)).astype(BF16)
            dq_ref[:, cols] = jnp.dot(ds, kk, preferred_element_type=F32) * LN_2
            dkt_ref[...] += jnp.dot(qv.T, ds, preferred_element_type=F32)
            dvt_ref[...] += jnp.dot(do.T, p.astype(BF16), preferred_element_type=F32)
        last = jnp.logical_and(pl.program_id(1) == gsteps - 1, pl.program_id(2) == nq - 1)

        @pl.when(last)
        def _():
            dkt_ref[...] *= LN_2

        if comm is not None:
            @pl.when(jnp.logical_and(last, pl.program_id(0) == N_KV_HEADS - 1))
            def _():
                comm.finish(c_ins, c_outs, *c_sems)

    c_in, c_out, c_scr = (comm.ins, comm.out_shapes, comm.scratch()) if comm is not None else ([], [], [])
    return pl.pallas_call(
        body, grid=(N_KV_HEADS, gsteps, nq),
        in_specs=[pl.BlockSpec((tq, hps * hd), lambda kh, g, i: (i + n_ctx_tiles, kh * gsteps + g)),
                  pl.BlockSpec((t_all, hd), lambda kh, g, i: (0, kh)),
                  pl.BlockSpec((t_all, hd), lambda kh, g, i: (0, N_KV_HEADS + kh)),
                  pl.BlockSpec((tq, hps * hd), lambda kh, g, i: (i, kh * gsteps + g))] + [_ANY] * len(c_in),
        out_specs=[pl.BlockSpec((tq, hps * hd), lambda kh, g, i: (i, kh * gsteps + g)),
                   pl.BlockSpec((hd, t_all), lambda kh, g, i: (kh, 0)),
                   pl.BlockSpec((hd, t_all), lambda kh, g, i: (kh, 0))] + [_ANY] * len(c_out),
        out_shape=[_sds((s_len, N_HEADS * hd), F32), _sds((N_KV_HEADS * hd, t_all), F32),
                   _sds((N_KV_HEADS * hd, t_all), F32)] + list(c_out),
        scratch_shapes=c_scr, name="attn_bwd",
        compiler_params=_params(("arbitrary", "arbitrary", "arbitrary")))(q, kv, kv, d_ar, *c_in)


def _scan_order(nb, nc, reverse):
    if not reverse:
        return lambda i: i
    return lambda i: jnp.where(i < nc, nc - 1 - i, nb - 1 - (i - nc))


def _scan_fwd(name, a, b, nc, reverse):
    t_all, r, l = a.shape
    tb = SCAN_BLOCK
    nb = t_all // tb
    order = _scan_order(nb, nc, reverse)

    def body(a_ref, b_ref, h_ref, hp_ref, carry):
        @pl.when(pl.program_id(0) == 0)
        def _():
            carry[...] = jnp.zeros_like(carry)

        def step(s, h):
            t = tb - 1 - s if reverse else s
            hp_ref[t] = h
            h = a_ref[t] * h + b_ref[t]
            h_ref[t] = h
            return h

        carry[...] = lax.fori_loop(0, tb, step, carry[...], unroll=8)

    spec = pl.BlockSpec((tb, r, l), lambda i: (order(i), 0, 0))
    return pl.pallas_call(body, grid=(nb,), in_specs=[spec, spec], out_specs=[spec, spec],
                          out_shape=[_sds(a.shape, F32)] * 2, scratch_shapes=[pltpu.VMEM((r, l), F32)], name=name,
                          compiler_params=_params(("arbitrary",)))(a, b)


def _scan_bwd(name, a, dh, hp, nc, reverse):
    t_all, r, l = a.shape
    tb = SCAN_BLOCK
    nb = t_all // tb
    primal = _scan_order(nb, nc, reverse)

    def order(i):
        return primal(nb - 1 - i)

    def body(a_ref, dh_ref, hp_ref, da_ref, db_ref, carry):
        @pl.when(pl.program_id(0) == 0)
        def _():
            carry[...] = jnp.zeros_like(carry)

        def step(s, cr):
            t = s if reverse else tb - 1 - s
            lam = dh_ref[t] + cr
            db_ref[t] = lam
            da_ref[t] = lam * hp_ref[t]
            return a_ref[t] * lam

        carry[...] = lax.fori_loop(0, tb, step, carry[...], unroll=8)

    spec = pl.BlockSpec((tb, r, l), lambda i: (order(i), 0, 0))
    return pl.pallas_call(body, grid=(nb,), in_specs=[spec] * 3, out_specs=[spec, spec],
                          out_shape=[_sds(a.shape, F32)] * 2, scratch_shapes=[pltpu.VMEM((r, l), F32)], name=name,
                          compiler_params=_params(("arbitrary",)))(a, dh, hp)


def _shifted(prev, cur, nxt, k, pid, n_ctx_tiles, n_tiles):
    if k == 0:
        return cur
    tm = cur.shape[0]
    row = lax.broadcasted_iota(jnp.int32, cur.shape, 0)
    if k < 0:
        at_start = jnp.logical_or(pid == 0, pid == n_ctx_tiles)
        edge = jnp.where(at_start, 0.0, pltpu.roll(prev, -k, 0))
        return jnp.where(row < -k, edge, pltpu.roll(cur, -k, 0))
    at_end = jnp.logical_or(pid == n_ctx_tiles - 1, pid == n_tiles - 1)
    edge = jnp.where(at_end, 0.0, pltpu.roll(nxt, tm - k, 0))
    return jnp.where(row >= tm - k, edge, pltpu.roll(cur, tm - k, 0))


def _f_conv(pid, xp, xc, xn, w, b, *, n_ctx_tiles, n_tiles):
    y = b
    for j in range(CONV_W):
        y = y + _shifted(xp, xc, xn, j - CONV_W // 2, pid, n_ctx_tiles, n_tiles) * w[j:j + 1]
    return (y,)


def _f_conv_bwd(pid, xp, xc, xn, dp, dc, dn, w, *, n_ctx_tiles, n_tiles):
    dx = jnp.zeros_like(dc)
    dw = []
    for j in range(CONV_W):
        k = j - CONV_W // 2
        dx = dx + _shifted(dp, dc, dn, -k, pid, n_ctx_tiles, n_tiles) * w[j:j + 1]
        dw.append(jnp.sum(dc * _shifted(xp, xc, xn, k, pid, n_ctx_tiles, n_tiles), axis=0, keepdims=True))
    return (dx, jnp.concatenate(dw, axis=0), jnp.sum(dc, axis=0, keepdims=True))


def _mesh_pos():
    return lax.axis_index("x"), lax.axis_index("y"), lax.axis_index("c")


def _remote(src, dst, send_sems, recv_sems, k, to):
    return pltpu.make_async_remote_copy(src_ref=src, dst_ref=dst, send_sem=send_sems.at[k], recv_sem=recv_sems.at[k],
                                        device_id=to, device_id_type=pl.DeviceIdType.MESH)


def _neighbours():
    x, y, c = _mesh_pos()
    return (x, y, c), (x, y, 1 - c), [(1 - x, y), (x, 1 - y), (1 - x, 1 - y)]


def _gather_comm(arrays):
    n = len(arrays)
    per = 7

    def slot(out, blk):
        return out.at[4 * blk[0] + 2 * blk[1] + blk[2]]

    def start(ins, outs, send, recv, local):
        me, sib, chips = _neighbours()
        for ai in range(n):
            pltpu.make_async_copy(ins[ai], slot(outs[ai], me), local.at[ai]).start()
            _remote(ins[ai], slot(outs[ai], me), send, recv, ai * per, sib).start()
            for j, chip in enumerate(chips):
                _remote(ins[ai], slot(outs[ai], me), send, recv, ai * per + 1 + j, (*chip, me[2])).start()

    def finish(ins, outs, send, recv, local):
        me, sib, chips = _neighbours()
        for ai in range(n):
            for j, chip in enumerate(chips):
                blk = slot(outs[ai], (*chip, me[2]))
                _remote(blk, blk, send, recv, ai * per + 1 + j, me).wait_recv()
                _remote(blk, blk, send, recv, ai * per + 4 + j, sib).start()
        for ai in range(n):
            blk = slot(outs[ai], sib)
            _remote(blk, blk, send, recv, ai * per, me).wait_recv()
            for j, chip in enumerate(chips):
                blk = slot(outs[ai], (*chip, 1 - me[2]))
                _remote(blk, blk, send, recv, ai * per + 4 + j, me).wait_recv()
            for k in range(per):
                _remote(ins[ai], slot(outs[ai], me), send, recv, ai * per + k, sib).wait_send()
            pltpu.make_async_copy(ins[ai], slot(outs[ai], me), local.at[ai]).wait()

    return _Comm(arrays, [_sds((N_DEV,) + a.shape, a.dtype) for a in arrays], n * per, start, finish)


def _swap_comm(arrays):
    n = len(arrays)

    def start(ins, outs, send, recv, local):
        me, sib, _ = _neighbours()
        for ai in range(n):
            for q in range(4):
                _remote(ins[ai].at[2 * q + 1 - me[2]], outs[ai].at[q], send, recv, ai * 4 + q, sib).start()

    def finish(ins, outs, send, recv, local):
        me, sib, _ = _neighbours()
        for ai in range(n):
            for q in range(4):
                cp = _remote(ins[ai].at[q], outs[ai].at[q], send, recv, ai * 4 + q, sib)
                cp.wait_recv()
                cp.wait_send()

    return _Comm(arrays, [_sds((4,) + a.shape[1:], a.dtype) for a in arrays], n * 4, start, finish)


def _chips_comm(arrays):
    n = len(arrays)

    def start(ins, outs, send, recv, local):
        me, _, chips = _neighbours()
        mine = 2 * me[0] + me[1]
        for ai in range(n):
            pltpu.make_async_copy(ins[ai].at[mine], outs[ai].at[mine], local.at[ai]).start()
            for j, chip in enumerate(chips):
                _remote(ins[ai].at[2 * chip[0] + chip[1]], outs[ai].at[mine], send, recv, ai * 3 + j, (*chip, me[2])).start()

    def finish(ins, outs, send, recv, local):
        me, _, chips = _neighbours()
        mine = 2 * me[0] + me[1]
        for ai in range(n):
            for j, chip in enumerate(chips):
                theirs = 2 * chip[0] + chip[1]
                cp = _remote(ins[ai].at[theirs], outs[ai].at[theirs], send, recv, ai * 3 + j, (*chip, me[2]))
                cp.wait_recv()
                cp.wait_send()
            pltpu.make_async_copy(ins[ai].at[mine], outs[ai].at[mine], local.at[ai]).wait()

    return _Comm(arrays, [_sds(a.shape, a.dtype) for a in arrays], n * 3, start, finish)


def _run_comm(name, comm):
    n_in, n_out = len(comm.ins), len(comm.out_shapes)

    def body(*refs):
        ins, outs, sems = refs[:n_in], refs[n_in:n_in + n_out], refs[n_in + n_out:]
        comm.start(ins, outs, *sems)
        comm.finish(ins, outs, *sems)

    return pl.pallas_call(body, in_specs=[_ANY] * n_in, out_specs=[_ANY] * n_out, out_shape=comm.out_shapes, name=name,
                          scratch_shapes=comm.scratch(), compiler_params=pltpu.CompilerParams(has_side_effects=True))(*comm.ins)


def _chip_add(name, blocks, theirs, core):
    _, r, c = blocks.shape
    tr = _tile(r, _row_block(c, 2), 16)

    def body(core_ref, a_ref, b_ref, o_ref):
        o_ref[...] = (a_ref[...].astype(F32) + b_ref[...].astype(F32)).astype(o_ref.dtype)

    spec = pl.BlockSpec((None, tr, c), lambda q, i, core_ref: (q, i, 0))
    grid_spec = pltpu.PrefetchScalarGridSpec(
        num_scalar_prefetch=1, grid=(4, r // tr),
        in_specs=[pl.BlockSpec((None, tr, c), lambda q, i, core_ref: (2 * q + core_ref[0], i, 0)), spec], out_specs=spec)
    return pl.pallas_call(body, grid_spec=grid_spec, out_shape=_sds((4, r, c), blocks.dtype), name=name,
                          compiler_params=_params(("parallel", "parallel")))(jnp.reshape(core, (1,)).astype(jnp.int32), blocks, theirs)


def _sum_lead(name, a):
    n, r, c = a.shape
    tr = _tile(r, _row_block(c, 4 * n // 2), 16)

    def body(a_ref, o_ref):
        acc = a_ref[0].astype(F32)
        for j in range(1, n):
            acc = acc + a_ref[j].astype(F32)
        o_ref[...] = acc

    return pl.pallas_call(body, grid=(r // tr,), in_specs=[pl.BlockSpec((n, tr, c), lambda i: (0, i, 0))],
                          out_specs=pl.BlockSpec((tr, c), lambda i: (i, 0)), out_shape=_sds((r, c), F32), name=name,
                          compiler_params=_params(("parallel",)))(a)


def _adam_math(w, g, m, v):
    m = ADAM_B1 * m + (1.0 - ADAM_B1) * g
    v = ADAM_B2 * v + (1.0 - ADAM_B2) * (g * g)
    m_hat = m / (1.0 - ADAM_B1 ** ADAM_STEP)
    v_hat = v / (1.0 - ADAM_B2 ** ADAM_STEP)
    delta = -ADAM_LR * (m_hat / (jnp.sqrt(v_hat) + ADAM_EPS) + ADAM_WD * w)
    return delta, m, v


def _adam_recv(name, recvs, w, m, v):
    nl = len(recvs)
    n, rl, c = recvs[0].shape
    tr = _tile(rl, _row_block(c, 4), 16)
    per = rl // tr

    def body(*refs):
        g_refs = refs[:nl]
        w_ref, m_ref, v_ref, go_ref, d_ref, mo_ref, vo_ref = refs[nl:]
        for layer in range(nl):
            @pl.when(pl.program_id(0) == layer)
            def _(g_ref=g_refs[layer]):
                g = g_ref[0].astype(F32)
                for j in range(1, n):
                    g = g + g_ref[j].astype(F32)
                delta, m2, v2 = _adam_math(w_ref[...], g, m_ref[...], v_ref[...])
                go_ref[...] = g
                d_ref[...] = delta
                mo_ref[...] = m2
                vo_ref[...] = v2

    g_specs = [pl.BlockSpec((n, tr, c), lambda l, i, layer=layer: (0, jnp.where(l == layer, i, 0), 0)) for layer in range(nl)]
    spec = pl.BlockSpec((tr, c), lambda l, i: (l * per + i, 0))
    return pl.pallas_call(body, grid=(nl, per), in_specs=g_specs + [spec] * 3, out_specs=[spec] * 4,
                          out_shape=[_sds((nl * rl, c), F32)] * 4, name=name,
                          compiler_params=_params(("arbitrary", "arbitrary")))(*recvs, w, m, v)


def _adam_f32(name, g, w, m, v, comm=None):
    r, c = g.shape
    tr = _tile(r, _row_block(c, 4), 8)
    steps = r // tr

    def body(*refs):
        (g_ref, w_ref, m_ref, v_ref), c_ins, (d_ref, mo_ref, vo_ref), c_outs, _, c_sems = _split_refs(refs, 4, 3, 0, comm)
        if comm is not None:
            @pl.when(pl.program_id(0) == 0)
            def _():
                comm.start(c_ins, c_outs, *c_sems)

        delta, m2, v2 = _adam_math(w_ref[...], g_ref[...], m_ref[...], v_ref[...])
        d_ref[...] = delta
        mo_ref[...] = m2
        vo_ref[...] = v2
        if comm is not None:
            @pl.when(pl.program_id(0) == steps - 1)
            def _():
                comm.finish(c_ins, c_outs, *c_sems)

    spec = pl.BlockSpec((tr, c), lambda i: (i, 0))
    c_in, c_out, c_scr = (comm.ins, comm.out_shapes, comm.scratch()) if comm is not None else ([], [], [])
    return pl.pallas_call(body, grid=(steps,), in_specs=[spec] * 4 + [_ANY] * len(c_in), out_specs=[spec] * 3 + [_ANY] * len(c_out),
                          out_shape=[_sds((r, c), F32)] * 3 + list(c_out), scratch_shapes=c_scr, name=name,
                          compiler_params=_params(("arbitrary",)))(g, w, m, v, *c_in)


def _mod_fwd(c16, w_mod, b_loc):
    nl, d, n6 = w_mod.shape
    tn = _tile(n6, 512, 128)

    def body(c_ref, w_ref, b_ref, o_ref):
        cv = c_ref[...]
        s = cv * _sigmoid(cv)
        o_ref[0] = jnp.dot(s, w_ref[0], precision=HIGHEST, preferred_element_type=F32) + b_ref[0]

    return pl.pallas_call(
        body, grid=(nl, n6 // tn),
        in_specs=[pl.BlockSpec((MOD_ROWS, d), lambda i, j: (0, 0)), pl.BlockSpec((1, d, tn), lambda i, j: (i, 0, j)),
                  pl.BlockSpec((1, 1, tn), lambda i, j: (i, 0, j))],
        out_specs=pl.BlockSpec((1, MOD_ROWS, tn), lambda i, j: (i, 0, j)), out_shape=_sds((nl, MOD_ROWS, n6), F32),
        name="mod_fwd", compiler_params=_params(("parallel", "parallel")))(c16, w_mod, b_loc)


def _mod_bwd(c16, w_mod, dmod_loc):
    nl, d, n6 = w_mod.shape
    tn = _tile(n6, 512, 128)

    def body(c_ref, w_ref, dm_ref, dw_ref, ds_ref):
        @pl.when(jnp.logical_and(pl.program_id(0) == 0, pl.program_id(1) == 0))
        def _():
            ds_ref[...] = jnp.zeros_like(ds_ref)

        cv = c_ref[...]
        s = cv * _sigmoid(cv)
        dm = dm_ref[0]
        dw_ref[0] = lax.dot_general(s, dm, (((0,), (0,)), ((), ())), precision=HIGHEST, preferred_element_type=F32)
        ds_ref[...] += lax.dot_general(dm, w_ref[0], (((1,), (1,)), ((), ())), precision=HIGHEST, preferred_element_type=F32)

    return pl.pallas_call(
        body, grid=(nl, n6 // tn),
        in_specs=[pl.BlockSpec((MOD_ROWS, d), lambda i, j: (0, 0)), pl.BlockSpec((1, d, tn), lambda i, j: (i, 0, j)),
                  pl.BlockSpec((1, MOD_ROWS, tn), lambda i, j: (i, 0, j))],
        out_specs=[pl.BlockSpec((1, d, tn), lambda i, j: (i, 0, j)), pl.BlockSpec((MOD_ROWS, d), lambda i, j: (0, 0))],
        out_shape=[_sds((nl, d, n6), F32), _sds((MOD_ROWS, d), F32)], name="mod_bwd",
        compiler_params=_params(("arbitrary", "arbitrary")))(c16, w_mod, dmod_loc)


def _pack(parts):
    flat = [p.reshape(-1).astype(F32) for p in parts]
    offs = np.cumsum([0] + [f.shape[0] for f in flat])
    total = int(offs[-1])
    unit = (PACK_ROWS if total > PACK_ROWS * LANES else 8) * LANES
    padded = -(-total // unit) * unit
    slab = jnp.concatenate(flat + [jnp.zeros((padded - total,), F32)])
    return slab.reshape(padded // LANES, LANES), [int(o) for o in offs]


def _unshard_cols(seg, lead):
    n = seg.shape[1] // int(np.prod(lead)) if lead else seg.shape[1]
    a = seg.reshape((N_DEV,) + tuple(lead) + (n,))
    a = jnp.moveaxis(a, 0, len(lead))
    return a.reshape(tuple(lead) + (N_DEV * n,))


def _my_cols(a, me, n):
    start = (0,) * (a.ndim - 1) + (me * n,)
    return lax.dynamic_slice(a, start, a.shape[:-1] + (n,))


def _rope_tables(seq, n_ctx):
    rows = seq // GRID_W
    r_idx, c_idx = jnp.meshgrid(jnp.arange(rows), jnp.arange(GRID_W), indexing='ij')
    r_idx = r_idx.reshape(-1).astype(F32)
    c_idx = c_idx.reshape(-1).astype(F32)
    pairs = HEAD_DIM // 4
    freqs = ROPE_THETA ** (-jnp.arange(pairs, dtype=F32) / pairs)
    ang_r, ang_c = r_idx[:, None] * freqs, c_idx[:, None] * freqs
    cos = jnp.concatenate([jnp.cos(ang_r)] * 2 + [jnp.cos(ang_c)] * 2, axis=1)
    sin = jnp.concatenate([-jnp.sin(ang_r), jnp.sin(ang_r), -jnp.sin(ang_c), jnp.sin(ang_c)], axis=1)
    cos = jnp.concatenate([jnp.ones((n_ctx, HEAD_DIM), F32), cos], axis=0)
    sin = jnp.concatenate([jnp.zeros((n_ctx, HEAD_DIM), F32), sin], axis=0)
    lane = np.arange(HEAD_DIM)
    partner = np.where(lane % (2 * pairs) < pairs, lane + pairs, lane - pairs)
    perm = np.zeros((HEAD_DIM, HEAD_DIM), np.float32)
    perm[partner, lane] = 1.0
    return cos, sin, jnp.asarray(perm)


def kernel(x, c, ctx, c_ctx, w_mod, b_mod, norm_g, w_ff_in, w_ff_out, ar_w_in, ar_q_g, ar_k_g, ar_conv_w, ar_conv_b, ar_wa, ar_ba, ar_wx, ar_bx, ar_lambda, ar_w_out, gm_w_in, gm_b_in, gm_v_g, gm_v_b, gm_w_sp, gm_b_sp, gm_w_out, loss_target, m_c_ctx, m_w_mod, m_b_mod, m_norm_g, m_w_ff_in, m_w_ff_out, m_ar_w_in, m_ar_q_g, m_ar_k_g, m_ar_conv_w, m_ar_conv_b, m_ar_wa, m_ar_ba, m_ar_wx, m_ar_bx, m_ar_lambda, m_ar_w_out, m_gm_w_in, m_gm_b_in, m_gm_v_g, m_gm_v_b, m_gm_w_sp, m_gm_b_sp, m_gm_w_out, v_c_ctx, v_w_mod, v_b_mod, v_norm_g, v_w_ff_in, v_w_ff_out, v_ar_w_in, v_ar_q_g, v_ar_k_g, v_ar_conv_w, v_ar_conv_b, v_ar_wa, v_ar_ba, v_ar_wx, v_ar_bx, v_ar_lambda, v_ar_w_out, v_gm_w_in, v_gm_b_in, v_gm_v_g, v_gm_v_b, v_gm_w_sp, v_gm_b_sp, v_gm_w_out):
    given = dict(locals())
    wts = {n: given[n] for n in WEIGHTS}
    mom1 = {n: given["m_" + n] for n in WEIGHTS}
    mom2 = {n: given["v_" + n] for n in WEIGHTS}

    xi, yi, ci = _mesh_pos()
    me = 4 * xi + 2 * yi + ci

    seq, d = x.shape[1], x.shape[2]
    n_ctx = ctx.shape[1]
    t_all = n_ctx + seq
    n_layers = w_mod.shape[0]
    assert n_layers == 2 and ar_w_in.shape[0] == 1 and gm_w_in.shape[0] == 1
    d_ff = w_ff_in.shape[2] * N_DEV
    attn_w, kv_w = N_HEADS * HEAD_DIM, N_KV_HEADS * HEAD_DIM
    rnn_blocks = ar_wa.shape[2]
    d_rnn = rnn_blocks * RNN_BLOCK_W
    gm_groups = gm_w_sp.shape[1]
    d_gm = gm_groups * GM_GROUP_W
    ar_in = ar_w_in.shape[2] * N_DEV
    n6 = w_mod.shape[2]
    tm, tmb = ROW_TILE, ROW_TILE_BWD
    assert attn_w == d_rnn and ar_in == 3 * attn_w + 2 * kv_w and (3 * attn_w) % (2 * kv_w) == 0
    assert n_ctx % tm == 0 and seq % tm == 0 and n_ctx % SCAN_BLOCK == 0 and seq % SCAN_BLOCK == 0 and tm % CHUNK == 0
    nct, nctb = n_ctx // tm, n_ctx // tmb
    kv_blk = (3 * attn_w) // (2 * kv_w)
    lr = d_rnn // LANES

    x2, ctx2, tgt = x[0], ctx[0], loss_target[0]

    def cols_full(g):
        return jnp.moveaxis(g, 0, 1).reshape(g.shape[1], N_DEV * g.shape[2])

    small0, off0 = _pack([c[0], norm_g, ar_conv_w[0], ar_ba[0], ar_bx[0], ar_lambda[0], gm_b_in[0], gm_v_g[0], gm_v_b[0]])
    (gs0,) = _run_comm("gather_first", _gather_comm([small0]))
    gs0 = gs0.reshape(N_DEV, -1)
    w1, w2 = [None] * n_layers, [None] * n_layers

    def seg0(k):
        return gs0[:, off0[k]:off0[k + 1]]

    c_all = seg0(0)
    norm_full = _unshard_cols(seg0(1), (n_layers, 4))
    conv_w = _unshard_cols(seg0(2), (CONV_W,))
    ba, bx, lam = (_unshard_cols(seg0(k), (2,)) for k in (3, 4, 5))
    gm_b_in_f = seg0(6).reshape(1, 2 * d_gm)
    gm_vg, gm_vb = seg0(7).reshape(1, d_gm), seg0(8).reshape(1, d_gm)

    c16 = jnp.concatenate([c_all, c_ctx[None], jnp.zeros((MOD_ROWS - N_DEV - 1, d), F32)], axis=0)
    b_loc = _my_cols(b_mod, me, n6)[:, None, :]
    mod_loc = _mod_fwd(c16, w_mod, b_loc)
    (g_mod,) = _run_comm("gather_mod", _gather_comm([mod_loc]))
    mod_all = jnp.moveaxis(g_mod, 0, 2).reshape(n_layers, MOD_ROWS, N_DEV * n6)
    ml = lax.dynamic_index_in_dim(mod_all, me, axis=1, keepdims=False).reshape(n_layers, 6, d)
    mc = mod_all[:, N_DEV].reshape(n_layers, 6, d)

    def row(a, *idx):
        return a[idx][None]

    cos, sin, perm = _rope_tables(seq, n_ctx)
    wa3 = ar_wa[0].reshape(2 * rnn_blocks, RNN_BLOCK_W, RNN_BLOCK_W)
    wx3 = ar_wx[0].reshape(2 * rnn_blocks, RNN_BLOCK_W, RNN_BLOCK_W)
    conv_b = ar_conv_b
    q_g, k_g = ar_q_g, ar_k_g
    w_sp = gm_w_sp[0]
    bsp_t = jnp.pad(gm_b_sp[0].T, ((0, 0), (0, LANES - gm_groups)))
    expand = np.zeros((LANES, d_gm), np.float32)
    for g in range(gm_groups):
        expand[g, g * GM_GROUP_W:(g + 1) * GM_GROUP_W] = 1.0
    expand = jnp.asarray(expand)

    def relu2(acc):
        r = jnp.maximum(acc, 0.0)
        return (r * r,)

    def relu2_bwd(acc, act):
        return (acc * (2.0 * jnp.sqrt(act.astype(F32))),)

    def ff_in_shard(i):
        return w_ff_in[i].astype(BF16)

    def ff_out_shard(i):
        return w_ff_out[i].astype(BF16)

    tokens = [_t(ctx2), _t(x2, -n_ctx)]
    pre0_args = [row(norm_full, 0, 0), row(mc, 0, 0), row(mc, 0, 1), row(ml, 0, 0), row(ml, 0, 1)]
    f_pre0 = functools.partial(_f_pre_ctx, n_ctx_tiles=nct)
    h0, g_ar_in = _rowwise("pre0", f_pre0, t_all, tm, tokens, pre0_args, [(d, BF16)], comm=_gather_comm([ar_w_in[0].astype(BF16)]))
    w_in = cols_full(g_ar_in)
    split = [attn_w, attn_w + 2 * kv_w, attn_w + 2 * kv_w + d_rnn]
    w_in = jnp.concatenate([w_in[:, :split[0]], w_in[:, split[1]:], w_in[:, split[0]:split[1]]], axis=1)
    tm_tok = _tile(t_all, 640, 16)
    proj, (g_gm_in,) = _matmul("ar_in", h0, w_in, tm=tm_tok, comm=_gather_comm([gm_w_in[0].astype(BF16)]))
    f_qkv = functools.partial(_f_qkv, nh=N_HEADS, nkv=N_KV_HEADS)
    qkv_tiled = [_t(proj, 0, 0, attn_w), _t(proj, 0, kv_blk, 2 * kv_w), _t(cos), _t(sin)]
    q_r, kv_r = _rowwise("qkv", f_qkv, t_all, tm, qkv_tiled, [q_g, k_g, perm], [(attn_w, BF16), (2 * kv_w, BF16)])
    attn_o, (g_ar_out, g_ff_in0) = _attn_fwd(q_r, kv_r, nct, tm,
                                             comm=_gather_comm([ar_w_out[0].astype(BF16), ff_in_shard(0)]))
    w_out = g_ar_out.reshape(attn_w + d_rnn, d)
    w1[0] = cols_full(g_ff_in0)

    def with_neighbours(a, t, col_blk=0, width=None):
        return [_t(a, -t, col_blk, width), _t(a, 0, col_blk, width), _t(a, t, col_blk, width)]

    f_conv = functools.partial(_f_conv, n_ctx_tiles=nct, n_tiles=t_all // tm)
    (xc,) = _rowwise("conv", f_conv, t_all, tm, with_neighbours(proj, tm, 1, d_rnn), [conv_w, conv_b], [(d_rnn, F32)])
    f_gates = functools.partial(_f_gates, nb=rnn_blocks)
    gate_full = [wa3, ba, wx3, bx, lam]
    a_f, b_f, a_b, b_b, g_gm_out = _rowwise("gates", f_gates, t_all, tm, [_t(xc)], gate_full, [(d_rnn, F32)] * 4,
                                            comm=_gather_comm([gm_w_out[0].astype(BF16)]))

    def to3(a):
        return a.reshape(a.shape[0], lr, LANES)

    nc_scan = n_ctx // SCAN_BLOCK
    h_f, hp_f = _scan_fwd("scan_f", to3(a_f), to3(b_f), nc_scan, False)
    h_b, hp_b = _scan_fwd("scan_b", to3(a_b), to3(b_b), nc_scan, True)
    h_f2, h_b2 = h_f.reshape(t_all, d_rnn), h_b.reshape(t_all, d_rnn)
    rnn_tiled = [_t(h_f2, n_ctx), _t(h_b2, n_ctx), _t(proj, n_ctx, 2, d_rnn)]
    (rnn_o,) = _rowwise("rnn_out", _f_rnnout, seq, tm, rnn_tiled, [], [(d_rnn, BF16)])
    ar = jnp.concatenate([attn_o, rnn_o], axis=1)
    o0 = _matmul("ar_out", ar, w_out)
    mid0_args = [row(norm_full, 0, 1), row(ml, 0, 2), row(norm_full, 0, 2), row(ml, 0, 3), row(ml, 0, 4)]
    x1, h2_0 = _rowwise("mid0", _f_mid, seq, tm, [_t(x2), _t(o0)], mid0_args, [(d, F32), (d, BF16)])
    act0, (g_ff_out0,) = _matmul("ff_in_0", h2_0, w1[0], outs=(BF16,), epilogue=relu2, comm=_gather_comm([ff_out_shard(0)]))
    w2[0] = g_ff_out0.reshape(d_ff, d)
    w_gi, w_go = cols_full(g_gm_in), g_gm_out.reshape(d_gm, d)
    m0, (g_ff_in1,) = _matmul("ff_out_0", act0, w2[0], tm=MM_TILE_M // 2, tk=2 * MM_TILE_K, comm=_gather_comm([ff_in_shard(1)]))
    w1[1] = cols_full(g_ff_in1)
    post0_args = [row(norm_full, 0, 3), row(ml, 0, 5)]
    pre1_args = [row(norm_full, 1, 0), row(ml, 1, 0), row(ml, 1, 1)]

    def f_between_fwd(pid, xv, mv, g3, gate, g, sh, sc):
        (xo,) = _f_post(pid, xv, mv, g3, gate)
        return (xo, _f_pre(pid, xo, g, sh, sc)[0])

    x2l, h1 = _rowwise("between", f_between_fwd, seq, tm, [_t(x1), _t(m0)], post0_args + pre1_args, [(d, F32), (d, BF16)])
    zg = _matmul("gm_in", h1, w_gi)
    f_gm = functools.partial(_f_gm, n_chunks=tmb // CHUNK, groups=gm_groups)
    gm_full = [gm_b_in_f[:, :d_gm], gm_b_in_f[:, d_gm:], gm_vg, gm_vb, w_sp, bsp_t, expand]
    gm_tiled = [_t(zg, 0, 0, d_gm), _t(zg, 0, 1, d_gm)]
    (gmix,) = _rowwise("gm_mix", f_gm, seq, tmb, gm_tiled, gm_full, [(d_gm, BF16)])
    o1 = _matmul("gm_out", gmix, w_go)
    mid1_args = [row(norm_full, 1, 1), row(ml, 1, 2), row(norm_full, 1, 2), row(ml, 1, 3), row(ml, 1, 4)]
    x3, h2_1 = _rowwise("mid1", _f_mid, seq, tm, [_t(x2l), _t(o1)], mid1_args, [(d, F32), (d, BF16)])
    act1, (g_ff_out1,) = _matmul("ff_in_1", h2_1, w1[1], outs=(BF16,), epilogue=relu2,
                                       comm=_gather_comm([ff_out_shard(1)]))
    w2[1] = g_ff_out1.reshape(d_ff, d)
    m1 = _matmul("ff_out_1", act1, w2[1], tm=MM_TILE_M // 2, tk=2 * MM_TILE_K)
    post1_args = [row(norm_full, 1, 3), row(ml, 1, 5)]

    def f_loss(pid, xv, ov, tv, g, gate):
        (y,), vjp = jax.vjp(lambda *a: _f_post(pid, *a), xv, ov, g, gate)
        err = y - tv
        part = 0.5 * jnp.sum(err * err) / d
        dxv, dov, dg, dgate = vjp((err / d,))
        return (dxv, dov, jnp.full((8, LANES), part, F32), dg, dgate)

    dx3, dm1, loss_acc, dg_last, dgate_last = _rowwise("loss", f_loss, seq, tmb, [_t(x3), _t(m1), _t(tgt)], post1_args,
                                                       [(d, F32), (d, BF16)], [(8, LANES), (1, d), (1, d)])
    loss = lax.psum(loss_acc[0, 0], ("x", "y", "c"))

    d_norm = [[None] * 4 for _ in range(n_layers)]
    d_ml = [[None] * 6 for _ in range(n_layers)]
    recv = {}

    def cols_blocks(g):
        return jnp.moveaxis(g.reshape(g.shape[0], N_DEV, g.shape[1] // N_DEV), 1, 0)

    def rows_blocks(g):
        return g.reshape(N_DEV, g.shape[0] // N_DEV, g.shape[1])

    chip_sums = {}

    def chip_add(key, blocks, theirs):
        chip_sums[key] = _chip_add(key + "_add", blocks, theirs, ci)

    def mlp_bwd(i, dm, act, h2, first_comm=None):
        dw2 = _matmul(f"ff_out_dw_{i}", act, dm, ta=True, outs=(BF16,), comm=first_comm)
        dw2, carried = dw2 if first_comm is not None else (dw2, ())
        blk2 = rows_blocks(dw2)
        dz, (theirs,) = _matmul(f"ff_out_dx_{i}", dm, w2[i], tb=True, outs=(BF16,), extras=(act,), epilogue=relu2_bwd,
                                comm=_swap_comm([blk2]))
        chip_add(f"ff_out_{i}", blk2, theirs)
        blk1, (recv[f"ff_out_{i}"],) = _matmul(f"ff_in_dw_{i}", h2, dz, ta=True, outs=(BF16,), col_blocks=N_DEV,
                                              comm=_chips_comm([chip_sums[f"ff_out_{i}"]]))
        dh2, (theirs,) = _matmul(f"ff_in_dx_{i}", dz, w1[i], tb=True, tm=MM_TILE_M // 2, tk=2 * MM_TILE_K,
                                 comm=_swap_comm([blk1]))
        chip_add(f"ff_in_{i}", blk1, theirs)
        return dh2, carried

    def mid_bwd(i, xin, o, args, dx1, dh2):
        res = _rowwise(f"mid_bwd{i}", _bwd_of(_f_mid, 2, 2, (0, 1, 2, 3, 4, 5, 6)), seq, tmb,
                       [_t(xin), _t(o), _t(dx1), _t(dh2)], args, [(d, F32), (d, BF16)], [(1, d)] * 5)
        d_norm[i][1], d_ml[i][2], d_norm[i][2], d_ml[i][3], d_ml[i][4] = res[2:]
        return res[0], res[1]

    d_norm[1][3], d_ml[1][5] = dg_last, dgate_last
    dh2_1, _ = mlp_bwd(1, dm1, act1, h2_1)
    dx2a, do1 = mid_bwd(1, x2l, o1, mid1_args, dx3, dh2_1)
    blk_go = rows_blocks(_matmul("gm_out_dw", gmix, do1, ta=True, outs=(BF16,)))
    dgmix, (theirs,) = _matmul("gm_out_dx", do1, w_go, tb=True, comm=_swap_comm([blk_go]))
    chip_add("gm_out", blk_go, theirs)
    gm_res = _rowwise("gm_mix_bwd", _bwd_of(f_gm, 2, 1, (0, 1, 2, 3, 4, 5, 6, 7)), seq, tmb,
                      gm_tiled + [_t(dgmix)], gm_full, [(d_gm, BF16), (d_gm, BF16)],
                      [(1, d_gm)] * 4 + [w_sp.shape, bsp_t.shape])
    dzg = jnp.concatenate([gm_res[0], gm_res[1]], axis=1)
    g_gm_b_in = jnp.concatenate([gm_res[2], gm_res[3]], axis=1)
    g_gm_vg, g_gm_vb, g_w_sp = gm_res[4], gm_res[5], gm_res[6]
    g_b_sp = gm_res[7][:, :gm_groups].T
    dh1, (recv["gm_out"],) = _matmul("gm_in_dx", dzg, w_gi, tb=True, comm=_chips_comm([chip_sums["gm_out"]]))
    blk_gi = _matmul("gm_in_dw", h1, dzg, ta=True, outs=(BF16,), col_blocks=N_DEV)

    def f_between(pid, xv, dh, dxa, x1v, m0v, g, sh, sc, g3, gate):
        dxv, dg, dsh, dsc = _bwd_of(_f_pre, 1, 1, (0, 1, 2, 3))(pid, xv, dh, g, sh, sc)
        dx1v, dm0v, dg3, dgate = _bwd_of(_f_post, 2, 1, (0, 1, 2, 3))(pid, x1v, m0v, dxv + dxa, g3, gate)
        return (dx1v, dm0v, dg, dsh, dsc, dg3, dgate)

    res = _rowwise("between_bwd", f_between, seq, tmb, [_t(x2l), _t(dh1), _t(dx2a), _t(x1), _t(m0)], pre1_args + post0_args,
                   [(d, F32), (d, BF16)], [(1, d)] * 5)
    dx1, dm0 = res[0], res[1]
    d_norm[1][0], d_ml[1][0], d_ml[1][1], d_norm[0][3], d_ml[0][5] = res[2:]

    dh2_0, (theirs,) = mlp_bwd(0, dm0, act0, h2_0, first_comm=_swap_comm([blk_gi]))
    chip_add("gm_in", blk_gi, theirs)
    dxa, do0 = mid_bwd(0, x2, o0, mid0_args, dx1, dh2_0)
    blk_out = rows_blocks(_matmul("ar_out_dw", ar, do0, ta=True, outs=(BF16,)))
    d_ar, (theirs,) = _matmul("ar_out_dx", do0, w_out, tb=True, comm=_swap_comm([blk_out]))
    chip_add("ar_out", blk_out, theirs)

    late = ["ff_in_1", "gm_in", "ff_in_0"]
    dq, dkt, dvt, *landed = _attn_bwd(q_r, kv_r, d_ar, nct, tm, comm=_chips_comm([chip_sums[k] for k in late]))
    recv.update(zip(late, landed))
    dkv_all = jnp.concatenate([dkt, dvt], axis=0).T

    def f_qkv_bwd(pid, pq, pkv, cos_t, sin_t, dq_t, dkv_t, *fulls):
        dq_t = jnp.where(pid < nctb, 0.0, dq_t)
        return _bwd_of(f_qkv, 4, 2, (0, 1, 4, 5))(pid, pq, pkv, cos_t, sin_t, dq_t, dkv_t, *fulls)

    qkv_res = _rowwise("qkv_bwd", f_qkv_bwd, t_all, tmb, qkv_tiled + [_t(dq, -n_ctx), _t(dkv_all)],
                       [q_g, k_g, perm], [(attn_w, BF16), (2 * kv_w, BF16)], [q_g.shape, k_g.shape])
    dproj_q, dproj_kv, g_q_g, g_k_g = qkv_res

    rnn_res = _rowwise("rnn_out_bwd", _bwd_of(_f_rnnout, 3, 1, (0, 2)), seq, tmb, rnn_tiled + [_t(d_ar, 0, 1, d_rnn)], [],
                       [(d_rnn, F32), (d_rnn, BF16)])
    zc = jnp.zeros((n_ctx, d_rnn), F32)
    dh_all = to3(jnp.concatenate([zc, rnn_res[0]], axis=0))
    dproj_g = jnp.concatenate([zc.astype(BF16), rnn_res[1]], axis=0)
    da_f, db_f = _scan_bwd("scan_f_bwd", to3(a_f), dh_all, hp_f, nc_scan, False)
    da_b, db_b = _scan_bwd("scan_b_bwd", to3(a_b), dh_all, hp_b, nc_scan, True)
    gate_cts = [_t(a.reshape(t_all, d_rnn)) for a in (da_f, db_f, da_b, db_b)]
    gates_res = _rowwise("gates_bwd", _bwd_of(f_gates, 1, 4, (0, 1, 2, 3, 4, 5)), t_all, min(tmb, GATES_BWD_TILE),
                         [_t(xc)] + gate_cts, gate_full, [(d_rnn, F32)], [wa3.shape, ba.shape, wx3.shape, bx.shape, lam.shape])
    dxc, g_wa, g_ba, g_wx, g_bx, g_lam = gates_res
    f_conv_b = functools.partial(_f_conv_bwd, n_ctx_tiles=nctb, n_tiles=t_all // tmb)
    conv_tiled = with_neighbours(proj, tmb, 1, d_rnn) + with_neighbours(dxc, tmb)
    dproj_x, g_conv_w, g_conv_b = _rowwise("conv_bwd", f_conv_b, t_all, tmb, conv_tiled, [conv_w],
                                           [(d_rnn, BF16)], [conv_w.shape, (1, d_rnn)])
    dproj = jnp.concatenate([dproj_q, dproj_x, dproj_g, dproj_kv], axis=1)
    sq_names = ['ar_wa', 'ar_wx', 'gm_w_sp']

    def stack_sq(parts):
        return jnp.concatenate([p.reshape(-1, LANES) for p in parts], axis=0)

    sq_pack = stack_sq([g_wa, g_wx, g_w_sp]).astype(BF16)
    g_w_in, (g_sq,) = _matmul("ar_in_dw", h0, dproj, ta=True, outs=(BF16,), comm=_gather_comm([sq_pack]))
    g_w_in = jnp.concatenate([g_w_in[:, :attn_w], g_w_in[:, 3 * attn_w:], g_w_in[:, attn_w:3 * attn_w]], axis=1)
    blk_in = cols_blocks(g_w_in)
    dh0, (recv["ar_out"], theirs) = _matmul("ar_in_dx", dproj, w_in, tb=True, tm=tm_tok,
                                            comm=_both(_chips_comm([chip_sums["ar_out"]]), _swap_comm([blk_in])))
    chip_add("ar_in", blk_in, theirs)

    f_pre0b = functools.partial(_f_pre_ctx, n_ctx_tiles=nctb)

    def f_pre0_bwd(pid, xcv, xlv, dh, dxp, g, shc, scc, shl, scl):
        grads = _bwd_of(f_pre0b, 2, 1, (1, 2, 3, 4, 5, 6))(pid, xcv, xlv, dh, g, shc, scc, shl, scl)
        return (grads[0] + dxp,) + tuple(grads[1:])

    res = _rowwise("pre_bwd0", f_pre0_bwd, seq, tmb, tokens + [_t(dh0), _t(dxa, -n_ctx)], pre0_args, [(d, F32)], [(1, d)] * 5,
                   comm=_chips_comm([chip_sums["ar_in"]]), skip_rows=n_ctx)
    grad_x = res[0][None]
    d_norm[0][0], d_mc_shift, d_mc_scale, d_ml[0][0], d_ml[0][1] = res[1:6]
    recv["ar_in"] = res[6]

    z1d = jnp.zeros((1, d), F32)
    dml = jnp.concatenate([jnp.concatenate(r, axis=0)[None] for r in d_ml], axis=0)
    dmc = jnp.concatenate([jnp.concatenate([d_mc_shift, d_mc_scale] + [z1d] * 4, axis=0)[None],
                           jnp.zeros((n_layers - 1, 6, d), F32)], axis=0)
    g_norm = jnp.concatenate([jnp.concatenate(r, axis=0)[None] for r in d_norm], axis=0)
    small_parts = [dmc, g_norm, g_q_g, g_k_g, g_conv_w, g_conv_b, g_ba, g_bx, g_lam, g_gm_b_in, g_gm_vg, g_gm_vb, g_b_sp]
    small2, off2 = _pack([dml] + small_parts)
    (gs2,) = _run_comm("gather_small_grads", _gather_comm([small2]))
    dml_all = gs2.reshape(N_DEV, -1)[:, :off2[1]].reshape(N_DEV, n_layers, 6 * d)
    summed = _sum_lead("sum_small_grads", gs2).reshape(-1)
    summed_sq = _sum_lead("sum_square_grads", g_sq)

    def seg2(k, shape):
        return summed[off2[k + 1]:off2[k + 2]].reshape(shape)

    dmc_sum = seg2(0, (n_layers, 6 * d))
    dmod_rows = jnp.concatenate([jnp.moveaxis(dml_all, 0, 1), dmc_sum[:, None, :],
                                 jnp.zeros((n_layers, MOD_ROWS - N_DEV - 1, 6 * d), F32)], axis=1)
    g_b_mod = _sum_lead("sum_b_mod", jnp.moveaxis(dmod_rows, 1, 0).reshape(MOD_ROWS, n_layers * 6 * d // LANES, LANES))
    g_b_mod = g_b_mod.reshape(n_layers, 6 * d)
    g_w_mod, ds16 = _mod_bwd(c16, w_mod, _my_cols(dmod_rows, me, n6))
    (g_ds,) = _run_comm("gather_dctx", _gather_comm([ds16[N_DEV].reshape(d // LANES, LANES)]))
    ds_ctx = _sum_lead("sum_dctx", g_ds)
    (g_c_ctx,) = _rowwise("silu_bwd", _f_silu_mul, d // LANES, d // LANES, [_t(c_ctx.reshape(d // LANES, LANES)), _t(ds_ctx)],
                          [], [(LANES, F32)])
    g_c_ctx = g_c_ctx.reshape(d)

    grads = {
        'c_ctx': g_c_ctx, 'b_mod': g_b_mod,
        'norm_g': _my_cols(seg2(1, (n_layers, 4, d)), me, d // N_DEV),
        'ar_q_g': seg2(2, ar_q_g.shape), 'ar_k_g': seg2(3, ar_k_g.shape),
        'ar_conv_w': _my_cols(seg2(4, (1, CONV_W, d_rnn)), me, d_rnn // N_DEV),
        'ar_conv_b': seg2(5, ar_conv_b.shape),
        'ar_ba': _my_cols(seg2(6, (1, 2, d_rnn)), me, d_rnn // N_DEV),
        'ar_bx': _my_cols(seg2(7, (1, 2, d_rnn)), me, d_rnn // N_DEV),
        'ar_lambda': _my_cols(seg2(8, (1, 2, d_rnn)), me, d_rnn // N_DEV),
        'gm_b_in': _my_cols(seg2(9, (1, 2 * d_gm)), me, 2 * d_gm // N_DEV),
        'gm_v_g': _my_cols(seg2(10, (1, d_gm)), me, d_gm // N_DEV),
        'gm_v_b': _my_cols(seg2(11, (1, d_gm)), me, d_gm // N_DEV),
        'gm_b_sp': seg2(12, gm_b_sp.shape),
    }
    small_names = list(grads)
    deltas, new_m, new_v = {}, {}, {}

    sq_res = (summed_sq,) + tuple(_adam_f32("adam_square", summed_sq, *[stack_sq([src[n] for n in sq_names])
                                                                        for src in (wts, mom1, mom2)]))
    first = 0
    for n in sq_names:
        rows_n = wts[n].size // LANES
        for dst, slab in zip((grads, deltas, new_m, new_v), sq_res):
            dst[n] = slab[first:first + rows_n].reshape(wts[n].shape)
        first += rows_n

    wp, offw = _pack([wts[n] for n in small_names])
    mp, _ = _pack([mom1[n] for n in small_names])
    vp, _ = _pack([mom2[n] for n in small_names])
    gp, _ = _pack([grads[n] for n in small_names])
    dp, mp2, vp2 = _adam_f32("adam_small", gp, wp, mp, vp)
    for k, n in enumerate(small_names):
        for dst, slab in ((deltas, dp), (new_m, mp2), (new_v, vp2)):
            dst[n] = slab.reshape(-1)[offw[k]:offw[k + 1]].reshape(wts[n].shape)

    grads['w_mod'] = g_w_mod
    dw, mw, vw = _adam_f32("adam_w_mod", g_w_mod.reshape(n_layers * d, n6), w_mod.reshape(n_layers * d, n6),
                           m_w_mod.reshape(n_layers * d, n6), v_w_mod.reshape(n_layers * d, n6))
    deltas['w_mod'], new_m['w_mod'], new_v['w_mod'] = (a.reshape(w_mod.shape) for a in (dw, mw, vw))

    received = {
        'w_ff_in': [recv[f"ff_in_{i}"] for i in range(n_layers)], 'w_ff_out': [recv[f"ff_out_{i}"] for i in range(n_layers)],
        'ar_w_in': [recv["ar_in"]], 'ar_w_out': [recv["ar_out"]], 'gm_w_in': [recv["gm_in"]], 'gm_w_out': [recv["gm_out"]]}
    for n, r in received.items():
        shp = wts[n].shape
        flat = (shp[0] * shp[1], shp[2])
        res = _adam_recv("adam_" + n, r, wts[n].reshape(flat), mom1[n].reshape(flat), mom2[n].reshape(flat))
        grads[n], deltas[n], new_m[n], new_v[n] = (a.reshape(shp) for a in res)

    return (loss, grad_x, *[grads[n] for n in WEIGHTS], *[deltas[n] for n in WEIGHTS],
            *[new_m[n] for n in WEIGHTS], *[new_v[n] for n in WEIGHTS])
```
